```python
import math
import jax, jax.numpy as jnp
from jax import lax
import numpy as np

D_MODEL = 1024
BATCH = 8
SEQ = 2048
DEPTH = 4
DEC_BATCH = 32
DEC_SEQ = 8
PAST_LEN = 16384
PAGE_SIZE = 128

ROPE_THETA = 500000.0
NORM_EPS = 1e-6
NEG_INF = -1e30
N_MLA_LAYERS = (DEPTH + 1) // 2
N_SWA_LAYERS = DEPTH // 2

MLA_HEADS = 16
MLA_Q_LORA = 384
MLA_KV_LORA = 256
MLA_NOPE = 64
MLA_ROPE = 32
MLA_V = 64
MLA_QBLK = 128
MLA_SCALE = (MLA_NOPE + MLA_ROPE) ** -0.5

SWA_HEADS = 16
SWA_KV_HEADS = 4
SWA_GROUP = SWA_HEADS // SWA_KV_HEADS
SWA_HEAD_DIM = D_MODEL // SWA_HEADS
SWA_ROT = SWA_HEAD_DIM // 4
SWA_SCALE = SWA_HEAD_DIM ** -0.5
WINDOW = 128

PEER_HEADS = 8
PEER_N_KEYS = 128
PEER_N_EXPERTS = PEER_N_KEYS * PEER_N_KEYS
PEER_TOPK = 16
PEER_KEY_DIM = 256
PEER_HALF = PEER_KEY_DIM // 2
PEER_BLOCK = 64

kernel_name = 'hybrid_mla_swa_peer_adaln_step'


def rmsnorm(x, g):
    xf = x.astype(jnp.float32)
    y = xf * lax.rsqrt(jnp.mean(xf * xf, axis=-1, keepdims=True) + NORM_EPS)
    return (y * g.astype(jnp.float32)).astype(x.dtype)


def rope(x, pos, rot_dim):
    half = rot_dim // 2
    inv_freq = ROPE_THETA ** (-jnp.arange(half, dtype=jnp.float32) * 2.0 / rot_dim)
    ang = pos.astype(jnp.float32)[:, None] * inv_freq[None, :]
    shape = (pos.shape[0],) + (1,) * (x.ndim - 3) + (half,)
    cos = jnp.cos(ang).reshape(shape)
    sin = jnp.sin(ang).reshape(shape)
    xf = x[..., :rot_dim].astype(jnp.float32)
    x1, x2 = xf[..., :half], xf[..., half:]
    rot = jnp.concatenate([x1 * cos - x2 * sin, x2 * cos + x1 * sin], axis=-1).astype(x.dtype)
    return jnp.concatenate([rot, x[..., rot_dim:]], axis=-1)


def modulation(c, w, b):
    m = jax.nn.silu(c) @ w + b
    return jnp.split(m[:, None, :], 6, axis=-1)


def modulate(x, g, shift, scale):
    return rmsnorm(x, g) * (1.0 + scale) + shift


def mla_project(h, pos, w_in, q_norm, kv_norm, w_uq):
    B, T, _ = h.shape
    z = h @ w_in
    cq = rmsnorm(z[..., :MLA_Q_LORA], q_norm)
    ckv = rmsnorm(z[..., MLA_Q_LORA:MLA_Q_LORA + MLA_KV_LORA], kv_norm)
    kpe = rope(z[..., MLA_Q_LORA + MLA_KV_LORA:], pos, MLA_ROPE)
    q = (cq @ w_uq).reshape(B, T, MLA_HEADS, MLA_NOPE + MLA_ROPE)
    q_nope = q[..., :MLA_NOPE]
    q_pe = rope(q[..., MLA_NOPE:], pos, MLA_ROPE)
    return q_nope, q_pe, ckv, kpe


def mla_prompt(h, pos, w_in, q_norm, kv_norm, w_uq, w_uk, w_uv, w_o):
    B, S, _ = h.shape
    q_nope, q_pe, ckv, kpe = mla_project(h, pos, w_in, q_norm, kv_norm, w_uq)
    k_nope = jnp.einsum('bsl,lhn->bshn', ckv, w_uk)
    v = jnp.einsum('bsl,lhd->bshd', ckv, w_uv)
    nb = S // MLA_QBLK
    to_blocks = lambda t: jnp.moveaxis(t.reshape((B, nb, MLA_QBLK) + t.shape[2:]), 1, 0)
    k_pos = jnp.arange(S)

    def block(args):
        qn, qp, n = args
        q_pos = n * MLA_QBLK + jnp.arange(MLA_QBLK)
        s = (jnp.einsum('bqhn,bkhn->bhqk', qn, k_nope)
             + jnp.einsum('bqhr,bkr->bhqk', qp, kpe)).astype(jnp.float32) * MLA_SCALE
        s = jnp.where(k_pos[None, :] <= q_pos[:, None], s, NEG_INF)
        p = jax.nn.softmax(s, axis=-1).astype(v.dtype)
        return jnp.einsum('bhqk,bkhd->bqhd', p, v)

    o = lax.map(block, (to_blocks(q_nope), to_blocks(q_pe), jnp.arange(nb)))
    o = jnp.moveaxis(o, 0, 1).reshape(B, S, MLA_HEADS * MLA_V)
    return o @ w_o, ckv, kpe


def mla_sample(h, pos, cache_ckv, cache_kpe, page_table, w_in, q_norm, kv_norm, w_uq, w_uk, w_uv, w_o):
    DB, T, _ = h.shape
    q_nope, q_pe, ckv, kpe = mla_project(h, pos, w_in, q_norm, kv_norm, w_uq)
    past_ckv = cache_ckv[page_table].reshape(DB, -1, MLA_KV_LORA)
    past_kpe = cache_kpe[page_table].reshape(DB, -1, MLA_ROPE)
    keys_ckv = jnp.concatenate([past_ckv, ckv], axis=1)
    keys_kpe = jnp.concatenate([past_kpe, kpe], axis=1)
    k_pos = jnp.arange(keys_ckv.shape[1])
    q_lat = jnp.einsum('bqhn,lhn->bqhl', q_nope, w_uk)
    s = (jnp.einsum('bqhl,bkl->bhqk', q_lat, keys_ckv)
         + jnp.einsum('bqhr,bkr->bhqk', q_pe, keys_kpe)).astype(jnp.float32) * MLA_SCALE
    s = jnp.where(k_pos[None, :] <= pos[:, None], s, NEG_INF)
    p = jax.nn.softmax(s, axis=-1).astype(keys_ckv.dtype)
    o_lat = jnp.einsum('bhqk,bkl->bqhl', p, keys_ckv)
    o = jnp.einsum('bqhl,lhd->bqhd', o_lat, w_uv).reshape(DB, T, MLA_HEADS * MLA_V)
    return o @ w_o, ckv, kpe


def sink_attention(q, k, v, mask, sinks):
    s = jnp.einsum('nqkgd,nskd->nkgqs', q, k).astype(jnp.float32) * SWA_SCALE
    s = jnp.where(mask[:, None, None], s, NEG_INF)
    sink = jnp.broadcast_to(sinks.astype(jnp.float32).reshape(1, SWA_KV_HEADS, SWA_GROUP, 1, 1),
                            s.shape[:-1] + (1,))
    p = jax.nn.softmax(jnp.concatenate([s, sink], axis=-1), axis=-1)[..., :-1]
    return jnp.einsum('nkgqs,nskd->nqkgd', p.astype(v.dtype), v)


def swa_project(h, pos, w_qkv, b_qkv):
    B, T, _ = h.shape
    z = h @ w_qkv + b_qkv
    nq = SWA_HEADS * SWA_HEAD_DIM
    nk = SWA_KV_HEADS * SWA_HEAD_DIM
    q = z[..., :nq].reshape(B, T, SWA_KV_HEADS, SWA_GROUP, SWA_HEAD_DIM)
    k = z[..., nq:nq + nk].reshape(B, T, SWA_KV_HEADS, SWA_HEAD_DIM)
    v = z[..., nq + nk:].reshape(B, T, SWA_KV_HEADS, SWA_HEAD_DIM)
    return rope(q, pos, SWA_ROT), rope(k, pos, SWA_ROT), v


def swa_prompt(h, pos, w_qkv, b_qkv, sinks, w_o):
    B, S, _ = h.shape
    q, k, v = swa_project(h, pos, w_qkv, b_qkv)
    nb = S // WINDOW
    qb = q.reshape(B * nb, WINDOW, SWA_KV_HEADS, SWA_GROUP, SWA_HEAD_DIM)

    def band(t):
        tb = t.reshape(B, nb, WINDOW, SWA_KV_HEADS, SWA_HEAD_DIM)
        prev = jnp.concatenate([jnp.zeros_like(tb[:, :1]), tb[:, :-1]], axis=1)
        return jnp.concatenate([prev, tb], axis=2).reshape(B * nb, 2 * WINDOW, SWA_KV_HEADS, SWA_HEAD_DIM)

    blk = jnp.arange(nb)[:, None, None]
    q_pos = blk * WINDOW + jnp.arange(WINDOW)[None, :, None]
    k_pos = (blk - 1) * WINDOW + jnp.arange(2 * WINDOW)[None, None, :]
    mask = (k_pos >= 0) & (k_pos <= q_pos) & (q_pos - k_pos < WINDOW)
    mask = jnp.broadcast_to(mask[None], (B,) + mask.shape).reshape(B * nb, WINDOW, 2 * WINDOW)
    o = sink_attention(qb, band(k), band(v), mask, sinks).reshape(B, S, SWA_HEADS * SWA_HEAD_DIM)
    n_buf = min(WINDOW, S)
    return o @ w_o, k[:, S - n_buf:], v[:, S - n_buf:]


def swa_sample(h, pos, buf_k, buf_v, w_qkv, b_qkv, sinks, w_o):
    DB, T, _ = h.shape
    q, k, v = swa_project(h, pos, w_qkv, b_qkv)
    n_buf = buf_k.shape[1]
    k_all = jnp.concatenate([buf_k, k], axis=1)
    v_all = jnp.concatenate([buf_v, v], axis=1)
    k_pos = PAST_LEN - n_buf + jnp.arange(n_buf + T)
    mask = (k_pos[None, :] <= pos[:, None]) & (pos[:, None] - k_pos[None, :] < WINDOW)
    o = sink_attention(q, k_all, v_all, mask[None], sinks).reshape(DB, T, SWA_HEADS * SWA_HEAD_DIM)
    return o @ w_o, k_all[:, -n_buf:], v_all[:, -n_buf:]


def peer(h, w_q, sub_keys, u_tab, v_tab):
    B, T, D = h.shape
    n = B * T
    xt = h.reshape(n, D)
    q = (xt @ w_q).reshape(n, PEER_HEADS, 2, PEER_HALF)
    s = jnp.einsum('nhpd,hpkd->nhpk', q, sub_keys).astype(jnp.float32)
    s_top, i_top = lax.top_k(s, PEER_TOPK)
    cand_s = (s_top[:, :, 0, :, None] + s_top[:, :, 1, None, :]).reshape(n, PEER_HEADS, PEER_TOPK * PEER_TOPK)
    cand_i = (i_top[:, :, 0, :, None] * PEER_N_KEYS + i_top[:, :, 1, None, :]).reshape(n, PEER_HEADS, PEER_TOPK * PEER_TOPK)
    best_s, best_pos = lax.top_k(cand_s, PEER_TOPK)
    experts = jnp.take_along_axis(cand_i, best_pos, axis=-1)
    gates = jax.nn.softmax(best_s, axis=-1).astype(h.dtype)
    pad = (-n) % PEER_BLOCK
    nb = (n + pad) // PEER_BLOCK
    xb = jnp.pad(xt, ((0, pad), (0, 0))).reshape(nb, PEER_BLOCK, D)
    eb = jnp.pad(experts, ((0, pad), (0, 0), (0, 0))).reshape(nb, PEER_BLOCK, PEER_HEADS, PEER_TOPK)
    gb = jnp.pad(gates, ((0, pad), (0, 0), (0, 0))).reshape(nb, PEER_BLOCK, PEER_HEADS, PEER_TOPK)

    def block(args):
        x_b, e_b, g_b = args
        u = u_tab[e_b]
        act = jax.nn.gelu(jnp.einsum('td,thkd->thk', x_b, u), approximate=False)
        return jnp.einsum('thk,thkd->td', g_b * act, v_tab[e_b])

    out = lax.map(block, (xb, eb, gb))
    return out.reshape(nb * PEER_BLOCK, D)[:n].reshape(B, T, D)


def setup_inputs(seed: int = 0) -> dict:
    key = jax.random.key(seed)
    ks = jax.random.split(key, 30)

    def nrm(k, shape, scale):
        return jax.random.normal(k, shape, jnp.float32) * scale

    d = D_MODEL
    n_pages = PAST_LEN // PAGE_SIZE
    used = DEC_BATCH * n_pages
    n_pool = used + max(1, used // 4)
    page_table = jax.random.permutation(ks[0], n_pool)[:used].reshape(DEC_BATCH, n_pages).astype(jnp.int32)
    n_buf = min(WINDOW, PAST_LEN)
    mla_in = MLA_Q_LORA + MLA_KV_LORA + MLA_ROPE
    swa_in = (SWA_HEADS + 2 * SWA_KV_HEADS) * SWA_HEAD_DIM
    return {
        'x_prompt': nrm(ks[1], (BATCH, SEQ, d), 1.0),
        'x_sample': nrm(ks[2], (DEC_BATCH, DEC_SEQ, d), 1.0),
        'cache_mla_ckv': nrm(ks[3], (N_MLA_LAYERS, n_pool, PAGE_SIZE, MLA_KV_LORA), 1.0),
        'cache_mla_kpe': nrm(ks[4], (N_MLA_LAYERS, n_pool, PAGE_SIZE, MLA_ROPE), 1.0),
        'cache_swa_k': nrm(ks[5], (N_SWA_LAYERS, DEC_BATCH, n_buf, SWA_KV_HEADS, SWA_HEAD_DIM), 1.0),
        'cache_swa_v': nrm(ks[6], (N_SWA_LAYERS, DEC_BATCH, n_buf, SWA_KV_HEADS, SWA_HEAD_DIM), 1.0),
        'page_table': page_table,
        'c_prompt': nrm(ks[7], (BATCH, d), 1.0),
        'c_sample': nrm(ks[8], (DEC_BATCH, d), 1.0),
        'ln1_g': 1.0 + nrm(ks[9], (DEPTH, d), 0.02),
        'ln2_g': 1.0 + nrm(ks[10], (DEPTH, d), 0.02),
        'w_mod': nrm(ks[11], (DEPTH, d, 6 * d), 0.5 * d ** -0.5),
        'b_mod': nrm(ks[12], (DEPTH, 6 * d), 0.01),
        'mla_w_in': nrm(ks[13], (N_MLA_LAYERS, d, mla_in), d ** -0.5),
        'mla_q_norm': 1.0 + nrm(ks[14], (N_MLA_LAYERS, MLA_Q_LORA), 0.02),
        'mla_kv_norm': 1.0 + nrm(ks[15], (N_MLA_LAYERS, MLA_KV_LORA), 0.02),
        'mla_w_uq': nrm(ks[16], (N_MLA_LAYERS, MLA_Q_LORA, MLA_HEADS * (MLA_NOPE + MLA_ROPE)), MLA_Q_LORA ** -0.5),
        'mla_w_uk': nrm(ks[17], (N_MLA_LAYERS, MLA_KV_LORA, MLA_HEADS, MLA_NOPE), MLA_KV_LORA ** -0.5),
        'mla_w_uv': nrm(ks[18], (N_MLA_LAYERS, MLA_KV_LORA, MLA_HEADS, MLA_V), MLA_KV_LORA ** -0.5),
        'mla_w_o': nrm(ks[19], (N_MLA_LAYERS, MLA_HEADS * MLA_V, d), (MLA_HEADS * MLA_V) ** -0.5),
        'swa_w_qkv': nrm(ks[20], (N_SWA_LAYERS, d, swa_in), d ** -0.5),
        'swa_b_qkv': nrm(ks[21], (N_SWA_LAYERS, swa_in), 0.01),
        'swa_sinks': nrm(ks[22], (N_SWA_LAYERS, SWA_HEADS), 1.0),
        'swa_w_o': nrm(ks[23], (N_SWA_LAYERS, SWA_HEADS * SWA_HEAD_DIM, d), (SWA_HEADS * SWA_HEAD_DIM) ** -0.5),
        'peer_w_q': nrm(ks[24], (DEPTH, d, PEER_HEADS * PEER_KEY_DIM), d ** -0.5),
        'peer_sub_keys': nrm(ks[25], (DEPTH, PEER_HEADS, 2, PEER_N_KEYS, PEER_HALF), PEER_HALF ** -0.5),
        'peer_u': nrm(ks[26], (DEPTH, PEER_N_EXPERTS, d), d ** -0.5),
        'peer_v': nrm(ks[27], (DEPTH, PEER_N_EXPERTS, d), PEER_HEADS ** -0.5),
        'final_g': 1.0 + nrm(ks[28], (d,), 0.02),
    }


def reference(x_prompt, x_sample, cache_mla_ckv, cache_mla_kpe, cache_swa_k, cache_swa_v, page_table,
              c_prompt, c_sample, ln1_g, ln2_g, w_mod, b_mod,
              mla_w_in, mla_q_norm, mla_kv_norm, mla_w_uq, mla_w_uk, mla_w_uv, mla_w_o,
              swa_w_qkv, swa_b_qkv, swa_sinks, swa_w_o,
              peer_w_q, peer_sub_keys, peer_u, peer_v, final_g):
    pos_p = jnp.arange(x_prompt.shape[1])
    pos_s = PAST_LEN + jnp.arange(x_sample.shape[1])
    x_p, x_s = x_prompt, x_sample
    ckv_p, kpe_p, ckv_s, kpe_s = [], [], [], []
    swk_p, swv_p, swk_s, swv_s = [], [], [], []
    for i in range(DEPTH):
        j = i // 2
        sh1p, sc1p, g1p, sh2p, sc2p, g2p = modulation(c_prompt, w_mod[i], b_mod[i])
        sh1s, sc1s, g1s, sh2s, sc2s, g2s = modulation(c_sample, w_mod[i], b_mod[i])
        hp = modulate(x_p, ln1_g[i], sh1p, sc1p)
        hs = modulate(x_s, ln1_g[i], sh1s, sc1s)
        if i % 2 == 0:
            op, a, b = mla_prompt(hp, pos_p, mla_w_in[j], mla_q_norm[j], mla_kv_norm[j], mla_w_uq[j],
                                  mla_w_uk[j], mla_w_uv[j], mla_w_o[j])
            os_, c, d = mla_sample(hs, pos_s, cache_mla_ckv[j], cache_mla_kpe[j], page_table,
                                   mla_w_in[j], mla_q_norm[j], mla_kv_norm[j], mla_w_uq[j],
                                   mla_w_uk[j], mla_w_uv[j], mla_w_o[j])
            ckv_p.append(a); kpe_p.append(b); ckv_s.append(c); kpe_s.append(d)
        else:
            op, a, b = swa_prompt(hp, pos_p, swa_w_qkv[j], swa_b_qkv[j], swa_sinks[j], swa_w_o[j])
            os_, c, d = swa_sample(hs, pos_s, cache_swa_k[j], cache_swa_v[j],
                                   swa_w_qkv[j], swa_b_qkv[j], swa_sinks[j], swa_w_o[j])
            swk_p.append(a); swv_p.append(b); swk_s.append(c); swv_s.append(d)
        x_p = x_p + g1p * op
        x_s = x_s + g1s * os_
        x_p = x_p + g2p * peer(modulate(x_p, ln2_g[i], sh2p, sc2p), peer_w_q[i], peer_sub_keys[i], peer_u[i], peer_v[i])
        x_s = x_s + g2s * peer(modulate(x_s, ln2_g[i], sh2s, sc2s), peer_w_q[i], peer_sub_keys[i], peer_u[i], peer_v[i])
    y_prompt = rmsnorm(x_p, final_g)
    y_sample = rmsnorm(x_s, final_g)
    return (y_prompt, y_sample,
            jnp.stack(ckv_p), jnp.stack(kpe_p), jnp.stack(ckv_s), jnp.stack(kpe_s),
            jnp.stack(swk_p), jnp.stack(swv_p), jnp.stack(swk_s), jnp.stack(swv_s))
```

```python
import functools

import numpy as np
import jax
import jax.numpy as jnp
from jax import lax
from jax.experimental import pallas as pl
from jax.experimental.pallas import tpu as pltpu

F32 = jnp.float32
BF16 = jnp.bfloat16
I32 = jnp.int32

D_MODEL = 1024
PAGE_SIZE = 128
ROPE_THETA = 500000.0
NORM_EPS = 1e-6
NEG_INF = -1e30

MLA_HEADS = 16
MLA_Q_LORA = 384
MLA_KV_LORA = 256
MLA_NOPE = 64
MLA_ROPE = 32
MLA_V = 64
MLA_SCALE = (MLA_NOPE + MLA_ROPE) ** -0.5

SWA_HEADS = 16
SWA_KV_HEADS = 4
SWA_GROUP = SWA_HEADS // SWA_KV_HEADS
SWA_HEAD_DIM = D_MODEL // SWA_HEADS
SWA_ROT = SWA_HEAD_DIM // 4
SWA_SCALE = SWA_HEAD_DIM ** -0.5
WINDOW = 128

PEER_HEADS = 8
PEER_N_KEYS = 128
PEER_TOPK = 16
PEER_HALF = 128
PEER_PICKS = PEER_HEADS * PEER_TOPK

LANES = 128
SUBLANES = 8
BF16_ROWS = 16
VMEM_LIMIT_BYTES = 56 * 1024 * 1024

ROW_BLOCK = 512
EXPERT_ROW_BLOCK = 256
ROUTE_BLOCK = 256
ROUTE_SUB = LANES
ATTN_Q_BLOCK = 256
ATTN_K_BLOCK = 256
PAGES_PER_STEP = 16
EXPERT_CHUNK_A = 8
GATE_TILE_PITCH = 136
MOD_COL_BLOCK = 1536


def _cparams(sem):
    return pltpu.CompilerParams(dimension_semantics=sem, vmem_limit_bytes=VMEM_LIMIT_BYTES)


def _dot(a, b):
    return jnp.dot(a, b, preferred_element_type=F32)


def _dot_nt(a, b):
    return lax.dot_general(a, b, (((1,), (1,)), ((), ())), preferred_element_type=F32)


def _rms(x, g):
    return x * lax.rsqrt(jnp.mean(x * x, axis=-1, keepdims=True) + NORM_EPS) * g


def _gelu(x):
    return 0.5 * x * (1.0 + lax.erf(x * np.float32(2.0 ** -0.5)))


def _modulate(x, g, shift, scale):
    return _rms(x, g) * (1.0 + scale) + shift


class _Rows:
    def __init__(self, n, tb, seq_len=None):
        assert n % tb == 0
        self.n, self.tb, self.grid = n, tb, n // tb
        self.bps = None
        if seq_len is not None:
            assert seq_len % tb == 0
            self.bps = seq_len // tb

    def rows(self, width):
        return pl.BlockSpec((self.tb, width), lambda i: (i, 0))

    def mod(self, k):
        if self.bps is None:
            return pl.BlockSpec((self.tb, D_MODEL), lambda i: (i, k))
        bps = self.bps
        return pl.BlockSpec((None, 1, D_MODEL), lambda i: (i // bps, 0, k))

    def pos(self, width):
        if self.bps is None:
            return pl.BlockSpec((self.tb, width), lambda i: (i, 0))
        bps = self.bps
        return pl.BlockSpec((self.tb, width), lambda i: (i % bps, 0))

    @staticmethod
    def full(shape):
        nd = len(shape)
        return pl.BlockSpec(shape, lambda i: (0,) * nd)


def _mod_kernel(c_ref, w_ref, b_ref, o_ref):
    c = c_ref[...]
    a = (c * jax.nn.sigmoid(c)).astype(BF16)
    o_ref[...] = _dot(a, w_ref[...].astype(BF16)) + b_ref[...]


def _modulation_all(c_all, w_mod, b_mod):
    depth, d, n6 = w_mod.shape
    nc = c_all.shape[0]
    nb = n6 // MOD_COL_BLOCK
    return pl.pallas_call(
        _mod_kernel,
        grid=(depth, nb),
        in_specs=[
            pl.BlockSpec((nc, d), lambda l, j: (0, 0)),
            pl.BlockSpec((None, d, MOD_COL_BLOCK), lambda l, j: (l, 0, j)),
            pl.BlockSpec((None, 1, MOD_COL_BLOCK), lambda l, j: (l, 0, j)),
        ],
        out_specs=pl.BlockSpec((None, nc, MOD_COL_BLOCK), lambda l, j: (l, 0, j)),
        out_shape=jax.ShapeDtypeStruct((depth, nc, n6), F32),
        compiler_params=_cparams(("arbitrary", "arbitrary")),
        name="adaln_modulation",
    )(c_all, w_mod, b_mod.reshape(depth, 1, n6))


MLA_IN_EXT = 896
MLA_KPE_COL = 640
MLA_KPE_SWAP_COL = 768
MLA_HEAD_PAD = LANES


def _mla_proj_kernel(x_ref, g_ref, sh_ref, sc_ref, c32_ref, s32_ref, c128_ref, s128_ref,
                     win_ref, qn_ref, kvn_ref, wuq_ref, wuqs_ref, wuk_ref, epe_ref, wuv_ref,
                     ckv_ref, kpe_ref, q_ref, k_ref, v_ref):
    h = _modulate(x_ref[...], g_ref[...], sh_ref[...], sc_ref[...]).astype(BF16)
    z = _dot(h, win_ref[...])
    cq = _rms(z[:, :MLA_Q_LORA], qn_ref[...]).astype(BF16)
    ckv = _rms(z[:, MLA_Q_LORA:MLA_Q_LORA + MLA_KV_LORA], kvn_ref[...])
    kpe = (z[:, MLA_KPE_COL:MLA_KPE_COL + MLA_ROPE] * c32_ref[...]
           + z[:, MLA_KPE_SWAP_COL:MLA_KPE_SWAP_COL + MLA_ROPE] * s32_ref[...])
    ckv_ref[...] = ckv
    kpe_ref[...] = kpe
    ckv_b = ckv.astype(BF16)
    q = _dot(cq, wuq_ref[...])
    qs = _dot(cq, wuqs_ref[...])
    cos = c128_ref[...]
    sin = s128_ref[...]
    for hd in range(MLA_HEADS):
        sl = slice(hd * MLA_HEAD_PAD, (hd + 1) * MLA_HEAD_PAD)
        q_ref[:, sl] = (q[:, sl] * cos + qs[:, sl] * sin).astype(BF16)
    k_ref[...] = (_dot(ckv_b, wuk_ref[...]) + _dot(kpe.astype(BF16), epe_ref[...])).astype(BF16)
    v_ref[...] = _dot(ckv_b, wuv_ref[...]).astype(BF16)


def _mla_project(rows, x, ln_g, mod, tabs, w):
    n = rows.n
    hp = MLA_HEADS * MLA_HEAD_PAD
    full = _Rows.full
    return pl.pallas_call(
        _mla_proj_kernel,
        grid=(rows.grid,),
        in_specs=[
            rows.rows(D_MODEL), full((1, D_MODEL)), rows.mod(0), rows.mod(1),
            rows.pos(MLA_ROPE), rows.pos(MLA_ROPE), rows.pos(MLA_HEAD_PAD), rows.pos(MLA_HEAD_PAD),
            full((D_MODEL, MLA_IN_EXT)), full((1, MLA_Q_LORA)), full((1, MLA_KV_LORA)),
            full((MLA_Q_LORA, hp)), full((MLA_Q_LORA, hp)), full((MLA_KV_LORA, hp)),
            full((MLA_ROPE, hp)), full((MLA_KV_LORA, MLA_HEADS * MLA_V)),
        ],
        out_specs=[rows.rows(MLA_KV_LORA), rows.rows(MLA_ROPE), rows.rows(hp), rows.rows(hp),
                   rows.rows(MLA_HEADS * MLA_V)],
        out_shape=[
            jax.ShapeDtypeStruct((n, MLA_KV_LORA), F32),
            jax.ShapeDtypeStruct((n, MLA_ROPE), F32),
            jax.ShapeDtypeStruct((n, hp), BF16),
            jax.ShapeDtypeStruct((n, hp), BF16),
            jax.ShapeDtypeStruct((n, MLA_HEADS * MLA_V), BF16),
        ],
        compiler_params=_cparams(("arbitrary",)),
        name="mla_project",
    )(x, ln_g, mod, mod, tabs["c32"], tabs["s32"], tabs["c128"], tabs["s128"],
      w["w_in"], w["q_norm"], w["kv_norm"], w["w_uq"], w["w_uq_swap"], w["w_uk"], w["e_pe"], w["w_uv"])


def _mla_attn_kernel(q_ref, k_ref, v_ref, o_ref, *, tq, tk):
    qi = pl.program_id(2)
    n_kv = ((qi + 1) * tq + tk - 1) // tk
    lane = lax.broadcasted_iota(I32, (tk, LANES), 1)
    row_pos = qi * tq + lax.broadcasted_iota(I32, (tq, tk), 0)
    col_iota = lax.broadcasted_iota(I32, (tq, tk), 1)
    out = None
    for hh in range(2):
        q = q_ref[:, hh * LANES:(hh + 1) * LANES]
        v_keep = (lane // MLA_V) == hh

        def body(j, carry, q=q, v_keep=v_keep, hh=hh):
            m, l, acc = carry
            start = pl.multiple_of(j * tk, tk)
            kb = k_ref[pl.ds(start, tk), hh * LANES:(hh + 1) * LANES]
            s = _dot_nt(q, kb) * MLA_SCALE
            s = jnp.where(col_iota + j * tk <= row_pos, s, NEG_INF)
            m_new = jnp.maximum(m, jnp.max(s, axis=-1, keepdims=True))
            alpha = jnp.exp(m - m_new)
            p = jnp.exp(s - m_new)
            l = alpha * l + jnp.sum(p, axis=-1, keepdims=True)
            vb = v_ref[pl.ds(start, tk), :]
            vb = jnp.where(v_keep, vb, jnp.zeros_like(vb))
            acc = alpha * acc + _dot(p.astype(BF16), vb)
            return m_new, l, acc

        init = (jnp.full((tq, 1), NEG_INF, F32), jnp.zeros((tq, 1), F32), jnp.zeros((tq, LANES), F32))
        _, l, acc = lax.fori_loop(0, n_kv, body, init)
        o = acc / l
        out = o if out is None else out + o
    o_ref[...] = out.astype(BF16)


def _mla_attention(q, k, v, batch, seq):
    tq, tk = ATTN_Q_BLOCK, ATTN_K_BLOCK
    nq = seq // tq
    q3 = q.reshape(batch, seq, -1)
    k3 = k.reshape(batch, seq, -1)
    v3 = v.reshape(batch, seq, -1)
    out = pl.pallas_call(
        functools.partial(_mla_attn_kernel, tq=tq, tk=tk),
        grid=(batch, MLA_HEADS // 2, nq),
        in_specs=[
            pl.BlockSpec((None, tq, 2 * LANES), lambda b, hp, i: (b, i, hp)),
            pl.BlockSpec((None, seq, 2 * LANES), lambda b, hp, i: (b, 0, hp)),
            pl.BlockSpec((None, seq, LANES), lambda b, hp, i: (b, 0, hp)),
        ],
        out_specs=pl.BlockSpec((None, tq, LANES), lambda b, hp, i: (b, i, hp)),
        out_shape=jax.ShapeDtypeStruct((batch, seq, MLA_HEADS * MLA_V), BF16),
        compiler_params=_cparams(("arbitrary", "arbitrary", "arbitrary")),
        name="mla_prompt_attention",
    )(q3, k3, v3)
    return out.reshape(batch * seq, MLA_HEADS * MLA_V)


def _qlat_kernel(q_ref, wk_ref, o_ref):
    for hd in range(MLA_HEADS):
        qh = q_ref[:, hd * MLA_HEAD_PAD:(hd + 1) * MLA_HEAD_PAD]
        o_ref[:, hd * MLA_KV_LORA:(hd + 1) * MLA_KV_LORA] = _dot(qh, wk_ref[hd]).astype(BF16)


def _mla_absorb_queries(q, w_ukt):
    n = q.shape[0]
    return pl.pallas_call(
        _qlat_kernel,
        grid=(1,),
        in_specs=[_Rows.full(q.shape), _Rows.full(w_ukt.shape)],
        out_specs=_Rows.full((n, MLA_HEADS * MLA_KV_LORA)),
        out_shape=jax.ShapeDtypeStruct((n, MLA_HEADS * MLA_KV_LORA), BF16),
        compiler_params=_cparams(("arbitrary",)),
        name="mla_absorb_queries",
    )(q, w_ukt)


def _paged_attn_kernel(pt_ref, qlat_ref, qpe_ref, nckv_ref, nkpe_ref, *rest, n_steps, dec_seq):
    pg = PAGES_PER_STEP
    ckv_refs = rest[:pg]
    kpe_refs = rest[pg:2 * pg]
    o_ref = rest[2 * pg]
    m_sc, l_sc, acc_sc = rest[2 * pg + 1:]
    g = pl.program_id(1)

    @pl.when(g == 0)
    def _():
        m_sc[...] = jnp.full(m_sc.shape, NEG_INF, F32)
        l_sc[...] = jnp.zeros(l_sc.shape, F32)
        acc_sc[...] = jnp.zeros(acc_sc.shape, F32)

    qlat = qlat_ref[...]
    qpe = qpe_ref[...]

    def update(s, vals):
        m = m_sc[...]
        m_new = jnp.maximum(m, jnp.max(s, axis=-1, keepdims=True))
        alpha = jnp.exp(m - m_new)
        p = jnp.exp(s - m_new)
        l_sc[...] = alpha * l_sc[...] + jnp.sum(p, axis=-1, keepdims=True)
        acc_sc[...] = alpha * acc_sc[...] + _dot(p.astype(BF16), vals)
        m_sc[...] = m_new

    for i in range(pg):
        ck = ckv_refs[i][...].astype(BF16)
        kp = kpe_refs[i][...].astype(BF16)
        update((_dot_nt(qlat, ck) + _dot_nt(qpe, kp)) * MLA_SCALE, ck)

    @pl.when(g == n_steps - 1)
    def _():
        pad = BF16_ROWS - dec_seq
        ck = jnp.concatenate([nckv_ref[...], jnp.zeros((pad, MLA_KV_LORA), F32)], axis=0).astype(BF16)
        kp = jnp.concatenate([nkpe_ref[...], jnp.zeros((pad, MLA_ROPE), F32)], axis=0).astype(BF16)
        s = (_dot_nt(qlat, ck) + _dot_nt(qpe, kp)) * MLA_SCALE
        q_t = lax.broadcasted_iota(I32, s.shape, 0) // MLA_HEADS
        k_t = lax.broadcasted_iota(I32, s.shape, 1)
        update(jnp.where(k_t <= q_t, s, NEG_INF), ck)
        o_ref[...] = acc_sc[...] / l_sc[...]


def _mla_paged_attention(qlat, qpe, new_ckv, new_kpe, cache_ckv, cache_kpe, page_table):
    db, rows, _ = qlat.shape
    dec_seq = new_ckv.shape[1]
    n_pages = page_table.shape[1]
    pg = PAGES_PER_STEP
    assert n_pages % pg == 0
    n_steps = n_pages // pg

    def page_spec(width, i):
        return pl.BlockSpec((None, PAGE_SIZE, width), lambda b, g, pt: (pt[b, g * pg + i], 0, 0))

    in_specs = [
        pl.BlockSpec((None, rows, MLA_KV_LORA), lambda b, g, pt: (b, 0, 0)),
        pl.BlockSpec((None, rows, MLA_ROPE), lambda b, g, pt: (b, 0, 0)),
        pl.BlockSpec((None, dec_seq, MLA_KV_LORA), lambda b, g, pt: (b, 0, 0)),
        pl.BlockSpec((None, dec_seq, MLA_ROPE), lambda b, g, pt: (b, 0, 0)),
    ]
    in_specs += [page_spec(MLA_KV_LORA, i) for i in range(pg)]
    in_specs += [page_spec(MLA_ROPE, i) for i in range(pg)]
    grid_spec = pltpu.PrefetchScalarGridSpec(
        num_scalar_prefetch=1,
        grid=(db, n_steps),
        in_specs=in_specs,
        out_specs=pl.BlockSpec((None, rows, MLA_KV_LORA), lambda b, g, pt: (b, 0, 0)),
        scratch_shapes=[pltpu.VMEM((rows, 1), F32), pltpu.VMEM((rows, 1), F32),
                        pltpu.VMEM((rows, MLA_KV_LORA), F32)],
    )
    return pl.pallas_call(
        functools.partial(_paged_attn_kernel, n_steps=n_steps, dec_seq=dec_seq),
        grid_spec=grid_spec,
        out_shape=jax.ShapeDtypeStruct((db, rows, MLA_KV_LORA), F32),
        compiler_params=_cparams(("arbitrary", "arbitrary")),
        name="mla_paged_attention",
    )(page_table, qlat, qpe, new_ckv, new_kpe, *([cache_ckv] * pg), *([cache_kpe] * pg))


def _sample_out_kernel(olat_ref, wbd_ref, wo_ref, x_ref, gate_ref, o_ref):
    o = _dot(olat_ref[...].astype(BF16), wbd_ref[...]).astype(BF16)
    o_ref[...] = x_ref[...] + gate_ref[...] * _dot(o, wo_ref[...])


def _mla_sample_out(rows, olat, w_bd, w_o, x, mod):
    return pl.pallas_call(
        _sample_out_kernel,
        grid=(rows.grid,),
        in_specs=[rows.rows(olat.shape[1]), _Rows.full(w_bd.shape), _Rows.full(w_o.shape),
                  rows.rows(D_MODEL), rows.mod(2)],
        out_specs=rows.rows(D_MODEL),
        out_shape=jax.ShapeDtypeStruct((rows.n, D_MODEL), F32),
        compiler_params=_cparams(("arbitrary",)),
        name="mla_sample_out",
    )(olat, w_bd, w_o, x, mod)


def _out_proj_kernel(o_ref, wo_ref, x_ref, gate_ref, y_ref):
    y_ref[...] = x_ref[...] + gate_ref[...] * _dot(o_ref[...], wo_ref[...])


def _out_proj(rows, o, w_o, x, mod):
    return pl.pallas_call(
        _out_proj_kernel,
        grid=(rows.grid,),
        in_specs=[rows.rows(o.shape[1]), _Rows.full(w_o.shape), rows.rows(D_MODEL), rows.mod(2)],
        out_specs=rows.rows(D_MODEL),
        out_shape=jax.ShapeDtypeStruct((rows.n, D_MODEL), F32),
        compiler_params=_cparams(("arbitrary",)),
        name="attn_out_proj",
    )(o, w_o, x, mod)


SWA_NQ = SWA_HEADS * SWA_HEAD_DIM
SWA_NK = SWA_KV_HEADS * SWA_HEAD_DIM
SWA_QKV = SWA_NQ + 2 * SWA_NK
SWA_EXT = SWA_QKV + SWA_NQ + SWA_NK


def _swa_proj_kernel(x_ref, g_ref, sh_ref, sc_ref, cos_ref, sin_ref, w_ref, b_ref,
                     q_ref, kb_ref, vb_ref, k_ref, v_ref):
    h = _modulate(x_ref[...], g_ref[...], sh_ref[...], sc_ref[...]).astype(BF16)
    z = _dot(h, w_ref[...]) + b_ref[...]
    cos = cos_ref[...]
    sin = sin_ref[...]
    for t in range((SWA_NQ + SWA_NK) // LANES):
        sl = slice(t * LANES, (t + 1) * LANES)
        sw = slice(SWA_QKV + t * LANES, SWA_QKV + (t + 1) * LANES)
        r = z[:, sl] * cos + z[:, sw] * sin
        if t < SWA_NQ // LANES:
            q_ref[:, sl] = (r * SWA_SCALE).astype(BF16)
        else:
            ks = slice(t * LANES - SWA_NQ, (t + 1) * LANES - SWA_NQ)
            k_ref[:, ks] = r
            kb_ref[:, ks] = r.astype(BF16)
    v = z[:, SWA_NQ + SWA_NK:SWA_QKV]
    v_ref[...] = v
    vb_ref[...] = v.astype(BF16)


def _swa_project(rows, x, ln_g, mod, tabs, w):
    n = rows.n
    full = _Rows.full
    return pl.pallas_call(
        _swa_proj_kernel,
        grid=(rows.grid,),
        in_specs=[rows.rows(D_MODEL), full((1, D_MODEL)), rows.mod(0), rows.mod(1),
                  rows.pos(LANES), rows.pos(LANES), full((D_MODEL, SWA_EXT)), full((1, SWA_EXT))],
        out_specs=[rows.rows(SWA_NQ), rows.rows(SWA_NK), rows.rows(SWA_NK), rows.rows(SWA_NK),
                   rows.rows(SWA_NK)],
        out_shape=[
            jax.ShapeDtypeStruct((n, SWA_NQ), BF16),
            jax.ShapeDtypeStruct((n, SWA_NK), BF16),
            jax.ShapeDtypeStruct((n, SWA_NK), BF16),
            jax.ShapeDtypeStruct((n, SWA_NK), F32),
            jax.ShapeDtypeStruct((n, SWA_NK), F32),
        ],
        compiler_params=_cparams(("arbitrary",)),
        name="swa_project",
    )(x, ln_g, mod, mod, tabs["cos"], tabs["sin"], w["w_qkv"], w["b_qkv"])


def _swa_core(q_all, k_all, v_all, mask, sinks_ref, o_ref):
    lane = lax.broadcasted_iota(I32, (1, LANES), 1)
    lo = lane < SWA_HEAD_DIM
    zero = jnp.zeros((), BF16)
    for kh in range(SWA_KV_HEADS):
        tile = kh // 2
        k_t = k_all[:, tile * LANES:(tile + 1) * LANES]
        v_t = v_all[:, tile * LANES:(tile + 1) * LANES]
        k_r = pltpu.roll(k_t, SWA_HEAD_DIM, 1)
        v_r = pltpu.roll(v_t, SWA_HEAD_DIM, 1)
        for pair in range(SWA_GROUP // 2):
            q_t = q_all[:, (kh * 2 + pair) * LANES:(kh * 2 + pair + 1) * LANES]
            out = None
            for half in range(2):
                keep = lo if half == 0 else jnp.logical_not(lo)
                kx = k_t if (kh % 2) == half else k_r
                vx = v_t if (kh % 2) == half else v_r
                s = _dot_nt(jnp.where(keep, q_t, zero), kx)
                s = jnp.where(mask, s, NEG_INF)
                sink = sinks_ref[kh * SWA_GROUP + 2 * pair + half]
                m = jnp.maximum(jnp.max(s, axis=-1, keepdims=True), sink)
                p = jnp.exp(s - m)
                l = jnp.sum(p, axis=-1, keepdims=True) + jnp.exp(sink - m)
                o = _dot((p / l).astype(BF16), jnp.where(keep, vx, zero))
                out = o if out is None else out + o
            o_ref[:, (kh * 2 + pair) * LANES:(kh * 2 + pair + 1) * LANES] = out.astype(o_ref.dtype)


def _swa_prompt_kernel(sinks_ref, q_ref, kc_ref, kp_ref, vc_ref, vp_ref, o_ref, *, tq):
    i = pl.program_id(1)
    k_all = jnp.concatenate([kp_ref[...], kc_ref[...]], axis=0)
    v_all = jnp.concatenate([vp_ref[...], vc_ref[...]], axis=0)
    tk = tq + WINDOW
    q_pos = i * tq + lax.broadcasted_iota(I32, (tq, tk), 0)
    k_pos = i * tq - WINDOW + lax.broadcasted_iota(I32, (tq, tk), 1)
    mask = (k_pos >= 0) & (k_pos <= q_pos) & (q_pos - k_pos < WINDOW)
    _swa_core(q_ref[...], k_all, v_all, mask, sinks_ref, o_ref)


def _swa_prompt_attention(q, kb, vb, sinks, batch, seq):
    tq = ATTN_Q_BLOCK
    r = tq // WINDOW
    q3 = q.reshape(batch, seq, SWA_NQ)
    k3 = kb.reshape(batch, seq, SWA_NK)
    v3 = vb.reshape(batch, seq, SWA_NK)
    cur = pl.BlockSpec((None, tq, SWA_NK), lambda b, i: (b, i, 0))
    prev = pl.BlockSpec((None, WINDOW, SWA_NK), lambda b, i: (b, jnp.maximum(i * r - 1, 0), 0))
    out = pl.pallas_call(
        functools.partial(_swa_prompt_kernel, tq=tq),
        grid=(batch, seq // tq),
        in_specs=[pl.BlockSpec(memory_space=pltpu.SMEM),
                  pl.BlockSpec((None, tq, SWA_NQ), lambda b, i: (b, i, 0)), cur, prev, cur, prev],
        out_specs=pl.BlockSpec((None, tq, SWA_NQ), lambda b, i: (b, i, 0)),
        out_shape=jax.ShapeDtypeStruct((batch, seq, SWA_NQ), BF16),
        compiler_params=_cparams(("arbitrary", "arbitrary")),
        name="swa_prompt_attention",
    )(sinks, q3, k3, k3, v3, v3)
    return out.reshape(batch * seq, SWA_NQ)


def _swa_sample_kernel(sinks_ref, q_ref, kbuf_ref, vbuf_ref, kn_ref, vn_ref, o_ref, o_sc, *, n_buf, dec_seq):
    pad = jnp.zeros((BF16_ROWS - dec_seq, SWA_NK), F32)
    k_all = jnp.concatenate([kbuf_ref[...], kn_ref[...], pad], axis=0).astype(BF16)
    v_all = jnp.concatenate([vbuf_ref[...], vn_ref[...], pad], axis=0).astype(BF16)
    tk = n_buf + BF16_ROWS
    t = lax.broadcasted_iota(I32, (BF16_ROWS, tk), 0)
    c = lax.broadcasted_iota(I32, (BF16_ROWS, tk), 1)
    in_buf = c < n_buf
    mask = ((in_buf & ((n_buf + t - c) < WINDOW))
            | (jnp.logical_not(in_buf) & ((c - n_buf) <= jnp.minimum(t, dec_seq - 1))))
    q = jnp.concatenate([q_ref[...].astype(F32), jnp.zeros((BF16_ROWS - dec_seq, SWA_NQ), F32)], axis=0)
    _swa_core(q.astype(BF16), k_all, v_all, mask, sinks_ref, o_sc)
    o_ref[...] = o_sc[:dec_seq, :].astype(o_ref.dtype)


def _swa_sample_attention(q, kn, vn, buf_k, buf_v, sinks, db, dec_seq):
    n_buf = buf_k.shape[1]
    spec = lambda rws, w: pl.BlockSpec((None, rws, w), lambda b: (b, 0, 0))
    out = pl.pallas_call(
        functools.partial(_swa_sample_kernel, n_buf=n_buf, dec_seq=dec_seq),
        grid=(db,),
        in_specs=[pl.BlockSpec(memory_space=pltpu.SMEM), spec(dec_seq, SWA_NQ), spec(n_buf, SWA_NK),
                  spec(n_buf, SWA_NK), spec(dec_seq, SWA_NK), spec(dec_seq, SWA_NK)],
        out_specs=spec(dec_seq, SWA_NQ),
        out_shape=jax.ShapeDtypeStruct((db, dec_seq, SWA_NQ), BF16),
        scratch_shapes=[pltpu.VMEM((BF16_ROWS, SWA_NQ), F32)],
        compiler_params=_cparams(("arbitrary",)),
        name="swa_sample_attention",
    )(sinks, q.reshape(db, dec_seq, SWA_NQ), buf_k, buf_v,
      kn.reshape(db, dec_seq, SWA_NK), vn.reshape(db, dec_seq, SWA_NK))
    return out.reshape(db * dec_seq, SWA_NQ)


def _peer_candidate_tables():
    k = PEER_TOPK
    flat, valid = [], []
    for i in range(8):
        width = 16 if i == 0 else 8
        for j in range(width):
            flat.append(i * k + j)
            valid.append((i + 1) * (j + 1) <= k)
    for i in range(8, 16):
        flat.append(i * k)
        valid.append(True)
    flat = np.asarray(flat, np.int32)[:, None] * np.ones((1, LANES), np.int32)
    pen = np.where(np.asarray(valid)[:, None], 0.0, -np.inf).astype(np.float32) * np.ones((1, LANES), np.float32)
    return flat, np.where(np.isnan(pen), 0.0, pen).astype(np.float32)


def _extract_top(s, row_id, k, payload=None):
    big = jnp.int32(2 ** 30)
    vals, ids = [], []
    for _ in range(k):
        m = jnp.max(s, axis=0, keepdims=True)
        sel = jnp.min(jnp.where(s == m, row_id, big), axis=0, keepdims=True)
        hit = row_id == sel
        if payload is not None:
            ids.append(jnp.max(jnp.where(hit, payload, -1), axis=0, keepdims=True))
        else:
            ids.append(sel)
        vals.append(m)
        s = jnp.where(hit, -jnp.inf, s)
    return jnp.concatenate(vals, axis=0), jnp.concatenate(ids, axis=0)


def _peer_route_kernel(x_ref, g_ref, sh_ref, sc_ref, wq_ref, sk_ref, flat_ref, pen_ref,
                       h_ref, e_ref, gate_ref, qs_sc, sv_sc, si_sc, pe_sc, pg_sc, *, tb):
    k = PEER_TOPK
    h = _modulate(x_ref[...], g_ref[...], sh_ref[...], sc_ref[...]).astype(BF16)
    h_ref[...] = h
    q = _dot(h, wq_ref[...])
    n_groups = 2 * PEER_HEADS
    for grp in range(n_groups):
        qs_sc[grp] = q[:, grp * PEER_HALF:(grp + 1) * PEER_HALF].astype(BF16)
    key_id = lax.broadcasted_iota(I32, (PEER_N_KEYS, ROUTE_SUB), 0)
    flat = flat_ref[...]
    pen = pen_ref[...]

    for sub in range(tb // ROUTE_SUB):
        rows = pl.ds(sub * ROUTE_SUB, ROUTE_SUB)

        def first_level(grp, carry, rows=rows):
            s = _dot_nt(sk_ref[grp], qs_sc[grp, rows, :])
            vals, ids = _extract_top(s, key_id, k)
            sv_sc[grp] = vals
            si_sc[grp] = ids
            return carry

        lax.fori_loop(0, n_groups, first_level, 0)

        def second_level(hd, carry, rows=rows):
            sa, sb = sv_sc[2 * hd], sv_sc[2 * hd + 1]
            ia, ib = si_sc[2 * hd], si_sc[2 * hd + 1]
            cs, ce = [], []
            for i in range(8):
                width = 16 if i == 0 else 8
                cs.append(sa[i:i + 1] + sb[:width])
                ce.append(ia[i:i + 1] * PEER_N_KEYS + ib[:width])
            cs.append(sa[8:] + sb[:1])
            ce.append(ia[8:] * PEER_N_KEYS + ib[:1])
            cand = jnp.concatenate(cs, axis=0) + pen
            cand_e = jnp.concatenate(ce, axis=0)
            best, experts = _extract_top(cand, flat, k, payload=cand_e)
            ex = jnp.exp(best - best[:1])
            gates = ex / jnp.sum(ex, axis=0, keepdims=True)
            pick = pl.ds(pl.multiple_of(hd * k, k), k)
            pe_sc[pick, :] = experts
            pg_sc[pick, :] = gates
            return carry

        lax.fori_loop(0, PEER_HEADS, second_level, 0)
        e_ref[rows, :] = pe_sc[...].T
        gate_ref[rows, :] = pg_sc[...].T


def _peer_route(rows, x, ln_g, mod, w):
    n, tb = rows.n, rows.tb
    flat, pen = _peer_candidate_tables()
    full = _Rows.full
    n_groups = 2 * PEER_HEADS
    return pl.pallas_call(
        functools.partial(_peer_route_kernel, tb=tb),
        grid=(rows.grid,),
        in_specs=[rows.rows(D_MODEL), full((1, D_MODEL)), rows.mod(3), rows.mod(4),
                  full(w["w_q"].shape), full(w["sub_keys"].shape), full(flat.shape), full(pen.shape)],
        out_specs=[rows.rows(D_MODEL), rows.rows(PEER_PICKS), rows.rows(PEER_PICKS)],
        out_shape=[jax.ShapeDtypeStruct((n, D_MODEL), BF16),
                   jax.ShapeDtypeStruct((n, PEER_PICKS), I32),
                   jax.ShapeDtypeStruct((n, PEER_PICKS), F32)],
        scratch_shapes=[pltpu.VMEM((n_groups, tb, PEER_HALF), BF16),
                        pltpu.VMEM((n_groups, PEER_TOPK, ROUTE_SUB), F32),
                        pltpu.VMEM((n_groups, PEER_TOPK, ROUTE_SUB), I32),
                        pltpu.VMEM((PEER_PICKS, ROUTE_SUB), I32),
                        pltpu.VMEM((PEER_PICKS, ROUTE_SUB), F32)],
        compiler_params=_cparams(("arbitrary",)),
        name="peer_route",
    )(x, ln_g, mod, mod, w["w_q"], w["sub_keys"], jnp.asarray(flat), jnp.asarray(pen))


def _peer_expert_kernel(h_ref, e_ref, gate_ref, u_ref, v_ref, x_ref, g2_ref, o_ref,
                        w_sc, p_sc, acc_sc, *, tb, n_chunks):
    c = pl.program_id(1)
    pitch = GATE_TILE_PITCH

    @pl.when(c == 0)
    def _():
        acc_sc[...] = jnp.zeros(acc_sc.shape, F32)
        key_id = lax.broadcasted_iota(I32, (PEER_N_KEYS, PEER_PICKS), 0)

        def token(t, carry):
            e = e_ref[pl.ds(t, 1), :]
            g = gate_ref[pl.ds(t, 1), :]
            g_hi = g.astype(BF16).astype(F32)
            g_lo = (g - g_hi).astype(BF16).astype(F32)
            hit_a = key_id == (e >> 7)
            hit_b = key_id == (e & (PEER_N_KEYS - 1))
            one_a = jnp.where(hit_a, 1.0, 0.0).astype(BF16)
            lhs = jnp.concatenate([one_a, one_a], axis=1)
            rhs = jnp.concatenate([jnp.where(hit_b, g_hi, 0.0).astype(BF16),
                                   jnp.where(hit_b, g_lo, 0.0).astype(BF16)], axis=1)
            w_sc[pl.ds(pl.multiple_of(t * pitch, SUBLANES), PEER_N_KEYS), :] = _dot_nt(lhs, rhs)
            return carry

        lax.fori_loop(0, tb, token, 0)

    z = _dot_nt(h_ref[...], u_ref[...])
    for al in range(EXPERT_CHUNK_A):
        sl = slice(al * PEER_N_KEYS, (al + 1) * PEER_N_KEYS)
        w = w_sc[pl.ds(c * EXPERT_CHUNK_A + al, tb, stride=pitch), :]
        p_sc[:, sl] = (w * _gelu(z[:, sl])).astype(BF16)
    acc_sc[...] += _dot(p_sc[...], v_ref[...])

    @pl.when(c == n_chunks - 1)
    def _():
        o_ref[...] = x_ref[...] + g2_ref[...] * acc_sc[...]


def _peer_experts(rows, h, experts, gates, u, v, x, mod):
    n, tb = rows.n, rows.tb
    ce = EXPERT_CHUNK_A * PEER_N_KEYS
    n_chunks = u.shape[0] // ce
    bps = rows.bps
    if bps is None:
        gate_spec = pl.BlockSpec((tb, D_MODEL), lambda i, c: (i, 5))
    else:
        gate_spec = pl.BlockSpec((None, 1, D_MODEL), lambda i, c: (i // bps, 0, 5))
    tok = lambda w: pl.BlockSpec((tb, w), lambda i, c: (i, 0))
    return pl.pallas_call(
        functools.partial(_peer_expert_kernel, tb=tb, n_chunks=n_chunks),
        grid=(rows.grid, n_chunks),
        in_specs=[tok(D_MODEL), tok(PEER_PICKS), tok(PEER_PICKS),
                  pl.BlockSpec((ce, D_MODEL), lambda i, c: (c, 0)),
                  pl.BlockSpec((ce, D_MODEL), lambda i, c: (c, 0)),
                  tok(D_MODEL), gate_spec],
        out_specs=tok(D_MODEL),
        out_shape=jax.ShapeDtypeStruct((n, D_MODEL), F32),
        scratch_shapes=[pltpu.VMEM((tb * GATE_TILE_PITCH, LANES), F32),
                        pltpu.VMEM((tb, ce), BF16),
                        pltpu.VMEM((tb, D_MODEL), F32)],
        compiler_params=_cparams(("arbitrary", "arbitrary")),
        name="peer_experts",
    )(h, experts, gates, u, v, x, mod)


def _final_norm_kernel(x_ref, g_ref, o_ref):
    o_ref[...] = _rms(x_ref[...], g_ref[...])


def _final_norm(rows, x, g):
    return pl.pallas_call(
        _final_norm_kernel,
        grid=(rows.grid,),
        in_specs=[rows.rows(D_MODEL), _Rows.full((1, D_MODEL))],
        out_specs=rows.rows(D_MODEL),
        out_shape=jax.ShapeDtypeStruct((rows.n, D_MODEL), F32),
        compiler_params=_cparams(("arbitrary",)),
        name="final_norm",
    )(x, g)


def _pair_swap(w, half):
    return jnp.concatenate([-w[..., half:2 * half], w[..., :half]], axis=-1)


def _rope_cos_sin(pos, rot_dim):
    half = rot_dim // 2
    inv_freq = ROPE_THETA ** (-jnp.arange(half, dtype=F32) * 2.0 / rot_dim)
    ang = pos.astype(F32)[:, None] * inv_freq[None, :]
    return jnp.cos(ang), jnp.sin(ang)


def _mla_tables(pos):
    cos, sin = _rope_cos_sin(pos, MLA_ROPE)
    n = pos.shape[0]
    ones = jnp.ones((n, MLA_NOPE), F32)
    zeros_n = jnp.zeros((n, MLA_NOPE), F32)
    pad = jnp.zeros((n, MLA_HEAD_PAD - MLA_NOPE - MLA_ROPE), F32)
    return {
        "c32": jnp.concatenate([cos, cos], axis=1),
        "s32": jnp.concatenate([sin, sin], axis=1),
        "c128": jnp.concatenate([ones, cos, cos, pad], axis=1),
        "s128": jnp.concatenate([zeros_n, sin, sin, pad], axis=1),
    }


def _swa_tables(pos):
    cos, sin = _rope_cos_sin(pos, SWA_ROT)
    n = pos.shape[0]
    rest = SWA_HEAD_DIM - SWA_ROT
    c = jnp.concatenate([cos, cos, jnp.ones((n, rest), F32)], axis=1)
    s = jnp.concatenate([sin, sin, jnp.zeros((n, rest), F32)], axis=1)
    reps = LANES // SWA_HEAD_DIM
    return {"cos": jnp.tile(c, (1, reps)), "sin": jnp.tile(s, (1, reps))}


def _mla_weights(w_in, q_norm, kv_norm, w_uq, w_uk, w_uv, w_o):
    d = w_in.shape[0]
    half = MLA_ROPE // 2
    kpe_cols = w_in[:, MLA_Q_LORA + MLA_KV_LORA:]
    w_in_ext = jnp.zeros((d, MLA_IN_EXT), F32)
    w_in_ext = w_in_ext.at[:, :MLA_KPE_COL + MLA_ROPE].set(w_in)
    w_in_ext = w_in_ext.at[:, MLA_KPE_SWAP_COL:MLA_KPE_SWAP_COL + MLA_ROPE].set(_pair_swap(kpe_cols, half))
    uq = w_uq.reshape(MLA_Q_LORA, MLA_HEADS, MLA_NOPE + MLA_ROPE)
    uq_pad = jnp.zeros((MLA_Q_LORA, MLA_HEADS, MLA_HEAD_PAD), F32).at[:, :, :MLA_NOPE + MLA_ROPE].set(uq)
    uq_swap = jnp.zeros((MLA_Q_LORA, MLA_HEADS, MLA_HEAD_PAD), F32)
    uq_swap = uq_swap.at[:, :, MLA_NOPE:MLA_NOPE + MLA_ROPE].set(_pair_swap(uq[:, :, MLA_NOPE:], half))
    uk_pad = jnp.zeros((MLA_KV_LORA, MLA_HEADS, MLA_HEAD_PAD), F32).at[:, :, :MLA_NOPE].set(w_uk)
    e_pe = jnp.zeros((MLA_ROPE, MLA_HEADS, MLA_HEAD_PAD), F32)
    e_pe = e_pe.at[jnp.arange(MLA_ROPE), :, MLA_NOPE + jnp.arange(MLA_ROPE)].set(1.0)
    ukt = jnp.zeros((MLA_HEADS, MLA_HEAD_PAD, MLA_KV_LORA), F32)
    ukt = ukt.at[:, :MLA_NOPE, :].set(jnp.transpose(w_uk, (1, 2, 0)))
    eye = jnp.eye(MLA_HEADS, dtype=F32)
    w_bd = jnp.einsum("lhd,hg->hlgd", w_uv, eye).reshape(MLA_HEADS * MLA_KV_LORA, MLA_HEADS * MLA_V)
    hp = MLA_HEADS * MLA_HEAD_PAD
    return {
        "w_in": w_in_ext.astype(BF16),
        "q_norm": q_norm.reshape(1, -1), "kv_norm": kv_norm.reshape(1, -1),
        "w_uq": uq_pad.reshape(MLA_Q_LORA, hp).astype(BF16),
        "w_uq_swap": uq_swap.reshape(MLA_Q_LORA, hp).astype(BF16),
        "w_uk": uk_pad.reshape(MLA_KV_LORA, hp).astype(BF16),
        "e_pe": e_pe.reshape(MLA_ROPE, hp).astype(BF16),
        "w_uv": w_uv.reshape(MLA_KV_LORA, MLA_HEADS * MLA_V).astype(BF16),
        "w_ukt": ukt.astype(BF16),
        "w_bd": w_bd.astype(BF16),
        "w_o": w_o.astype(BF16),
    }


def _swa_weights(w_qkv, b_qkv, w_o):
    half = SWA_ROT // 2
    n_rot_heads = SWA_HEADS + SWA_KV_HEADS

    def swap_cols(w):
        lead = w.shape[:-1]
        wh = w[..., :SWA_NQ + SWA_NK].reshape(lead + (n_rot_heads, SWA_HEAD_DIM))
        sw = jnp.concatenate([_pair_swap(wh[..., :SWA_ROT], half),
                              jnp.zeros(lead + (n_rot_heads, SWA_HEAD_DIM - SWA_ROT), F32)], axis=-1)
        return sw.reshape(lead + (SWA_NQ + SWA_NK,))

    w_ext = jnp.concatenate([w_qkv, swap_cols(w_qkv)], axis=-1)
    b_ext = jnp.concatenate([b_qkv, swap_cols(b_qkv)], axis=-1)
    return {"w_qkv": w_ext.astype(BF16), "b_qkv": b_ext.reshape(1, -1), "w_o": w_o.astype(BF16)}


def kernel(x_prompt, x_sample, cache_mla_ckv, cache_mla_kpe, cache_swa_k, cache_swa_v, page_table,
           c_prompt, c_sample, ln1_g, ln2_g, w_mod, b_mod,
           mla_w_in, mla_q_norm, mla_kv_norm, mla_w_uq, mla_w_uk, mla_w_uv, mla_w_o,
           swa_w_qkv, swa_b_qkv, swa_sinks, swa_w_o,
           peer_w_q, peer_sub_keys, peer_u, peer_v, final_g):
    batch, seq, d = x_prompt.shape
    db, dec_seq, _ = x_sample.shape
    depth = w_mod.shape[0]
    past_len = page_table.shape[1] * PAGE_SIZE
    n_p, n_s = batch * seq, db * dec_seq

    rows_p = _Rows(n_p, min(ROW_BLOCK, seq), seq_len=seq)
    rows_s = _Rows(n_s, n_s)
    route_p = _Rows(n_p, ROUTE_BLOCK, seq_len=seq)
    route_s = _Rows(n_s, min(ROUTE_BLOCK, n_s))
    expert_p = _Rows(n_p, EXPERT_ROW_BLOCK, seq_len=seq)
    expert_s = _Rows(n_s, min(EXPERT_ROW_BLOCK, n_s))

    pos_p = jnp.arange(seq)
    pos_s = jnp.tile(past_len + jnp.arange(dec_seq), db)
    mla_tab_p, mla_tab_s = _mla_tables(pos_p), _mla_tables(pos_s)
    swa_tab_p, swa_tab_s = _swa_tables(pos_p), _swa_tables(pos_s)

    m_all = _modulation_all(jnp.concatenate([c_prompt, c_sample], axis=0), w_mod, b_mod)

    x_p = x_prompt.reshape(n_p, d)
    x_s = x_sample.reshape(n_s, d)
    ckv_p, kpe_p, ckv_s, kpe_s = [], [], [], []
    swk_p, swv_p, swk_s, swv_s = [], [], [], []
    n_buf = cache_swa_k.shape[2]

    for i in range(depth):
        j = i // 2
        mod_p = m_all[i, :batch].reshape(batch, 1, 6 * d)
        mod_s = jnp.repeat(m_all[i, batch:], dec_seq, axis=0)
        g1 = ln1_g[i].reshape(1, d)
        g2 = ln2_g[i].reshape(1, d)
        if i % 2 == 0:
            w = _mla_weights(mla_w_in[j], mla_q_norm[j], mla_kv_norm[j], mla_w_uq[j], mla_w_uk[j],
                             mla_w_uv[j], mla_w_o[j])
            ckv, kpe, q, k, v = _mla_project(rows_p, x_p, g1, mod_p, mla_tab_p, w)
            o = _mla_attention(q, k, v, batch, seq)
            x_p = _out_proj(rows_p, o, w["w_o"], x_p, mod_p)
            ckv_p.append(ckv.reshape(batch, seq, MLA_KV_LORA))
            kpe_p.append(kpe.reshape(batch, seq, MLA_ROPE))

            ckv, kpe, q, _, _ = _mla_project(rows_s, x_s, g1, mod_s, mla_tab_s, w)
            qlat = _mla_absorb_queries(q, w["w_ukt"]).reshape(db, dec_seq * MLA_HEADS, MLA_KV_LORA)
            qpe = q.reshape(n_s, MLA_HEADS, MLA_HEAD_PAD)[:, :, MLA_NOPE:MLA_NOPE + MLA_ROPE]
            qpe = qpe.reshape(db, dec_seq * MLA_HEADS, MLA_ROPE)
            ckv3 = ckv.reshape(db, dec_seq, MLA_KV_LORA)
            kpe3 = kpe.reshape(db, dec_seq, MLA_ROPE)
            olat = _mla_paged_attention(qlat, qpe, ckv3, kpe3, cache_mla_ckv[j], cache_mla_kpe[j], page_table)
            olat = olat.reshape(n_s, MLA_HEADS * MLA_KV_LORA)
            x_s = _mla_sample_out(rows_s, olat, w["w_bd"], w["w_o"], x_s, mod_s)
            ckv_s.append(ckv3)
            kpe_s.append(kpe3)
        else:
            w = _swa_weights(swa_w_qkv[j], swa_b_qkv[j], swa_w_o[j])
            q, kb, vb, k, v = _swa_project(rows_p, x_p, g1, mod_p, swa_tab_p, w)
            o = _swa_prompt_attention(q, kb, vb, swa_sinks[j], batch, seq)
            x_p = _out_proj(rows_p, o, w["w_o"], x_p, mod_p)
            nb = min(WINDOW, seq)
            swk_p.append(k.reshape(batch, seq, SWA_KV_HEADS, SWA_HEAD_DIM)[:, seq - nb:])
            swv_p.append(v.reshape(batch, seq, SWA_KV_HEADS, SWA_HEAD_DIM)[:, seq - nb:])

            q, kb, vb, k, v = _swa_project(rows_s, x_s, g1, mod_s, swa_tab_s, w)
            buf_k = cache_swa_k[j].reshape(db, n_buf, SWA_NK)
            buf_v = cache_swa_v[j].reshape(db, n_buf, SWA_NK)
            o = _swa_sample_attention(q, k, v, buf_k, buf_v, swa_sinks[j], db, dec_seq)
            x_s = _out_proj(rows_s, o, w["w_o"], x_s, mod_s)
            k_all = jnp.concatenate([buf_k, k.reshape(db, dec_seq, SWA_NK)], axis=1)[:, -n_buf:]
            v_all = jnp.concatenate([buf_v, v.reshape(db, dec_seq, SWA_NK)], axis=1)[:, -n_buf:]
            swk_s.append(k_all.reshape(db, n_buf, SWA_KV_HEADS, SWA_HEAD_DIM))
            swv_s.append(v_all.reshape(db, n_buf, SWA_KV_HEADS, SWA_HEAD_DIM))

        pw = {"w_q": peer_w_q[i].astype(BF16),
              "sub_keys": peer_sub_keys[i].reshape(2 * PEER_HEADS, PEER_N_KEYS, PEER_HALF).astype(BF16)}
        u_b = peer_u[i].astype(BF16)
        v_b = peer_v[i].astype(BF16)
        h, experts, gates = _peer_route(route_p, x_p, g2, mod_p, pw)
        x_p = _peer_experts(expert_p, h, experts, gates, u_b, v_b, x_p, mod_p)
        h, experts, gates = _peer_route(route_s, x_s, g2, mod_s, pw)
        x_s = _peer_experts(expert_s, h, experts, gates, u_b, v_b, x_s, mod_s)

    fg = final_g.reshape(1, d)
    y_p = _final_norm(rows_p, x_p, fg).reshape(batch, seq, d)
    y_s = _final_norm(rows_s, x_s, fg).reshape(db, dec_seq, d)
    return (y_p, y_s,
            jnp.stack(ckv_p), jnp.stack(kpe_p), jnp.stack(ckv_s), jnp.stack(kpe_s),
            jnp.stack(swk_p), jnp.stack(swv_p), jnp.stack(swk_s), jnp.stack(swv_s))
```

```python
import functools

import numpy as np
import jax
import jax.numpy as jnp
from jax import lax
from jax.experimental import pallas as pl
from jax.experimental.pallas import tpu as pltpu

F32 = jnp.float32
BF16 = jnp.bfloat16
I32 = jnp.int32

D_MODEL = 1024
PAGE_SIZE = 128
ROPE_THETA = 500000.0
NORM_EPS = 1e-6
NEG_INF = -1e30

MLA_HEADS = 16
MLA_Q_LORA = 384
MLA_KV_LORA = 256
MLA_NOPE = 64
MLA_ROPE = 32
MLA_V = 64
MLA_SCALE = (MLA_NOPE + MLA_ROPE) ** -0.5

SWA_HEADS = 16
SWA_KV_HEADS = 4
SWA_GROUP = SWA_HEADS // SWA_KV_HEADS
SWA_HEAD_DIM = D_MODEL // SWA_HEADS
SWA_ROT = SWA_HEAD_DIM // 4
SWA_SCALE = SWA_HEAD_DIM ** -0.5
WINDOW = 128

PEER_HEADS = 8
PEER_N_KEYS = 128
PEER_TOPK = 16
PEER_HALF = 128
PEER_PICKS = PEER_HEADS * PEER_TOPK

LANES = 128
SUBLANES = 8
BF16_ROWS = 16
VMEM_LIMIT_BYTES = 56 * 1024 * 1024

ROW_BLOCK = 512
EXPERT_ROW_BLOCK = 512
TOKEN_UNROLL = 8
ROUTE_BLOCK = 256
ROUTE_UNROLL = 2
ATTN_Q_BLOCK = 256
ATTN_K_BLOCK = 256
PAGES_PER_STEP = 16
EXPERT_CHUNK_A = 4
GATE_TILE_PITCH = 136
MOD_COL_BLOCK = 1536


def _cparams(sem):
    return pltpu.CompilerParams(dimension_semantics=sem, vmem_limit_bytes=VMEM_LIMIT_BYTES)


def _dot(a, b):
    return jnp.dot(a, b, preferred_element_type=F32)


def _dot_nt(a, b):
    return lax.dot_general(a, b, (((1,), (1,)), ((), ())), preferred_element_type=F32)


def _rms(x, g):
    return x * lax.rsqrt(jnp.mean(x * x, axis=-1, keepdims=True) + NORM_EPS) * g


def _gelu(x):
    return 0.5 * x * (1.0 + lax.erf(x * np.float32(2.0 ** -0.5)))


def _modulate(x, g, shift, scale):
    return _rms(x, g) * (1.0 + scale) + shift


class _Rows:
    def __init__(self, n, tb, seq_len=None):
        assert n % tb == 0
        self.n, self.tb, self.grid = n, tb, n // tb
        self.bps = None
        if seq_len is not None:
            assert seq_len % tb == 0
            self.bps = seq_len // tb

    def rows(self, width):
        return pl.BlockSpec((self.tb, width), lambda i: (i, 0))

    def mod(self, k):
        if self.bps is None:
            return pl.BlockSpec((self.tb, D_MODEL), lambda i: (i, k))
        bps = self.bps
        return pl.BlockSpec((None, 1, D_MODEL), lambda i: (i // bps, 0, k))

    def pos(self, width):
        if self.bps is None:
            return pl.BlockSpec((self.tb, width), lambda i: (i, 0))
        bps = self.bps
        return pl.BlockSpec((self.tb, width), lambda i: (i % bps, 0))

    @staticmethod
    def full(shape):
        nd = len(shape)
        return pl.BlockSpec(shape, lambda i: (0,) * nd)


def _mod_kernel(c_ref, w_ref, b_ref, o_ref):
    c = c_ref[...]
    a = (c * jax.nn.sigmoid(c)).astype(BF16)
    o_ref[...] = _dot(a, w_ref[...].astype(BF16)) + b_ref[...]


def _modulation_all(c_all, w_mod, b_mod):
    depth, d, n6 = w_mod.shape
    nc = c_all.shape[0]
    nb = n6 // MOD_COL_BLOCK
    return pl.pallas_call(
        _mod_kernel,
        grid=(depth, nb),
        in_specs=[
            pl.BlockSpec((nc, d), lambda l, j: (0, 0)),
            pl.BlockSpec((None, d, MOD_COL_BLOCK), lambda l, j: (l, 0, j)),
            pl.BlockSpec((None, 1, MOD_COL_BLOCK), lambda l, j: (l, 0, j)),
        ],
        out_specs=pl.BlockSpec((None, nc, MOD_COL_BLOCK), lambda l, j: (l, 0, j)),
        out_shape=jax.ShapeDtypeStruct((depth, nc, n6), F32),
        compiler_params=_cparams(("arbitrary", "arbitrary")),
        name="adaln_modulation",
    )(c_all, w_mod, b_mod.reshape(depth, 1, n6))


MLA_IN_EXT = 896
MLA_KPE_COL = 640
MLA_KPE_SWAP_COL = 768
MLA_HEAD_PAD = LANES


def _mla_proj_kernel(x_ref, g_ref, sh_ref, sc_ref, c32_ref, s32_ref, c128_ref, s128_ref,
                     win_ref, qn_ref, kvn_ref, wuq_ref, wuqs_ref, wuk_ref, epe_ref, wuv_ref,
                     ckv_ref, kpe_ref, q_ref, k_ref, v_ref):
    h = _modulate(x_ref[...], g_ref[...], sh_ref[...], sc_ref[...]).astype(BF16)
    z = _dot(h, win_ref[...])
    cq = _rms(z[:, :MLA_Q_LORA], qn_ref[...]).astype(BF16)
    ckv = _rms(z[:, MLA_Q_LORA:MLA_Q_LORA + MLA_KV_LORA], kvn_ref[...])
    kpe = (z[:, MLA_KPE_COL:MLA_KPE_COL + MLA_ROPE] * c32_ref[...]
           + z[:, MLA_KPE_SWAP_COL:MLA_KPE_SWAP_COL + MLA_ROPE] * s32_ref[...])
    ckv_ref[...] = ckv
    kpe_ref[...] = kpe
    ckv_b = ckv.astype(BF16)
    q = _dot(cq, wuq_ref[...])
    qs = _dot(cq, wuqs_ref[...])
    cos = c128_ref[...]
    sin = s128_ref[...]
    for hd in range(MLA_HEADS):
        sl = slice(hd * MLA_HEAD_PAD, (hd + 1) * MLA_HEAD_PAD)
        q_ref[:, sl] = (q[:, sl] * cos + qs[:, sl] * sin).astype(BF16)
    k_ref[...] = (_dot(ckv_b, wuk_ref[...]) + _dot(kpe.astype(BF16), epe_ref[...])).astype(BF16)
    v_ref[...] = _dot(ckv_b, wuv_ref[...]).astype(BF16)


def _mla_project(rows, x, ln_g, mod, tabs, w):
    n = rows.n
    hp = MLA_HEADS * MLA_HEAD_PAD
    full = _Rows.full
    return pl.pallas_call(
        _mla_proj_kernel,
        grid=(rows.grid,),
        in_specs=[
            rows.rows(D_MODEL), full((1, D_MODEL)), rows.mod(0), rows.mod(1),
            rows.pos(MLA_ROPE), rows.pos(MLA_ROPE), rows.pos(MLA_HEAD_PAD), rows.pos(MLA_HEAD_PAD),
            full((D_MODEL, MLA_IN_EXT)), full((1, MLA_Q_LORA)), full((1, MLA_KV_LORA)),
            full((MLA_Q_LORA, hp)), full((MLA_Q_LORA, hp)), full((MLA_KV_LORA, hp)),
            full((MLA_ROPE, hp)), full((MLA_KV_LORA, MLA_HEADS * MLA_V)),
        ],
        out_specs=[rows.rows(MLA_KV_LORA), rows.rows(MLA_ROPE), rows.rows(hp), rows.rows(hp),
                   rows.rows(MLA_HEADS * MLA_V)],
        out_shape=[
            jax.ShapeDtypeStruct((n, MLA_KV_LORA), F32),
            jax.ShapeDtypeStruct((n, MLA_ROPE), F32),
            jax.ShapeDtypeStruct((n, hp), BF16),
            jax.ShapeDtypeStruct((n, hp), BF16),
            jax.ShapeDtypeStruct((n, MLA_HEADS * MLA_V), BF16),
        ],
        compiler_params=_cparams(("arbitrary",)),
        name="mla_project",
    )(x, ln_g, mod, mod, tabs["c32"], tabs["s32"], tabs["c128"], tabs["s128"],
      w["w_in"], w["q_norm"], w["kv_norm"], w["w_uq"], w["w_uq_swap"], w["w_uk"], w["e_pe"], w["w_uv"])


def _mla_attn_kernel(q_ref, k_ref, v_ref, o_ref, *, tq, tk):
    assert tq == tk
    qi = pl.program_id(2)
    lane = lax.broadcasted_iota(I32, (tk, LANES), 1)
    causal = lax.broadcasted_iota(I32, (tq, tk), 1) <= lax.broadcasted_iota(I32, (tq, tk), 0)
    qs = [q_ref[:, hh * LANES:(hh + 1) * LANES] for hh in range(2)]
    v_keep = [(lane // MLA_V) == hh for hh in range(2)]

    def step(j, carry, masked):
        start = pl.multiple_of(j * tk, tk)
        vb = v_ref[pl.ds(start, tk), :]
        new = []
        for hh in range(2):
            m, l, acc = carry[hh]
            kb = k_ref[pl.ds(start, tk), hh * LANES:(hh + 1) * LANES]
            s = _dot_nt(qs[hh], kb) * MLA_SCALE
            if masked:
                s = jnp.where(causal, s, NEG_INF)
            m_new = jnp.maximum(m, jnp.max(s, axis=-1, keepdims=True))
            alpha = jnp.exp(m - m_new)
            p = jnp.exp(s - m_new)
            l = alpha * l + jnp.sum(p, axis=-1, keepdims=True)
            acc = alpha * acc + _dot(p.astype(BF16), jnp.where(v_keep[hh], vb, jnp.zeros_like(vb)))
            new.append((m_new, l, acc))
        return tuple(new)

    head0 = (jnp.full((tq, 1), NEG_INF, F32), jnp.zeros((tq, 1), F32), jnp.zeros((tq, LANES), F32))
    carry = lax.fori_loop(0, qi, lambda j, c: step(j, c, False), (head0, head0))
    (_, l0, acc0), (_, l1, acc1) = step(qi, carry, True)
    o_ref[...] = (acc0 / l0 + acc1 / l1).astype(BF16)


def _mla_attention(q, k, v, batch, seq):
    tq, tk = ATTN_Q_BLOCK, ATTN_K_BLOCK
    nq = seq // tq
    q3 = q.reshape(batch, seq, -1)
    k3 = k.reshape(batch, seq, -1)
    v3 = v.reshape(batch, seq, -1)
    out = pl.pallas_call(
        functools.partial(_mla_attn_kernel, tq=tq, tk=tk),
        grid=(batch, MLA_HEADS // 2, nq),
        in_specs=[
            pl.BlockSpec((None, tq, 2 * LANES), lambda b, hp, i: (b, i, hp)),
            pl.BlockSpec((None, seq, 2 * LANES), lambda b, hp, i: (b, 0, hp)),
            pl.BlockSpec((None, seq, LANES), lambda b, hp, i: (b, 0, hp)),
        ],
        out_specs=pl.BlockSpec((None, tq, LANES), lambda b, hp, i: (b, i, hp)),
        out_shape=jax.ShapeDtypeStruct((batch, seq, MLA_HEADS * MLA_V), BF16),
        compiler_params=_cparams(("arbitrary", "arbitrary", "arbitrary")),
        name="mla_prompt_attention",
    )(q3, k3, v3)
    return out.reshape(batch * seq, MLA_HEADS * MLA_V)


def _qlat_kernel(q_ref, wk_ref, o_ref):
    for hd in range(MLA_HEADS):
        qh = q_ref[:, hd * MLA_HEAD_PAD:(hd + 1) * MLA_HEAD_PAD]
        o_ref[:, hd * MLA_KV_LORA:(hd + 1) * MLA_KV_LORA] = _dot(qh, wk_ref[hd]).astype(BF16)


def _mla_absorb_queries(q, w_ukt):
    n = q.shape[0]
    return pl.pallas_call(
        _qlat_kernel,
        grid=(1,),
        in_specs=[_Rows.full(q.shape), _Rows.full(w_ukt.shape)],
        out_specs=_Rows.full((n, MLA_HEADS * MLA_KV_LORA)),
        out_shape=jax.ShapeDtypeStruct((n, MLA_HEADS * MLA_KV_LORA), BF16),
        compiler_params=_cparams(("arbitrary",)),
        name="mla_absorb_queries",
    )(q, w_ukt)


def _paged_attn_kernel(pt_ref, qlat_ref, qpe_ref, nckv_ref, nkpe_ref, *rest, n_steps, dec_seq):
    pg = PAGES_PER_STEP
    ckv_refs = rest[:pg]
    kpe_refs = rest[pg:2 * pg]
    o_ref = rest[2 * pg]
    m_sc, l_sc, acc_sc = rest[2 * pg + 1:]
    g = pl.program_id(1)

    @pl.when(g == 0)
    def _():
        m_sc[...] = jnp.full(m_sc.shape, NEG_INF, F32)
        l_sc[...] = jnp.zeros(l_sc.shape, F32)
        acc_sc[...] = jnp.zeros(acc_sc.shape, F32)

    qlat = qlat_ref[...]
    qpe = qpe_ref[...]

    def update(s, vals):
        m = m_sc[...]
        m_new = jnp.maximum(m, jnp.max(s, axis=-1, keepdims=True))
        alpha = jnp.exp(m - m_new)
        p = jnp.exp(s - m_new)
        l_sc[...] = alpha * l_sc[...] + jnp.sum(p, axis=-1, keepdims=True)
        acc_sc[...] = alpha * acc_sc[...] + _dot(p.astype(BF16), vals)
        m_sc[...] = m_new

    for i in range(pg):
        ck = ckv_refs[i][...].astype(BF16)
        kp = kpe_refs[i][...].astype(BF16)
        update((_dot_nt(qlat, ck) + _dot_nt(qpe, kp)) * MLA_SCALE, ck)

    @pl.when(g == n_steps - 1)
    def _():
        pad = BF16_ROWS - dec_seq
        ck = jnp.concatenate([nckv_ref[...], jnp.zeros((pad, MLA_KV_LORA), F32)], axis=0).astype(BF16)
        kp = jnp.concatenate([nkpe_ref[...], jnp.zeros((pad, MLA_ROPE), F32)], axis=0).astype(BF16)
        s = (_dot_nt(qlat, ck) + _dot_nt(qpe, kp)) * MLA_SCALE
        q_t = lax.broadcasted_iota(I32, s.shape, 0) // MLA_HEADS
        k_t = lax.broadcasted_iota(I32, s.shape, 1)
        update(jnp.where(k_t <= q_t, s, NEG_INF), ck)
        o_ref[...] = acc_sc[...] / l_sc[...]


def _mla_paged_attention(qlat, qpe, new_ckv, new_kpe, cache_ckv, cache_kpe, page_table, layer):
    db, rows, _ = qlat.shape
    dec_seq = new_ckv.shape[1]
    n_pages = page_table.shape[1]
    pg = PAGES_PER_STEP
    assert n_pages % pg == 0
    n_steps = n_pages // pg

    def page_spec(width, i):
        return pl.BlockSpec((None, None, PAGE_SIZE, width), lambda b, g, pt: (layer, pt[b, g * pg + i], 0, 0))

    in_specs = [
        pl.BlockSpec((None, rows, MLA_KV_LORA), lambda b, g, pt: (b, 0, 0)),
        pl.BlockSpec((None, rows, MLA_ROPE), lambda b, g, pt: (b, 0, 0)),
        pl.BlockSpec((None, dec_seq, MLA_KV_LORA), lambda b, g, pt: (b, 0, 0)),
        pl.BlockSpec((None, dec_seq, MLA_ROPE), lambda b, g, pt: (b, 0, 0)),
    ]
    in_specs += [page_spec(MLA_KV_LORA, i) for i in range(pg)]
    in_specs += [page_spec(MLA_ROPE, i) for i in range(pg)]
    grid_spec = pltpu.PrefetchScalarGridSpec(
        num_scalar_prefetch=1,
        grid=(db, n_steps),
        in_specs=in_specs,
        out_specs=pl.BlockSpec((None, rows, MLA_KV_LORA), lambda b, g, pt: (b, 0, 0)),
        scratch_shapes=[pltpu.VMEM((rows, 1), F32), pltpu.VMEM((rows, 1), F32),
                        pltpu.VMEM((rows, MLA_KV_LORA), F32)],
    )
    return pl.pallas_call(
        functools.partial(_paged_attn_kernel, n_steps=n_steps, dec_seq=dec_seq),
        grid_spec=grid_spec,
        out_shape=jax.ShapeDtypeStruct((db, rows, MLA_KV_LORA), F32),
        compiler_params=_cparams(("arbitrary", "arbitrary")),
        name="mla_paged_attention",
    )(page_table, qlat, qpe, new_ckv, new_kpe, *([cache_ckv] * pg), *([cache_kpe] * pg))


def _sample_out_kernel(olat_ref, wbd_ref, wo_ref, x_ref, gate_ref, o_ref):
    o = _dot(olat_ref[...].astype(BF16), wbd_ref[...]).astype(BF16)
    o_ref[...] = x_ref[...] + gate_ref[...] * _dot(o, wo_ref[...])


def _mla_sample_out(rows, olat, w_bd, w_o, x, mod):
    return pl.pallas_call(
        _sample_out_kernel,
        grid=(rows.grid,),
        in_specs=[rows.rows(olat.shape[1]), _Rows.full(w_bd.shape), _Rows.full(w_o.shape),
                  rows.rows(D_MODEL), rows.mod(2)],
        out_specs=rows.rows(D_MODEL),
        out_shape=jax.ShapeDtypeStruct((rows.n, D_MODEL), F32),
        compiler_params=_cparams(("arbitrary",)),
        name="mla_sample_out",
    )(olat, w_bd, w_o, x, mod)


def _out_proj_kernel(o_ref, wo_ref, x_ref, gate_ref, y_ref):
    y_ref[...] = x_ref[...] + gate_ref[...] * _dot(o_ref[...], wo_ref[...])


def _out_proj(rows, o, w_o, x, mod):
    return pl.pallas_call(
        _out_proj_kernel,
        grid=(rows.grid,),
        in_specs=[rows.rows(o.shape[1]), _Rows.full(w_o.shape), rows.rows(D_MODEL), rows.mod(2)],
        out_specs=rows.rows(D_MODEL),
        out_shape=jax.ShapeDtypeStruct((rows.n, D_MODEL), F32),
        compiler_params=_cparams(("arbitrary",)),
        name="attn_out_proj",
    )(o, w_o, x, mod)


SWA_NQ = SWA_HEADS * SWA_HEAD_DIM
SWA_NK = SWA_KV_HEADS * SWA_HEAD_DIM
SWA_QKV = SWA_NQ + 2 * SWA_NK
SWA_EXT = SWA_QKV + SWA_NQ + SWA_NK


def _swa_proj_kernel(x_ref, g_ref, sh_ref, sc_ref, cos_ref, sin_ref, w_ref, b_ref,
                     q_ref, kb_ref, vb_ref, k_ref, v_ref):
    h = _modulate(x_ref[...], g_ref[...], sh_ref[...], sc_ref[...]).astype(BF16)
    z = _dot(h, w_ref[...]) + b_ref[...]
    cos = cos_ref[...]
    sin = sin_ref[...]
    for t in range((SWA_NQ + SWA_NK) // LANES):
        sl = slice(t * LANES, (t + 1) * LANES)
        sw = slice(SWA_QKV + t * LANES, SWA_QKV + (t + 1) * LANES)
        r = z[:, sl] * cos + z[:, sw] * sin
        if t < SWA_NQ // LANES:
            q_ref[:, sl] = (r * SWA_SCALE).astype(BF16)
        else:
            ks = slice(t * LANES - SWA_NQ, (t + 1) * LANES - SWA_NQ)
            k_ref[:, ks] = r
            kb_ref[:, ks] = r.astype(BF16)
    v = z[:, SWA_NQ + SWA_NK:SWA_QKV]
    v_ref[...] = v
    vb_ref[...] = v.astype(BF16)


def _swa_project(rows, x, ln_g, mod, tabs, w):
    n = rows.n
    full = _Rows.full
    return pl.pallas_call(
        _swa_proj_kernel,
        grid=(rows.grid,),
        in_specs=[rows.rows(D_MODEL), full((1, D_MODEL)), rows.mod(0), rows.mod(1),
                  rows.pos(LANES), rows.pos(LANES), full((D_MODEL, SWA_EXT)), full((1, SWA_EXT))],
        out_specs=[rows.rows(SWA_NQ), rows.rows(SWA_NK), rows.rows(SWA_NK), rows.rows(SWA_NK),
                   rows.rows(SWA_NK)],
        out_shape=[
            jax.ShapeDtypeStruct((n, SWA_NQ), BF16),
            jax.ShapeDtypeStruct((n, SWA_NK), BF16),
            jax.ShapeDtypeStruct((n, SWA_NK), BF16),
            jax.ShapeDtypeStruct((n, SWA_NK), F32),
            jax.ShapeDtypeStruct((n, SWA_NK), F32),
        ],
        compiler_params=_cparams(("arbitrary",)),
        name="swa_project",
    )(x, ln_g, mod, mod, tabs["cos"], tabs["sin"], w["w_qkv"], w["b_qkv"])


def _swa_core(q_all, k_all, v_all, mask, sinks_ref, o_ref):
    lane = lax.broadcasted_iota(I32, (1, LANES), 1)
    lo = lane < SWA_HEAD_DIM
    zero = jnp.zeros((), BF16)
    for kh in range(SWA_KV_HEADS):
        tile = kh // 2
        k_t = k_all[:, tile * LANES:(tile + 1) * LANES]
        v_t = v_all[:, tile * LANES:(tile + 1) * LANES]
        k_r = pltpu.roll(k_t, SWA_HEAD_DIM, 1)
        v_r = pltpu.roll(v_t, SWA_HEAD_DIM, 1)
        for pair in range(SWA_GROUP // 2):
            q_t = q_all[:, (kh * 2 + pair) * LANES:(kh * 2 + pair + 1) * LANES]
            out = None
            for half in range(2):
                keep = lo if half == 0 else jnp.logical_not(lo)
                kx = k_t if (kh % 2) == half else k_r
                vx = v_t if (kh % 2) == half else v_r
                s = _dot_nt(jnp.where(keep, q_t, zero), kx)
                s = jnp.where(mask, s, NEG_INF)
                sink = sinks_ref[kh * SWA_GROUP + 2 * pair + half]
                m = jnp.maximum(jnp.max(s, axis=-1, keepdims=True), sink)
                p = jnp.exp(s - m)
                l = jnp.sum(p, axis=-1, keepdims=True) + jnp.exp(sink - m)
                o = _dot((p / l).astype(BF16), jnp.where(keep, vx, zero))
                out = o if out is None else out + o
            o_ref[:, (kh * 2 + pair) * LANES:(kh * 2 + pair + 1) * LANES] = out.astype(o_ref.dtype)


def _swa_prompt_kernel(sinks_ref, q_ref, kc_ref, kp_ref, vc_ref, vp_ref, o_ref, *, tq):
    i = pl.program_id(1)
    k_all = jnp.concatenate([kp_ref[...], kc_ref[...]], axis=0)
    v_all = jnp.concatenate([vp_ref[...], vc_ref[...]], axis=0)
    tk = tq + WINDOW
    q_pos = i * tq + lax.broadcasted_iota(I32, (tq, tk), 0)
    k_pos = i * tq - WINDOW + lax.broadcasted_iota(I32, (tq, tk), 1)
    mask = (k_pos >= 0) & (k_pos <= q_pos) & (q_pos - k_pos < WINDOW)
    _swa_core(q_ref[...], k_all, v_all, mask, sinks_ref, o_ref)


def _swa_prompt_attention(q, kb, vb, sinks, batch, seq):
    tq = ATTN_Q_BLOCK
    r = tq // WINDOW
    q3 = q.reshape(batch, seq, SWA_NQ)
    k3 = kb.reshape(batch, seq, SWA_NK)
    v3 = vb.reshape(batch, seq, SWA_NK)
    cur = pl.BlockSpec((None, tq, SWA_NK), lambda b, i: (b, i, 0))
    prev = pl.BlockSpec((None, WINDOW, SWA_NK), lambda b, i: (b, jnp.maximum(i * r - 1, 0), 0))
    out = pl.pallas_call(
        functools.partial(_swa_prompt_kernel, tq=tq),
        grid=(batch, seq // tq),
        in_specs=[pl.BlockSpec(memory_space=pltpu.SMEM),
                  pl.BlockSpec((None, tq, SWA_NQ), lambda b, i: (b, i, 0)), cur, prev, cur, prev],
        out_specs=pl.BlockSpec((None, tq, SWA_NQ), lambda b, i: (b, i, 0)),
        out_shape=jax.ShapeDtypeStruct((batch, seq, SWA_NQ), BF16),
        compiler_params=_cparams(("arbitrary", "arbitrary")),
        name="swa_prompt_attention",
    )(sinks, q3, k3, k3, v3, v3)
    return out.reshape(batch * seq, SWA_NQ)


def _swa_sample_kernel(sinks_ref, q_ref, kbuf_ref, vbuf_ref, kn_ref, vn_ref, o_ref, o_sc, *, n_buf, dec_seq):
    pad = jnp.zeros((BF16_ROWS - dec_seq, SWA_NK), F32)
    k_all = jnp.concatenate([kbuf_ref[...], kn_ref[...], pad], axis=0).astype(BF16)
    v_all = jnp.concatenate([vbuf_ref[...], vn_ref[...], pad], axis=0).astype(BF16)
    tk = n_buf + BF16_ROWS
    t = lax.broadcasted_iota(I32, (BF16_ROWS, tk), 0)
    c = lax.broadcasted_iota(I32, (BF16_ROWS, tk), 1)
    in_buf = c < n_buf
    mask = ((in_buf & ((n_buf + t - c) < WINDOW))
            | (jnp.logical_not(in_buf) & ((c - n_buf) <= jnp.minimum(t, dec_seq - 1))))
    q = jnp.concatenate([q_ref[...].astype(F32), jnp.zeros((BF16_ROWS - dec_seq, SWA_NQ), F32)], axis=0)
    _swa_core(q.astype(BF16), k_all, v_all, mask, sinks_ref, o_sc)
    o_ref[...] = o_sc[:dec_seq, :].astype(o_ref.dtype)


def _swa_sample_attention(q, kn, vn, buf_k, buf_v, sinks, db, dec_seq):
    n_buf = buf_k.shape[1]
    spec = lambda rws, w: pl.BlockSpec((None, rws, w), lambda b: (b, 0, 0))
    out = pl.pallas_call(
        functools.partial(_swa_sample_kernel, n_buf=n_buf, dec_seq=dec_seq),
        grid=(db,),
        in_specs=[pl.BlockSpec(memory_space=pltpu.SMEM), spec(dec_seq, SWA_NQ), spec(n_buf, SWA_NK),
                  spec(n_buf, SWA_NK), spec(dec_seq, SWA_NK), spec(dec_seq, SWA_NK)],
        out_specs=spec(dec_seq, SWA_NQ),
        out_shape=jax.ShapeDtypeStruct((db, dec_seq, SWA_NQ), BF16),
        scratch_shapes=[pltpu.VMEM((BF16_ROWS, SWA_NQ), F32)],
        compiler_params=_cparams(("arbitrary",)),
        name="swa_sample_attention",
    )(sinks, q.reshape(db, dec_seq, SWA_NQ), buf_k, buf_v,
      kn.reshape(db, dec_seq, SWA_NK), vn.reshape(db, dec_seq, SWA_NK))
    return out.reshape(db * dec_seq, SWA_NQ)


def _peer_candidate_tables(width_tokens):
    k = PEER_TOPK
    flat, valid = [], []
    for i in range(8):
        width = 16 if i == 0 else 8
        for j in range(width):
            flat.append(i * k + j)
            valid.append((i + 1) * (j + 1) <= k)
    for i in range(8, 16):
        flat.append(i * k)
        valid.append(True)
    flat = np.broadcast_to(np.asarray(flat, np.int32)[:, None], (len(flat), width_tokens))
    pen = np.where(np.asarray(valid), 0.0, -np.inf).astype(np.float32)
    return np.ascontiguousarray(flat), np.ascontiguousarray(np.broadcast_to(pen[:, None], flat.shape))


def _extract_top(s, row_id, k, payload=None):
    big = jnp.int32(2 ** 30)
    vals, ids = [], []
    for _ in range(k):
        m = jnp.max(s, axis=0, keepdims=True)
        sel = jnp.min(jnp.where(s == m, row_id, big), axis=0, keepdims=True)
        hit = row_id == sel
        if payload is not None:
            ids.append(jnp.max(jnp.where(hit, payload, -1), axis=0, keepdims=True))
        else:
            ids.append(sel)
        vals.append(m)
        s = jnp.where(hit, -jnp.inf, s)
    return jnp.concatenate(vals, axis=0), jnp.concatenate(ids, axis=0)


def _peer_route_kernel(x_ref, g_ref, sh_ref, sc_ref, wq_ref, sk_ref, flat_ref, pen_ref,
                       h_ref, e_ref, gate_ref, qs_sc, sv_sc, si_sc, pe_sc, pg_sc, *, tb):
    k = PEER_TOPK
    h = _modulate(x_ref[...], g_ref[...], sh_ref[...], sc_ref[...]).astype(BF16)
    h_ref[...] = h
    q = _dot(h, wq_ref[...])
    n_groups = 2 * PEER_HEADS
    for grp in range(n_groups):
        qs_sc[grp] = q[:, grp * PEER_HALF:(grp + 1) * PEER_HALF].astype(BF16)
    key_id = lax.broadcasted_iota(I32, (PEER_N_KEYS, tb), 0)

    def first_level(grp, carry):
        s = _dot_nt(sk_ref[grp], qs_sc[grp])
        vals, ids = _extract_top(s, key_id, k)
        sv_sc[grp] = vals
        si_sc[grp] = ids
        return carry

    lax.fori_loop(0, n_groups, first_level, 0, unroll=ROUTE_UNROLL)

    def second_level(hd, carry):
        sa, sb = sv_sc[2 * hd], sv_sc[2 * hd + 1]
        ia, ib = si_sc[2 * hd], si_sc[2 * hd + 1]
        cs, ce = [], []
        for i in range(8):
            width = 16 if i == 0 else 8
            cs.append(sa[i:i + 1] + sb[:width])
            ce.append(ia[i:i + 1] * PEER_N_KEYS + ib[:width])
        cs.append(sa[8:] + sb[:1])
        ce.append(ia[8:] * PEER_N_KEYS + ib[:1])
        cand = jnp.concatenate(cs, axis=0) + pen_ref[...]
        cand_e = jnp.concatenate(ce, axis=0)
        best, experts = _extract_top(cand, flat_ref[...], k, payload=cand_e)
        ex = jnp.exp(best - best[:1])
        gates = ex / jnp.sum(ex, axis=0, keepdims=True)
        pick = pl.ds(pl.multiple_of(hd * k, k), k)
        pe_sc[pick, :] = experts
        pg_sc[pick, :] = gates
        return carry

    lax.fori_loop(0, PEER_HEADS, second_level, 0, unroll=ROUTE_UNROLL)
    e_ref[...] = pe_sc[...].T
    gate_ref[...] = pg_sc[...].T


def _peer_route(rows, x, ln_g, mod, w):
    n, tb = rows.n, rows.tb
    flat, pen = _peer_candidate_tables(tb)
    full = _Rows.full
    n_groups = 2 * PEER_HEADS
    return pl.pallas_call(
        functools.partial(_peer_route_kernel, tb=tb),
        grid=(rows.grid,),
        in_specs=[rows.rows(D_MODEL), full((1, D_MODEL)), rows.mod(3), rows.mod(4),
                  full(w["w_q"].shape), full(w["sub_keys"].shape), full(flat.shape), full(pen.shape)],
        out_specs=[rows.rows(D_MODEL), rows.rows(PEER_PICKS), rows.rows(PEER_PICKS)],
        out_shape=[jax.ShapeDtypeStruct((n, D_MODEL), BF16),
                   jax.ShapeDtypeStruct((n, PEER_PICKS), I32),
                   jax.ShapeDtypeStruct((n, PEER_PICKS), F32)],
        scratch_shapes=[pltpu.VMEM((n_groups, tb, PEER_HALF), BF16),
                        pltpu.VMEM((n_groups, PEER_TOPK, tb), F32),
                        pltpu.VMEM((n_groups, PEER_TOPK, tb), I32),
                        pltpu.VMEM((PEER_PICKS, tb), I32),
                        pltpu.VMEM((PEER_PICKS, tb), F32)],
        compiler_params=_cparams(("arbitrary",)),
        name="peer_route",
    )(x, ln_g, mod, mod, w["w_q"], w["sub_keys"], jnp.asarray(flat), jnp.asarray(pen))


def _peer_expert_kernel(h_ref, e_ref, gate_ref, u_ref, v_ref, x_ref, g2_ref, o_ref,
                        w_sc, p_sc, acc_sc, *, tb, n_chunks):
    c = pl.program_id(1)
    pitch = GATE_TILE_PITCH

    @pl.when(c == 0)
    def _():
        acc_sc[...] = jnp.zeros(acc_sc.shape, F32)
        key_id = lax.broadcasted_iota(I32, (PEER_N_KEYS, PEER_PICKS), 0)

        def token(t, carry):
            e = e_ref[pl.ds(t, 1), :]
            g = gate_ref[pl.ds(t, 1), :]
            g_hi = g.astype(BF16).astype(F32)
            g_lo = (g - g_hi).astype(BF16).astype(F32)
            hit_a = key_id == (e >> 7)
            hit_b = key_id == (e & (PEER_N_KEYS - 1))
            one_a = jnp.where(hit_a, 1.0, 0.0).astype(BF16)
            lhs = jnp.concatenate([one_a, one_a], axis=1)
            rhs = jnp.concatenate([jnp.where(hit_b, g_hi, 0.0).astype(BF16),
                                   jnp.where(hit_b, g_lo, 0.0).astype(BF16)], axis=1)
            w_sc[pl.ds(pl.multiple_of(t * pitch, SUBLANES), PEER_N_KEYS), :] = _dot_nt(lhs, rhs)
            return carry

        lax.fori_loop(0, tb, token, 0, unroll=TOKEN_UNROLL)

    h = h_ref[...]
    for pair in range(EXPERT_CHUNK_A // 2):
        z = _dot_nt(h, u_ref[pair * 2 * PEER_N_KEYS:(pair + 1) * 2 * PEER_N_KEYS, :])
        for half in range(2):
            al = 2 * pair + half
            sl = slice(al * PEER_N_KEYS, (al + 1) * PEER_N_KEYS)
            w = w_sc[pl.ds(c * EXPERT_CHUNK_A + al, tb, stride=pitch), :]
            p_sc[:, sl] = (w * _gelu(z[:, half * PEER_N_KEYS:(half + 1) * PEER_N_KEYS])).astype(BF16)
    acc_sc[...] += _dot(p_sc[...], v_ref[...])

    @pl.when(c == n_chunks - 1)
    def _():
        o_ref[...] = x_ref[...] + g2_ref[...] * acc_sc[...]


def _peer_experts(rows, h, experts, gates, u, v, x, mod):
    n, tb = rows.n, rows.tb
    ce = EXPERT_CHUNK_A * PEER_N_KEYS
    n_chunks = u.shape[0] // ce
    bps = rows.bps
    if bps is None:
        gate_spec = pl.BlockSpec((tb, D_MODEL), lambda i, c: (i, 5))
    else:
        gate_spec = pl.BlockSpec((None, 1, D_MODEL), lambda i, c: (i // bps, 0, 5))
    tok = lambda w: pl.BlockSpec((tb, w), lambda i, c: (i, 0))
    tok1 = lambda w: pl.BlockSpec((tb, w), lambda i, c: (i, 0), pipeline_mode=pl.Buffered(1))
    return pl.pallas_call(
        functools.partial(_peer_expert_kernel, tb=tb, n_chunks=n_chunks),
        grid=(rows.grid, n_chunks),
        in_specs=[tok1(D_MODEL), tok(PEER_PICKS), tok(PEER_PICKS),
                  pl.BlockSpec((ce, D_MODEL), lambda i, c: (c, 0)),
                  pl.BlockSpec((ce, D_MODEL), lambda i, c: (c, 0)),
                  tok1(D_MODEL), gate_spec],
        out_specs=tok(D_MODEL),
        out_shape=jax.ShapeDtypeStruct((n, D_MODEL), F32),
        scratch_shapes=[pltpu.VMEM((tb * GATE_TILE_PITCH, LANES), F32),
                        pltpu.VMEM((tb, ce), BF16),
                        pltpu.VMEM((tb, D_MODEL), F32)],
        compiler_params=_cparams(("arbitrary", "arbitrary")),
        name="peer_experts",
    )(h, experts, gates, u, v, x, mod)


def _final_norm_kernel(x_ref, g_ref, o_ref):
    o_ref[...] = _rms(x_ref[...], g_ref[...])


def _final_norm(rows, x, g):
    return pl.pallas_call(
        _final_norm_kernel,
        grid=(rows.grid,),
        in_specs=[rows.rows(D_MODEL), _Rows.full((1, D_MODEL))],
        out_specs=rows.rows(D_MODEL),
        out_shape=jax.ShapeDtypeStruct((rows.n, D_MODEL), F32),
        compiler_params=_cparams(("arbitrary",)),
        name="final_norm",
    )(x, g)


def _pair_swap(w, half):
    return jnp.concatenate([-w[..., half:2 * half], w[..., :half]], axis=-1)


def _rope_cos_sin(pos, rot_dim):
    half = rot_dim // 2
    inv_freq = ROPE_THETA ** (-jnp.arange(half, dtype=F32) * 2.0 / rot_dim)
    ang = pos.astype(F32)[:, None] * inv_freq[None, :]
    return jnp.cos(ang), jnp.sin(ang)


def _mla_tables(pos):
    cos, sin = _rope_cos_sin(pos, MLA_ROPE)
    n = pos.shape[0]
    ones = jnp.ones((n, MLA_NOPE), F32)
    zeros_n = jnp.zeros((n, MLA_NOPE), F32)
    pad = jnp.zeros((n, MLA_HEAD_PAD - MLA_NOPE - MLA_ROPE), F32)
    return {
        "c32": jnp.concatenate([cos, cos], axis=1),
        "s32": jnp.concatenate([sin, sin], axis=1),
        "c128": jnp.concatenate([ones, cos, cos, pad], axis=1),
        "s128": jnp.concatenate([zeros_n, sin, sin, pad], axis=1),
    }


def _swa_tables(pos):
    cos, sin = _rope_cos_sin(pos, SWA_ROT)
    n = pos.shape[0]
    rest = SWA_HEAD_DIM - SWA_ROT
    c = jnp.concatenate([cos, cos, jnp.ones((n, rest), F32)], axis=1)
    s = jnp.concatenate([sin, sin, jnp.zeros((n, rest), F32)], axis=1)
    reps = LANES // SWA_HEAD_DIM
    return {"cos": jnp.tile(c, (1, reps)), "sin": jnp.tile(s, (1, reps))}


def _mla_weights(w_in, q_norm, kv_norm, w_uq, w_uk, w_uv, w_o):
    d = w_in.shape[0]
    half = MLA_ROPE // 2
    kpe_cols = w_in[:, MLA_Q_LORA + MLA_KV_LORA:]
    w_in_ext = jnp.zeros((d, MLA_IN_EXT), F32)
    w_in_ext = w_in_ext.at[:, :MLA_KPE_COL + MLA_ROPE].set(w_in)
    w_in_ext = w_in_ext.at[:, MLA_KPE_SWAP_COL:MLA_KPE_SWAP_COL + MLA_ROPE].set(_pair_swap(kpe_cols, half))
    uq = w_uq.reshape(MLA_Q_LORA, MLA_HEADS, MLA_NOPE + MLA_ROPE)
    uq_pad = jnp.zeros((MLA_Q_LORA, MLA_HEADS, MLA_HEAD_PAD), F32).at[:, :, :MLA_NOPE + MLA_ROPE].set(uq)
    uq_swap = jnp.zeros((MLA_Q_LORA, MLA_HEADS, MLA_HEAD_PAD), F32)
    uq_swap = uq_swap.at[:, :, MLA_NOPE:MLA_NOPE + MLA_ROPE].set(_pair_swap(uq[:, :, MLA_NOPE:], half))
    uk_pad = jnp.zeros((MLA_KV_LORA, MLA_HEADS, MLA_HEAD_PAD), F32).at[:, :, :MLA_NOPE].set(w_uk)
    e_pe = jnp.zeros((MLA_ROPE, MLA_HEADS, MLA_HEAD_PAD), F32)
    e_pe = e_pe.at[jnp.arange(MLA_ROPE), :, MLA_NOPE + jnp.arange(MLA_ROPE)].set(1.0)
    ukt = jnp.zeros((MLA_HEADS, MLA_HEAD_PAD, MLA_KV_LORA), F32)
    ukt = ukt.at[:, :MLA_NOPE, :].set(jnp.transpose(w_uk, (1, 2, 0)))
    eye = jnp.eye(MLA_HEADS, dtype=F32)
    w_bd = jnp.einsum("lhd,hg->hlgd", w_uv, eye).reshape(MLA_HEADS * MLA_KV_LORA, MLA_HEADS * MLA_V)
    hp = MLA_HEADS * MLA_HEAD_PAD
    return {
        "w_in": w_in_ext.astype(BF16),
        "q_norm": q_norm.reshape(1, -1), "kv_norm": kv_norm.reshape(1, -1),
        "w_uq": uq_pad.reshape(MLA_Q_LORA, hp).astype(BF16),
        "w_uq_swap": uq_swap.reshape(MLA_Q_LORA, hp).astype(BF16),
        "w_uk": uk_pad.reshape(MLA_KV_LORA, hp).astype(BF16),
        "e_pe": e_pe.reshape(MLA_ROPE, hp).astype(BF16),
        "w_uv": w_uv.reshape(MLA_KV_LORA, MLA_HEADS * MLA_V).astype(BF16),
        "w_ukt": ukt.astype(BF16),
        "w_bd": w_bd.astype(BF16),
        "w_o": w_o.astype(BF16),
    }


def _swa_weights(w_qkv, b_qkv, w_o):
    half = SWA_ROT // 2
    n_rot_heads = SWA_HEADS + SWA_KV_HEADS

    def swap_cols(w):
        lead = w.shape[:-1]
        wh = w[..., :SWA_NQ + SWA_NK].reshape(lead + (n_rot_heads, SWA_HEAD_DIM))
        sw = jnp.concatenate([_pair_swap(wh[..., :SWA_ROT], half),
                              jnp.zeros(lead + (n_rot_heads, SWA_HEAD_DIM - SWA_ROT), F32)], axis=-1)
        return sw.reshape(lead + (SWA_NQ + SWA_NK,))

    w_ext = jnp.concatenate([w_qkv, swap_cols(w_qkv)], axis=-1)
    b_ext = jnp.concatenate([b_qkv, swap_cols(b_qkv)], axis=-1)
    return {"w_qkv": w_ext.astype(BF16), "b_qkv": b_ext.reshape(1, -1), "w_o": w_o.astype(BF16)}


def kernel(x_prompt, x_sample, cache_mla_ckv, cache_mla_kpe, cache_swa_k, cache_swa_v, page_table,
           c_prompt, c_sample, ln1_g, ln2_g, w_mod, b_mod,
           mla_w_in, mla_q_norm, mla_kv_norm, mla_w_uq, mla_w_uk, mla_w_uv, mla_w_o,
           swa_w_qkv, swa_b_qkv, swa_sinks, swa_w_o,
           peer_w_q, peer_sub_keys, peer_u, peer_v, final_g):
    batch, seq, d = x_prompt.shape
    db, dec_seq, _ = x_sample.shape
    depth = w_mod.shape[0]
    past_len = page_table.shape[1] * PAGE_SIZE
    n_p, n_s = batch * seq, db * dec_seq

    rows_p = _Rows(n_p, min(ROW_BLOCK, seq), seq_len=seq)
    rows_s = _Rows(n_s, n_s)
    route_p = _Rows(n_p, ROUTE_BLOCK, seq_len=seq)
    route_s = _Rows(n_s, min(ROUTE_BLOCK, n_s))
    expert_p = _Rows(n_p, EXPERT_ROW_BLOCK, seq_len=seq)
    expert_s = _Rows(n_s, min(EXPERT_ROW_BLOCK, n_s))

    pos_p = jnp.arange(seq)
    pos_s = jnp.tile(past_len + jnp.arange(dec_seq), db)
    mla_tab_p, mla_tab_s = _mla_tables(pos_p), _mla_tables(pos_s)
    swa_tab_p, swa_tab_s = _swa_tables(pos_p), _swa_tables(pos_s)

    m_all = _modulation_all(jnp.concatenate([c_prompt, c_sample], axis=0), w_mod, b_mod)

    x_p = x_prompt.reshape(n_p, d)
    x_s = x_sample.reshape(n_s, d)
    ckv_p, kpe_p, ckv_s, kpe_s = [], [], [], []
    swk_p, swv_p, swk_s, swv_s = [], [], [], []
    n_buf = cache_swa_k.shape[2]

    for i in range(depth):
        j = i // 2
        mod_p = m_all[i, :batch].reshape(batch, 1, 6 * d)
        mod_s = jnp.repeat(m_all[i, batch:], dec_seq, axis=0)
        g1 = ln1_g[i].reshape(1, d)
        g2 = ln2_g[i].reshape(1, d)
        if i % 2 == 0:
            w = _mla_weights(mla_w_in[j], mla_q_norm[j], mla_kv_norm[j], mla_w_uq[j], mla_w_uk[j],
                             mla_w_uv[j], mla_w_o[j])
            ckv, kpe, q, k, v = _mla_project(rows_p, x_p, g1, mod_p, mla_tab_p, w)
            o = _mla_attention(q, k, v, batch, seq)
            x_p = _out_proj(rows_p, o, w["w_o"], x_p, mod_p)
            ckv_p.append(ckv.reshape(batch, seq, MLA_KV_LORA))
            kpe_p.append(kpe.reshape(batch, seq, MLA_ROPE))

            ckv, kpe, q, _, _ = _mla_project(rows_s, x_s, g1, mod_s, mla_tab_s, w)
            qlat = _mla_absorb_queries(q, w["w_ukt"]).reshape(db, dec_seq * MLA_HEADS, MLA_KV_LORA)
            qpe = q.reshape(n_s, MLA_HEADS, MLA_HEAD_PAD)[:, :, MLA_NOPE:MLA_NOPE + MLA_ROPE]
            qpe = qpe.reshape(db, dec_seq * MLA_HEADS, MLA_ROPE)
            ckv3 = ckv.reshape(db, dec_seq, MLA_KV_LORA)
            kpe3 = kpe.reshape(db, dec_seq, MLA_ROPE)
            olat = _mla_paged_attention(qlat, qpe, ckv3, kpe3, cache_mla_ckv, cache_mla_kpe, page_table, j)
            olat = olat.reshape(n_s, MLA_HEADS * MLA_KV_LORA)
            x_s = _mla_sample_out(rows_s, olat, w["w_bd"], w["w_o"], x_s, mod_s)
            ckv_s.append(ckv3)
            kpe_s.append(kpe3)
        else:
            w = _swa_weights(swa_w_qkv[j], swa_b_qkv[j], swa_w_o[j])
            q, kb, vb, k, v = _swa_project(rows_p, x_p, g1, mod_p, swa_tab_p, w)
            o = _swa_prompt_attention(q, kb, vb, swa_sinks[j], batch, seq)
            x_p = _out_proj(rows_p, o, w["w_o"], x_p, mod_p)
            nb = min(WINDOW, seq)
            swk_p.append(k.reshape(batch, seq, SWA_KV_HEADS, SWA_HEAD_DIM)[:, seq - nb:])
            swv_p.append(v.reshape(batch, seq, SWA_KV_HEADS, SWA_HEAD_DIM)[:, seq - nb:])

            q, kb, vb, k, v = _swa_project(rows_s, x_s, g1, mod_s, swa_tab_s, w)
            buf_k = cache_swa_k[j].reshape(db, n_buf, SWA_NK)
            buf_v = cache_swa_v[j].reshape(db, n_buf, SWA_NK)
            o = _swa_sample_attention(q, k, v, buf_k, buf_v, swa_sinks[j], db, dec_seq)
            x_s = _out_proj(rows_s, o, w["w_o"], x_s, mod_s)
            k_all = jnp.concatenate([buf_k, k.reshape(db, dec_seq, SWA_NK)], axis=1)[:, -n_buf:]
            v_all = jnp.concatenate([buf_v, v.reshape(db, dec_seq, SWA_NK)], axis=1)[:, -n_buf:]
            swk_s.append(k_all.reshape(db, n_buf, SWA_KV_HEADS, SWA_HEAD_DIM))
            swv_s.append(v_all.reshape(db, n_buf, SWA_KV_HEADS, SWA_HEAD_DIM))

        pw = {"w_q": peer_w_q[i].astype(BF16),
              "sub_keys": peer_sub_keys[i].reshape(2 * PEER_HEADS, PEER_N_KEYS, PEER_HALF).astype(BF16)}
        u_b = peer_u[i].astype(BF16)
        v_b = peer_v[i].astype(BF16)
        h, experts, gates = _peer_route(route_p, x_p, g2, mod_p, pw)
        x_p = _peer_experts(expert_p, h, experts, gates, u_b, v_b, x_p, mod_p)
        h, experts, gates = _peer_route(route_s, x_s, g2, mod_s, pw)
        x_s = _peer_experts(expert_s, h, experts, gates, u_b, v_b, x_s, mod_s)

    fg = final_g.reshape(1, d)
    y_p = _final_norm(rows_p, x_p, fg).reshape(batch, seq, d)
    y_s = _final_norm(rows_s, x_s, fg).reshape(db, dec_seq, d)
    return (y_p, y_s,
            jnp.stack(ckv_p), jnp.stack(kpe_p), jnp.stack(ckv_s), jnp.stack(kpe_s),
            jnp.stack(swk_p), jnp.stack(swv_p), jnp.stack(swk_s), jnp.stack(swv_s))
```

```python
import functools

import numpy as np
import jax
import jax.numpy as jnp
from jax import lax
from jax.experimental import pallas as pl
from jax.experimental.pallas import tpu as pltpu

F32 = jnp.float32
BF16 = jnp.bfloat16
I32 = jnp.int32

D_MODEL = 1024
PAGE_SIZE = 128
ROPE_THETA = 500000.0
NORM_EPS = 1e-6
NEG_INF = -1e30

MLA_HEADS = 16
MLA_Q_LORA = 384
MLA_KV_LORA = 256
MLA_NOPE = 64
MLA_ROPE = 32
MLA_V = 64
MLA_SCALE = (MLA_NOPE + MLA_ROPE) ** -0.5

SWA_HEADS = 16
SWA_KV_HEADS = 4
SWA_GROUP = SWA_HEADS // SWA_KV_HEADS
SWA_HEAD_DIM = D_MODEL // SWA_HEADS
SWA_ROT = SWA_HEAD_DIM // 4
SWA_SCALE = SWA_HEAD_DIM ** -0.5
WINDOW = 128

PEER_HEADS = 8
PEER_N_KEYS = 128
PEER_TOPK = 16
PEER_HALF = 128
PEER_PICKS = PEER_HEADS * PEER_TOPK

LANES = 128
SUBLANES = 8
BF16_ROWS = 16
VMEM_LIMIT_BYTES = 56 * 1024 * 1024

ROW_BLOCK = 512
EXPERT_ROW_BLOCK = 512
TOKEN_UNROLL = 8
ROUTE_BLOCK = 256
ROUTE_UNROLL = 2
MLA_ATTN_BLOCK = 512
SWA_Q_BLOCK = 256
PAGES_PER_STEP = 16
PAGE_GROUP = 16
EXPERT_CHUNK_A = 8
GATE_TILE_PITCH = 136
MOD_COL_BLOCK = 1536


def _cparams(sem):
    return pltpu.CompilerParams(dimension_semantics=sem, vmem_limit_bytes=VMEM_LIMIT_BYTES)


def _dot(a, b):
    return jnp.dot(a, b, preferred_element_type=F32)


def _dot_nt(a, b):
    return lax.dot_general(a, b, (((1,), (1,)), ((), ())), preferred_element_type=F32)


def _rms(x, g):
    return x * lax.rsqrt(jnp.mean(x * x, axis=-1, keepdims=True) + NORM_EPS) * g


def _gelu(x):
    return 0.5 * x * (1.0 + lax.erf(x * np.float32(2.0 ** -0.5)))


def _modulate(x, g, shift, scale):
    return _rms(x, g) * (1.0 + scale) + shift


class _Rows:
    def __init__(self, n, tb, seq_len=None):
        assert n % tb == 0
        self.n, self.tb, self.grid = n, tb, n // tb
        self.bps = None
        if seq_len is not None:
            assert seq_len % tb == 0
            self.bps = seq_len // tb

    def rows(self, width):
        return pl.BlockSpec((self.tb, width), lambda i: (i, 0))

    def mod(self, k):
        if self.bps is None:
            return pl.BlockSpec((self.tb, D_MODEL), lambda i: (i, k))
        bps = self.bps
        return pl.BlockSpec((None, 1, D_MODEL), lambda i: (i // bps, 0, k))

    def pos(self, width):
        if self.bps is None:
            return pl.BlockSpec((self.tb, width), lambda i: (i, 0))
        bps = self.bps
        return pl.BlockSpec((self.tb, width), lambda i: (i % bps, 0))

    @staticmethod
    def full(shape):
        nd = len(shape)
        return pl.BlockSpec(shape, lambda i: (0,) * nd)


def _mod_kernel(c_ref, w_ref, b_ref, o_ref):
    c = c_ref[...]
    a = (c * jax.nn.sigmoid(c)).astype(BF16)
    o_ref[...] = _dot(a, w_ref[...].astype(BF16)) + b_ref[...]


def _modulation_all(c_all, w_mod, b_mod):
    depth, d, n6 = w_mod.shape
    nc = c_all.shape[0]
    nb = n6 // MOD_COL_BLOCK
    return pl.pallas_call(
        _mod_kernel,
        grid=(depth, nb),
        in_specs=[
            pl.BlockSpec((nc, d), lambda l, j: (0, 0)),
            pl.BlockSpec((None, d, MOD_COL_BLOCK), lambda l, j: (l, 0, j)),
            pl.BlockSpec((None, 1, MOD_COL_BLOCK), lambda l, j: (l, 0, j)),
        ],
        out_specs=pl.BlockSpec((None, nc, MOD_COL_BLOCK), lambda l, j: (l, 0, j)),
        out_shape=jax.ShapeDtypeStruct((depth, nc, n6), F32),
        compiler_params=_cparams(("arbitrary", "arbitrary")),
        name="adaln_modulation",
    )(c_all, w_mod, b_mod.reshape(depth, 1, n6))


MLA_IN_EXT = 896
MLA_KPE_COL = 640
MLA_KPE_SWAP_COL = 768
MLA_HEAD_PAD = LANES


def _mla_proj_kernel(x_ref, g_ref, sh_ref, sc_ref, c32_ref, s32_ref, c128_ref, s128_ref,
                     win_ref, qn_ref, kvn_ref, wuq_ref, wuqs_ref, wuk_ref, epe_ref, wuvt_ref,
                     ckv_ref, kpe_ref, q_ref, k_ref, vt_ref):
    h = _modulate(x_ref[...], g_ref[...], sh_ref[...], sc_ref[...]).astype(BF16)
    z = _dot(h, win_ref[...])
    cq = _rms(z[:, :MLA_Q_LORA], qn_ref[...]).astype(BF16)
    ckv = _rms(z[:, MLA_Q_LORA:MLA_Q_LORA + MLA_KV_LORA], kvn_ref[...])
    kpe = (z[:, MLA_KPE_COL:MLA_KPE_COL + MLA_ROPE] * c32_ref[...]
           + z[:, MLA_KPE_SWAP_COL:MLA_KPE_SWAP_COL + MLA_ROPE] * s32_ref[...])
    ckv_ref[...] = ckv
    kpe_ref[...] = kpe
    ckv_b = ckv.astype(BF16)
    q = _dot(cq, wuq_ref[...])
    qs = _dot(cq, wuqs_ref[...])
    cos = c128_ref[...]
    sin = s128_ref[...]
    for hd in range(MLA_HEADS):
        sl = slice(hd * MLA_HEAD_PAD, (hd + 1) * MLA_HEAD_PAD)
        q_ref[:, sl] = (q[:, sl] * cos + qs[:, sl] * sin).astype(BF16)
    k_ref[...] = (_dot(ckv_b, wuk_ref[...]) + _dot(kpe.astype(BF16), epe_ref[...])).astype(BF16)
    vt = _dot_nt(wuvt_ref[...], ckv_b).astype(BF16)
    chunks, _, width = vt_ref.shape
    for c in range(chunks):
        vt_ref[c] = vt[:, c * width:(c + 1) * width]


def _mla_project(rows, x, ln_g, mod, tabs, w):
    n = rows.n
    hp = MLA_HEADS * MLA_HEAD_PAD
    full = _Rows.full
    nv = MLA_HEADS * MLA_V
    chunk = min(MLA_ATTN_BLOCK, rows.tb)
    per_block = rows.tb // chunk
    return pl.pallas_call(
        _mla_proj_kernel,
        grid=(rows.grid,),
        in_specs=[
            rows.rows(D_MODEL), full((1, D_MODEL)), rows.mod(0), rows.mod(1),
            rows.pos(MLA_ROPE), rows.pos(MLA_ROPE), rows.pos(MLA_HEAD_PAD), rows.pos(MLA_HEAD_PAD),
            full((D_MODEL, MLA_IN_EXT)), full((1, MLA_Q_LORA)), full((1, MLA_KV_LORA)),
            full((MLA_Q_LORA, hp)), full((MLA_Q_LORA, hp)), full((MLA_KV_LORA, hp)),
            full((MLA_ROPE, hp)), full((nv, MLA_KV_LORA)),
        ],
        out_specs=[rows.rows(MLA_KV_LORA), rows.rows(MLA_ROPE), rows.rows(hp), rows.rows(hp),
                   pl.BlockSpec((per_block, nv, chunk), lambda i: (i, 0, 0))],
        out_shape=[
            jax.ShapeDtypeStruct((n, MLA_KV_LORA), F32),
            jax.ShapeDtypeStruct((n, MLA_ROPE), F32),
            jax.ShapeDtypeStruct((n, hp), BF16),
            jax.ShapeDtypeStruct((n, hp), BF16),
            jax.ShapeDtypeStruct((n // chunk, nv, chunk), BF16),
        ],
        compiler_params=_cparams(("arbitrary",)),
        name="mla_project",
    )(x, ln_g, mod, mod, tabs["c32"], tabs["s32"], tabs["c128"], tabs["s128"],
      w["w_in"], w["q_norm"], w["kv_norm"], w["w_uq"], w["w_uq_swap"], w["w_uk"], w["e_pe"], w["w_uvt"])


def _mla_attn_kernel(q_ref, k_ref, vt_ref, o_ref, *, tq, tk):
    assert tq == tk
    qi = pl.program_id(2)
    causal = lax.broadcasted_iota(I32, (tk, tq), 0) <= lax.broadcasted_iota(I32, (tk, tq), 1)
    v_row = lax.broadcasted_iota(I32, (LANES, tk), 0)
    qs = [q_ref[:, hh * LANES:(hh + 1) * LANES] for hh in range(2)]
    v_keep = [(v_row // MLA_V) == hh for hh in range(2)]

    def step(j, carry, masked):
        start = pl.multiple_of(j * tk, tk)
        vt = vt_ref[j]
        new = []
        for hh in range(2):
            m, l, acc = carry[hh]
            kb = k_ref[pl.ds(start, tk), hh * LANES:(hh + 1) * LANES]
            s = _dot_nt(kb, qs[hh]) * MLA_SCALE
            if masked:
                s = jnp.where(causal, s, NEG_INF)
            m_new = jnp.maximum(m, jnp.max(s, axis=0, keepdims=True))
            alpha = jnp.exp(m - m_new)
            p = jnp.exp(s - m_new)
            l = alpha * l + jnp.sum(p, axis=0, keepdims=True)
            acc = alpha * acc + _dot(jnp.where(v_keep[hh], vt, jnp.zeros_like(vt)), p.astype(BF16))
            new.append((m_new, l, acc))
        return tuple(new)

    head0 = (jnp.full((1, tq), NEG_INF, F32), jnp.zeros((1, tq), F32), jnp.zeros((LANES, tq), F32))
    carry = lax.fori_loop(0, qi, lambda j, c: step(j, c, False), (head0, head0))
    (_, l0, acc0), (_, l1, acc1) = step(qi, carry, True)
    o_ref[...] = (acc0 / l0 + acc1 / l1).T.astype(BF16)


def _mla_attention(q, k, vt, batch, seq):
    tq = tk = MLA_ATTN_BLOCK
    nq = seq // tq
    q3 = q.reshape(batch, seq, -1)
    k3 = k.reshape(batch, seq, -1)
    out = pl.pallas_call(
        functools.partial(_mla_attn_kernel, tq=tq, tk=tk),
        grid=(batch, MLA_HEADS // 2, nq),
        in_specs=[
            pl.BlockSpec((None, tq, 2 * LANES), lambda b, hp, i: (b, i, hp)),
            pl.BlockSpec((None, seq, 2 * LANES), lambda b, hp, i: (b, 0, hp)),
            pl.BlockSpec((seq // tk, LANES, tk), lambda b, hp, i: (b, hp, 0)),
        ],
        out_specs=pl.BlockSpec((None, tq, LANES), lambda b, hp, i: (b, i, hp)),
        out_shape=jax.ShapeDtypeStruct((batch, seq, MLA_HEADS * MLA_V), BF16),
        compiler_params=_cparams(("arbitrary", "arbitrary", "arbitrary")),
        name="mla_prompt_attention",
    )(q3, k3, vt)
    return out.reshape(batch * seq, MLA_HEADS * MLA_V)


def _qlat_kernel(q_ref, wk_ref, o_ref):
    for hd in range(MLA_HEADS):
        qh = q_ref[:, hd * MLA_HEAD_PAD:(hd + 1) * MLA_HEAD_PAD]
        o_ref[:, hd * MLA_KV_LORA:(hd + 1) * MLA_KV_LORA] = _dot(qh, wk_ref[hd]).astype(BF16)


def _mla_absorb_queries(q, w_ukt):
    n = q.shape[0]
    return pl.pallas_call(
        _qlat_kernel,
        grid=(1,),
        in_specs=[_Rows.full(q.shape), _Rows.full(w_ukt.shape)],
        out_specs=_Rows.full((n, MLA_HEADS * MLA_KV_LORA)),
        out_shape=jax.ShapeDtypeStruct((n, MLA_HEADS * MLA_KV_LORA), BF16),
        compiler_params=_cparams(("arbitrary",)),
        name="mla_absorb_queries",
    )(q, w_ukt)


def _paged_attn_kernel(pt_ref, qlat_ref, qpe_ref, nckv_ref, nkpe_ref, *rest, n_steps, dec_seq):
    pg = PAGES_PER_STEP
    ckv_refs = rest[:pg]
    kpe_refs = rest[pg:2 * pg]
    o_ref = rest[2 * pg]
    m_sc, l_sc, acc_sc, kc_sc, kp_sc = rest[2 * pg + 1:]
    g = pl.program_id(1)
    rows = qlat_ref.shape[0]

    @pl.when(g == 0)
    def _():
        m_sc[...] = jnp.full(m_sc.shape, NEG_INF, F32)
        l_sc[...] = jnp.zeros(l_sc.shape, F32)
        acc_sc[...] = jnp.zeros(acc_sc.shape, F32)

    qlat = qlat_ref[...]
    qpe = qpe_ref[...]

    def column(row_vec):
        return jnp.broadcast_to(row_vec, (LANES, rows)).T[:, :1]

    def update(s_t, vals):
        m = m_sc[...]
        m_new = jnp.maximum(m, jnp.max(s_t, axis=0, keepdims=True))
        alpha = jnp.exp(m - m_new)
        p_t = jnp.exp(s_t - m_new)
        l_sc[...] = alpha * l_sc[...] + jnp.sum(p_t, axis=0, keepdims=True)
        acc_sc[...] = column(alpha) * acc_sc[...] + _dot(p_t.T.astype(BF16), vals)
        m_sc[...] = m_new

    def scores_t(kc, kp):
        return (_dot_nt(kc, qlat) + _dot_nt(kp, qpe)) * MLA_SCALE

    for grp in range(pg // PAGE_GROUP):
        for i in range(PAGE_GROUP):
            sl = slice(i * PAGE_SIZE, (i + 1) * PAGE_SIZE)
            kc_sc[sl, :] = ckv_refs[grp * PAGE_GROUP + i][...].astype(BF16)
            kp_sc[sl, :] = kpe_refs[grp * PAGE_GROUP + i][...].astype(BF16)
        kc = kc_sc[...]
        update(scores_t(kc, kp_sc[...]), kc)

    @pl.when(g == n_steps - 1)
    def _():
        pad = PAGE_SIZE - dec_seq
        ck = jnp.concatenate([nckv_ref[...], jnp.zeros((pad, MLA_KV_LORA), F32)], axis=0).astype(BF16)
        kp = jnp.concatenate([nkpe_ref[...], jnp.zeros((pad, MLA_ROPE), F32)], axis=0).astype(BF16)
        s_t = scores_t(ck, kp)
        k_t = lax.broadcasted_iota(I32, s_t.shape, 0)
        q_t = lax.broadcasted_iota(I32, s_t.shape, 1) // MLA_HEADS
        update(jnp.where(k_t <= q_t, s_t, NEG_INF), ck)
        o_ref[...] = acc_sc[...] / column(l_sc[...])


def _mla_paged_attention(qlat, qpe, new_ckv, new_kpe, cache_ckv, cache_kpe, page_table, layer):
    db, rows, _ = qlat.shape
    dec_seq = new_ckv.shape[1]
    n_pages = page_table.shape[1]
    pg = PAGES_PER_STEP
    assert n_pages % pg == 0
    n_steps = n_pages // pg

    def page_spec(width, i):
        return pl.BlockSpec((None, None, PAGE_SIZE, width), lambda b, g, pt: (layer, pt[b, g * pg + i], 0, 0))

    in_specs = [
        pl.BlockSpec((None, rows, MLA_KV_LORA), lambda b, g, pt: (b, 0, 0)),
        pl.BlockSpec((None, rows, MLA_ROPE), lambda b, g, pt: (b, 0, 0)),
        pl.BlockSpec((None, dec_seq, MLA_KV_LORA), lambda b, g, pt: (b, 0, 0)),
        pl.BlockSpec((None, dec_seq, MLA_ROPE), lambda b, g, pt: (b, 0, 0)),
    ]
    in_specs += [page_spec(MLA_KV_LORA, i) for i in range(pg)]
    in_specs += [page_spec(MLA_ROPE, i) for i in range(pg)]
    grid_spec = pltpu.PrefetchScalarGridSpec(
        num_scalar_prefetch=1,
        grid=(db, n_steps),
        in_specs=in_specs,
        out_specs=pl.BlockSpec((None, rows, MLA_KV_LORA), lambda b, g, pt: (b, 0, 0)),
        scratch_shapes=[pltpu.VMEM((1, rows), F32), pltpu.VMEM((1, rows), F32),
                        pltpu.VMEM((rows, MLA_KV_LORA), F32),
                        pltpu.VMEM((PAGE_GROUP * PAGE_SIZE, MLA_KV_LORA), BF16),
                        pltpu.VMEM((PAGE_GROUP * PAGE_SIZE, MLA_ROPE), BF16)],
    )
    return pl.pallas_call(
        functools.partial(_paged_attn_kernel, n_steps=n_steps, dec_seq=dec_seq),
        grid_spec=grid_spec,
        out_shape=jax.ShapeDtypeStruct((db, rows, MLA_KV_LORA), F32),
        compiler_params=_cparams(("arbitrary", "arbitrary")),
        name="mla_paged_attention",
    )(page_table, qlat, qpe, new_ckv, new_kpe, *([cache_ckv] * pg), *([cache_kpe] * pg))


def _sample_out_kernel(olat_ref, wuv_ref, wo_ref, x_ref, gate_ref, o_ref):
    col_head = lax.broadcasted_iota(I32, (1, MLA_HEADS * MLA_V), 1) // MLA_V
    wuv = wuv_ref[...]
    o = jnp.zeros((olat_ref.shape[0], MLA_HEADS * MLA_V), F32)
    for hd in range(MLA_HEADS):
        lat = olat_ref[:, hd * MLA_KV_LORA:(hd + 1) * MLA_KV_LORA].astype(BF16)
        o = jnp.where(col_head == hd, _dot(lat, wuv), o)
    o_ref[...] = x_ref[...] + gate_ref[...] * _dot(o.astype(BF16), wo_ref[...])


def _mla_sample_out(rows, olat, w_uv, w_o, x, mod):
    return pl.pallas_call(
        _sample_out_kernel,
        grid=(rows.grid,),
        in_specs=[rows.rows(olat.shape[1]), _Rows.full(w_uv.shape), _Rows.full(w_o.shape),
                  rows.rows(D_MODEL), rows.mod(2)],
        out_specs=rows.rows(D_MODEL),
        out_shape=jax.ShapeDtypeStruct((rows.n, D_MODEL), F32),
        compiler_params=_cparams(("arbitrary",)),
        name="mla_sample_out",
    )(olat, w_uv, w_o, x, mod)


def _out_proj_kernel(o_ref, wo_ref, x_ref, gate_ref, y_ref):
    y_ref[...] = x_ref[...] + gate_ref[...] * _dot(o_ref[...], wo_ref[...])


def _out_proj(rows, o, w_o, x, mod):
    return pl.pallas_call(
        _out_proj_kernel,
        grid=(rows.grid,),
        in_specs=[rows.rows(o.shape[1]), _Rows.full(w_o.shape), rows.rows(D_MODEL), rows.mod(2)],
        out_specs=rows.rows(D_MODEL),
        out_shape=jax.ShapeDtypeStruct((rows.n, D_MODEL), F32),
        compiler_params=_cparams(("arbitrary",)),
        name="attn_out_proj",
    )(o, w_o, x, mod)


SWA_NQ = SWA_HEADS * SWA_HEAD_DIM
SWA_NK = SWA_KV_HEADS * SWA_HEAD_DIM
SWA_QKV = SWA_NQ + 2 * SWA_NK
SWA_EXT = SWA_QKV + SWA_NQ + SWA_NK


def _swa_proj_kernel(x_ref, g_ref, sh_ref, sc_ref, cos_ref, sin_ref, w_ref, b_ref,
                     q_ref, kb_ref, vb_ref, k_ref, v_ref):
    h = _modulate(x_ref[...], g_ref[...], sh_ref[...], sc_ref[...]).astype(BF16)
    z = _dot(h, w_ref[...]) + b_ref[...]
    cos = cos_ref[...]
    sin = sin_ref[...]
    for t in range((SWA_NQ + SWA_NK) // LANES):
        sl = slice(t * LANES, (t + 1) * LANES)
        sw = slice(SWA_QKV + t * LANES, SWA_QKV + (t + 1) * LANES)
        r = z[:, sl] * cos + z[:, sw] * sin
        if t < SWA_NQ // LANES:
            q_ref[:, sl] = (r * SWA_SCALE).astype(BF16)
        else:
            ks = slice(t * LANES - SWA_NQ, (t + 1) * LANES - SWA_NQ)
            k_ref[:, ks] = r
            kb_ref[:, ks] = r.astype(BF16)
    v = z[:, SWA_NQ + SWA_NK:SWA_QKV]
    v_ref[...] = v
    vb_ref[...] = v.astype(BF16)


def _swa_project(rows, x, ln_g, mod, tabs, w):
    n = rows.n
    full = _Rows.full
    return pl.pallas_call(
        _swa_proj_kernel,
        grid=(rows.grid,),
        in_specs=[rows.rows(D_MODEL), full((1, D_MODEL)), rows.mod(0), rows.mod(1),
                  rows.pos(LANES), rows.pos(LANES), full((D_MODEL, SWA_EXT)), full((1, SWA_EXT))],
        out_specs=[rows.rows(SWA_NQ), rows.rows(SWA_NK), rows.rows(SWA_NK), rows.rows(SWA_NK),
                   rows.rows(SWA_NK)],
        out_shape=[
            jax.ShapeDtypeStruct((n, SWA_NQ), BF16),
            jax.ShapeDtypeStruct((n, SWA_NK), BF16),
            jax.ShapeDtypeStruct((n, SWA_NK), BF16),
            jax.ShapeDtypeStruct((n, SWA_NK), F32),
            jax.ShapeDtypeStruct((n, SWA_NK), F32),
        ],
        compiler_params=_cparams(("arbitrary",)),
        name="swa_project",
    )(x, ln_g, mod, mod, tabs["cos"], tabs["sin"], w["w_qkv"], w["b_qkv"])


def _swa_core(q_all, k_all, v_all, mask, sinks_ref, o_ref):
    lane = lax.broadcasted_iota(I32, (1, LANES), 1)
    lo = lane < SWA_HEAD_DIM
    zero = jnp.zeros((), BF16)
    for kh in range(SWA_KV_HEADS):
        tile = kh // 2
        k_t = k_all[:, tile * LANES:(tile + 1) * LANES]
        v_t = v_all[:, tile * LANES:(tile + 1) * LANES]
        k_r = pltpu.roll(k_t, SWA_HEAD_DIM, 1)
        v_r = pltpu.roll(v_t, SWA_HEAD_DIM, 1)
        for pair in range(SWA_GROUP // 2):
            q_t = q_all[:, (kh * 2 + pair) * LANES:(kh * 2 + pair + 1) * LANES]
            out = None
            for half in range(2):
                keep = lo if half == 0 else jnp.logical_not(lo)
                kx = k_t if (kh % 2) == half else k_r
                vx = v_t if (kh % 2) == half else v_r
                s = _dot_nt(jnp.where(keep, q_t, zero), kx)
                s = jnp.where(mask, s, NEG_INF)
                sink = sinks_ref[kh * SWA_GROUP + 2 * pair + half]
                m = jnp.maximum(jnp.max(s, axis=-1, keepdims=True), sink)
                p = jnp.exp(s - m)
                l = jnp.sum(p, axis=-1, keepdims=True) + jnp.exp(sink - m)
                o = _dot((p / l).astype(BF16), jnp.where(keep, vx, zero))
                out = o if out is None else out + o
            o_ref[:, (kh * 2 + pair) * LANES:(kh * 2 + pair + 1) * LANES] = out.astype(o_ref.dtype)


def _swa_prompt_kernel(sinks_ref, q_ref, kc_ref, kp_ref, vc_ref, vp_ref, o_ref, *, tq):
    i = pl.program_id(1)
    k_all = jnp.concatenate([kp_ref[...], kc_ref[...]], axis=0)
    v_all = jnp.concatenate([vp_ref[...], vc_ref[...]], axis=0)
    tk = tq + WINDOW
    q_pos = i * tq + lax.broadcasted_iota(I32, (tq, tk), 0)
    k_pos = i * tq - WINDOW + lax.broadcasted_iota(I32, (tq, tk), 1)
    mask = (k_pos >= 0) & (k_pos <= q_pos) & (q_pos - k_pos < WINDOW)
    _swa_core(q_ref[...], k_all, v_all, mask, sinks_ref, o_ref)


def _swa_prompt_attention(q, kb, vb, sinks, batch, seq):
    tq = SWA_Q_BLOCK
    r = tq // WINDOW
    q3 = q.reshape(batch, seq, SWA_NQ)
    k3 = kb.reshape(batch, seq, SWA_NK)
    v3 = vb.reshape(batch, seq, SWA_NK)
    cur = pl.BlockSpec((None, tq, SWA_NK), lambda b, i: (b, i, 0))
    prev = pl.BlockSpec((None, WINDOW, SWA_NK), lambda b, i: (b, jnp.maximum(i * r - 1, 0), 0))
    out = pl.pallas_call(
        functools.partial(_swa_prompt_kernel, tq=tq),
        grid=(batch, seq // tq),
        in_specs=[pl.BlockSpec(memory_space=pltpu.SMEM),
                  pl.BlockSpec((None, tq, SWA_NQ), lambda b, i: (b, i, 0)), cur, prev, cur, prev],
        out_specs=pl.BlockSpec((None, tq, SWA_NQ), lambda b, i: (b, i, 0)),
        out_shape=jax.ShapeDtypeStruct((batch, seq, SWA_NQ), BF16),
        compiler_params=_cparams(("arbitrary", "arbitrary")),
        name="swa_prompt_attention",
    )(sinks, q3, k3, k3, v3, v3)
    return out.reshape(batch * seq, SWA_NQ)


def _swa_sample_kernel(sinks_ref, q_ref, kbuf_ref, vbuf_ref, kn_ref, vn_ref, o_ref, o_sc, *, n_buf, dec_seq):
    pad = jnp.zeros((BF16_ROWS - dec_seq, SWA_NK), F32)
    k_all = jnp.concatenate([kbuf_ref[...], kn_ref[...], pad], axis=0).astype(BF16)
    v_all = jnp.concatenate([vbuf_ref[...], vn_ref[...], pad], axis=0).astype(BF16)
    tk = n_buf + BF16_ROWS
    t = lax.broadcasted_iota(I32, (BF16_ROWS, tk), 0)
    c = lax.broadcasted_iota(I32, (BF16_ROWS, tk), 1)
    in_buf = c < n_buf
    mask = ((in_buf & ((n_buf + t - c) < WINDOW))
            | (jnp.logical_not(in_buf) & ((c - n_buf) <= jnp.minimum(t, dec_seq - 1))))
    q = jnp.concatenate([q_ref[...].astype(F32), jnp.zeros((BF16_ROWS - dec_seq, SWA_NQ), F32)], axis=0)
    _swa_core(q.astype(BF16), k_all, v_all, mask, sinks_ref, o_sc)
    o_ref[...] = o_sc[:dec_seq, :].astype(o_ref.dtype)


def _swa_sample_attention(q, kn, vn, buf_k, buf_v, sinks, db, dec_seq):
    n_buf = buf_k.shape[1]
    spec = lambda rws, w: pl.BlockSpec((None, rws, w), lambda b: (b, 0, 0))
    out = pl.pallas_call(
        functools.partial(_swa_sample_kernel, n_buf=n_buf, dec_seq=dec_seq),
        grid=(db,),
        in_specs=[pl.BlockSpec(memory_space=pltpu.SMEM), spec(dec_seq, SWA_NQ), spec(n_buf, SWA_NK),
                  spec(n_buf, SWA_NK), spec(dec_seq, SWA_NK), spec(dec_seq, SWA_NK)],
        out_specs=spec(dec_seq, SWA_NQ),
        out_shape=jax.ShapeDtypeStruct((db, dec_seq, SWA_NQ), BF16),
        scratch_shapes=[pltpu.VMEM((BF16_ROWS, SWA_NQ), F32)],
        compiler_params=_cparams(("arbitrary",)),
        name="swa_sample_attention",
    )(sinks, q.reshape(db, dec_seq, SWA_NQ), buf_k, buf_v,
      kn.reshape(db, dec_seq, SWA_NK), vn.reshape(db, dec_seq, SWA_NK))
    return out.reshape(db * dec_seq, SWA_NQ)


def _peer_candidate_tables(width_tokens):
    k = PEER_TOPK
    flat, valid = [], []
    for i in range(8):
        width = 16 if i == 0 else 8
        for j in range(width):
            flat.append(i * k + j)
            valid.append((i + 1) * (j + 1) <= k)
    for i in range(8, 16):
        flat.append(i * k)
        valid.append(True)
    flat = np.broadcast_to(np.asarray(flat, np.int32)[:, None], (len(flat), width_tokens))
    pen = np.where(np.asarray(valid), 0.0, -np.inf).astype(np.float32)
    return np.ascontiguousarray(flat), np.ascontiguousarray(np.broadcast_to(pen[:, None], flat.shape))


def _extract_top(s, row_id, k, payload=None):
    big = jnp.int32(2 ** 30)
    vals, ids = [], []
    for _ in range(k):
        m = jnp.max(s, axis=0, keepdims=True)
        sel = jnp.min(jnp.where(s == m, row_id, big), axis=0, keepdims=True)
        hit = row_id == sel
        if payload is not None:
            ids.append(jnp.max(jnp.where(hit, payload, -1), axis=0, keepdims=True))
        else:
            ids.append(sel)
        vals.append(m)
        s = jnp.where(hit, -jnp.inf, s)
    return jnp.concatenate(vals, axis=0), jnp.concatenate(ids, axis=0)


def _peer_route_kernel(x_ref, g_ref, sh_ref, sc_ref, wq_ref, sk_ref, flat_ref, pen_ref,
                       h_ref, e_ref, gate_ref, qs_sc, sv_sc, si_sc, pe_sc, pg_sc, *, tb):
    k = PEER_TOPK
    h = _modulate(x_ref[...], g_ref[...], sh_ref[...], sc_ref[...]).astype(BF16)
    h_ref[...] = h
    q = _dot(h, wq_ref[...])
    n_groups = 2 * PEER_HEADS
    for grp in range(n_groups):
        qs_sc[grp] = q[:, grp * PEER_HALF:(grp + 1) * PEER_HALF].astype(BF16)
    key_id = lax.broadcasted_iota(I32, (PEER_N_KEYS, tb), 0)

    def first_level(grp, carry):
        s = _dot_nt(sk_ref[grp], qs_sc[grp])
        vals, ids = _extract_top(s, key_id, k)
        sv_sc[grp] = vals
        si_sc[grp] = ids
        return carry

    lax.fori_loop(0, n_groups, first_level, 0, unroll=ROUTE_UNROLL)

    def second_level(hd, carry):
        sa, sb = sv_sc[2 * hd], sv_sc[2 * hd + 1]
        ia, ib = si_sc[2 * hd], si_sc[2 * hd + 1]
        cs, ce = [], []
        for i in range(8):
            width = 16 if i == 0 else 8
            cs.append(sa[i:i + 1] + sb[:width])
            ce.append(ia[i:i + 1] * PEER_N_KEYS + ib[:width])
        cs.append(sa[8:] + sb[:1])
        ce.append(ia[8:] * PEER_N_KEYS + ib[:1])
        cand = jnp.concatenate(cs, axis=0) + pen_ref[...]
        cand_e = jnp.concatenate(ce, axis=0)
        best, experts = _extract_top(cand, flat_ref[...], k, payload=cand_e)
        ex = jnp.exp(best - best[:1])
        gates = ex / jnp.sum(ex, axis=0, keepdims=True)
        pick = pl.ds(pl.multiple_of(hd * k, k), k)
        pe_sc[pick, :] = experts
        pg_sc[pick, :] = gates
        return carry

    lax.fori_loop(0, PEER_HEADS, second_level, 0, unroll=ROUTE_UNROLL)
    e_ref[...] = pe_sc[...].T
    gate_ref[...] = pg_sc[...].T


def _peer_route(rows, x, ln_g, mod, w):
    n, tb = rows.n, rows.tb
    flat, pen = _peer_candidate_tables(tb)
    full = _Rows.full
    n_groups = 2 * PEER_HEADS
    return pl.pallas_call(
        functools.partial(_peer_route_kernel, tb=tb),
        grid=(rows.grid,),
        in_specs=[rows.rows(D_MODEL), full((1, D_MODEL)), rows.mod(3), rows.mod(4),
                  full(w["w_q"].shape), full(w["sub_keys"].shape), full(flat.shape), full(pen.shape)],
        out_specs=[rows.rows(D_MODEL), rows.rows(PEER_PICKS), rows.rows(PEER_PICKS)],
        out_shape=[jax.ShapeDtypeStruct((n, D_MODEL), BF16),
                   jax.ShapeDtypeStruct((n, PEER_PICKS), I32),
                   jax.ShapeDtypeStruct((n, PEER_PICKS), F32)],
        scratch_shapes=[pltpu.VMEM((n_groups, tb, PEER_HALF), BF16),
                        pltpu.VMEM((n_groups, PEER_TOPK, tb), F32),
                        pltpu.VMEM((n_groups, PEER_TOPK, tb), I32),
                        pltpu.VMEM((PEER_PICKS, tb), I32),
                        pltpu.VMEM((PEER_PICKS, tb), F32)],
        compiler_params=_cparams(("arbitrary",)),
        name="peer_route",
    )(x, ln_g, mod, mod, w["w_q"], w["sub_keys"], jnp.asarray(flat), jnp.asarray(pen))


def _peer_expert_kernel(h_ref, e_ref, gate_ref, u_ref, v_ref, x_ref, g2_ref, o_ref,
                        w_sc, p_sc, acc_sc, *, tb, n_chunks):
    c = pl.program_id(1)
    pitch = GATE_TILE_PITCH

    @pl.when(c == 0)
    def _():
        acc_sc[...] = jnp.zeros(acc_sc.shape, F32)
        key_id = lax.broadcasted_iota(I32, (PEER_N_KEYS, PEER_PICKS), 0)

        def token(t, carry):
            e = e_ref[pl.ds(t, 1), :]
            g = gate_ref[pl.ds(t, 1), :]
            g_hi = g.astype(BF16).astype(F32)
            g_lo = (g - g_hi).astype(BF16).astype(F32)
            hit_a = key_id == (e >> 7)
            hit_b = key_id == (e & (PEER_N_KEYS - 1))
            one_a = jnp.where(hit_a, 1.0, 0.0).astype(BF16)
            lhs = jnp.concatenate([one_a, one_a], axis=1)
            rhs = jnp.concatenate([jnp.where(hit_b, g_hi, 0.0).astype(BF16),
                                   jnp.where(hit_b, g_lo, 0.0).astype(BF16)], axis=1)
            w_sc[pl.ds(pl.multiple_of(t * pitch, SUBLANES), PEER_N_KEYS), :] = _dot_nt(lhs, rhs)
            return carry

        lax.fori_loop(0, tb, token, 0, unroll=TOKEN_UNROLL)

    h = h_ref[...]
    for pair in range(EXPERT_CHUNK_A // 2):
        z = _dot_nt(h, u_ref[pair * 2 * PEER_N_KEYS:(pair + 1) * 2 * PEER_N_KEYS, :])
        for half in range(2):
            al = 2 * pair + half
            sl = slice(al * PEER_N_KEYS, (al + 1) * PEER_N_KEYS)
            w = w_sc[pl.ds(c * EXPERT_CHUNK_A + al, tb, stride=pitch), :]
            p_sc[:, sl] = (w * _gelu(z[:, half * PEER_N_KEYS:(half + 1) * PEER_N_KEYS])).astype(BF16)
    acc_sc[...] += _dot(p_sc[...], v_ref[...])

    @pl.when(c == n_chunks - 1)
    def _():
        o_ref[...] = x_ref[...] + g2_ref[...] * acc_sc[...]


def _peer_experts(rows, h, experts, gates, u, v, layer, x, mod):
    n, tb = rows.n, rows.tb
    ce = EXPERT_CHUNK_A * PEER_N_KEYS
    n_chunks = u.shape[1] // ce
    bps = rows.bps
    if bps is None:
        gate_spec = pl.BlockSpec((tb, D_MODEL), lambda i, c: (i, 5))
    else:
        gate_spec = pl.BlockSpec((None, 1, D_MODEL), lambda i, c: (i // bps, 0, 5))
    tok = lambda w: pl.BlockSpec((tb, w), lambda i, c: (i, 0))
    tok1 = lambda w: pl.BlockSpec((tb, w), lambda i, c: (i, 0), pipeline_mode=pl.Buffered(1))
    return pl.pallas_call(
        functools.partial(_peer_expert_kernel, tb=tb, n_chunks=n_chunks),
        grid=(rows.grid, n_chunks),
        in_specs=[tok1(D_MODEL), tok(PEER_PICKS), tok(PEER_PICKS),
                  pl.BlockSpec((None, ce, D_MODEL), lambda i, c: (layer, c, 0)),
                  pl.BlockSpec((None, ce, D_MODEL), lambda i, c: (layer, c, 0)),
                  tok1(D_MODEL), gate_spec],
        out_specs=tok1(D_MODEL),
        out_shape=jax.ShapeDtypeStruct((n, D_MODEL), F32),
        scratch_shapes=[pltpu.VMEM((tb * GATE_TILE_PITCH, LANES), F32),
                        pltpu.VMEM((tb, ce), BF16),
                        pltpu.VMEM((tb, D_MODEL), F32)],
        compiler_params=_cparams(("arbitrary", "arbitrary")),
        name="peer_experts",
    )(h, experts, gates, u, v, x, mod)


def _final_norm_kernel(x_ref, g_ref, o_ref):
    o_ref[...] = _rms(x_ref[...], g_ref[...])


def _final_norm(rows, x, g):
    return pl.pallas_call(
        _final_norm_kernel,
        grid=(rows.grid,),
        in_specs=[rows.rows(D_MODEL), _Rows.full((1, D_MODEL))],
        out_specs=rows.rows(D_MODEL),
        out_shape=jax.ShapeDtypeStruct((rows.n, D_MODEL), F32),
        compiler_params=_cparams(("arbitrary",)),
        name="final_norm",
    )(x, g)


def _pair_swap(w, half):
    return jnp.concatenate([-w[..., half:2 * half], w[..., :half]], axis=-1)


def _rope_cos_sin(pos, rot_dim):
    half = rot_dim // 2
    inv_freq = ROPE_THETA ** (-jnp.arange(half, dtype=F32) * 2.0 / rot_dim)
    ang = pos.astype(F32)[:, None] * inv_freq[None, :]
    return jnp.cos(ang), jnp.sin(ang)


def _mla_tables(pos):
    cos, sin = _rope_cos_sin(pos, MLA_ROPE)
    n = pos.shape[0]
    ones = jnp.ones((n, MLA_NOPE), F32)
    zeros_n = jnp.zeros((n, MLA_NOPE), F32)
    pad = jnp.zeros((n, MLA_HEAD_PAD - MLA_NOPE - MLA_ROPE), F32)
    return {
        "c32": jnp.concatenate([cos, cos], axis=1),
        "s32": jnp.concatenate([sin, sin], axis=1),
        "c128": jnp.concatenate([ones, cos, cos, pad], axis=1),
        "s128": jnp.concatenate([zeros_n, sin, sin, pad], axis=1),
    }


def _swa_tables(pos):
    cos, sin = _rope_cos_sin(pos, SWA_ROT)
    n = pos.shape[0]
    rest = SWA_HEAD_DIM - SWA_ROT
    c = jnp.concatenate([cos, cos, jnp.ones((n, rest), F32)], axis=1)
    s = jnp.concatenate([sin, sin, jnp.zeros((n, rest), F32)], axis=1)
    reps = LANES // SWA_HEAD_DIM
    return {"cos": jnp.tile(c, (1, reps)), "sin": jnp.tile(s, (1, reps))}


def _mla_weights(w_in, q_norm, kv_norm, w_uq, w_uk, w_uv, w_o):
    d = w_in.shape[0]
    half = MLA_ROPE // 2
    kpe_cols = w_in[:, MLA_Q_LORA + MLA_KV_LORA:]
    w_in_ext = jnp.zeros((d, MLA_IN_EXT), F32)
    w_in_ext = w_in_ext.at[:, :MLA_KPE_COL + MLA_ROPE].set(w_in)
    w_in_ext = w_in_ext.at[:, MLA_KPE_SWAP_COL:MLA_KPE_SWAP_COL + MLA_ROPE].set(_pair_swap(kpe_cols, half))
    uq = w_uq.reshape(MLA_Q_LORA, MLA_HEADS, MLA_NOPE + MLA_ROPE)
    uq_pad = jnp.zeros((MLA_Q_LORA, MLA_HEADS, MLA_HEAD_PAD), F32).at[:, :, :MLA_NOPE + MLA_ROPE].set(uq)
    uq_swap = jnp.zeros((MLA_Q_LORA, MLA_HEADS, MLA_HEAD_PAD), F32)
    uq_swap = uq_swap.at[:, :, MLA_NOPE:MLA_NOPE + MLA_ROPE].set(_pair_swap(uq[:, :, MLA_NOPE:], half))
    uk_pad = jnp.zeros((MLA_KV_LORA, MLA_HEADS, MLA_HEAD_PAD), F32).at[:, :, :MLA_NOPE].set(w_uk)
    e_pe = jnp.zeros((MLA_ROPE, MLA_HEADS, MLA_HEAD_PAD), F32)
    e_pe = e_pe.at[jnp.arange(MLA_ROPE), :, MLA_NOPE + jnp.arange(MLA_ROPE)].set(1.0)
    ukt = jnp.zeros((MLA_HEADS, MLA_HEAD_PAD, MLA_KV_LORA), F32)
    ukt = ukt.at[:, :MLA_NOPE, :].set(jnp.transpose(w_uk, (1, 2, 0)))
    hp = MLA_HEADS * MLA_HEAD_PAD
    return {
        "w_in": w_in_ext.astype(BF16),
        "q_norm": q_norm.reshape(1, -1), "kv_norm": kv_norm.reshape(1, -1),
        "w_uq": uq_pad.reshape(MLA_Q_LORA, hp).astype(BF16),
        "w_uq_swap": uq_swap.reshape(MLA_Q_LORA, hp).astype(BF16),
        "w_uk": uk_pad.reshape(MLA_KV_LORA, hp).astype(BF16),
        "e_pe": e_pe.reshape(MLA_ROPE, hp).astype(BF16),
        "w_uvt": w_uv.reshape(MLA_KV_LORA, MLA_HEADS * MLA_V).T.astype(BF16),
        "w_ukt": ukt.astype(BF16),
        "w_uv": w_uv.reshape(MLA_KV_LORA, MLA_HEADS * MLA_V).astype(BF16),
        "w_o": w_o.astype(BF16),
    }


def _swa_weights(w_qkv, b_qkv, w_o):
    half = SWA_ROT // 2
    n_rot_heads = SWA_HEADS + SWA_KV_HEADS

    def swap_cols(w):
        lead = w.shape[:-1]
        wh = w[..., :SWA_NQ + SWA_NK].reshape(lead + (n_rot_heads, SWA_HEAD_DIM))
        sw = jnp.concatenate([_pair_swap(wh[..., :SWA_ROT], half),
                              jnp.zeros(lead + (n_rot_heads, SWA_HEAD_DIM - SWA_ROT), F32)], axis=-1)
        return sw.reshape(lead + (SWA_NQ + SWA_NK,))

    w_ext = jnp.concatenate([w_qkv, swap_cols(w_qkv)], axis=-1)
    b_ext = jnp.concatenate([b_qkv, swap_cols(b_qkv)], axis=-1)
    return {"w_qkv": w_ext.astype(BF16), "b_qkv": b_ext.reshape(1, -1), "w_o": w_o.astype(BF16)}


def kernel(x_prompt, x_sample, cache_mla_ckv, cache_mla_kpe, cache_swa_k, cache_swa_v, page_table,
           c_prompt, c_sample, ln1_g, ln2_g, w_mod, b_mod,
           mla_w_in, mla_q_norm, mla_kv_norm, mla_w_uq, mla_w_uk, mla_w_uv, mla_w_o,
           swa_w_qkv, swa_b_qkv, swa_sinks, swa_w_o,
           peer_w_q, peer_sub_keys, peer_u, peer_v, final_g):
    batch, seq, d = x_prompt.shape
    db, dec_seq, _ = x_sample.shape
    depth = w_mod.shape[0]
    past_len = page_table.shape[1] * PAGE_SIZE
    n_p, n_s = batch * seq, db * dec_seq

    rows_p = _Rows(n_p, min(ROW_BLOCK, seq), seq_len=seq)
    rows_s = _Rows(n_s, n_s)
    route_p = _Rows(n_p, ROUTE_BLOCK, seq_len=seq)
    route_s = _Rows(n_s, min(ROUTE_BLOCK, n_s))
    expert_p = _Rows(n_p, EXPERT_ROW_BLOCK, seq_len=seq)
    expert_s = _Rows(n_s, min(EXPERT_ROW_BLOCK, n_s))

    pos_p = jnp.arange(seq)
    pos_s = jnp.tile(past_len + jnp.arange(dec_seq), db)
    mla_tab_p, mla_tab_s = _mla_tables(pos_p), _mla_tables(pos_s)
    swa_tab_p, swa_tab_s = _swa_tables(pos_p), _swa_tables(pos_s)

    m_all = _modulation_all(jnp.concatenate([c_prompt, c_sample], axis=0), w_mod, b_mod)
    u_b = peer_u.astype(BF16)
    v_b = peer_v.astype(BF16)

    x_p = x_prompt.reshape(n_p, d)
    x_s = x_sample.reshape(n_s, d)
    ckv_p, kpe_p, ckv_s, kpe_s = [], [], [], []
    swk_p, swv_p, swk_s, swv_s = [], [], [], []
    n_buf = cache_swa_k.shape[2]

    for i in range(depth):
        j = i // 2
        mod_p = m_all[i, :batch].reshape(batch, 1, 6 * d)
        mod_s = jnp.repeat(m_all[i, batch:], dec_seq, axis=0)
        g1 = ln1_g[i].reshape(1, d)
        g2 = ln2_g[i].reshape(1, d)
        if i % 2 == 0:
            w = _mla_weights(mla_w_in[j], mla_q_norm[j], mla_kv_norm[j], mla_w_uq[j], mla_w_uk[j],
                             mla_w_uv[j], mla_w_o[j])
            ckv, kpe, q, k, vt = _mla_project(rows_p, x_p, g1, mod_p, mla_tab_p, w)
            o = _mla_attention(q, k, vt, batch, seq)
            x_p = _out_proj(rows_p, o, w["w_o"], x_p, mod_p)
            ckv_p.append(ckv.reshape(batch, seq, MLA_KV_LORA))
            kpe_p.append(kpe.reshape(batch, seq, MLA_ROPE))

            ckv, kpe, q, _, _ = _mla_project(rows_s, x_s, g1, mod_s, mla_tab_s, w)
            qlat = _mla_absorb_queries(q, w["w_ukt"]).reshape(db, dec_seq * MLA_HEADS, MLA_KV_LORA)
            qpe = q.reshape(n_s, MLA_HEADS, MLA_HEAD_PAD)[:, :, MLA_NOPE:MLA_NOPE + MLA_ROPE]
            qpe = qpe.reshape(db, dec_seq * MLA_HEADS, MLA_ROPE)
            ckv3 = ckv.reshape(db, dec_seq, MLA_KV_LORA)
            kpe3 = kpe.reshape(db, dec_seq, MLA_ROPE)
            olat = _mla_paged_attention(qlat, qpe, ckv3, kpe3, cache_mla_ckv, cache_mla_kpe, page_table, j)
            olat = olat.reshape(n_s, MLA_HEADS * MLA_KV_LORA)
            x_s = _mla_sample_out(rows_s, olat, w["w_uv"], w["w_o"], x_s, mod_s)
            ckv_s.append(ckv3)
            kpe_s.append(kpe3)
        else:
            w = _swa_weights(swa_w_qkv[j], swa_b_qkv[j], swa_w_o[j])
            q, kb, vb, k, v = _swa_project(rows_p, x_p, g1, mod_p, swa_tab_p, w)
            o = _swa_prompt_attention(q, kb, vb, swa_sinks[j], batch, seq)
            x_p = _out_proj(rows_p, o, w["w_o"], x_p, mod_p)
            nb = min(WINDOW, seq)
            swk_p.append(k.reshape(batch, seq, SWA_KV_HEADS, SWA_HEAD_DIM)[:, seq - nb:])
            swv_p.append(v.reshape(batch, seq, SWA_KV_HEADS, SWA_HEAD_DIM)[:, seq - nb:])

            q, kb, vb, k, v = _swa_project(rows_s, x_s, g1, mod_s, swa_tab_s, w)
            buf_k = cache_swa_k[j].reshape(db, n_buf, SWA_NK)
            buf_v = cache_swa_v[j].reshape(db, n_buf, SWA_NK)
            o = _swa_sample_attention(q, k, v, buf_k, buf_v, swa_sinks[j], db, dec_seq)
            x_s = _out_proj(rows_s, o, w["w_o"], x_s, mod_s)
            k_all = jnp.concatenate([buf_k, k.reshape(db, dec_seq, SWA_NK)], axis=1)[:, -n_buf:]
            v_all = jnp.concatenate([buf_v, v.reshape(db, dec_seq, SWA_NK)], axis=1)[:, -n_buf:]
            swk_s.append(k_all.reshape(db, n_buf, SWA_KV_HEADS, SWA_HEAD_DIM))
            swv_s.append(v_all.reshape(db, n_buf, SWA_KV_HEADS, SWA_HEAD_DIM))

        pw = {"w_q": peer_w_q[i].astype(BF16),
              "sub_keys": peer_sub_keys[i].reshape(2 * PEER_HEADS, PEER_N_KEYS, PEER_HALF).astype(BF16)}
        h, experts, gates = _peer_route(route_p, x_p, g2, mod_p, pw)
        x_p = _peer_experts(expert_p, h, experts, gates, u_b, v_b, i, x_p, mod_p)
        h, experts, gates = _peer_route(route_s, x_s, g2, mod_s, pw)
        x_s = _peer_experts(expert_s, h, experts, gates, u_b, v_b, i, x_s, mod_s)

    fg = final_g.reshape(1, d)
    y_p = _final_norm(rows_p, x_p, fg).reshape(batch, seq, d)
    y_s = _final_norm(rows_s, x_s, fg).reshape(db, dec_seq, d)
    return (y_p, y_s,
            jnp.stack(ckv_p), jnp.stack(kpe_p), jnp.stack(ckv_s), jnp.stack(kpe_s),
            jnp.stack(swk_p), jnp.stack(swv_p), jnp.stack(swk_s), jnp.stack(swv_s))
```

```python
import functools

import numpy as np
import jax
import jax.numpy as jnp
from jax import lax
from jax.experimental import pallas as pl
from jax.experimental.pallas import tpu as pltpu

F32 = jnp.float32
BF16 = jnp.bfloat16
I32 = jnp.int32

D_MODEL = 1024
PAGE_SIZE = 128
ROPE_THETA = 500000.0
NORM_EPS = 1e-6
NEG_INF = -1e30

MLA_HEADS = 16
MLA_Q_LORA = 384
MLA_KV_LORA = 256
MLA_NOPE = 64
MLA_ROPE = 32
MLA_V = 64
MLA_SCALE = (MLA_NOPE + MLA_ROPE) ** -0.5

SWA_HEADS = 16
SWA_KV_HEADS = 4
SWA_GROUP = SWA_HEADS // SWA_KV_HEADS
SWA_HEAD_DIM = D_MODEL // SWA_HEADS
SWA_ROT = SWA_HEAD_DIM // 4
SWA_SCALE = SWA_HEAD_DIM ** -0.5
WINDOW = 128

PEER_HEADS = 8
PEER_N_KEYS = 128
PEER_TOPK = 16
PEER_HALF = 128
PEER_PICKS = PEER_HEADS * PEER_TOPK

LANES = 128
SUBLANES = 8
BF16_ROWS = 16
VMEM_LIMIT_BYTES = 56 * 1024 * 1024

ROW_BLOCK = 512
EXPERT_ROW_BLOCK = 512
TOKEN_UNROLL = 32
ROUTE_BLOCK = 256
ROUTE_UNROLL = 4
MLA_ATTN_BLOCK = 512
SWA_Q_BLOCK = 256
PAGES_PER_STEP = 16
PAGE_GROUP = 16
EXPERT_CHUNK_A = 8
GATE_TILE_PITCH = 136
MOD_COL_BLOCK = 1536


def _cparams(sem):
    return pltpu.CompilerParams(dimension_semantics=sem, vmem_limit_bytes=VMEM_LIMIT_BYTES)


def _dot(a, b):
    return jnp.dot(a, b, preferred_element_type=F32)


def _dot_nt(a, b):
    return lax.dot_general(a, b, (((1,), (1,)), ((), ())), preferred_element_type=F32)


def _rms(x, g):
    return x * lax.rsqrt(jnp.mean(x * x, axis=-1, keepdims=True) + NORM_EPS) * g


def _gelu(x):
    return 0.5 * x * (1.0 + lax.erf(x * np.float32(2.0 ** -0.5)))


def _modulate(x, g, shift, scale):
    return _rms(x, g) * (1.0 + scale) + shift


class _Rows:
    def __init__(self, n, tb, seq_len=None):
        assert n % tb == 0
        self.n, self.tb, self.grid = n, tb, n // tb
        self.bps = None
        if seq_len is not None:
            assert seq_len % tb == 0
            self.bps = seq_len // tb

    def rows(self, width):
        return pl.BlockSpec((self.tb, width), lambda i: (i, 0))

    def mod(self, k):
        if self.bps is None:
            return pl.BlockSpec((self.tb, D_MODEL), lambda i: (i, k))
        bps = self.bps
        return pl.BlockSpec((None, 1, D_MODEL), lambda i: (i // bps, 0, k))

    def pos(self, width):
        if self.bps is None:
            return pl.BlockSpec((self.tb, width), lambda i: (i, 0))
        bps = self.bps
        return pl.BlockSpec((self.tb, width), lambda i: (i % bps, 0))

    @staticmethod
    def full(shape):
        nd = len(shape)
        return pl.BlockSpec(shape, lambda i: (0,) * nd)


def _mod_kernel(c_ref, w_ref, b_ref, o_ref):
    c = c_ref[...]
    a = (c * jax.nn.sigmoid(c)).astype(BF16)
    o_ref[...] = _dot(a, w_ref[...].astype(BF16)) + b_ref[...]


def _modulation_all(c_all, w_mod, b_mod):
    depth, d, n6 = w_mod.shape
    nc = c_all.shape[0]
    nb = n6 // MOD_COL_BLOCK
    return pl.pallas_call(
        _mod_kernel,
        grid=(depth, nb),
        in_specs=[
            pl.BlockSpec((nc, d), lambda l, j: (0, 0)),
            pl.BlockSpec((None, d, MOD_COL_BLOCK), lambda l, j: (l, 0, j)),
            pl.BlockSpec((None, 1, MOD_COL_BLOCK), lambda l, j: (l, 0, j)),
        ],
        out_specs=pl.BlockSpec((None, nc, MOD_COL_BLOCK), lambda l, j: (l, 0, j)),
        out_shape=jax.ShapeDtypeStruct((depth, nc, n6), F32),
        compiler_params=_cparams(("arbitrary", "arbitrary")),
        name="adaln_modulation",
    )(c_all, w_mod, b_mod.reshape(depth, 1, n6))


MLA_IN_EXT = 896
MLA_KPE_COL = 640
MLA_KPE_SWAP_COL = 768
MLA_HEAD_PAD = LANES


def _mla_proj_kernel(x_ref, g_ref, sh_ref, sc_ref, c32_ref, s32_ref, c128_ref, s128_ref,
                     win_ref, qn_ref, kvn_ref, wuq_ref, wuqs_ref, wuk_ref, epe_ref, wuvt_ref,
                     ckv_ref, kpe_ref, q_ref, k_ref, vt_ref):
    h = _modulate(x_ref[...], g_ref[...], sh_ref[...], sc_ref[...]).astype(BF16)
    z = _dot(h, win_ref[...])
    cq = _rms(z[:, :MLA_Q_LORA], qn_ref[...]).astype(BF16)
    ckv = _rms(z[:, MLA_Q_LORA:MLA_Q_LORA + MLA_KV_LORA], kvn_ref[...])
    kpe = (z[:, MLA_KPE_COL:MLA_KPE_COL + MLA_ROPE] * c32_ref[...]
           + z[:, MLA_KPE_SWAP_COL:MLA_KPE_SWAP_COL + MLA_ROPE] * s32_ref[...])
    ckv_ref[...] = ckv
    kpe_ref[...] = kpe
    ckv_b = ckv.astype(BF16)
    q = _dot(cq, wuq_ref[...])
    qs = _dot(cq, wuqs_ref[...])
    cos = c128_ref[...]
    sin = s128_ref[...]
    for hd in range(MLA_HEADS):
        sl = slice(hd * MLA_HEAD_PAD, (hd + 1) * MLA_HEAD_PAD)
        q_ref[:, sl] = (q[:, sl] * cos + qs[:, sl] * sin).astype(BF16)
    k_ref[...] = (_dot(ckv_b, wuk_ref[...]) + _dot(kpe.astype(BF16), epe_ref[...])).astype(BF16)
    vt = _dot_nt(wuvt_ref[...], ckv_b).astype(BF16)
    chunks, _, width = vt_ref.shape
    for c in range(chunks):
        vt_ref[c] = vt[:, c * width:(c + 1) * width]


def _mla_project(rows, x, ln_g, mod, tabs, w):
    n = rows.n
    hp = MLA_HEADS * MLA_HEAD_PAD
    full = _Rows.full
    nv = MLA_HEADS * MLA_V
    chunk = min(MLA_ATTN_BLOCK, rows.tb)
    per_block = rows.tb // chunk
    return pl.pallas_call(
        _mla_proj_kernel,
        grid=(rows.grid,),
        in_specs=[
            rows.rows(D_MODEL), full((1, D_MODEL)), rows.mod(0), rows.mod(1),
            rows.pos(MLA_ROPE), rows.pos(MLA_ROPE), rows.pos(MLA_HEAD_PAD), rows.pos(MLA_HEAD_PAD),
            full((D_MODEL, MLA_IN_EXT)), full((1, MLA_Q_LORA)), full((1, MLA_KV_LORA)),
            full((MLA_Q_LORA, hp)), full((MLA_Q_LORA, hp)), full((MLA_KV_LORA, hp)),
            full((MLA_ROPE, hp)), full((nv, MLA_KV_LORA)),
        ],
        out_specs=[rows.rows(MLA_KV_LORA), rows.rows(MLA_ROPE), rows.rows(hp), rows.rows(hp),
                   pl.BlockSpec((per_block, nv, chunk), lambda i: (i, 0, 0))],
        out_shape=[
            jax.ShapeDtypeStruct((n, MLA_KV_LORA), F32),
            jax.ShapeDtypeStruct((n, MLA_ROPE), F32),
            jax.ShapeDtypeStruct((n, hp), BF16),
            jax.ShapeDtypeStruct((n, hp), BF16),
            jax.ShapeDtypeStruct((n // chunk, nv, chunk), BF16),
        ],
        compiler_params=_cparams(("arbitrary",)),
        name="mla_project",
    )(x, ln_g, mod, mod, tabs["c32"], tabs["s32"], tabs["c128"], tabs["s128"],
      w["w_in"], w["q_norm"], w["kv_norm"], w["w_uq"], w["w_uq_swap"], w["w_uk"], w["e_pe"], w["w_uvt"])


def _mla_attn_kernel(q_ref, k_ref, vt_ref, o_ref, *, tq, tk):
    assert tq == tk
    qi = pl.program_id(2)
    causal = lax.broadcasted_iota(I32, (tk, tq), 0) <= lax.broadcasted_iota(I32, (tk, tq), 1)
    v_row = lax.broadcasted_iota(I32, (LANES, tk), 0)
    qs = [q_ref[:, hh * LANES:(hh + 1) * LANES] for hh in range(2)]
    v_keep = [(v_row // MLA_V) == hh for hh in range(2)]

    def step(j, carry, masked):
        start = pl.multiple_of(j * tk, tk)
        vt = vt_ref[j]
        new = []
        for hh in range(2):
            m, l, acc = carry[hh]
            kb = k_ref[pl.ds(start, tk), hh * LANES:(hh + 1) * LANES]
            s = _dot_nt(kb, qs[hh]) * MLA_SCALE
            if masked:
                s = jnp.where(causal, s, NEG_INF)
            m_new = jnp.maximum(m, jnp.max(s, axis=0, keepdims=True))
            alpha = jnp.exp(m - m_new)
            p = jnp.exp(s - m_new)
            l = alpha * l + jnp.sum(p, axis=0, keepdims=True)
            acc = alpha * acc + _dot(jnp.where(v_keep[hh], vt, jnp.zeros_like(vt)), p.astype(BF16))
            new.append((m_new, l, acc))
        return tuple(new)

    head0 = (jnp.full((1, tq), NEG_INF, F32), jnp.zeros((1, tq), F32), jnp.zeros((LANES, tq), F32))
    carry = lax.fori_loop(0, qi, lambda j, c: step(j, c, False), (head0, head0))
    (_, l0, acc0), (_, l1, acc1) = step(qi, carry, True)
    o_ref[...] = (acc0 / l0 + acc1 / l1).T.astype(BF16)


def _mla_attention(q, k, vt, batch, seq):
    tq = tk = MLA_ATTN_BLOCK
    nq = seq // tq
    q3 = q.reshape(batch, seq, -1)
    k3 = k.reshape(batch, seq, -1)
    out = pl.pallas_call(
        functools.partial(_mla_attn_kernel, tq=tq, tk=tk),
        grid=(batch, MLA_HEADS // 2, nq),
        in_specs=[
            pl.BlockSpec((None, tq, 2 * LANES), lambda b, hp, i: (b, i, hp)),
            pl.BlockSpec((None, seq, 2 * LANES), lambda b, hp, i: (b, 0, hp)),
            pl.BlockSpec((seq // tk, LANES, tk), lambda b, hp, i: (b, hp, 0)),
        ],
        out_specs=pl.BlockSpec((None, tq, LANES), lambda b, hp, i: (b, i, hp)),
        out_shape=jax.ShapeDtypeStruct((batch, seq, MLA_HEADS * MLA_V), BF16),
        compiler_params=_cparams(("arbitrary", "arbitrary", "arbitrary")),
        name="mla_prompt_attention",
    )(q3, k3, vt)
    return out.reshape(batch * seq, MLA_HEADS * MLA_V)


def _qlat_kernel(q_ref, wk_ref, o_ref):
    for hd in range(MLA_HEADS):
        qh = q_ref[:, hd * MLA_HEAD_PAD:(hd + 1) * MLA_HEAD_PAD]
        o_ref[:, hd * MLA_KV_LORA:(hd + 1) * MLA_KV_LORA] = _dot(qh, wk_ref[hd]).astype(BF16)


def _mla_absorb_queries(q, w_ukt):
    n = q.shape[0]
    return pl.pallas_call(
        _qlat_kernel,
        grid=(1,),
        in_specs=[_Rows.full(q.shape), _Rows.full(w_ukt.shape)],
        out_specs=_Rows.full((n, MLA_HEADS * MLA_KV_LORA)),
        out_shape=jax.ShapeDtypeStruct((n, MLA_HEADS * MLA_KV_LORA), BF16),
        compiler_params=_cparams(("arbitrary",)),
        name="mla_absorb_queries",
    )(q, w_ukt)


def _paged_attn_kernel(pt_ref, qlat_ref, qpe_ref, nckv_ref, nkpe_ref, *rest, n_steps, dec_seq):
    pg = PAGES_PER_STEP
    ckv_refs = rest[:pg]
    kpe_refs = rest[pg:2 * pg]
    o_ref = rest[2 * pg]
    m_sc, l_sc, acc_sc, kc_sc, kp_sc = rest[2 * pg + 1:]
    g = pl.program_id(1)
    rows = qlat_ref.shape[0]

    @pl.when(g == 0)
    def _():
        m_sc[...] = jnp.full(m_sc.shape, NEG_INF, F32)
        l_sc[...] = jnp.zeros(l_sc.shape, F32)
        acc_sc[...] = jnp.zeros(acc_sc.shape, F32)

    qlat = qlat_ref[...]
    qpe = qpe_ref[...]

    def column(row_vec):
        return jnp.broadcast_to(row_vec, (LANES, rows)).T[:, :1]

    def update(s_t, vals):
        m = m_sc[...]
        m_new = jnp.maximum(m, jnp.max(s_t, axis=0, keepdims=True))
        alpha = jnp.exp(m - m_new)
        p_t = jnp.exp(s_t - m_new)
        l_sc[...] = alpha * l_sc[...] + jnp.sum(p_t, axis=0, keepdims=True)
        acc_sc[...] = column(alpha) * acc_sc[...] + _dot(p_t.T.astype(BF16), vals)
        m_sc[...] = m_new

    def scores_t(kc, kp):
        return (_dot_nt(kc, qlat) + _dot_nt(kp, qpe)) * MLA_SCALE

    for grp in range(pg // PAGE_GROUP):
        for i in range(PAGE_GROUP):
            sl = slice(i * PAGE_SIZE, (i + 1) * PAGE_SIZE)
            kc_sc[sl, :] = ckv_refs[grp * PAGE_GROUP + i][...].astype(BF16)
            kp_sc[sl, :] = kpe_refs[grp * PAGE_GROUP + i][...].T.astype(BF16)
        kc = kc_sc[...]
        update(scores_t(kc, kp_sc[...]), kc)

    @pl.when(g == n_steps - 1)
    def _():
        pad = PAGE_SIZE - dec_seq
        ck = jnp.concatenate([nckv_ref[...], jnp.zeros((pad, MLA_KV_LORA), F32)], axis=0).astype(BF16)
        kp = jnp.concatenate([nkpe_ref[...], jnp.zeros((pad, MLA_ROPE), F32)], axis=0).astype(BF16)
        s_t = scores_t(ck, kp)
        k_t = lax.broadcasted_iota(I32, s_t.shape, 0)
        q_t = lax.broadcasted_iota(I32, s_t.shape, 1) // MLA_HEADS
        update(jnp.where(k_t <= q_t, s_t, NEG_INF), ck)
        o_ref[...] = acc_sc[...] / column(l_sc[...])


def _mla_paged_attention(qlat, qpe, new_ckv, new_kpe, cache_ckv, cache_kpe, page_table, layer):
    db, rows, _ = qlat.shape
    dec_seq = new_ckv.shape[1]
    n_pages = page_table.shape[1]
    pg = PAGES_PER_STEP
    assert n_pages % pg == 0
    n_steps = n_pages // pg

    def page_spec(shape, i):
        return pl.BlockSpec((None, None) + shape, lambda b, g, pt: (layer, pt[b, g * pg + i], 0, 0))

    cache_kpe_t = jnp.swapaxes(cache_kpe, 2, 3)
    in_specs = [
        pl.BlockSpec((None, rows, MLA_KV_LORA), lambda b, g, pt: (b, 0, 0)),
        pl.BlockSpec((None, rows, MLA_ROPE), lambda b, g, pt: (b, 0, 0)),
        pl.BlockSpec((None, dec_seq, MLA_KV_LORA), lambda b, g, pt: (b, 0, 0)),
        pl.BlockSpec((None, dec_seq, MLA_ROPE), lambda b, g, pt: (b, 0, 0)),
    ]
    in_specs += [page_spec((PAGE_SIZE, MLA_KV_LORA), i) for i in range(pg)]
    in_specs += [page_spec((MLA_ROPE, PAGE_SIZE), i) for i in range(pg)]
    grid_spec = pltpu.PrefetchScalarGridSpec(
        num_scalar_prefetch=1,
        grid=(db, n_steps),
        in_specs=in_specs,
        out_specs=pl.BlockSpec((None, rows, MLA_KV_LORA), lambda b, g, pt: (b, 0, 0)),
        scratch_shapes=[pltpu.VMEM((1, rows), F32), pltpu.VMEM((1, rows), F32),
                        pltpu.VMEM((rows, MLA_KV_LORA), F32),
                        pltpu.VMEM((PAGE_GROUP * PAGE_SIZE, MLA_KV_LORA), BF16),
                        pltpu.VMEM((PAGE_GROUP * PAGE_SIZE, MLA_ROPE), BF16)],
    )
    return pl.pallas_call(
        functools.partial(_paged_attn_kernel, n_steps=n_steps, dec_seq=dec_seq),
        grid_spec=grid_spec,
        out_shape=jax.ShapeDtypeStruct((db, rows, MLA_KV_LORA), F32),
        compiler_params=_cparams(("arbitrary", "arbitrary")),
        name="mla_paged_attention",
    )(page_table, qlat, qpe, new_ckv, new_kpe, *([cache_ckv] * pg), *([cache_kpe_t] * pg))


def _sample_out_kernel(olat_ref, wuv_ref, wo_ref, x_ref, gate_ref, o_ref):
    col_head = lax.broadcasted_iota(I32, (1, MLA_HEADS * MLA_V), 1) // MLA_V
    wuv = wuv_ref[...]
    o = jnp.zeros((olat_ref.shape[0], MLA_HEADS * MLA_V), F32)
    for hd in range(MLA_HEADS):
        lat = olat_ref[:, hd * MLA_KV_LORA:(hd + 1) * MLA_KV_LORA].astype(BF16)
        o = jnp.where(col_head == hd, _dot(lat, wuv), o)
    o_ref[...] = x_ref[...] + gate_ref[...] * _dot(o.astype(BF16), wo_ref[...])


def _mla_sample_out(rows, olat, w_uv, w_o, x, mod):
    return pl.pallas_call(
        _sample_out_kernel,
        grid=(rows.grid,),
        in_specs=[rows.rows(olat.shape[1]), _Rows.full(w_uv.shape), _Rows.full(w_o.shape),
                  rows.rows(D_MODEL), rows.mod(2)],
        out_specs=rows.rows(D_MODEL),
        out_shape=jax.ShapeDtypeStruct((rows.n, D_MODEL), F32),
        compiler_params=_cparams(("arbitrary",)),
        name="mla_sample_out",
    )(olat, w_uv, w_o, x, mod)


def _out_proj_kernel(o_ref, wo_ref, x_ref, gate_ref, y_ref):
    y_ref[...] = x_ref[...] + gate_ref[...] * _dot(o_ref[...], wo_ref[...])


def _out_proj(rows, o, w_o, x, mod):
    return pl.pallas_call(
        _out_proj_kernel,
        grid=(rows.grid,),
        in_specs=[rows.rows(o.shape[1]), _Rows.full(w_o.shape), rows.rows(D_MODEL), rows.mod(2)],
        out_specs=rows.rows(D_MODEL),
        out_shape=jax.ShapeDtypeStruct((rows.n, D_MODEL), F32),
        compiler_params=_cparams(("arbitrary",)),
        name="attn_out_proj",
    )(o, w_o, x, mod)


SWA_NQ = SWA_HEADS * SWA_HEAD_DIM
SWA_NK = SWA_KV_HEADS * SWA_HEAD_DIM
SWA_QKV = SWA_NQ + 2 * SWA_NK
SWA_EXT = SWA_QKV + SWA_NQ + SWA_NK


def _swa_proj_kernel(x_ref, g_ref, sh_ref, sc_ref, cos_ref, sin_ref, w_ref, b_ref,
                     q_ref, kb_ref, vb_ref, k_ref, v_ref):
    h = _modulate(x_ref[...], g_ref[...], sh_ref[...], sc_ref[...]).astype(BF16)
    z = _dot(h, w_ref[...]) + b_ref[...]
    cos = cos_ref[...]
    sin = sin_ref[...]
    for t in range((SWA_NQ + SWA_NK) // LANES):
        sl = slice(t * LANES, (t + 1) * LANES)
        sw = slice(SWA_QKV + t * LANES, SWA_QKV + (t + 1) * LANES)
        r = z[:, sl] * cos + z[:, sw] * sin
        if t < SWA_NQ // LANES:
            q_ref[:, sl] = (r * SWA_SCALE).astype(BF16)
        else:
            ks = slice(t * LANES - SWA_NQ, (t + 1) * LANES - SWA_NQ)
            k_ref[:, ks] = r
            kb_ref[:, ks] = r.astype(BF16)
    v = z[:, SWA_NQ + SWA_NK:SWA_QKV]
    v_ref[...] = v
    vb_ref[...] = v.astype(BF16)


def _swa_project(rows, x, ln_g, mod, tabs, w):
    n = rows.n
    full = _Rows.full
    return pl.pallas_call(
        _swa_proj_kernel,
        grid=(rows.grid,),
        in_specs=[rows.rows(D_MODEL), full((1, D_MODEL)), rows.mod(0), rows.mod(1),
                  rows.pos(LANES), rows.pos(LANES), full((D_MODEL, SWA_EXT)), full((1, SWA_EXT))],
        out_specs=[rows.rows(SWA_NQ), rows.rows(SWA_NK), rows.rows(SWA_NK), rows.rows(SWA_NK),
                   rows.rows(SWA_NK)],
        out_shape=[
            jax.ShapeDtypeStruct((n, SWA_NQ), BF16),
            jax.ShapeDtypeStruct((n, SWA_NK), BF16),
            jax.ShapeDtypeStruct((n, SWA_NK), BF16),
            jax.ShapeDtypeStruct((n, SWA_NK), F32),
            jax.ShapeDtypeStruct((n, SWA_NK), F32),
        ],
        compiler_params=_cparams(("arbitrary",)),
        name="swa_project",
    )(x, ln_g, mod, mod, tabs["cos"], tabs["sin"], w["w_qkv"], w["b_qkv"])


def _swa_core(q_all, k_all, v_all, mask, sinks_ref, o_ref):
    lane = lax.broadcasted_iota(I32, (1, LANES), 1)
    lo = lane < SWA_HEAD_DIM
    zero = jnp.zeros((), BF16)
    for kh in range(SWA_KV_HEADS):
        tile = kh // 2
        k_t = k_all[:, tile * LANES:(tile + 1) * LANES]
        v_t = v_all[:, tile * LANES:(tile + 1) * LANES]
        k_r = pltpu.roll(k_t, SWA_HEAD_DIM, 1)
        v_r = pltpu.roll(v_t, SWA_HEAD_DIM, 1)
        for pair in range(SWA_GROUP // 2):
            q_t = q_all[:, (kh * 2 + pair) * LANES:(kh * 2 + pair + 1) * LANES]
            out = None
            for half in range(2):
                keep = lo if half == 0 else jnp.logical_not(lo)
                kx = k_t if (kh % 2) == half else k_r
                vx = v_t if (kh % 2) == half else v_r
                s = _dot_nt(jnp.where(keep, q_t, zero), kx)
                s = jnp.where(mask, s, NEG_INF)
                sink = sinks_ref[kh * SWA_GROUP + 2 * pair + half]
                m = jnp.maximum(jnp.max(s, axis=-1, keepdims=True), sink)
                p = jnp.exp(s - m)
                l = jnp.sum(p, axis=-1, keepdims=True) + jnp.exp(sink - m)
                o = _dot((p / l).astype(BF16), jnp.where(keep, vx, zero))
                out = o if out is None else out + o
            o_ref[:, (kh * 2 + pair) * LANES:(kh * 2 + pair + 1) * LANES] = out.astype(o_ref.dtype)


def _swa_prompt_kernel(sinks_ref, q_ref, kc_ref, kp_ref, vc_ref, vp_ref, o_ref, *, tq):
    i = pl.program_id(1)
    k_all = jnp.concatenate([kp_ref[...], kc_ref[...]], axis=0)
    v_all = jnp.concatenate([vp_ref[...], vc_ref[...]], axis=0)
    tk = tq + WINDOW
    q_pos = i * tq + lax.broadcasted_iota(I32, (tq, tk), 0)
    k_pos = i * tq - WINDOW + lax.broadcasted_iota(I32, (tq, tk), 1)
    mask = (k_pos >= 0) & (k_pos <= q_pos) & (q_pos - k_pos < WINDOW)
    _swa_core(q_ref[...], k_all, v_all, mask, sinks_ref, o_ref)


def _swa_prompt_attention(q, kb, vb, sinks, batch, seq):
    tq = SWA_Q_BLOCK
    r = tq // WINDOW
    q3 = q.reshape(batch, seq, SWA_NQ)
    k3 = kb.reshape(batch, seq, SWA_NK)
    v3 = vb.reshape(batch, seq, SWA_NK)
    cur = pl.BlockSpec((None, tq, SWA_NK), lambda b, i: (b, i, 0))
    prev = pl.BlockSpec((None, WINDOW, SWA_NK), lambda b, i: (b, jnp.maximum(i * r - 1, 0), 0))
    out = pl.pallas_call(
        functools.partial(_swa_prompt_kernel, tq=tq),
        grid=(batch, seq // tq),
        in_specs=[pl.BlockSpec(memory_space=pltpu.SMEM),
                  pl.BlockSpec((None, tq, SWA_NQ), lambda b, i: (b, i, 0)), cur, prev, cur, prev],
        out_specs=pl.BlockSpec((None, tq, SWA_NQ), lambda b, i: (b, i, 0)),
        out_shape=jax.ShapeDtypeStruct((batch, seq, SWA_NQ), BF16),
        compiler_params=_cparams(("arbitrary", "arbitrary")),
        name="swa_prompt_attention",
    )(sinks, q3, k3, k3, v3, v3)
    return out.reshape(batch * seq, SWA_NQ)


def _swa_sample_kernel(sinks_ref, q_ref, kbuf_ref, vbuf_ref, kn_ref, vn_ref, o_ref, o_sc, *, n_buf, dec_seq):
    pad = jnp.zeros((BF16_ROWS - dec_seq, SWA_NK), F32)
    k_all = jnp.concatenate([kbuf_ref[...], kn_ref[...], pad], axis=0).astype(BF16)
    v_all = jnp.concatenate([vbuf_ref[...], vn_ref[...], pad], axis=0).astype(BF16)
    tk = n_buf + BF16_ROWS
    t = lax.broadcasted_iota(I32, (BF16_ROWS, tk), 0)
    c = lax.broadcasted_iota(I32, (BF16_ROWS, tk), 1)
    in_buf = c < n_buf
    mask = ((in_buf & ((n_buf + t - c) < WINDOW))
            | (jnp.logical_not(in_buf) & ((c - n_buf) <= jnp.minimum(t, dec_seq - 1))))
    q = jnp.concatenate([q_ref[...].astype(F32), jnp.zeros((BF16_ROWS - dec_seq, SWA_NQ), F32)], axis=0)
    _swa_core(q.astype(BF16), k_all, v_all, mask, sinks_ref, o_sc)
    o_ref[...] = o_sc[:dec_seq, :].astype(o_ref.dtype)


def _swa_sample_attention(q, kn, vn, buf_k, buf_v, sinks, db, dec_seq):
    n_buf = buf_k.shape[1]
    spec = lambda rws, w: pl.BlockSpec((None, rws, w), lambda b: (b, 0, 0))
    out = pl.pallas_call(
        functools.partial(_swa_sample_kernel, n_buf=n_buf, dec_seq=dec_seq),
        grid=(db,),
        in_specs=[pl.BlockSpec(memory_space=pltpu.SMEM), spec(dec_seq, SWA_NQ), spec(n_buf, SWA_NK),
                  spec(n_buf, SWA_NK), spec(dec_seq, SWA_NK), spec(dec_seq, SWA_NK)],
        out_specs=spec(dec_seq, SWA_NQ),
        out_shape=jax.ShapeDtypeStruct((db, dec_seq, SWA_NQ), BF16),
        scratch_shapes=[pltpu.VMEM((BF16_ROWS, SWA_NQ), F32)],
        compiler_params=_cparams(("arbitrary",)),
        name="swa_sample_attention",
    )(sinks, q.reshape(db, dec_seq, SWA_NQ), buf_k, buf_v,
      kn.reshape(db, dec_seq, SWA_NK), vn.reshape(db, dec_seq, SWA_NK))
    return out.reshape(db * dec_seq, SWA_NQ)


def _peer_candidate_tables(width_tokens):
    k = PEER_TOPK
    flat, valid = [], []
    for i in range(8):
        width = 16 if i == 0 else 8
        for j in range(width):
            flat.append(i * k + j)
            valid.append((i + 1) * (j + 1) <= k)
    for i in range(8, 16):
        flat.append(i * k)
        valid.append(True)
    flat = np.broadcast_to(np.asarray(flat, np.float32)[:, None], (len(flat), width_tokens))
    pen = np.where(np.asarray(valid), 0.0, -np.inf).astype(np.float32)
    return np.ascontiguousarray(flat), np.ascontiguousarray(np.broadcast_to(pen[:, None], flat.shape))


def _extract_top(s, row_id, k, payload=None):
    big = jnp.float32(2 ** 30)
    vals, ids = [], []
    for _ in range(k):
        m = jnp.max(s, axis=0, keepdims=True)
        sel = jnp.min(jnp.where(s == m, row_id, big), axis=0, keepdims=True)
        hit = row_id == sel
        if payload is not None:
            ids.append(jnp.max(jnp.where(hit, payload, -1.0), axis=0, keepdims=True))
        else:
            ids.append(sel)
        vals.append(m)
        s = jnp.where(hit, -jnp.inf, s)
    return jnp.concatenate(vals, axis=0), jnp.concatenate(ids, axis=0)


def _peer_route_kernel(x_ref, g_ref, sh_ref, sc_ref, wq_ref, sk_ref, flat_ref, pen_ref,
                       h_ref, e_ref, gate_ref, qs_sc, sv_sc, si_sc, pe_sc, pg_sc, *, tb):
    k = PEER_TOPK
    h = _modulate(x_ref[...], g_ref[...], sh_ref[...], sc_ref[...]).astype(BF16)
    h_ref[...] = h
    q = _dot(h, wq_ref[...])
    n_groups = 2 * PEER_HEADS
    for grp in range(n_groups):
        qs_sc[grp] = q[:, grp * PEER_HALF:(grp + 1) * PEER_HALF].astype(BF16)
    key_id = lax.broadcasted_iota(I32, (PEER_N_KEYS, tb), 0).astype(F32)

    def first_level(grp, carry):
        s = _dot_nt(sk_ref[grp], qs_sc[grp])
        vals, ids = _extract_top(s, key_id, k)
        sv_sc[grp] = vals
        si_sc[grp] = ids
        return carry

    lax.fori_loop(0, n_groups, first_level, 0, unroll=ROUTE_UNROLL)

    def second_level(hd, carry):
        sa, sb = sv_sc[2 * hd], sv_sc[2 * hd + 1]
        ia, ib = si_sc[2 * hd], si_sc[2 * hd + 1]
        cs, ce = [], []
        for i in range(8):
            width = 16 if i == 0 else 8
            cs.append(sa[i:i + 1] + sb[:width])
            ce.append(ia[i:i + 1] * PEER_N_KEYS + ib[:width])
        cs.append(sa[8:] + sb[:1])
        ce.append(ia[8:] * PEER_N_KEYS + ib[:1])
        cand = jnp.concatenate(cs, axis=0) + pen_ref[...]
        cand_e = jnp.concatenate(ce, axis=0)
        best, experts = _extract_top(cand, flat_ref[...], k, payload=cand_e)
        ex = jnp.exp(best - best[:1])
        gates = ex / jnp.sum(ex, axis=0, keepdims=True)
        pick = pl.ds(pl.multiple_of(hd * k, k), k)
        pe_sc[pick, :] = experts
        pg_sc[pick, :] = gates
        return carry

    lax.fori_loop(0, PEER_HEADS, second_level, 0, unroll=ROUTE_UNROLL)
    e_ref[...] = pe_sc[...].T.astype(I32)
    gate_ref[...] = pg_sc[...].T


def _peer_route(rows, x, ln_g, mod, w):
    n, tb = rows.n, rows.tb
    flat, pen = _peer_candidate_tables(tb)
    full = _Rows.full
    n_groups = 2 * PEER_HEADS
    return pl.pallas_call(
        functools.partial(_peer_route_kernel, tb=tb),
        grid=(rows.grid,),
        in_specs=[rows.rows(D_MODEL), full((1, D_MODEL)), rows.mod(3), rows.mod(4),
                  full(w["w_q"].shape), full(w["sub_keys"].shape), full(flat.shape), full(pen.shape)],
        out_specs=[rows.rows(D_MODEL), rows.rows(PEER_PICKS), rows.rows(PEER_PICKS)],
        out_shape=[jax.ShapeDtypeStruct((n, D_MODEL), BF16),
                   jax.ShapeDtypeStruct((n, PEER_PICKS), I32),
                   jax.ShapeDtypeStruct((n, PEER_PICKS), F32)],
        scratch_shapes=[pltpu.VMEM((n_groups, tb, PEER_HALF), BF16),
                        pltpu.VMEM((n_groups, PEER_TOPK, tb), F32),
                        pltpu.VMEM((n_groups, PEER_TOPK, tb), F32),
                        pltpu.VMEM((PEER_PICKS, tb), F32),
                        pltpu.VMEM((PEER_PICKS, tb), F32)],
        compiler_params=_cparams(("arbitrary",)),
        name="peer_route",
    )(x, ln_g, mod, mod, w["w_q"], w["sub_keys"], jnp.asarray(flat), jnp.asarray(pen))


def _peer_expert_kernel(h_ref, e_ref, gate_ref, u_ref, v_ref, x_ref, g2_ref, o_ref,
                        w_sc, p_sc, acc_sc, *, tb, n_chunks):
    c = pl.program_id(1)
    pitch = GATE_TILE_PITCH

    @pl.when(c == 0)
    def _():
        acc_sc[...] = jnp.zeros(acc_sc.shape, F32)
        key_id = lax.broadcasted_iota(I32, (PEER_N_KEYS, PEER_PICKS), 0)

        def token(t, carry):
            e = e_ref[pl.ds(t, 1), :]
            g = gate_ref[pl.ds(t, 1), :]
            g_hi = g.astype(BF16).astype(F32)
            g_lo = (g - g_hi).astype(BF16).astype(F32)
            hit_a = key_id == (e >> 7)
            hit_b = key_id == (e & (PEER_N_KEYS - 1))
            one_a = jnp.where(hit_a, 1.0, 0.0).astype(BF16)
            lhs = jnp.concatenate([one_a, one_a], axis=1)
            rhs = jnp.concatenate([jnp.where(hit_b, g_hi, 0.0).astype(BF16),
                                   jnp.where(hit_b, g_lo, 0.0).astype(BF16)], axis=1)
            w_sc[pl.ds(pl.multiple_of(t * pitch, SUBLANES), PEER_N_KEYS), :] = _dot_nt(lhs, rhs)
            return carry

        lax.fori_loop(0, tb, token, 0, unroll=TOKEN_UNROLL)

    h = h_ref[...]
    for pair in range(EXPERT_CHUNK_A // 2):
        z = _dot_nt(h, u_ref[pair * 2 * PEER_N_KEYS:(pair + 1) * 2 * PEER_N_KEYS, :])
        for half in range(2):
            al = 2 * pair + half
            sl = slice(al * PEER_N_KEYS, (al + 1) * PEER_N_KEYS)
            w = w_sc[pl.ds(c * EXPERT_CHUNK_A + al, tb, stride=pitch), :]
            p_sc[:, sl] = (w * _gelu(z[:, half * PEER_N_KEYS:(half + 1) * PEER_N_KEYS])).astype(BF16)
    acc_sc[...] += _dot(p_sc[...], v_ref[...])

    @pl.when(c == n_chunks - 1)
    def _():
        o_ref[...] = x_ref[...] + g2_ref[...] * acc_sc[...]


def _peer_experts(rows, h, experts, gates, u, v, layer, x, mod):
    n, tb = rows.n, rows.tb
    ce = EXPERT_CHUNK_A * PEER_N_KEYS
    n_chunks = u.shape[1] // ce
    bps = rows.bps
    if bps is None:
        gate_spec = pl.BlockSpec((tb, D_MODEL), lambda i, c: (i, 5))
    else:
        gate_spec = pl.BlockSpec((None, 1, D_MODEL), lambda i, c: (i // bps, 0, 5))
    tok = lambda w: pl.BlockSpec((tb, w), lambda i, c: (i, 0))
    tok1 = lambda w: pl.BlockSpec((tb, w), lambda i, c: (i, 0), pipeline_mode=pl.Buffered(1))
    return pl.pallas_call(
        functools.partial(_peer_expert_kernel, tb=tb, n_chunks=n_chunks),
        grid=(rows.grid, n_chunks),
        in_specs=[tok1(D_MODEL), tok(PEER_PICKS), tok(PEER_PICKS),
                  pl.BlockSpec((None, ce, D_MODEL), lambda i, c: (layer, c, 0)),
                  pl.BlockSpec((None, ce, D_MODEL), lambda i, c: (layer, c, 0)),
                  tok1(D_MODEL), gate_spec],
        out_specs=tok1(D_MODEL),
        out_shape=jax.ShapeDtypeStruct((n, D_MODEL), F32),
        scratch_shapes=[pltpu.VMEM((tb * GATE_TILE_PITCH, LANES), F32),
                        pltpu.VMEM((tb, ce), BF16),
                        pltpu.VMEM((tb, D_MODEL), F32)],
        compiler_params=_cparams(("arbitrary", "arbitrary")),
        name="peer_experts",
    )(h, experts, gates, u, v, x, mod)


def _final_norm_kernel(x_ref, g_ref, o_ref):
    o_ref[...] = _rms(x_ref[...], g_ref[...])


def _final_norm(rows, x, g):
    return pl.pallas_call(
        _final_norm_kernel,
        grid=(rows.grid,),
        in_specs=[rows.rows(D_MODEL), _Rows.full((1, D_MODEL))],
        out_specs=rows.rows(D_MODEL),
        out_shape=jax.ShapeDtypeStruct((rows.n, D_MODEL), F32),
        compiler_params=_cparams(("arbitrary",)),
        name="final_norm",
    )(x, g)


def _pair_swap(w, half):
    return jnp.concatenate([-w[..., half:2 * half], w[..., :half]], axis=-1)


def _rope_cos_sin(pos, rot_dim):
    half = rot_dim // 2
    inv_freq = ROPE_THETA ** (-jnp.arange(half, dtype=F32) * 2.0 / rot_dim)
    ang = pos.astype(F32)[:, None] * inv_freq[None, :]
    return jnp.cos(ang), jnp.sin(ang)


def _mla_tables(pos):
    cos, sin = _rope_cos_sin(pos, MLA_ROPE)
    n = pos.shape[0]
    ones = jnp.ones((n, MLA_NOPE), F32)
    zeros_n = jnp.zeros((n, MLA_NOPE), F32)
    pad = jnp.zeros((n, MLA_HEAD_PAD - MLA_NOPE - MLA_ROPE), F32)
    return {
        "c32": jnp.concatenate([cos, cos], axis=1),
        "s32": jnp.concatenate([sin, sin], axis=1),
        "c128": jnp.concatenate([ones, cos, cos, pad], axis=1),
        "s128": jnp.concatenate([zeros_n, sin, sin, pad], axis=1),
    }


def _swa_tables(pos):
    cos, sin = _rope_cos_sin(pos, SWA_ROT)
    n = pos.shape[0]
    rest = SWA_HEAD_DIM - SWA_ROT
    c = jnp.concatenate([cos, cos, jnp.ones((n, rest), F32)], axis=1)
    s = jnp.concatenate([sin, sin, jnp.zeros((n, rest), F32)], axis=1)
    reps = LANES // SWA_HEAD_DIM
    return {"cos": jnp.tile(c, (1, reps)), "sin": jnp.tile(s, (1, reps))}


def _mla_weights(w_in, q_norm, kv_norm, w_uq, w_uk, w_uv, w_o):
    d = w_in.shape[0]
    half = MLA_ROPE // 2
    kpe_cols = w_in[:, MLA_Q_LORA + MLA_KV_LORA:]
    w_in_ext = jnp.zeros((d, MLA_IN_EXT), F32)
    w_in_ext = w_in_ext.at[:, :MLA_KPE_COL + MLA_ROPE].set(w_in)
    w_in_ext = w_in_ext.at[:, MLA_KPE_SWAP_COL:MLA_KPE_SWAP_COL + MLA_ROPE].set(_pair_swap(kpe_cols, half))
    uq = w_uq.reshape(MLA_Q_LORA, MLA_HEADS, MLA_NOPE + MLA_ROPE)
    uq_pad = jnp.zeros((MLA_Q_LORA, MLA_HEADS, MLA_HEAD_PAD), F32).at[:, :, :MLA_NOPE + MLA_ROPE].set(uq)
    uq_swap = jnp.zeros((MLA_Q_LORA, MLA_HEADS, MLA_HEAD_PAD), F32)
    uq_swap = uq_swap.at[:, :, MLA_NOPE:MLA_NOPE + MLA_ROPE].set(_pair_swap(uq[:, :, MLA_NOPE:], half))
    uk_pad = jnp.zeros((MLA_KV_LORA, MLA_HEADS, MLA_HEAD_PAD), F32).at[:, :, :MLA_NOPE].set(w_uk)
    e_pe = jnp.zeros((MLA_ROPE, MLA_HEADS, MLA_HEAD_PAD), F32)
    e_pe = e_pe.at[jnp.arange(MLA_ROPE), :, MLA_NOPE + jnp.arange(MLA_ROPE)].set(1.0)
    ukt = jnp.zeros((MLA_HEADS, MLA_HEAD_PAD, MLA_KV_LORA), F32)
    ukt = ukt.at[:, :MLA_NOPE, :].set(jnp.transpose(w_uk, (1, 2, 0)))
    hp = MLA_HEADS * MLA_HEAD_PAD
    return {
        "w_in": w_in_ext.astype(BF16),
        "q_norm": q_norm.reshape(1, -1), "kv_norm": kv_norm.reshape(1, -1),
        "w_uq": uq_pad.reshape(MLA_Q_LORA, hp).astype(BF16),
        "w_uq_swap": uq_swap.reshape(MLA_Q_LORA, hp).astype(BF16),
        "w_uk": uk_pad.reshape(MLA_KV_LORA, hp).astype(BF16),
        "e_pe": e_pe.reshape(MLA_ROPE, hp).astype(BF16),
        "w_uvt": w_uv.reshape(MLA_KV_LORA, MLA_HEADS * MLA_V).T.astype(BF16),
        "w_ukt": ukt.astype(BF16),
        "w_uv": w_uv.reshape(MLA_KV_LORA, MLA_HEADS * MLA_V).astype(BF16),
        "w_o": w_o.astype(BF16),
    }


def _swa_weights(w_qkv, b_qkv, w_o):
    half = SWA_ROT // 2
    n_rot_heads = SWA_HEADS + SWA_KV_HEADS

    def swap_cols(w):
        lead = w.shape[:-1]
        wh = w[..., :SWA_NQ + SWA_NK].reshape(lead + (n_rot_heads, SWA_HEAD_DIM))
        sw = jnp.concatenate([_pair_swap(wh[..., :SWA_ROT], half),
                              jnp.zeros(lead + (n_rot_heads, SWA_HEAD_DIM - SWA_ROT), F32)], axis=-1)
        return sw.reshape(lead + (SWA_NQ + SWA_NK,))

    w_ext = jnp.concatenate([w_qkv, swap_cols(w_qkv)], axis=-1)
    b_ext = jnp.concatenate([b_qkv, swap_cols(b_qkv)], axis=-1)
    return {"w_qkv": w_ext.astype(BF16), "b_qkv": b_ext.reshape(1, -1), "w_o": w_o.astype(BF16)}


def kernel(x_prompt, x_sample, cache_mla_ckv, cache_mla_kpe, cache_swa_k, cache_swa_v, page_table,
           c_prompt, c_sample, ln1_g, ln2_g, w_mod, b_mod,
           mla_w_in, mla_q_norm, mla_kv_norm, mla_w_uq, mla_w_uk, mla_w_uv, mla_w_o,
           swa_w_qkv, swa_b_qkv, swa_sinks, swa_w_o,
           peer_w_q, peer_sub_keys, peer_u, peer_v, final_g):
    batch, seq, d = x_prompt.shape
    db, dec_seq, _ = x_sample.shape
    depth = w_mod.shape[0]
    past_len = page_table.shape[1] * PAGE_SIZE
    n_p, n_s = batch * seq, db * dec_seq

    rows_p = _Rows(n_p, min(ROW_BLOCK, seq), seq_len=seq)
    rows_s = _Rows(n_s, n_s)
    route_p = _Rows(n_p, ROUTE_BLOCK, seq_len=seq)
    route_s = _Rows(n_s, min(ROUTE_BLOCK, n_s))
    expert_p = _Rows(n_p, EXPERT_ROW_BLOCK, seq_len=seq)
    expert_s = _Rows(n_s, min(EXPERT_ROW_BLOCK, n_s))

    pos_p = jnp.arange(seq)
    pos_s = jnp.tile(past_len + jnp.arange(dec_seq), db)
    mla_tab_p, mla_tab_s = _mla_tables(pos_p), _mla_tables(pos_s)
    swa_tab_p, swa_tab_s = _swa_tables(pos_p), _swa_tables(pos_s)

    m_all = _modulation_all(jnp.concatenate([c_prompt, c_sample], axis=0), w_mod, b_mod)
    u_b = peer_u.astype(BF16)
    v_b = peer_v.astype(BF16)

    x_p = x_prompt.reshape(n_p, d)
    x_s = x_sample.reshape(n_s, d)
    ckv_p, kpe_p, ckv_s, kpe_s = [], [], [], []
    swk_p, swv_p, swk_s, swv_s = [], [], [], []
    n_buf = cache_swa_k.shape[2]

    for i in range(depth):
        j = i // 2
        mod_p = m_all[i, :batch].reshape(batch, 1, 6 * d)
        mod_s = jnp.repeat(m_all[i, batch:], dec_seq, axis=0)
        g1 = ln1_g[i].reshape(1, d)
        g2 = ln2_g[i].reshape(1, d)
        if i % 2 == 0:
            w = _mla_weights(mla_w_in[j], mla_q_norm[j], mla_kv_norm[j], mla_w_uq[j], mla_w_uk[j],
                             mla_w_uv[j], mla_w_o[j])
            ckv, kpe, q, k, vt = _mla_project(rows_p, x_p, g1, mod_p, mla_tab_p, w)
            o = _mla_attention(q, k, vt, batch, seq)
            x_p = _out_proj(rows_p, o, w["w_o"], x_p, mod_p)
            ckv_p.append(ckv.reshape(batch, seq, MLA_KV_LORA))
            kpe_p.append(kpe.reshape(batch, seq, MLA_ROPE))

            ckv, kpe, q, _, _ = _mla_project(rows_s, x_s, g1, mod_s, mla_tab_s, w)
            qlat = _mla_absorb_queries(q, w["w_ukt"]).reshape(db, dec_seq * MLA_HEADS, MLA_KV_LORA)
            qpe = q.reshape(n_s, MLA_HEADS, MLA_HEAD_PAD)[:, :, MLA_NOPE:MLA_NOPE + MLA_ROPE]
            qpe = qpe.reshape(db, dec_seq * MLA_HEADS, MLA_ROPE)
            ckv3 = ckv.reshape(db, dec_seq, MLA_KV_LORA)
            kpe3 = kpe.reshape(db, dec_seq, MLA_ROPE)
            olat = _mla_paged_attention(qlat, qpe, ckv3, kpe3, cache_mla_ckv, cache_mla_kpe, page_table, j)
            olat = olat.reshape(n_s, MLA_HEADS * MLA_KV_LORA)
            x_s = _mla_sample_out(rows_s, olat, w["w_uv"], w["w_o"], x_s, mod_s)
            ckv_s.append(ckv3)
            kpe_s.append(kpe3)
        else:
            w = _swa_weights(swa_w_qkv[j], swa_b_qkv[j], swa_w_o[j])
            q, kb, vb, k, v = _swa_project(rows_p, x_p, g1, mod_p, swa_tab_p, w)
            o = _swa_prompt_attention(q, kb, vb, swa_sinks[j], batch, seq)
            x_p = _out_proj(rows_p, o, w["w_o"], x_p, mod_p)
            nb = min(WINDOW, seq)
            swk_p.append(k.reshape(batch, seq, SWA_KV_HEADS, SWA_HEAD_DIM)[:, seq - nb:])
            swv_p.append(v.reshape(batch, seq, SWA_KV_HEADS, SWA_HEAD_DIM)[:, seq - nb:])

            q, kb, vb, k, v = _swa_project(rows_s, x_s, g1, mod_s, swa_tab_s, w)
            buf_k = cache_swa_k[j].reshape(db, n_buf, SWA_NK)
            buf_v = cache_swa_v[j].reshape(db, n_buf, SWA_NK)
            o = _swa_sample_attention(q, k, v, buf_k, buf_v, swa_sinks[j], db, dec_seq)
            x_s = _out_proj(rows_s, o, w["w_o"], x_s, mod_s)
            k_all = jnp.concatenate([buf_k, k.reshape(db, dec_seq, SWA_NK)], axis=1)[:, -n_buf:]
            v_all = jnp.concatenate([buf_v, v.reshape(db, dec_seq, SWA_NK)], axis=1)[:, -n_buf:]
            swk_s.append(k_all.reshape(db, n_buf, SWA_KV_HEADS, SWA_HEAD_DIM))
            swv_s.append(v_all.reshape(db, n_buf, SWA_KV_HEADS, SWA_HEAD_DIM))

        pw = {"w_q": peer_w_q[i].astype(BF16),
              "sub_keys": peer_sub_keys[i].reshape(2 * PEER_HEADS, PEER_N_KEYS, PEER_HALF).astype(BF16)}
        h, experts, gates = _peer_route(route_p, x_p, g2, mod_p, pw)
        x_p = _peer_experts(expert_p, h, experts, gates, u_b, v_b, i, x_p, mod_p)
        h, experts, gates = _peer_route(route_s, x_s, g2, mod_s, pw)
        x_s = _peer_experts(expert_s, h, experts, gates, u_b, v_b, i, x_s, mod_s)

    fg = final_g.reshape(1, d)
    y_p = _final_norm(rows_p, x_p, fg).reshape(batch, seq, d)
    y_s = _final_norm(rows_s, x_s, fg).reshape(db, dec_seq, d)
    return (y_p, y_s,
            jnp.stack(ckv_p), jnp.stack(kpe_p), jnp.stack(ckv_s), jnp.stack(kpe_s),
            jnp.stack(swk_p), jnp.stack(swv_p), jnp.stack(swk_s), jnp.stack(swv_s))
```

```python
import functools

import numpy as np
import jax
import jax.numpy as jnp
from jax import lax
from jax.experimental import pallas as pl
from jax.experimental.pallas import tpu as pltpu

F32 = jnp.float32
BF16 = jnp.bfloat16
I32 = jnp.int32

D_MODEL = 1024
PAGE_SIZE = 128
ROPE_THETA = 500000.0
NORM_EPS = 1e-6
NEG_INF = -1e30

MLA_HEADS = 16
MLA_Q_LORA = 384
MLA_KV_LORA = 256
MLA_NOPE = 64
MLA_ROPE = 32
MLA_V = 64
MLA_SCALE = (MLA_NOPE + MLA_ROPE) ** -0.5

SWA_HEADS = 16
SWA_KV_HEADS = 4
SWA_GROUP = SWA_HEADS // SWA_KV_HEADS
SWA_HEAD_DIM = D_MODEL // SWA_HEADS
SWA_ROT = SWA_HEAD_DIM // 4
SWA_SCALE = SWA_HEAD_DIM ** -0.5
WINDOW = 128

PEER_HEADS = 8
PEER_N_KEYS = 128
PEER_TOPK = 16
PEER_HALF = 128
PEER_PICKS = PEER_HEADS * PEER_TOPK

LANES = 128
SUBLANES = 8
BF16_ROWS = 16
VMEM_LIMIT_BYTES = 56 * 1024 * 1024

ROW_BLOCK = 512
EXPERT_ROW_BLOCK = 512
TOKEN_UNROLL = 32
ROUTE_BLOCK = 256
ROUTE_UNROLL = 4
MLA_ATTN_BLOCK = 512
SWA_Q_BLOCK = 256
PAGES_PER_STEP = 16
PAGE_GROUP = 16
EXPERT_CHUNK_A = 8
GATE_TILE_PITCH = 72
MOD_COL_BLOCK = 1536


def _cparams(sem):
    return pltpu.CompilerParams(dimension_semantics=sem, vmem_limit_bytes=VMEM_LIMIT_BYTES)


def _dot(a, b):
    return jnp.dot(a, b, preferred_element_type=F32)


def _dot_nt(a, b):
    return lax.dot_general(a, b, (((1,), (1,)), ((), ())), preferred_element_type=F32)


def _rms(x, g):
    return x * lax.rsqrt(jnp.mean(x * x, axis=-1, keepdims=True) + NORM_EPS) * g


def _gelu(x):
    return 0.5 * x * (1.0 + lax.erf(x * np.float32(2.0 ** -0.5)))


def _modulate(x, g, shift, scale):
    return _rms(x, g) * (1.0 + scale) + shift


class _Rows:
    def __init__(self, n, tb, seq_len=None):
        assert n % tb == 0
        self.n, self.tb, self.grid = n, tb, n // tb
        self.bps = None
        if seq_len is not None:
            assert seq_len % tb == 0
            self.bps = seq_len // tb

    def rows(self, width):
        return pl.BlockSpec((self.tb, width), lambda i: (i, 0))

    def mod(self, k):
        if self.bps is None:
            return pl.BlockSpec((self.tb, D_MODEL), lambda i: (i, k))
        bps = self.bps
        return pl.BlockSpec((None, 1, D_MODEL), lambda i: (i // bps, 0, k))

    def pos(self, width):
        if self.bps is None:
            return pl.BlockSpec((self.tb, width), lambda i: (i, 0))
        bps = self.bps
        return pl.BlockSpec((self.tb, width), lambda i: (i % bps, 0))

    @staticmethod
    def full(shape):
        nd = len(shape)
        return pl.BlockSpec(shape, lambda i: (0,) * nd)


def _mod_kernel(c_ref, w_ref, b_ref, o_ref):
    c = c_ref[...]
    a = (c * jax.nn.sigmoid(c)).astype(BF16)
    o_ref[...] = _dot(a, w_ref[...].astype(BF16)) + b_ref[...]


def _modulation_all(c_all, w_mod, b_mod):
    depth, d, n6 = w_mod.shape
    nc = c_all.shape[0]
    nb = n6 // MOD_COL_BLOCK
    return pl.pallas_call(
        _mod_kernel,
        grid=(depth, nb),
        in_specs=[
            pl.BlockSpec((nc, d), lambda l, j: (0, 0)),
            pl.BlockSpec((None, d, MOD_COL_BLOCK), lambda l, j: (l, 0, j)),
            pl.BlockSpec((None, 1, MOD_COL_BLOCK), lambda l, j: (l, 0, j)),
        ],
        out_specs=pl.BlockSpec((None, nc, MOD_COL_BLOCK), lambda l, j: (l, 0, j)),
        out_shape=jax.ShapeDtypeStruct((depth, nc, n6), F32),
        compiler_params=_cparams(("arbitrary", "arbitrary")),
        name="adaln_modulation",
    )(c_all, w_mod, b_mod.reshape(depth, 1, n6))


MLA_IN_EXT = 896
MLA_KPE_COL = 640
MLA_KPE_SWAP_COL = 768
MLA_HEAD_PAD = LANES


def _mla_proj_kernel(x_ref, g_ref, sh_ref, sc_ref, c32_ref, s32_ref, c128_ref, s128_ref,
                     win_ref, qn_ref, kvn_ref, wuq_ref, wuqs_ref, wuk_ref, epe_ref, wuvt_ref,
                     ckv_ref, kpe_ref, q_ref, k_ref, vt_ref):
    h = _modulate(x_ref[...], g_ref[...], sh_ref[...], sc_ref[...]).astype(BF16)
    z = _dot(h, win_ref[...])
    cq = _rms(z[:, :MLA_Q_LORA], qn_ref[...]).astype(BF16)
    ckv = _rms(z[:, MLA_Q_LORA:MLA_Q_LORA + MLA_KV_LORA], kvn_ref[...])
    kpe = (z[:, MLA_KPE_COL:MLA_KPE_COL + MLA_ROPE] * c32_ref[...]
           + z[:, MLA_KPE_SWAP_COL:MLA_KPE_SWAP_COL + MLA_ROPE] * s32_ref[...])
    ckv_ref[...] = ckv
    kpe_ref[...] = kpe
    ckv_b = ckv.astype(BF16)
    q = _dot(cq, wuq_ref[...])
    qs = _dot(cq, wuqs_ref[...])
    cos = c128_ref[...]
    sin = s128_ref[...]
    for hd in range(MLA_HEADS):
        sl = slice(hd * MLA_HEAD_PAD, (hd + 1) * MLA_HEAD_PAD)
        q_ref[:, sl] = (q[:, sl] * cos + qs[:, sl] * sin).astype(BF16)
    k_ref[...] = (_dot(ckv_b, wuk_ref[...]) + _dot(kpe.astype(BF16), epe_ref[...])).astype(BF16)
    vt = _dot_nt(wuvt_ref[...], ckv_b).astype(BF16)
    chunks, _, width = vt_ref.shape
    for c in range(chunks):
        vt_ref[c] = vt[:, c * width:(c + 1) * width]


def _mla_project(rows, x, ln_g, mod, tabs, w):
    n = rows.n
    hp = MLA_HEADS * MLA_HEAD_PAD
    full = _Rows.full
    nv = MLA_HEADS * MLA_V
    chunk = min(MLA_ATTN_BLOCK, rows.tb)
    per_block = rows.tb // chunk
    return pl.pallas_call(
        _mla_proj_kernel,
        grid=(rows.grid,),
        in_specs=[
            rows.rows(D_MODEL), full((1, D_MODEL)), rows.mod(0), rows.mod(1),
            rows.pos(MLA_ROPE), rows.pos(MLA_ROPE), rows.pos(MLA_HEAD_PAD), rows.pos(MLA_HEAD_PAD),
            full((D_MODEL, MLA_IN_EXT)), full((1, MLA_Q_LORA)), full((1, MLA_KV_LORA)),
            full((MLA_Q_LORA, hp)), full((MLA_Q_LORA, hp)), full((MLA_KV_LORA, hp)),
            full((MLA_ROPE, hp)), full((nv, MLA_KV_LORA)),
        ],
        out_specs=[rows.rows(MLA_KV_LORA), rows.rows(MLA_ROPE), rows.rows(hp), rows.rows(hp),
                   pl.BlockSpec((per_block, nv, chunk), lambda i: (i, 0, 0))],
        out_shape=[
            jax.ShapeDtypeStruct((n, MLA_KV_LORA), F32),
            jax.ShapeDtypeStruct((n, MLA_ROPE), F32),
            jax.ShapeDtypeStruct((n, hp), BF16),
            jax.ShapeDtypeStruct((n, hp), BF16),
            jax.ShapeDtypeStruct((n // chunk, nv, chunk), BF16),
        ],
        compiler_params=_cparams(("arbitrary",)),
        name="mla_project",
    )(x, ln_g, mod, mod, tabs["c32"], tabs["s32"], tabs["c128"], tabs["s128"],
      w["w_in"], w["q_norm"], w["kv_norm"], w["w_uq"], w["w_uq_swap"], w["w_uk"], w["e_pe"], w["w_uvt"])


def _mla_attn_kernel(q_ref, k_ref, vt_ref, o_ref, *, tq, tk):
    assert tq == tk
    qi = pl.program_id(2)
    causal = lax.broadcasted_iota(I32, (tk, tq), 0) <= lax.broadcasted_iota(I32, (tk, tq), 1)
    v_row = lax.broadcasted_iota(I32, (LANES, tk), 0)
    qs = [q_ref[:, hh * LANES:(hh + 1) * LANES] for hh in range(2)]
    v_keep = [(v_row // MLA_V) == hh for hh in range(2)]

    def step(j, carry, masked):
        start = pl.multiple_of(j * tk, tk)
        vt = vt_ref[j]
        new = []
        for hh in range(2):
            m, l, acc = carry[hh]
            kb = k_ref[pl.ds(start, tk), hh * LANES:(hh + 1) * LANES]
            s = _dot_nt(kb, qs[hh]) * MLA_SCALE
            if masked:
                s = jnp.where(causal, s, NEG_INF)
            m_new = jnp.maximum(m, jnp.max(s, axis=0, keepdims=True))
            alpha = jnp.exp(m - m_new)
            p = jnp.exp(s - m_new)
            l = alpha * l + jnp.sum(p, axis=0, keepdims=True)
            acc = alpha * acc + _dot(jnp.where(v_keep[hh], vt, jnp.zeros_like(vt)), p.astype(BF16))
            new.append((m_new, l, acc))
        return tuple(new)

    head0 = (jnp.full((1, tq), NEG_INF, F32), jnp.zeros((1, tq), F32), jnp.zeros((LANES, tq), F32))
    carry = lax.fori_loop(0, qi, lambda j, c: step(j, c, False), (head0, head0))
    (_, l0, acc0), (_, l1, acc1) = step(qi, carry, True)
    o_ref[...] = (acc0 / l0 + acc1 / l1).T.astype(BF16)


def _mla_attention(q, k, vt, batch, seq):
    tq = tk = MLA_ATTN_BLOCK
    nq = seq // tq
    q3 = q.reshape(batch, seq, -1)
    k3 = k.reshape(batch, seq, -1)
    out = pl.pallas_call(
        functools.partial(_mla_attn_kernel, tq=tq, tk=tk),
        grid=(batch, MLA_HEADS // 2, nq),
        in_specs=[
            pl.BlockSpec((None, tq, 2 * LANES), lambda b, hp, i: (b, i, hp)),
            pl.BlockSpec((None, seq, 2 * LANES), lambda b, hp, i: (b, 0, hp)),
            pl.BlockSpec((seq // tk, LANES, tk), lambda b, hp, i: (b, hp, 0)),
        ],
        out_specs=pl.BlockSpec((None, tq, LANES), lambda b, hp, i: (b, i, hp)),
        out_shape=jax.ShapeDtypeStruct((batch, seq, MLA_HEADS * MLA_V), BF16),
        compiler_params=_cparams(("arbitrary", "arbitrary", "arbitrary")),
        name="mla_prompt_attention",
    )(q3, k3, vt)
    return out.reshape(batch * seq, MLA_HEADS * MLA_V)


def _qlat_kernel(q_ref, wk_ref, o_ref):
    for hd in range(MLA_HEADS):
        qh = q_ref[:, hd * MLA_HEAD_PAD:(hd + 1) * MLA_HEAD_PAD]
        o_ref[:, hd * MLA_KV_LORA:(hd + 1) * MLA_KV_LORA] = _dot(qh, wk_ref[hd]).astype(BF16)


def _mla_absorb_queries(q, w_ukt):
    n = q.shape[0]
    return pl.pallas_call(
        _qlat_kernel,
        grid=(1,),
        in_specs=[_Rows.full(q.shape), _Rows.full(w_ukt.shape)],
        out_specs=_Rows.full((n, MLA_HEADS * MLA_KV_LORA)),
        out_shape=jax.ShapeDtypeStruct((n, MLA_HEADS * MLA_KV_LORA), BF16),
        compiler_params=_cparams(("arbitrary",)),
        name="mla_absorb_queries",
    )(q, w_ukt)


def _paged_attn_kernel(pt_ref, qlat_ref, qpe_ref, nckv_ref, nkpe_ref, *rest, n_steps, dec_seq):
    pg = PAGES_PER_STEP
    ckv_refs = rest[:pg]
    kpe_refs = rest[pg:2 * pg]
    o_ref = rest[2 * pg]
    m_sc, l_sc, acc_sc, kc_sc, kp_sc = rest[2 * pg + 1:]
    g = pl.program_id(1)
    rows = qlat_ref.shape[0]

    @pl.when(g == 0)
    def _():
        m_sc[...] = jnp.full(m_sc.shape, NEG_INF, F32)
        l_sc[...] = jnp.zeros(l_sc.shape, F32)
        acc_sc[...] = jnp.zeros(acc_sc.shape, F32)

    qlat = qlat_ref[...]
    qpe = qpe_ref[...]

    def column(row_vec):
        return jnp.broadcast_to(row_vec, (LANES, rows)).T[:, :1]

    def update(s_t, vals):
        m = m_sc[...]
        m_new = jnp.maximum(m, jnp.max(s_t, axis=0, keepdims=True))
        alpha = jnp.exp(m - m_new)
        p_t = jnp.exp(s_t - m_new)
        l_sc[...] = alpha * l_sc[...] + jnp.sum(p_t, axis=0, keepdims=True)
        acc_sc[...] = column(alpha) * acc_sc[...] + _dot(p_t.T.astype(BF16), vals)
        m_sc[...] = m_new

    def scores_t(kc, kp):
        return (_dot_nt(kc, qlat) + _dot_nt(kp, qpe)) * MLA_SCALE

    for grp in range(pg // PAGE_GROUP):
        for i in range(PAGE_GROUP):
            sl = slice(i * PAGE_SIZE, (i + 1) * PAGE_SIZE)
            kc_sc[sl, :] = ckv_refs[grp * PAGE_GROUP + i][...].astype(BF16)
            kp_sc[sl, :] = kpe_refs[grp * PAGE_GROUP + i][...].T.astype(BF16)
        kc = kc_sc[...]
        update(scores_t(kc, kp_sc[...]), kc)

    @pl.when(g == n_steps - 1)
    def _():
        pad = PAGE_SIZE - dec_seq
        ck = jnp.concatenate([nckv_ref[...], jnp.zeros((pad, MLA_KV_LORA), F32)], axis=0).astype(BF16)
        kp = jnp.concatenate([nkpe_ref[...], jnp.zeros((pad, MLA_ROPE), F32)], axis=0).astype(BF16)
        s_t = scores_t(ck, kp)
        k_t = lax.broadcasted_iota(I32, s_t.shape, 0)
        q_t = lax.broadcasted_iota(I32, s_t.shape, 1) // MLA_HEADS
        update(jnp.where(k_t <= q_t, s_t, NEG_INF), ck)
        o_ref[...] = acc_sc[...] / column(l_sc[...])


def _mla_paged_attention(qlat, qpe, new_ckv, new_kpe, cache_ckv, cache_kpe, page_table, layer):
    db, rows, _ = qlat.shape
    dec_seq = new_ckv.shape[1]
    n_pages = page_table.shape[1]
    pg = PAGES_PER_STEP
    assert n_pages % pg == 0
    n_steps = n_pages // pg

    def page_spec(shape, i):
        return pl.BlockSpec((None, None) + shape, lambda b, g, pt: (layer, pt[b, g * pg + i], 0, 0))

    cache_kpe_t = jnp.swapaxes(cache_kpe, 2, 3)
    in_specs = [
        pl.BlockSpec((None, rows, MLA_KV_LORA), lambda b, g, pt: (b, 0, 0)),
        pl.BlockSpec((None, rows, MLA_ROPE), lambda b, g, pt: (b, 0, 0)),
        pl.BlockSpec((None, dec_seq, MLA_KV_LORA), lambda b, g, pt: (b, 0, 0)),
        pl.BlockSpec((None, dec_seq, MLA_ROPE), lambda b, g, pt: (b, 0, 0)),
    ]
    in_specs += [page_spec((PAGE_SIZE, MLA_KV_LORA), i) for i in range(pg)]
    in_specs += [page_spec((MLA_ROPE, PAGE_SIZE), i) for i in range(pg)]
    grid_spec = pltpu.PrefetchScalarGridSpec(
        num_scalar_prefetch=1,
        grid=(db, n_steps),
        in_specs=in_specs,
        out_specs=pl.BlockSpec((None, rows, MLA_KV_LORA), lambda b, g, pt: (b, 0, 0)),
        scratch_shapes=[pltpu.VMEM((1, rows), F32), pltpu.VMEM((1, rows), F32),
                        pltpu.VMEM((rows, MLA_KV_LORA), F32),
                        pltpu.VMEM((PAGE_GROUP * PAGE_SIZE, MLA_KV_LORA), BF16),
                        pltpu.VMEM((PAGE_GROUP * PAGE_SIZE, MLA_ROPE), BF16)],
    )
    return pl.pallas_call(
        functools.partial(_paged_attn_kernel, n_steps=n_steps, dec_seq=dec_seq),
        grid_spec=grid_spec,
        out_shape=jax.ShapeDtypeStruct((db, rows, MLA_KV_LORA), F32),
        compiler_params=_cparams(("arbitrary", "arbitrary")),
        name="mla_paged_attention",
    )(page_table, qlat, qpe, new_ckv, new_kpe, *([cache_ckv] * pg), *([cache_kpe_t] * pg))


def _sample_out_kernel(olat_ref, wuv_ref, wo_ref, x_ref, gate_ref, o_ref):
    col_head = lax.broadcasted_iota(I32, (1, MLA_HEADS * MLA_V), 1) // MLA_V
    wuv = wuv_ref[...]
    o = jnp.zeros((olat_ref.shape[0], MLA_HEADS * MLA_V), F32)
    for hd in range(MLA_HEADS):
        lat = olat_ref[:, hd * MLA_KV_LORA:(hd + 1) * MLA_KV_LORA].astype(BF16)
        o = jnp.where(col_head == hd, _dot(lat, wuv), o)
    o_ref[...] = x_ref[...] + gate_ref[...] * _dot(o.astype(BF16), wo_ref[...])


def _mla_sample_out(rows, olat, w_uv, w_o, x, mod):
    return pl.pallas_call(
        _sample_out_kernel,
        grid=(rows.grid,),
        in_specs=[rows.rows(olat.shape[1]), _Rows.full(w_uv.shape), _Rows.full(w_o.shape),
                  rows.rows(D_MODEL), rows.mod(2)],
        out_specs=rows.rows(D_MODEL),
        out_shape=jax.ShapeDtypeStruct((rows.n, D_MODEL), F32),
        compiler_params=_cparams(("arbitrary",)),
        name="mla_sample_out",
    )(olat, w_uv, w_o, x, mod)


def _out_proj_kernel(o_ref, wo_ref, x_ref, gate_ref, y_ref):
    y_ref[...] = x_ref[...] + gate_ref[...] * _dot(o_ref[...], wo_ref[...])


def _out_proj(rows, o, w_o, x, mod):
    return pl.pallas_call(
        _out_proj_kernel,
        grid=(rows.grid,),
        in_specs=[rows.rows(o.shape[1]), _Rows.full(w_o.shape), rows.rows(D_MODEL), rows.mod(2)],
        out_specs=rows.rows(D_MODEL),
        out_shape=jax.ShapeDtypeStruct((rows.n, D_MODEL), F32),
        compiler_params=_cparams(("arbitrary",)),
        name="attn_out_proj",
    )(o, w_o, x, mod)


SWA_NQ = SWA_HEADS * SWA_HEAD_DIM
SWA_NK = SWA_KV_HEADS * SWA_HEAD_DIM
SWA_QKV = SWA_NQ + 2 * SWA_NK
SWA_EXT = SWA_QKV + SWA_NQ + SWA_NK


def _swa_proj_kernel(x_ref, g_ref, sh_ref, sc_ref, cos_ref, sin_ref, w_ref, b_ref,
                     q_ref, kb_ref, vb_ref, k_ref, v_ref):
    h = _modulate(x_ref[...], g_ref[...], sh_ref[...], sc_ref[...]).astype(BF16)
    z = _dot(h, w_ref[...]) + b_ref[...]
    cos = cos_ref[...]
    sin = sin_ref[...]
    for t in range((SWA_NQ + SWA_NK) // LANES):
        sl = slice(t * LANES, (t + 1) * LANES)
        sw = slice(SWA_QKV + t * LANES, SWA_QKV + (t + 1) * LANES)
        r = z[:, sl] * cos + z[:, sw] * sin
        if t < SWA_NQ // LANES:
            q_ref[:, sl] = (r * SWA_SCALE).astype(BF16)
        else:
            ks = slice(t * LANES - SWA_NQ, (t + 1) * LANES - SWA_NQ)
            k_ref[:, ks] = r
            kb_ref[:, ks] = r.astype(BF16)
    v = z[:, SWA_NQ + SWA_NK:SWA_QKV]
    v_ref[...] = v
    vb_ref[...] = v.astype(BF16)


def _swa_project(rows, x, ln_g, mod, tabs, w):
    n = rows.n
    full = _Rows.full
    return pl.pallas_call(
        _swa_proj_kernel,
        grid=(rows.grid,),
        in_specs=[rows.rows(D_MODEL), full((1, D_MODEL)), rows.mod(0), rows.mod(1),
                  rows.pos(LANES), rows.pos(LANES), full((D_MODEL, SWA_EXT)), full((1, SWA_EXT))],
        out_specs=[rows.rows(SWA_NQ), rows.rows(SWA_NK), rows.rows(SWA_NK), rows.rows(SWA_NK),
                   rows.rows(SWA_NK)],
        out_shape=[
            jax.ShapeDtypeStruct((n, SWA_NQ), BF16),
            jax.ShapeDtypeStruct((n, SWA_NK), BF16),
            jax.ShapeDtypeStruct((n, SWA_NK), BF16),
            jax.ShapeDtypeStruct((n, SWA_NK), F32),
            jax.ShapeDtypeStruct((n, SWA_NK), F32),
        ],
        compiler_params=_cparams(("arbitrary",)),
        name="swa_project",
    )(x, ln_g, mod, mod, tabs["cos"], tabs["sin"], w["w_qkv"], w["b_qkv"])


def _swa_core(q_all, k_all, v_all, mask, sinks_ref, o_ref):
    lane = lax.broadcasted_iota(I32, (1, LANES), 1)
    lo = lane < SWA_HEAD_DIM
    zero = jnp.zeros((), BF16)
    for kh in range(SWA_KV_HEADS):
        tile = kh // 2
        k_t = k_all[:, tile * LANES:(tile + 1) * LANES]
        v_t = v_all[:, tile * LANES:(tile + 1) * LANES]
        k_r = pltpu.roll(k_t, SWA_HEAD_DIM, 1)
        v_r = pltpu.roll(v_t, SWA_HEAD_DIM, 1)
        for pair in range(SWA_GROUP // 2):
            q_t = q_all[:, (kh * 2 + pair) * LANES:(kh * 2 + pair + 1) * LANES]
            out = None
            for half in range(2):
                keep = lo if half == 0 else jnp.logical_not(lo)
                kx = k_t if (kh % 2) == half else k_r
                vx = v_t if (kh % 2) == half else v_r
                s = _dot_nt(jnp.where(keep, q_t, zero), kx)
                s = jnp.where(mask, s, NEG_INF)
                sink = sinks_ref[kh * SWA_GROUP + 2 * pair + half]
                m = jnp.maximum(jnp.max(s, axis=-1, keepdims=True), sink)
                p = jnp.exp(s - m)
                l = jnp.sum(p, axis=-1, keepdims=True) + jnp.exp(sink - m)
                o = _dot((p / l).astype(BF16), jnp.where(keep, vx, zero))
                out = o if out is None else out + o
            o_ref[:, (kh * 2 + pair) * LANES:(kh * 2 + pair + 1) * LANES] = out.astype(o_ref.dtype)


def _swa_prompt_kernel(sinks_ref, q_ref, kc_ref, kp_ref, vc_ref, vp_ref, o_ref, *, tq):
    i = pl.program_id(1)
    k_all = jnp.concatenate([kp_ref[...], kc_ref[...]], axis=0)
    v_all = jnp.concatenate([vp_ref[...], vc_ref[...]], axis=0)
    tk = tq + WINDOW
    q_pos = i * tq + lax.broadcasted_iota(I32, (tq, tk), 0)
    k_pos = i * tq - WINDOW + lax.broadcasted_iota(I32, (tq, tk), 1)
    mask = (k_pos >= 0) & (k_pos <= q_pos) & (q_pos - k_pos < WINDOW)
    _swa_core(q_ref[...], k_all, v_all, mask, sinks_ref, o_ref)


def _swa_prompt_attention(q, kb, vb, sinks, batch, seq):
    tq = SWA_Q_BLOCK
    r = tq // WINDOW
    q3 = q.reshape(batch, seq, SWA_NQ)
    k3 = kb.reshape(batch, seq, SWA_NK)
    v3 = vb.reshape(batch, seq, SWA_NK)
    cur = pl.BlockSpec((None, tq, SWA_NK), lambda b, i: (b, i, 0))
    prev = pl.BlockSpec((None, WINDOW, SWA_NK), lambda b, i: (b, jnp.maximum(i * r - 1, 0), 0))
    out = pl.pallas_call(
        functools.partial(_swa_prompt_kernel, tq=tq),
        grid=(batch, seq // tq),
        in_specs=[pl.BlockSpec(memory_space=pltpu.SMEM),
                  pl.BlockSpec((None, tq, SWA_NQ), lambda b, i: (b, i, 0)), cur, prev, cur, prev],
        out_specs=pl.BlockSpec((None, tq, SWA_NQ), lambda b, i: (b, i, 0)),
        out_shape=jax.ShapeDtypeStruct((batch, seq, SWA_NQ), BF16),
        compiler_params=_cparams(("arbitrary", "arbitrary")),
        name="swa_prompt_attention",
    )(sinks, q3, k3, k3, v3, v3)
    return out.reshape(batch * seq, SWA_NQ)


def _swa_sample_kernel(sinks_ref, q_ref, kbuf_ref, vbuf_ref, kn_ref, vn_ref, o_ref, o_sc, *, n_buf, dec_seq):
    pad = jnp.zeros((BF16_ROWS - dec_seq, SWA_NK), F32)
    k_all = jnp.concatenate([kbuf_ref[...], kn_ref[...], pad], axis=0).astype(BF16)
    v_all = jnp.concatenate([vbuf_ref[...], vn_ref[...], pad], axis=0).astype(BF16)
    tk = n_buf + BF16_ROWS
    t = lax.broadcasted_iota(I32, (BF16_ROWS, tk), 0)
    c = lax.broadcasted_iota(I32, (BF16_ROWS, tk), 1)
    in_buf = c < n_buf
    mask = ((in_buf & ((n_buf + t - c) < WINDOW))
            | (jnp.logical_not(in_buf) & ((c - n_buf) <= jnp.minimum(t, dec_seq - 1))))
    q = jnp.concatenate([q_ref[...].astype(F32), jnp.zeros((BF16_ROWS - dec_seq, SWA_NQ), F32)], axis=0)
    _swa_core(q.astype(BF16), k_all, v_all, mask, sinks_ref, o_sc)
    o_ref[...] = o_sc[:dec_seq, :].astype(o_ref.dtype)


def _swa_sample_attention(q, kn, vn, buf_k, buf_v, sinks, db, dec_seq):
    n_buf = buf_k.shape[1]
    spec = lambda rws, w: pl.BlockSpec((None, rws, w), lambda b: (b, 0, 0))
    out = pl.pallas_call(
        functools.partial(_swa_sample_kernel, n_buf=n_buf, dec_seq=dec_seq),
        grid=(db,),
        in_specs=[pl.BlockSpec(memory_space=pltpu.SMEM), spec(dec_seq, SWA_NQ), spec(n_buf, SWA_NK),
                  spec(n_buf, SWA_NK), spec(dec_seq, SWA_NK), spec(dec_seq, SWA_NK)],
        out_specs=spec(dec_seq, SWA_NQ),
        out_shape=jax.ShapeDtypeStruct((db, dec_seq, SWA_NQ), BF16),
        scratch_shapes=[pltpu.VMEM((BF16_ROWS, SWA_NQ), F32)],
        compiler_params=_cparams(("arbitrary",)),
        name="swa_sample_attention",
    )(sinks, q.reshape(db, dec_seq, SWA_NQ), buf_k, buf_v,
      kn.reshape(db, dec_seq, SWA_NK), vn.reshape(db, dec_seq, SWA_NK))
    return out.reshape(db * dec_seq, SWA_NQ)


def _peer_candidate_tables(width_tokens):
    k = PEER_TOPK
    flat, valid = [], []
    for i in range(8):
        width = 16 if i == 0 else 8
        for j in range(width):
            flat.append(i * k + j)
            valid.append((i + 1) * (j + 1) <= k)
    for i in range(8, 16):
        flat.append(i * k)
        valid.append(True)
    flat = np.broadcast_to(np.asarray(flat, np.float32)[:, None], (len(flat), width_tokens))
    pen = np.where(np.asarray(valid), 0.0, -np.inf).astype(np.float32)
    return np.ascontiguousarray(flat), np.ascontiguousarray(np.broadcast_to(pen[:, None], flat.shape))


def _extract_top(s, row_id, k, payload=None):
    big = jnp.float32(2 ** 30)
    vals, ids = [], []
    for _ in range(k):
        m = jnp.max(s, axis=0, keepdims=True)
        sel = jnp.min(jnp.where(s == m, row_id, big), axis=0, keepdims=True)
        hit = row_id == sel
        if payload is not None:
            ids.append(jnp.max(jnp.where(hit, payload, -1.0), axis=0, keepdims=True))
        else:
            ids.append(sel)
        vals.append(m)
        s = jnp.where(hit, -jnp.inf, s)
    return jnp.concatenate(vals, axis=0), jnp.concatenate(ids, axis=0)


def _route_queries(x_ref, g_ref, sh_ref, sc_ref, wq_ref, qs_sc):
    h = _modulate(x_ref[...], g_ref[...], sh_ref[...], sc_ref[...]).astype(BF16)
    q = _dot(h, wq_ref[...])
    for grp in range(2 * PEER_HEADS):
        qs_sc[grp] = q[:, grp * PEER_HALF:(grp + 1) * PEER_HALF].astype(BF16)
    return h


def _route_first_level(grp, sk_ref, qs_sc, sv_sc, si_sc, tb):
    key_id = lax.broadcasted_iota(I32, (PEER_N_KEYS, tb), 0).astype(F32)
    s = _dot_nt(sk_ref[grp], qs_sc[grp])
    vals, ids = _extract_top(s, key_id, PEER_TOPK)
    sv_sc[grp] = vals
    si_sc[grp] = ids


def _route_second_level(sv_sc, si_sc, flat_ref, pen_ref, pe_sc, pg_sc):
    k = PEER_TOPK

    def head(hd, carry):
        sa, sb = sv_sc[2 * hd], sv_sc[2 * hd + 1]
        ia, ib = si_sc[2 * hd], si_sc[2 * hd + 1]
        cs, ce = [], []
        for i in range(8):
            width = 16 if i == 0 else 8
            cs.append(sa[i:i + 1] + sb[:width])
            ce.append(ia[i:i + 1] * PEER_N_KEYS + ib[:width])
        cs.append(sa[8:] + sb[:1])
        ce.append(ia[8:] * PEER_N_KEYS + ib[:1])
        cand = jnp.concatenate(cs, axis=0) + pen_ref[...]
        cand_e = jnp.concatenate(ce, axis=0)
        best, experts = _extract_top(cand, flat_ref[...], k, payload=cand_e)
        ex = jnp.exp(best - best[:1])
        gates = ex / jnp.sum(ex, axis=0, keepdims=True)
        pick = pl.ds(pl.multiple_of(hd * k, k), k)
        pe_sc[pick, :] = experts
        pg_sc[pick, :] = gates
        return carry

    lax.fori_loop(0, PEER_HEADS, head, 0, unroll=ROUTE_UNROLL)


def _peer_route_kernel(x_ref, g_ref, sh_ref, sc_ref, wq_ref, sk_ref, flat_ref, pen_ref,
                       h_ref, e_ref, gate_ref, qs_sc, sv_sc, si_sc, pe_sc, pg_sc, *, tb):
    h = _route_queries(x_ref, g_ref, sh_ref, sc_ref, wq_ref, qs_sc)
    h_ref[...] = h

    def first_level(grp, carry):
        _route_first_level(grp, sk_ref, qs_sc, sv_sc, si_sc, tb)
        return carry

    lax.fori_loop(0, 2 * PEER_HEADS, first_level, 0, unroll=ROUTE_UNROLL)
    _route_second_level(sv_sc, si_sc, flat_ref, pen_ref, pe_sc, pg_sc)
    e_ref[...] = pe_sc[...].T.astype(I32)
    gate_ref[...] = pg_sc[...].T


def _peer_route(rows, x, ln_g, mod, w):
    n, tb = rows.n, rows.tb
    flat, pen = _peer_candidate_tables(tb)
    full = _Rows.full
    n_groups = 2 * PEER_HEADS
    return pl.pallas_call(
        functools.partial(_peer_route_kernel, tb=tb),
        grid=(rows.grid,),
        in_specs=[rows.rows(D_MODEL), full((1, D_MODEL)), rows.mod(3), rows.mod(4),
                  full(w["w_q"].shape), full(w["sub_keys"].shape), full(flat.shape), full(pen.shape)],
        out_specs=[rows.rows(D_MODEL), rows.rows(PEER_PICKS), rows.rows(PEER_PICKS)],
        out_shape=[jax.ShapeDtypeStruct((n, D_MODEL), BF16),
                   jax.ShapeDtypeStruct((n, PEER_PICKS), I32),
                   jax.ShapeDtypeStruct((n, PEER_PICKS), F32)],
        scratch_shapes=[pltpu.VMEM((n_groups, tb, PEER_HALF), BF16),
                        pltpu.VMEM((n_groups, PEER_TOPK, tb), F32),
                        pltpu.VMEM((n_groups, PEER_TOPK, tb), F32),
                        pltpu.VMEM((PEER_PICKS, tb), F32),
                        pltpu.VMEM((PEER_PICKS, tb), F32)],
        compiler_params=_cparams(("arbitrary",)),
        name="peer_route",
    )(x, ln_g, mod, mod, w["w_q"], w["sub_keys"], jnp.asarray(flat), jnp.asarray(pen))


def _build_gate_tiles(e_ref, gate_ref, w_sc, tb):
    pitch = GATE_TILE_PITCH
    row = lax.broadcasted_iota(I32, (PEER_N_KEYS, PEER_PICKS), 0)
    half_keys = PEER_N_KEYS // 2
    key_a = jnp.where(row < half_keys, 2 * row, 2 * (row - half_keys) + 1)

    def token(t, carry):
        e = e_ref[pl.ds(t, 1), :]
        g = gate_ref[pl.ds(t, 1), :]
        g_hi = g.astype(BF16).astype(F32)
        g_lo = (g - g_hi).astype(BF16).astype(F32)
        hit_a = key_a == (e >> 7)
        hit_b = row == (e & (PEER_N_KEYS - 1))
        one_a = jnp.where(hit_a, 1.0, 0.0).astype(BF16)
        lhs = jnp.concatenate([one_a, one_a], axis=1)
        rhs = jnp.concatenate([jnp.where(hit_b, g_hi, 0.0).astype(BF16),
                               jnp.where(hit_b, g_lo, 0.0).astype(BF16)], axis=1)
        w = _dot_nt(lhs, rhs)
        even = lax.bitcast_convert_type(w[:half_keys].astype(BF16).astype(F32), jnp.uint32)
        odd = lax.bitcast_convert_type(w[half_keys:].astype(BF16).astype(F32), jnp.uint32)
        w_sc[pl.ds(pl.multiple_of(t * pitch, SUBLANES), half_keys), :] = odd | (even >> 16)
        return carry

    lax.fori_loop(0, tb, token, 0, unroll=TOKEN_UNROLL)


def _expert_chunk(h, u_ref, v_ref, w_sc, p_sc, acc_sc, c, tb):
    for pair in range(EXPERT_CHUNK_A // 2):
        z = _dot_nt(h, u_ref[pair * 2 * PEER_N_KEYS:(pair + 1) * 2 * PEER_N_KEYS, :])
        word = w_sc[pl.ds(c * (EXPERT_CHUNK_A // 2) + pair, tb, stride=GATE_TILE_PITCH), :]
        w_pair = (lax.bitcast_convert_type(word << 16, F32),
                  lax.bitcast_convert_type(word & jnp.uint32(0xFFFF0000), F32))
        for half in range(2):
            al = 2 * pair + half
            sl = slice(al * PEER_N_KEYS, (al + 1) * PEER_N_KEYS)
            p_sc[:, sl] = (w_pair[half] * _gelu(z[:, half * PEER_N_KEYS:(half + 1) * PEER_N_KEYS])).astype(BF16)
    acc_sc[...] += _dot(p_sc[...], v_ref[...])


def _peer_expert_kernel(h_ref, e_ref, gate_ref, u_ref, v_ref, x_ref, g2_ref, o_ref,
                        w_sc, p_sc, acc_sc, *, tb, n_chunks):
    c = pl.program_id(1)

    @pl.when(c == 0)
    def _():
        acc_sc[...] = jnp.zeros(acc_sc.shape, F32)
        _build_gate_tiles(e_ref, gate_ref, w_sc, tb)

    _expert_chunk(h_ref[...], u_ref, v_ref, w_sc, p_sc, acc_sc, c, tb)

    @pl.when(c == n_chunks - 1)
    def _():
        o_ref[...] = x_ref[...] + g2_ref[...] * acc_sc[...]


def _peer_experts(rows, h, experts, gates, u, v, layer, x, mod):
    n, tb = rows.n, rows.tb
    ce = EXPERT_CHUNK_A * PEER_N_KEYS
    n_chunks = u.shape[1] // ce
    bps = rows.bps
    if bps is None:
        gate_spec = pl.BlockSpec((tb, D_MODEL), lambda i, c: (i, 5))
    else:
        gate_spec = pl.BlockSpec((None, 1, D_MODEL), lambda i, c: (i // bps, 0, 5))
    tok = lambda w: pl.BlockSpec((tb, w), lambda i, c: (i, 0))
    tok1 = lambda w: pl.BlockSpec((tb, w), lambda i, c: (i, 0), pipeline_mode=pl.Buffered(1))
    return pl.pallas_call(
        functools.partial(_peer_expert_kernel, tb=tb, n_chunks=n_chunks),
        grid=(rows.grid, n_chunks),
        in_specs=[tok1(D_MODEL), tok(PEER_PICKS), tok(PEER_PICKS),
                  pl.BlockSpec((None, ce, D_MODEL), lambda i, c: (layer, c, 0)),
                  pl.BlockSpec((None, ce, D_MODEL), lambda i, c: (layer, c, 0)),
                  tok1(D_MODEL), gate_spec],
        out_specs=tok1(D_MODEL),
        out_shape=jax.ShapeDtypeStruct((n, D_MODEL), F32),
        scratch_shapes=[pltpu.VMEM((tb * GATE_TILE_PITCH, LANES), jnp.uint32),
                        pltpu.VMEM((tb, ce), BF16),
                        pltpu.VMEM((tb, D_MODEL), F32)],
        compiler_params=_cparams(("arbitrary", "arbitrary")),
        name="peer_experts",
    )(h, experts, gates, u, v, x, mod)


def _peer_fused_kernel(xr_ref, lng_ref, sh_ref, sc_ref, wq_ref, sk_ref, flat_ref, pen_ref,
                       u_ref, v_ref, xe_ref, g2_ref, o_ref,
                       qs_sc, sv_sc, si_sc, pe_sc, pg_sc, h_sc, e_sc, gt_sc, w_sc, p_sc, acc_sc,
                       *, tb, n_chunks):
    s = pl.program_id(0)
    c = pl.program_id(1)
    cur = s % 2
    prev = 1 - cur

    @pl.when(jnp.logical_and(s == 0, c == 0))
    def _():
        h_sc[...] = jnp.zeros(h_sc.shape, BF16)
        e_sc[...] = jnp.zeros(e_sc.shape, I32)
        gt_sc[...] = jnp.zeros(gt_sc.shape, F32)

    @pl.when(c == 0)
    def _():
        h_sc[cur] = _route_queries(xr_ref, lng_ref, sh_ref, sc_ref, wq_ref, qs_sc)
        acc_sc[...] = jnp.zeros(acc_sc.shape, F32)
        _build_gate_tiles(e_sc.at[prev], gt_sc.at[prev], w_sc, tb)

    _route_first_level(c, sk_ref, qs_sc, sv_sc, si_sc, tb)
    _expert_chunk(h_sc[prev], u_ref, v_ref, w_sc, p_sc, acc_sc, c, tb)

    @pl.when(c == n_chunks - 1)
    def _():
        _route_second_level(sv_sc, si_sc, flat_ref, pen_ref, pe_sc, pg_sc)
        e_sc[cur] = pe_sc[...].T.astype(I32)
        gt_sc[cur] = pg_sc[...].T
        o_ref[...] = xe_ref[...] + g2_ref[...] * acc_sc[...]


def _peer_fused(rows, x, ln_g, mod, w, u, v, layer):
    n, tb, nb, bps = rows.n, rows.tb, rows.grid, rows.bps
    ce = EXPERT_CHUNK_A * PEER_N_KEYS
    n_chunks = u.shape[1] // ce
    n_groups = 2 * PEER_HEADS
    assert n_chunks == n_groups
    flat, pen = _peer_candidate_tables(tb)
    route_blk = lambda s: jnp.minimum(s, nb - 1)
    expert_blk = lambda s: jnp.maximum(s - 1, 0)
    one = pl.Buffered(1)
    const = lambda shape: pl.BlockSpec(shape, lambda s, c: (0,) * len(shape), pipeline_mode=one)
    mod_spec = lambda blk, k: pl.BlockSpec((None, 1, D_MODEL), lambda s, c: (blk(s) // bps, 0, k))
    return pl.pallas_call(
        functools.partial(_peer_fused_kernel, tb=tb, n_chunks=n_chunks),
        grid=(nb + 1, n_chunks),
        in_specs=[pl.BlockSpec((tb, D_MODEL), lambda s, c: (route_blk(s), 0), pipeline_mode=one),
                  const((1, D_MODEL)), mod_spec(route_blk, 3), mod_spec(route_blk, 4),
                  const(w["w_q"].shape), const(w["sub_keys"].shape), const(flat.shape), const(pen.shape),
                  pl.BlockSpec((None, ce, D_MODEL), lambda s, c: (layer, c, 0)),
                  pl.BlockSpec((None, ce, D_MODEL), lambda s, c: (layer, c, 0)),
                  pl.BlockSpec((tb, D_MODEL), lambda s, c: (expert_blk(s), 0), pipeline_mode=one),
                  mod_spec(expert_blk, 5)],
        out_specs=pl.BlockSpec((tb, D_MODEL), lambda s, c: (expert_blk(s), 0), pipeline_mode=one),
        out_shape=jax.ShapeDtypeStruct((n, D_MODEL), F32),
        scratch_shapes=[pltpu.VMEM((n_groups, tb, PEER_HALF), BF16),
                        pltpu.VMEM((n_groups, PEER_TOPK, tb), F32),
                        pltpu.VMEM((n_groups, PEER_TOPK, tb), F32),
                        pltpu.VMEM((PEER_PICKS, tb), F32),
                        pltpu.VMEM((PEER_PICKS, tb), F32),
                        pltpu.VMEM((2, tb, D_MODEL), BF16),
                        pltpu.VMEM((2, tb, PEER_PICKS), I32),
                        pltpu.VMEM((2, tb, PEER_PICKS), F32),
                        pltpu.VMEM((tb * GATE_TILE_PITCH, LANES), jnp.uint32),
                        pltpu.VMEM((tb, ce), BF16),
                        pltpu.VMEM((tb, D_MODEL), F32)],
        compiler_params=_cparams(("arbitrary", "arbitrary")),
        name="peer_fused",
    )(x, ln_g, mod, mod, w["w_q"], w["sub_keys"], jnp.asarray(flat), jnp.asarray(pen), u, v, x, mod)


def _final_norm_kernel(x_ref, g_ref, o_ref):
    o_ref[...] = _rms(x_ref[...], g_ref[...])


def _final_norm(rows, x, g):
    return pl.pallas_call(
        _final_norm_kernel,
        grid=(rows.grid,),
        in_specs=[rows.rows(D_MODEL), _Rows.full((1, D_MODEL))],
        out_specs=rows.rows(D_MODEL),
        out_shape=jax.ShapeDtypeStruct((rows.n, D_MODEL), F32),
        compiler_params=_cparams(("arbitrary",)),
        name="final_norm",
    )(x, g)


def _pair_swap(w, half):
    return jnp.concatenate([-w[..., half:2 * half], w[..., :half]], axis=-1)


def _rope_cos_sin(pos, rot_dim):
    half = rot_dim // 2
    inv_freq = ROPE_THETA ** (-jnp.arange(half, dtype=F32) * 2.0 / rot_dim)
    ang = pos.astype(F32)[:, None] * inv_freq[None, :]
    return jnp.cos(ang), jnp.sin(ang)


def _mla_tables(pos):
    cos, sin = _rope_cos_sin(pos, MLA_ROPE)
    n = pos.shape[0]
    ones = jnp.ones((n, MLA_NOPE), F32)
    zeros_n = jnp.zeros((n, MLA_NOPE), F32)
    pad = jnp.zeros((n, MLA_HEAD_PAD - MLA_NOPE - MLA_ROPE), F32)
    return {
        "c32": jnp.concatenate([cos, cos], axis=1),
        "s32": jnp.concatenate([sin, sin], axis=1),
        "c128": jnp.concatenate([ones, cos, cos, pad], axis=1),
        "s128": jnp.concatenate([zeros_n, sin, sin, pad], axis=1),
    }


def _swa_tables(pos):
    cos, sin = _rope_cos_sin(pos, SWA_ROT)
    n = pos.shape[0]
    rest = SWA_HEAD_DIM - SWA_ROT
    c = jnp.concatenate([cos, cos, jnp.ones((n, rest), F32)], axis=1)
    s = jnp.concatenate([sin, sin, jnp.zeros((n, rest), F32)], axis=1)
    reps = LANES // SWA_HEAD_DIM
    return {"cos": jnp.tile(c, (1, reps)), "sin": jnp.tile(s, (1, reps))}


def _mla_weights(w_in, q_norm, kv_norm, w_uq, w_uk, w_uv, w_o):
    d = w_in.shape[0]
    half = MLA_ROPE // 2
    kpe_cols = w_in[:, MLA_Q_LORA + MLA_KV_LORA:]
    w_in_ext = jnp.zeros((d, MLA_IN_EXT), F32)
    w_in_ext = w_in_ext.at[:, :MLA_KPE_COL + MLA_ROPE].set(w_in)
    w_in_ext = w_in_ext.at[:, MLA_KPE_SWAP_COL:MLA_KPE_SWAP_COL + MLA_ROPE].set(_pair_swap(kpe_cols, half))
    uq = w_uq.reshape(MLA_Q_LORA, MLA_HEADS, MLA_NOPE + MLA_ROPE)
    uq_pad = jnp.zeros((MLA_Q_LORA, MLA_HEADS, MLA_HEAD_PAD), F32).at[:, :, :MLA_NOPE + MLA_ROPE].set(uq)
    uq_swap = jnp.zeros((MLA_Q_LORA, MLA_HEADS, MLA_HEAD_PAD), F32)
    uq_swap = uq_swap.at[:, :, MLA_NOPE:MLA_NOPE + MLA_ROPE].set(_pair_swap(uq[:, :, MLA_NOPE:], half))
    uk_pad = jnp.zeros((MLA_KV_LORA, MLA_HEADS, MLA_HEAD_PAD), F32).at[:, :, :MLA_NOPE].set(w_uk)
    e_pe = jnp.zeros((MLA_ROPE, MLA_HEADS, MLA_HEAD_PAD), F32)
    e_pe = e_pe.at[jnp.arange(MLA_ROPE), :, MLA_NOPE + jnp.arange(MLA_ROPE)].set(1.0)
    ukt = jnp.zeros((MLA_HEADS, MLA_HEAD_PAD, MLA_KV_LORA), F32)
    ukt = ukt.at[:, :MLA_NOPE, :].set(jnp.transpose(w_uk, (1, 2, 0)))
    hp = MLA_HEADS * MLA_HEAD_PAD
    return {
        "w_in": w_in_ext.astype(BF16),
        "q_norm": q_norm.reshape(1, -1), "kv_norm": kv_norm.reshape(1, -1),
        "w_uq": uq_pad.reshape(MLA_Q_LORA, hp).astype(BF16),
        "w_uq_swap": uq_swap.reshape(MLA_Q_LORA, hp).astype(BF16),
        "w_uk": uk_pad.reshape(MLA_KV_LORA, hp).astype(BF16),
        "e_pe": e_pe.reshape(MLA_ROPE, hp).astype(BF16),
        "w_uvt": w_uv.reshape(MLA_KV_LORA, MLA_HEADS * MLA_V).T.astype(BF16),
        "w_ukt": ukt.astype(BF16),
        "w_uv": w_uv.reshape(MLA_KV_LORA, MLA_HEADS * MLA_V).astype(BF16),
        "w_o": w_o.astype(BF16),
    }


def _swa_weights(w_qkv, b_qkv, w_o):
    half = SWA_ROT // 2
    n_rot_heads = SWA_HEADS + SWA_KV_HEADS

    def swap_cols(w):
        lead = w.shape[:-1]
        wh = w[..., :SWA_NQ + SWA_NK].reshape(lead + (n_rot_heads, SWA_HEAD_DIM))
        sw = jnp.concatenate([_pair_swap(wh[..., :SWA_ROT], half),
                              jnp.zeros(lead + (n_rot_heads, SWA_HEAD_DIM - SWA_ROT), F32)], axis=-1)
        return sw.reshape(lead + (SWA_NQ + SWA_NK,))

    w_ext = jnp.concatenate([w_qkv, swap_cols(w_qkv)], axis=-1)
    b_ext = jnp.concatenate([b_qkv, swap_cols(b_qkv)], axis=-1)
    return {"w_qkv": w_ext.astype(BF16), "b_qkv": b_ext.reshape(1, -1), "w_o": w_o.astype(BF16)}


def kernel(x_prompt, x_sample, cache_mla_ckv, cache_mla_kpe, cache_swa_k, cache_swa_v, page_table,
           c_prompt, c_sample, ln1_g, ln2_g, w_mod, b_mod,
           mla_w_in, mla_q_norm, mla_kv_norm, mla_w_uq, mla_w_uk, mla_w_uv, mla_w_o,
           swa_w_qkv, swa_b_qkv, swa_sinks, swa_w_o,
           peer_w_q, peer_sub_keys, peer_u, peer_v, final_g):
    batch, seq, d = x_prompt.shape
    db, dec_seq, _ = x_sample.shape
    depth = w_mod.shape[0]
    past_len = page_table.shape[1] * PAGE_SIZE
    n_p, n_s = batch * seq, db * dec_seq

    rows_p = _Rows(n_p, min(ROW_BLOCK, seq), seq_len=seq)
    rows_s = _Rows(n_s, n_s)
    route_p = _Rows(n_p, ROUTE_BLOCK, seq_len=seq)
    route_s = _Rows(n_s, min(ROUTE_BLOCK, n_s))
    expert_p = _Rows(n_p, EXPERT_ROW_BLOCK, seq_len=seq)
    expert_s = _Rows(n_s, min(EXPERT_ROW_BLOCK, n_s))

    pos_p = jnp.arange(seq)
    pos_s = jnp.tile(past_len + jnp.arange(dec_seq), db)
    mla_tab_p, mla_tab_s = _mla_tables(pos_p), _mla_tables(pos_s)
    swa_tab_p, swa_tab_s = _swa_tables(pos_p), _swa_tables(pos_s)

    m_all = _modulation_all(jnp.concatenate([c_prompt, c_sample], axis=0), w_mod, b_mod)
    u_b = peer_u.astype(BF16)
    v_b = peer_v.astype(BF16)

    x_p = x_prompt.reshape(n_p, d)
    x_s = x_sample.reshape(n_s, d)
    ckv_p, kpe_p, ckv_s, kpe_s = [], [], [], []
    swk_p, swv_p, swk_s, swv_s = [], [], [], []
    n_buf = cache_swa_k.shape[2]

    for i in range(depth):
        j = i // 2
        mod_p = m_all[i, :batch].reshape(batch, 1, 6 * d)
        mod_s = jnp.repeat(m_all[i, batch:], dec_seq, axis=0)
        g1 = ln1_g[i].reshape(1, d)
        g2 = ln2_g[i].reshape(1, d)
        if i % 2 == 0:
            w = _mla_weights(mla_w_in[j], mla_q_norm[j], mla_kv_norm[j], mla_w_uq[j], mla_w_uk[j],
                             mla_w_uv[j], mla_w_o[j])
            ckv, kpe, q, k, vt = _mla_project(rows_p, x_p, g1, mod_p, mla_tab_p, w)
            o = _mla_attention(q, k, vt, batch, seq)
            x_p = _out_proj(rows_p, o, w["w_o"], x_p, mod_p)
            ckv_p.append(ckv.reshape(batch, seq, MLA_KV_LORA))
            kpe_p.append(kpe.reshape(batch, seq, MLA_ROPE))

            ckv, kpe, q, _, _ = _mla_project(rows_s, x_s, g1, mod_s, mla_tab_s, w)
            qlat = _mla_absorb_queries(q, w["w_ukt"]).reshape(db, dec_seq * MLA_HEADS, MLA_KV_LORA)
            qpe = q.reshape(n_s, MLA_HEADS, MLA_HEAD_PAD)[:, :, MLA_NOPE:MLA_NOPE + MLA_ROPE]
            qpe = qpe.reshape(db, dec_seq * MLA_HEADS, MLA_ROPE)
            ckv3 = ckv.reshape(db, dec_seq, MLA_KV_LORA)
            kpe3 = kpe.reshape(db, dec_seq, MLA_ROPE)
            olat = _mla_paged_attention(qlat, qpe, ckv3, kpe3, cache_mla_ckv, cache_mla_kpe, page_table, j)
            olat = olat.reshape(n_s, MLA_HEADS * MLA_KV_LORA)
            x_s = _mla_sample_out(rows_s, olat, w["w_uv"], w["w_o"], x_s, mod_s)
            ckv_s.append(ckv3)
            kpe_s.append(kpe3)
        else:
            w = _swa_weights(swa_w_qkv[j], swa_b_qkv[j], swa_w_o[j])
            q, kb, vb, k, v = _swa_project(rows_p, x_p, g1, mod_p, swa_tab_p, w)
            o = _swa_prompt_attention(q, kb, vb, swa_sinks[j], batch, seq)
            x_p = _out_proj(rows_p, o, w["w_o"], x_p, mod_p)
            nb = min(WINDOW, seq)
            swk_p.append(k.reshape(batch, seq, SWA_KV_HEADS, SWA_HEAD_DIM)[:, seq - nb:])
            swv_p.append(v.reshape(batch, seq, SWA_KV_HEADS, SWA_HEAD_DIM)[:, seq - nb:])

            q, kb, vb, k, v = _swa_project(rows_s, x_s, g1, mod_s, swa_tab_s, w)
            buf_k = cache_swa_k[j].reshape(db, n_buf, SWA_NK)
            buf_v = cache_swa_v[j].reshape(db, n_buf, SWA_NK)
            o = _swa_sample_attention(q, k, v, buf_k, buf_v, swa_sinks[j], db, dec_seq)
            x_s = _out_proj(rows_s, o, w["w_o"], x_s, mod_s)
            k_all = jnp.concatenate([buf_k, k.reshape(db, dec_seq, SWA_NK)], axis=1)[:, -n_buf:]
            v_all = jnp.concatenate([buf_v, v.reshape(db, dec_seq, SWA_NK)], axis=1)[:, -n_buf:]
            swk_s.append(k_all.reshape(db, n_buf, SWA_KV_HEADS, SWA_HEAD_DIM))
            swv_s.append(v_all.reshape(db, n_buf, SWA_KV_HEADS, SWA_HEAD_DIM))

        pw = {"w_q": peer_w_q[i].astype(BF16),
              "sub_keys": peer_sub_keys[i].reshape(2 * PEER_HEADS, PEER_N_KEYS, PEER_HALF).astype(BF16)}
        x_p = _peer_fused(expert_p, x_p, g2, mod_p, pw, u_b, v_b, i)
        h, experts, gates = _peer_route(route_s, x_s, g2, mod_s, pw)
        x_s = _peer_experts(expert_s, h, experts, gates, u_b, v_b, i, x_s, mod_s)

    fg = final_g.reshape(1, d)
    y_p = _final_norm(rows_p, x_p, fg).reshape(batch, seq, d)
    y_s = _final_norm(rows_s, x_s, fg).reshape(db, dec_seq, d)
    return (y_p, y_s,
            jnp.stack(ckv_p), jnp.stack(kpe_p), jnp.stack(ckv_s), jnp.stack(kpe_s),
            jnp.stack(swk_p), jnp.stack(swv_p), jnp.stack(swk_s), jnp.stack(swv_s))
```

```python
import functools

import numpy as np
import jax
import jax.numpy as jnp
from jax import lax
from jax.experimental import pallas as pl
from jax.experimental.pallas import tpu as pltpu

F32 = jnp.float32
BF16 = jnp.bfloat16
I32 = jnp.int32

D_MODEL = 1024
PAGE_SIZE = 128
ROPE_THETA = 500000.0
NORM_EPS = 1e-6
NEG_INF = -1e30

MLA_HEADS = 16
MLA_Q_LORA = 384
MLA_KV_LORA = 256
MLA_NOPE = 64
MLA_ROPE = 32
MLA_V = 64
MLA_SCALE = (MLA_NOPE + MLA_ROPE) ** -0.5

SWA_HEADS = 16
SWA_KV_HEADS = 4
SWA_GROUP = SWA_HEADS // SWA_KV_HEADS
SWA_HEAD_DIM = D_MODEL // SWA_HEADS
SWA_ROT = SWA_HEAD_DIM // 4
SWA_SCALE = SWA_HEAD_DIM ** -0.5
WINDOW = 128

PEER_HEADS = 8
PEER_N_KEYS = 128
PEER_TOPK = 16
PEER_HALF = 128
PEER_PICKS = PEER_HEADS * PEER_TOPK

LANES = 128
SUBLANES = 8
BF16_ROWS = 16
VMEM_LIMIT_BYTES = 56 * 1024 * 1024

ROW_BLOCK = 512
EXPERT_ROW_BLOCK = 512
TOKEN_UNROLL = 64
ROUTE_BLOCK = 256
ROUTE_UNROLL = 4
MLA_ATTN_BLOCK = 1024
SWA_Q_BLOCK = 256
PAGES_PER_STEP = 16
PAGE_GROUP = 16
EXPERT_CHUNK_A = 16
GATE_TILE_PITCH = 72
MOD_COL_BLOCK = 1536


def _cparams(sem):
    return pltpu.CompilerParams(dimension_semantics=sem, vmem_limit_bytes=VMEM_LIMIT_BYTES)


def _dot(a, b):
    return jnp.dot(a, b, preferred_element_type=F32)


def _dot_nt(a, b):
    return lax.dot_general(a, b, (((1,), (1,)), ((), ())), preferred_element_type=F32)


def _rms(x, g):
    return x * lax.rsqrt(jnp.mean(x * x, axis=-1, keepdims=True) + NORM_EPS) * g


def _gelu(x):
    return 0.5 * x * (1.0 + lax.erf(x * np.float32(2.0 ** -0.5)))


def _modulate(x, g, shift, scale):
    return _rms(x, g) * (1.0 + scale) + shift


class _Rows:
    def __init__(self, n, tb, seq_len=None):
        assert n % tb == 0
        self.n, self.tb, self.grid = n, tb, n // tb
        self.bps = None
        if seq_len is not None:
            assert seq_len % tb == 0
            self.bps = seq_len // tb

    def rows(self, width):
        return pl.BlockSpec((self.tb, width), lambda i: (i, 0))

    def mod(self, k):
        if self.bps is None:
            return pl.BlockSpec((self.tb, D_MODEL), lambda i: (i, k))
        bps = self.bps
        return pl.BlockSpec((None, 1, D_MODEL), lambda i: (i // bps, 0, k))

    def pos(self, width):
        if self.bps is None:
            return pl.BlockSpec((self.tb, width), lambda i: (i, 0))
        bps = self.bps
        return pl.BlockSpec((self.tb, width), lambda i: (i % bps, 0))

    @staticmethod
    def full(shape):
        nd = len(shape)
        return pl.BlockSpec(shape, lambda i: (0,) * nd)


def _mod_kernel(c_ref, w_ref, b_ref, o_ref):
    c = c_ref[...]
    a = (c * jax.nn.sigmoid(c)).astype(BF16)
    o_ref[...] = _dot(a, w_ref[...].astype(BF16)) + b_ref[...]


def _modulation_all(c_all, w_mod, b_mod):
    depth, d, n6 = w_mod.shape
    nc = c_all.shape[0]
    nb = n6 // MOD_COL_BLOCK
    return pl.pallas_call(
        _mod_kernel,
        grid=(depth, nb),
        in_specs=[
            pl.BlockSpec((nc, d), lambda l, j: (0, 0)),
            pl.BlockSpec((None, d, MOD_COL_BLOCK), lambda l, j: (l, 0, j)),
            pl.BlockSpec((None, 1, MOD_COL_BLOCK), lambda l, j: (l, 0, j)),
        ],
        out_specs=pl.BlockSpec((None, nc, MOD_COL_BLOCK), lambda l, j: (l, 0, j)),
        out_shape=jax.ShapeDtypeStruct((depth, nc, n6), F32),
        compiler_params=_cparams(("arbitrary", "arbitrary")),
        name="adaln_modulation",
    )(c_all, w_mod, b_mod.reshape(depth, 1, n6))


MLA_IN_EXT = 896
MLA_KPE_COL = 640
MLA_KPE_SWAP_COL = 768
MLA_HEAD_PAD = LANES


def _mla_proj_kernel(x_ref, g_ref, sh_ref, sc_ref, c32_ref, s32_ref, c128_ref, s128_ref,
                     win_ref, qn_ref, kvn_ref, wuq_ref, wuqs_ref, wuk_ref, epe_ref, wuvt_ref,
                     ckv_ref, kpe_ref, q_ref, k_ref, vt_ref):
    h = _modulate(x_ref[...], g_ref[...], sh_ref[...], sc_ref[...]).astype(BF16)
    z = _dot(h, win_ref[...])
    cq = _rms(z[:, :MLA_Q_LORA], qn_ref[...]).astype(BF16)
    ckv = _rms(z[:, MLA_Q_LORA:MLA_Q_LORA + MLA_KV_LORA], kvn_ref[...])
    kpe = (z[:, MLA_KPE_COL:MLA_KPE_COL + MLA_ROPE] * c32_ref[...]
           + z[:, MLA_KPE_SWAP_COL:MLA_KPE_SWAP_COL + MLA_ROPE] * s32_ref[...])
    ckv_ref[...] = ckv
    kpe_ref[...] = kpe
    ckv_b = ckv.astype(BF16)
    q = _dot(cq, wuq_ref[...])
    qs = _dot(cq, wuqs_ref[...])
    cos = c128_ref[...]
    sin = s128_ref[...]
    for hd in range(MLA_HEADS):
        sl = slice(hd * MLA_HEAD_PAD, (hd + 1) * MLA_HEAD_PAD)
        q_ref[:, sl] = (q[:, sl] * cos + qs[:, sl] * sin).astype(BF16)
    k_ref[...] = (_dot(ckv_b, wuk_ref[...]) + _dot(kpe.astype(BF16), epe_ref[...])).astype(BF16)
    vt = _dot_nt(wuvt_ref[...], ckv_b).astype(BF16)
    if len(vt_ref.shape) == 2:
        vt_ref[...] = vt
    else:
        chunks, _, width = vt_ref.shape
        for c in range(chunks):
            vt_ref[c] = vt[:, c * width:(c + 1) * width]


def _mla_project(rows, x, ln_g, mod, tabs, w, chunk):
    n = rows.n
    hp = MLA_HEADS * MLA_HEAD_PAD
    full = _Rows.full
    nv = MLA_HEADS * MLA_V
    if rows.tb >= chunk:
        vt_spec = pl.BlockSpec((rows.tb // chunk, nv, chunk), lambda i: (i, 0, 0))
    else:
        per_chunk = chunk // rows.tb
        vt_spec = pl.BlockSpec((None, nv, rows.tb), lambda i: (i // per_chunk, 0, i % per_chunk))
    return pl.pallas_call(
        _mla_proj_kernel,
        grid=(rows.grid,),
        in_specs=[
            rows.rows(D_MODEL), full((1, D_MODEL)), rows.mod(0), rows.mod(1),
            rows.pos(MLA_ROPE), rows.pos(MLA_ROPE), rows.pos(MLA_HEAD_PAD), rows.pos(MLA_HEAD_PAD),
            full((D_MODEL, MLA_IN_EXT)), full((1, MLA_Q_LORA)), full((1, MLA_KV_LORA)),
            full((MLA_Q_LORA, hp)), full((MLA_Q_LORA, hp)), full((MLA_KV_LORA, hp)),
            full((MLA_ROPE, hp)), full((nv, MLA_KV_LORA)),
        ],
        out_specs=[rows.rows(MLA_KV_LORA), rows.rows(MLA_ROPE), rows.rows(hp), rows.rows(hp),
                   vt_spec],
        out_shape=[
            jax.ShapeDtypeStruct((n, MLA_KV_LORA), F32),
            jax.ShapeDtypeStruct((n, MLA_ROPE), F32),
            jax.ShapeDtypeStruct((n, hp), BF16),
            jax.ShapeDtypeStruct((n, hp), BF16),
            jax.ShapeDtypeStruct((n // chunk, nv, chunk), BF16),
        ],
        compiler_params=_cparams(("arbitrary",)),
        name="mla_project",
    )(x, ln_g, mod, mod, tabs["c32"], tabs["s32"], tabs["c128"], tabs["s128"],
      w["w_in"], w["q_norm"], w["kv_norm"], w["w_uq"], w["w_uq_swap"], w["w_uk"], w["e_pe"], w["w_uvt"])


def _mla_attn_kernel(q_ref, k_ref, vt_ref, o_ref, *, tq, tk):
    assert tq == tk
    qi = pl.program_id(2)
    causal = lax.broadcasted_iota(I32, (tk, tq), 0) <= lax.broadcasted_iota(I32, (tk, tq), 1)
    v_row = lax.broadcasted_iota(I32, (LANES, tk), 0)
    qs = [q_ref[:, hh * LANES:(hh + 1) * LANES] for hh in range(2)]
    v_keep = [(v_row // MLA_V) == hh for hh in range(2)]

    def step(j, carry, masked):
        start = pl.multiple_of(j * tk, tk)
        vt = vt_ref[j]
        new = []
        for hh in range(2):
            m, l, acc = carry[hh]
            kb = k_ref[pl.ds(start, tk), hh * LANES:(hh + 1) * LANES]
            s = _dot_nt(kb, qs[hh]) * MLA_SCALE
            if masked:
                s = jnp.where(causal, s, NEG_INF)
            m_new = jnp.maximum(m, jnp.max(s, axis=0, keepdims=True))
            alpha = jnp.exp(m - m_new)
            p = jnp.exp(s - m_new)
            l = alpha * l + jnp.sum(p, axis=0, keepdims=True)
            acc = alpha * acc + _dot(jnp.where(v_keep[hh], vt, jnp.zeros_like(vt)), p.astype(BF16))
            new.append((m_new, l, acc))
        return tuple(new)

    head0 = (jnp.full((1, tq), NEG_INF, F32), jnp.zeros((1, tq), F32), jnp.zeros((LANES, tq), F32))
    carry = lax.fori_loop(0, qi, lambda j, c: step(j, c, False), (head0, head0))
    (_, l0, acc0), (_, l1, acc1) = step(qi, carry, True)
    o_ref[...] = (acc0 / l0 + acc1 / l1).T.astype(BF16)


def _mla_attention(q, k, vt, batch, seq):
    tq = tk = vt.shape[2]
    nq = seq // tq
    q3 = q.reshape(batch, seq, -1)
    k3 = k.reshape(batch, seq, -1)
    out = pl.pallas_call(
        functools.partial(_mla_attn_kernel, tq=tq, tk=tk),
        grid=(batch, MLA_HEADS // 2, nq),
        in_specs=[
            pl.BlockSpec((None, tq, 2 * LANES), lambda b, hp, i: (b, i, hp)),
            pl.BlockSpec((None, seq, 2 * LANES), lambda b, hp, i: (b, 0, hp)),
            pl.BlockSpec((seq // tk, LANES, tk), lambda b, hp, i: (b, hp, 0)),
        ],
        out_specs=pl.BlockSpec((None, tq, LANES), lambda b, hp, i: (b, i, hp)),
        out_shape=jax.ShapeDtypeStruct((batch, seq, MLA_HEADS * MLA_V), BF16),
        compiler_params=_cparams(("arbitrary", "arbitrary", "arbitrary")),
        name="mla_prompt_attention",
    )(q3, k3, vt)
    return out.reshape(batch * seq, MLA_HEADS * MLA_V)


def _qlat_kernel(q_ref, wk_ref, o_ref):
    for hd in range(MLA_HEADS):
        qh = q_ref[:, hd * MLA_HEAD_PAD:(hd + 1) * MLA_HEAD_PAD]
        o_ref[:, hd * MLA_KV_LORA:(hd + 1) * MLA_KV_LORA] = _dot(qh, wk_ref[hd]).astype(BF16)


def _mla_absorb_queries(q, w_ukt):
    n = q.shape[0]
    return pl.pallas_call(
        _qlat_kernel,
        grid=(1,),
        in_specs=[_Rows.full(q.shape), _Rows.full(w_ukt.shape)],
        out_specs=_Rows.full((n, MLA_HEADS * MLA_KV_LORA)),
        out_shape=jax.ShapeDtypeStruct((n, MLA_HEADS * MLA_KV_LORA), BF16),
        compiler_params=_cparams(("arbitrary",)),
        name="mla_absorb_queries",
    )(q, w_ukt)


def _paged_attn_kernel(pt_ref, qlat_ref, qpe_ref, nckv_ref, nkpe_ref, *rest, n_steps, dec_seq):
    pg = PAGES_PER_STEP
    ckv_refs = rest[:pg]
    kpe_refs = rest[pg:2 * pg]
    o_ref = rest[2 * pg]
    m_sc, l_sc, acc_sc, kc_sc, kp_sc = rest[2 * pg + 1:]
    g = pl.program_id(1)
    rows = qlat_ref.shape[0]

    @pl.when(g == 0)
    def _():
        m_sc[...] = jnp.full(m_sc.shape, NEG_INF, F32)
        l_sc[...] = jnp.zeros(l_sc.shape, F32)
        acc_sc[...] = jnp.zeros(acc_sc.shape, F32)

    qlat = qlat_ref[...]
    qpe = qpe_ref[...]

    def column(row_vec):
        return jnp.broadcast_to(row_vec, (LANES, rows)).T[:, :1]

    def update(s_t, vals):
        m = m_sc[...]
        m_new = jnp.maximum(m, jnp.max(s_t, axis=0, keepdims=True))
        alpha = jnp.exp(m - m_new)
        p_t = jnp.exp(s_t - m_new)
        l_sc[...] = alpha * l_sc[...] + jnp.sum(p_t, axis=0, keepdims=True)
        acc_sc[...] = column(alpha) * acc_sc[...] + _dot(p_t.T.astype(BF16), vals)
        m_sc[...] = m_new

    def scores_t(kc, kp):
        return (_dot_nt(kc, qlat) + _dot_nt(kp, qpe)) * MLA_SCALE

    for grp in range(pg // PAGE_GROUP):
        for i in range(PAGE_GROUP):
            sl = slice(i * PAGE_SIZE, (i + 1) * PAGE_SIZE)
            kc_sc[sl, :] = ckv_refs[grp * PAGE_GROUP + i][...].astype(BF16)
            kp_sc[sl, :] = kpe_refs[grp * PAGE_GROUP + i][...].T.astype(BF16)
        kc = kc_sc[...]
        update(scores_t(kc, kp_sc[...]), kc)

    @pl.when(g == n_steps - 1)
    def _():
        pad = PAGE_SIZE - dec_seq
        ck = jnp.concatenate([nckv_ref[...], jnp.zeros((pad, MLA_KV_LORA), F32)], axis=0).astype(BF16)
        kp = jnp.concatenate([nkpe_ref[...], jnp.zeros((pad, MLA_ROPE), F32)], axis=0).astype(BF16)
        s_t = scores_t(ck, kp)
        k_t = lax.broadcasted_iota(I32, s_t.shape, 0)
        q_t = lax.broadcasted_iota(I32, s_t.shape, 1) // MLA_HEADS
        update(jnp.where(k_t <= q_t, s_t, NEG_INF), ck)
        o_ref[...] = acc_sc[...] / column(l_sc[...])


def _mla_paged_attention(qlat, qpe, new_ckv, new_kpe, cache_ckv, cache_kpe, page_table, layer):
    db, rows, _ = qlat.shape
    dec_seq = new_ckv.shape[1]
    n_pages = page_table.shape[1]
    pg = PAGES_PER_STEP
    assert n_pages % pg == 0
    n_steps = n_pages // pg

    def page_spec(shape, i):
        return pl.BlockSpec((None, None) + shape, lambda b, g, pt: (layer, pt[b, g * pg + i], 0, 0))

    cache_kpe_t = jnp.swapaxes(cache_kpe, 2, 3)
    in_specs = [
        pl.BlockSpec((None, rows, MLA_KV_LORA), lambda b, g, pt: (b, 0, 0)),
        pl.BlockSpec((None, rows, MLA_ROPE), lambda b, g, pt: (b, 0, 0)),
        pl.BlockSpec((None, dec_seq, MLA_KV_LORA), lambda b, g, pt: (b, 0, 0)),
        pl.BlockSpec((None, dec_seq, MLA_ROPE), lambda b, g, pt: (b, 0, 0)),
    ]
    in_specs += [page_spec((PAGE_SIZE, MLA_KV_LORA), i) for i in range(pg)]
    in_specs += [page_spec((MLA_ROPE, PAGE_SIZE), i) for i in range(pg)]
    grid_spec = pltpu.PrefetchScalarGridSpec(
        num_scalar_prefetch=1,
        grid=(db, n_steps),
        in_specs=in_specs,
        out_specs=pl.BlockSpec((None, rows, MLA_KV_LORA), lambda b, g, pt: (b, 0, 0)),
        scratch_shapes=[pltpu.VMEM((1, rows), F32), pltpu.VMEM((1, rows), F32),
                        pltpu.VMEM((rows, MLA_KV_LORA), F32),
                        pltpu.VMEM((PAGE_GROUP * PAGE_SIZE, MLA_KV_LORA), BF16),
                        pltpu.VMEM((PAGE_GROUP * PAGE_SIZE, MLA_ROPE), BF16)],
    )
    return pl.pallas_call(
        functools.partial(_paged_attn_kernel, n_steps=n_steps, dec_seq=dec_seq),
        grid_spec=grid_spec,
        out_shape=jax.ShapeDtypeStruct((db, rows, MLA_KV_LORA), F32),
        compiler_params=_cparams(("arbitrary", "arbitrary")),
        name="mla_paged_attention",
    )(page_table, qlat, qpe, new_ckv, new_kpe, *([cache_ckv] * pg), *([cache_kpe_t] * pg))


def _sample_out_kernel(olat_ref, wuv_ref, wo_ref, x_ref, gate_ref, o_ref):
    col_head = lax.broadcasted_iota(I32, (1, MLA_HEADS * MLA_V), 1) // MLA_V
    wuv = wuv_ref[...]
    o = jnp.zeros((olat_ref.shape[0], MLA_HEADS * MLA_V), F32)
    for hd in range(MLA_HEADS):
        lat = olat_ref[:, hd * MLA_KV_LORA:(hd + 1) * MLA_KV_LORA].astype(BF16)
        o = jnp.where(col_head == hd, _dot(lat, wuv), o)
    o_ref[...] = x_ref[...] + gate_ref[...] * _dot(o.astype(BF16), wo_ref[...])


def _mla_sample_out(rows, olat, w_uv, w_o, x, mod):
    return pl.pallas_call(
        _sample_out_kernel,
        grid=(rows.grid,),
        in_specs=[rows.rows(olat.shape[1]), _Rows.full(w_uv.shape), _Rows.full(w_o.shape),
                  rows.rows(D_MODEL), rows.mod(2)],
        out_specs=rows.rows(D_MODEL),
        out_shape=jax.ShapeDtypeStruct((rows.n, D_MODEL), F32),
        compiler_params=_cparams(("arbitrary",)),
        name="mla_sample_out",
    )(olat, w_uv, w_o, x, mod)


def _out_proj_kernel(o_ref, wo_ref, x_ref, gate_ref, y_ref):
    y_ref[...] = x_ref[...] + gate_ref[...] * _dot(o_ref[...], wo_ref[...])


def _out_proj(rows, o, w_o, x, mod):
    return pl.pallas_call(
        _out_proj_kernel,
        grid=(rows.grid,),
        in_specs=[rows.rows(o.shape[1]), _Rows.full(w_o.shape), rows.rows(D_MODEL), rows.mod(2)],
        out_specs=rows.rows(D_MODEL),
        out_shape=jax.ShapeDtypeStruct((rows.n, D_MODEL), F32),
        compiler_params=_cparams(("arbitrary",)),
        name="attn_out_proj",
    )(o, w_o, x, mod)


SWA_NQ = SWA_HEADS * SWA_HEAD_DIM
SWA_NK = SWA_KV_HEADS * SWA_HEAD_DIM
SWA_QKV = SWA_NQ + 2 * SWA_NK
SWA_EXT = SWA_QKV + SWA_NQ + SWA_NK


def _swa_proj_kernel(x_ref, g_ref, sh_ref, sc_ref, cos_ref, sin_ref, w_ref, b_ref,
                     q_ref, kb_ref, vb_ref, k_ref, v_ref):
    h = _modulate(x_ref[...], g_ref[...], sh_ref[...], sc_ref[...]).astype(BF16)
    z = _dot(h, w_ref[...]) + b_ref[...]
    cos = cos_ref[...]
    sin = sin_ref[...]
    for t in range((SWA_NQ + SWA_NK) // LANES):
        sl = slice(t * LANES, (t + 1) * LANES)
        sw = slice(SWA_QKV + t * LANES, SWA_QKV + (t + 1) * LANES)
        r = z[:, sl] * cos + z[:, sw] * sin
        if t < SWA_NQ // LANES:
            q_ref[:, sl] = (r * SWA_SCALE).astype(BF16)
        else:
            ks = slice(t * LANES - SWA_NQ, (t + 1) * LANES - SWA_NQ)
            k_ref[:, ks] = r
            kb_ref[:, ks] = r.astype(BF16)
    v = z[:, SWA_NQ + SWA_NK:SWA_QKV]
    v_ref[...] = v
    vb_ref[...] = v.astype(BF16)


def _swa_project(rows, x, ln_g, mod, tabs, w):
    n = rows.n
    full = _Rows.full
    return pl.pallas_call(
        _swa_proj_kernel,
        grid=(rows.grid,),
        in_specs=[rows.rows(D_MODEL), full((1, D_MODEL)), rows.mod(0), rows.mod(1),
                  rows.pos(LANES), rows.pos(LANES), full((D_MODEL, SWA_EXT)), full((1, SWA_EXT))],
        out_specs=[rows.rows(SWA_NQ), rows.rows(SWA_NK), rows.rows(SWA_NK), rows.rows(SWA_NK),
                   rows.rows(SWA_NK)],
        out_shape=[
            jax.ShapeDtypeStruct((n, SWA_NQ), BF16),
            jax.ShapeDtypeStruct((n, SWA_NK), BF16),
            jax.ShapeDtypeStruct((n, SWA_NK), BF16),
            jax.ShapeDtypeStruct((n, SWA_NK), F32),
            jax.ShapeDtypeStruct((n, SWA_NK), F32),
        ],
        compiler_params=_cparams(("arbitrary",)),
        name="swa_project",
    )(x, ln_g, mod, mod, tabs["cos"], tabs["sin"], w["w_qkv"], w["b_qkv"])


def _swa_core(q_all, k_all, v_all, mask, sinks_ref, o_ref):
    lane = lax.broadcasted_iota(I32, (1, LANES), 1)
    lo = lane < SWA_HEAD_DIM
    zero = jnp.zeros((), BF16)
    for kh in range(SWA_KV_HEADS):
        tile = kh // 2
        k_t = k_all[:, tile * LANES:(tile + 1) * LANES]
        v_t = v_all[:, tile * LANES:(tile + 1) * LANES]
        k_r = pltpu.roll(k_t, SWA_HEAD_DIM, 1)
        v_r = pltpu.roll(v_t, SWA_HEAD_DIM, 1)
        for pair in range(SWA_GROUP // 2):
            q_t = q_all[:, (kh * 2 + pair) * LANES:(kh * 2 + pair + 1) * LANES]
            out = None
            for half in range(2):
                keep = lo if half == 0 else jnp.logical_not(lo)
                kx = k_t if (kh % 2) == half else k_r
                vx = v_t if (kh % 2) == half else v_r
                s = _dot_nt(jnp.where(keep, q_t, zero), kx)
                s = jnp.where(mask, s, NEG_INF)
                sink = sinks_ref[kh * SWA_GROUP + 2 * pair + half]
                m = jnp.maximum(jnp.max(s, axis=-1, keepdims=True), sink)
                p = jnp.exp(s - m)
                l = jnp.sum(p, axis=-1, keepdims=True) + jnp.exp(sink - m)
                o = _dot((p / l).astype(BF16), jnp.where(keep, vx, zero))
                out = o if out is None else out + o
            o_ref[:, (kh * 2 + pair) * LANES:(kh * 2 + pair + 1) * LANES] = out.astype(o_ref.dtype)


def _swa_prompt_kernel(sinks_ref, q_ref, kc_ref, kp_ref, vc_ref, vp_ref, o_ref, *, tq):
    i = pl.program_id(1)
    k_all = jnp.concatenate([kp_ref[...], kc_ref[...]], axis=0)
    v_all = jnp.concatenate([vp_ref[...], vc_ref[...]], axis=0)
    tk = tq + WINDOW
    q_pos = i * tq + lax.broadcasted_iota(I32, (tq, tk), 0)
    k_pos = i * tq - WINDOW + lax.broadcasted_iota(I32, (tq, tk), 1)
    mask = (k_pos >= 0) & (k_pos <= q_pos) & (q_pos - k_pos < WINDOW)
    _swa_core(q_ref[...], k_all, v_all, mask, sinks_ref, o_ref)


def _swa_prompt_attention(q, kb, vb, sinks, batch, seq):
    tq = SWA_Q_BLOCK
    r = tq // WINDOW
    q3 = q.reshape(batch, seq, SWA_NQ)
    k3 = kb.reshape(batch, seq, SWA_NK)
    v3 = vb.reshape(batch, seq, SWA_NK)
    cur = pl.BlockSpec((None, tq, SWA_NK), lambda b, i: (b, i, 0))
    prev = pl.BlockSpec((None, WINDOW, SWA_NK), lambda b, i: (b, jnp.maximum(i * r - 1, 0), 0))
    out = pl.pallas_call(
        functools.partial(_swa_prompt_kernel, tq=tq),
        grid=(batch, seq // tq),
        in_specs=[pl.BlockSpec(memory_space=pltpu.SMEM),
                  pl.BlockSpec((None, tq, SWA_NQ), lambda b, i: (b, i, 0)), cur, prev, cur, prev],
        out_specs=pl.BlockSpec((None, tq, SWA_NQ), lambda b, i: (b, i, 0)),
        out_shape=jax.ShapeDtypeStruct((batch, seq, SWA_NQ), BF16),
        compiler_params=_cparams(("arbitrary", "arbitrary")),
        name="swa_prompt_attention",
    )(sinks, q3, k3, k3, v3, v3)
    return out.reshape(batch * seq, SWA_NQ)


def _swa_sample_kernel(sinks_ref, q_ref, kbuf_ref, vbuf_ref, kn_ref, vn_ref, o_ref, o_sc, *, n_buf, dec_seq):
    pad = jnp.zeros((BF16_ROWS - dec_seq, SWA_NK), F32)
    k_all = jnp.concatenate([kbuf_ref[...], kn_ref[...], pad], axis=0).astype(BF16)
    v_all = jnp.concatenate([vbuf_ref[...], vn_ref[...], pad], axis=0).astype(BF16)
    tk = n_buf + BF16_ROWS
    t = lax.broadcasted_iota(I32, (BF16_ROWS, tk), 0)
    c = lax.broadcasted_iota(I32, (BF16_ROWS, tk), 1)
    in_buf = c < n_buf
    mask = ((in_buf & ((n_buf + t - c) < WINDOW))
            | (jnp.logical_not(in_buf) & ((c - n_buf) <= jnp.minimum(t, dec_seq - 1))))
    q = jnp.concatenate([q_ref[...].astype(F32), jnp.zeros((BF16_ROWS - dec_seq, SWA_NQ), F32)], axis=0)
    _swa_core(q.astype(BF16), k_all, v_all, mask, sinks_ref, o_sc)
    o_ref[...] = o_sc[:dec_seq, :].astype(o_ref.dtype)


def _swa_sample_attention(q, kn, vn, buf_k, buf_v, sinks, db, dec_seq):
    n_buf = buf_k.shape[1]
    spec = lambda rws, w: pl.BlockSpec((None, rws, w), lambda b: (b, 0, 0))
    out = pl.pallas_call(
        functools.partial(_swa_sample_kernel, n_buf=n_buf, dec_seq=dec_seq),
        grid=(db,),
        in_specs=[pl.BlockSpec(memory_space=pltpu.SMEM), spec(dec_seq, SWA_NQ), spec(n_buf, SWA_NK),
                  spec(n_buf, SWA_NK), spec(dec_seq, SWA_NK), spec(dec_seq, SWA_NK)],
        out_specs=spec(dec_seq, SWA_NQ),
        out_shape=jax.ShapeDtypeStruct((db, dec_seq, SWA_NQ), BF16),
        scratch_shapes=[pltpu.VMEM((BF16_ROWS, SWA_NQ), F32)],
        compiler_params=_cparams(("arbitrary",)),
        name="swa_sample_attention",
    )(sinks, q.reshape(db, dec_seq, SWA_NQ), buf_k, buf_v,
      kn.reshape(db, dec_seq, SWA_NK), vn.reshape(db, dec_seq, SWA_NK))
    return out.reshape(db * dec_seq, SWA_NQ)


def _peer_candidate_tables(width_tokens):
    k = PEER_TOPK
    flat, valid = [], []
    for i in range(8):
        width = 16 if i == 0 else 8
        for j in range(width):
            flat.append(i * k + j)
            valid.append((i + 1) * (j + 1) <= k)
    for i in range(8, 16):
        flat.append(i * k)
        valid.append(True)
    flat = np.broadcast_to(np.asarray(flat, np.float32)[:, None], (len(flat), width_tokens))
    pen = np.where(np.asarray(valid), 0.0, -np.inf).astype(np.float32)
    return np.ascontiguousarray(flat), np.ascontiguousarray(np.broadcast_to(pen[:, None], flat.shape))


def _extract_top(s, row_id, k, payload=None):
    big = jnp.float32(2 ** 30)
    vals, ids = [], []
    for _ in range(k):
        m = jnp.max(s, axis=0, keepdims=True)
        sel = jnp.min(jnp.where(s == m, row_id, big), axis=0, keepdims=True)
        hit = row_id == sel
        if payload is not None:
            ids.append(jnp.max(jnp.where(hit, payload, -1.0), axis=0, keepdims=True))
        else:
            ids.append(sel)
        vals.append(m)
        s = jnp.where(hit, -jnp.inf, s)
    return jnp.concatenate(vals, axis=0), jnp.concatenate(ids, axis=0)


def _route_queries(x_ref, g_ref, sh_ref, sc_ref, wq_ref, qs_sc):
    h = _modulate(x_ref[...], g_ref[...], sh_ref[...], sc_ref[...]).astype(BF16)
    q = _dot(h, wq_ref[...])
    for grp in range(2 * PEER_HEADS):
        qs_sc[grp] = q[:, grp * PEER_HALF:(grp + 1) * PEER_HALF].astype(BF16)
    return h


def _route_first_level(grp, sk_ref, qs_sc, sv_sc, si_sc, tb):
    key_id = lax.broadcasted_iota(I32, (PEER_N_KEYS, tb), 0).astype(F32)
    s = _dot_nt(sk_ref[grp], qs_sc[grp])
    vals, ids = _extract_top(s, key_id, PEER_TOPK)
    sv_sc[grp] = vals
    si_sc[grp] = ids


def _route_second_level(sv_sc, si_sc, flat_ref, pen_ref, pe_sc, pg_sc):
    k = PEER_TOPK

    def head(hd, carry):
        sa, sb = sv_sc[2 * hd], sv_sc[2 * hd + 1]
        ia, ib = si_sc[2 * hd], si_sc[2 * hd + 1]
        cs, ce = [], []
        for i in range(8):
            width = 16 if i == 0 else 8
            cs.append(sa[i:i + 1] + sb[:width])
            ce.append(ia[i:i + 1] * PEER_N_KEYS + ib[:width])
        cs.append(sa[8:] + sb[:1])
        ce.append(ia[8:] * PEER_N_KEYS + ib[:1])
        cand = jnp.concatenate(cs, axis=0) + pen_ref[...]
        cand_e = jnp.concatenate(ce, axis=0)
        best, experts = _extract_top(cand, flat_ref[...], k, payload=cand_e)
        ex = jnp.exp(best - best[:1])
        gates = ex / jnp.sum(ex, axis=0, keepdims=True)
        pick = pl.ds(pl.multiple_of(hd * k, k), k)
        pe_sc[pick, :] = experts
        pg_sc[pick, :] = gates
        return carry

    lax.fori_loop(0, PEER_HEADS, head, 0, unroll=ROUTE_UNROLL)


def _peer_route_kernel(x_ref, g_ref, sh_ref, sc_ref, wq_ref, sk_ref, flat_ref, pen_ref,
                       h_ref, e_ref, gate_ref, qs_sc, sv_sc, si_sc, pe_sc, pg_sc, *, tb):
    h = _route_queries(x_ref, g_ref, sh_ref, sc_ref, wq_ref, qs_sc)
    h_ref[...] = h

    def first_level(grp, carry):
        _route_first_level(grp, sk_ref, qs_sc, sv_sc, si_sc, tb)
        return carry

    lax.fori_loop(0, 2 * PEER_HEADS, first_level, 0, unroll=ROUTE_UNROLL)
    _route_second_level(sv_sc, si_sc, flat_ref, pen_ref, pe_sc, pg_sc)
    e_ref[...] = pe_sc[...].T.astype(I32)
    gate_ref[...] = pg_sc[...].T


def _peer_route(rows, x, ln_g, mod, w):
    n, tb = rows.n, rows.tb
    flat, pen = _peer_candidate_tables(tb)
    full = _Rows.full
    n_groups = 2 * PEER_HEADS
    return pl.pallas_call(
        functools.partial(_peer_route_kernel, tb=tb),
        grid=(rows.grid,),
        in_specs=[rows.rows(D_MODEL), full((1, D_MODEL)), rows.mod(3), rows.mod(4),
                  full(w["w_q"].shape), full(w["sub_keys"].shape), full(flat.shape), full(pen.shape)],
        out_specs=[rows.rows(D_MODEL), rows.rows(PEER_PICKS), rows.rows(PEER_PICKS)],
        out_shape=[jax.ShapeDtypeStruct((n, D_MODEL), BF16),
                   jax.ShapeDtypeStruct((n, PEER_PICKS), I32),
                   jax.ShapeDtypeStruct((n, PEER_PICKS), F32)],
        scratch_shapes=[pltpu.VMEM((n_groups, tb, PEER_HALF), BF16),
                        pltpu.VMEM((n_groups, PEER_TOPK, tb), F32),
                        pltpu.VMEM((n_groups, PEER_TOPK, tb), F32),
                        pltpu.VMEM((PEER_PICKS, tb), F32),
                        pltpu.VMEM((PEER_PICKS, tb), F32)],
        compiler_params=_cparams(("arbitrary",)),
        name="peer_route",
    )(x, ln_g, mod, mod, w["w_q"], w["sub_keys"], jnp.asarray(flat), jnp.asarray(pen))


def _build_gate_tiles(e_ref, gate_ref, w_sc, tb):
    pitch = GATE_TILE_PITCH
    row = lax.broadcasted_iota(I32, (PEER_N_KEYS, PEER_PICKS), 0)
    half_keys = PEER_N_KEYS // 2
    key_a = jnp.where(row < half_keys, 2 * row, 2 * (row - half_keys) + 1)

    zero_tile = jnp.zeros((PEER_N_KEYS, PEER_PICKS), BF16)

    def operands(t):
        e = e_ref[pl.ds(t, 1), :]
        g = gate_ref[pl.ds(t, 1), :].astype(BF16).astype(F32)
        one_a = jnp.where(key_a == (e >> 7), 1.0, 0.0).astype(BF16)
        g_b = jnp.where(row == (e & (PEER_N_KEYS - 1)), g, 0.0).astype(BF16)
        return one_a, g_b

    def token_pair(i, carry):
        t = 2 * i
        a0, b0 = operands(t)
        a1, b1 = operands(t + 1)
        lhs = jnp.concatenate([a0, a1], axis=1)
        rhs = jnp.concatenate([jnp.concatenate([b0, zero_tile], axis=1),
                               jnp.concatenate([zero_tile, b1], axis=1)], axis=0)
        w = _dot_nt(lhs, rhs)
        even = lax.bitcast_convert_type(w[:half_keys].astype(BF16).astype(F32), jnp.uint32)
        odd = lax.bitcast_convert_type(w[half_keys:].astype(BF16).astype(F32), jnp.uint32)
        word = odd | (even >> 16)
        for j in range(2):
            start = pl.multiple_of((t + j) * pitch, SUBLANES)
            w_sc[pl.ds(start, half_keys), :] = word[:, j * PEER_N_KEYS:(j + 1) * PEER_N_KEYS]
        return carry

    lax.fori_loop(0, tb // 2, token_pair, 0, unroll=TOKEN_UNROLL // 2)


def _expert_chunk(h, u_ref, v_ref, w_sc, p_sc, acc_sc, c, tb):
    for pair in range(EXPERT_CHUNK_A // 2):
        z = _dot_nt(h, u_ref[pair * 2 * PEER_N_KEYS:(pair + 1) * 2 * PEER_N_KEYS, :])
        word = w_sc[pl.ds(c * (EXPERT_CHUNK_A // 2) + pair, tb, stride=GATE_TILE_PITCH), :]
        w_pair = (lax.bitcast_convert_type(word << 16, F32),
                  lax.bitcast_convert_type(word & jnp.uint32(0xFFFF0000), F32))
        for half in range(2):
            al = 2 * pair + half
            sl = slice(al * PEER_N_KEYS, (al + 1) * PEER_N_KEYS)
            p_sc[:, sl] = (w_pair[half] * _gelu(z[:, half * PEER_N_KEYS:(half + 1) * PEER_N_KEYS])).astype(BF16)
    acc_sc[...] += _dot(p_sc[...], v_ref[...])


def _peer_expert_kernel(h_ref, e_ref, gate_ref, u_ref, v_ref, x_ref, g2_ref, o_ref,
                        w_sc, p_sc, acc_sc, *, tb, n_chunks):
    c = pl.program_id(1)

    @pl.when(c == 0)
    def _():
        acc_sc[...] = jnp.zeros(acc_sc.shape, F32)
        _build_gate_tiles(e_ref, gate_ref, w_sc, tb)

    _expert_chunk(h_ref[...], u_ref, v_ref, w_sc, p_sc, acc_sc, c, tb)

    @pl.when(c == n_chunks - 1)
    def _():
        o_ref[...] = x_ref[...] + g2_ref[...] * acc_sc[...]


def _peer_experts(rows, h, experts, gates, u, v, layer, x, mod):
    n, tb = rows.n, rows.tb
    ce = EXPERT_CHUNK_A * PEER_N_KEYS
    n_chunks = u.shape[1] // ce
    bps = rows.bps
    if bps is None:
        gate_spec = pl.BlockSpec((tb, D_MODEL), lambda i, c: (i, 5))
    else:
        gate_spec = pl.BlockSpec((None, 1, D_MODEL), lambda i, c: (i // bps, 0, 5))
    tok = lambda w: pl.BlockSpec((tb, w), lambda i, c: (i, 0))
    tok1 = lambda w: pl.BlockSpec((tb, w), lambda i, c: (i, 0), pipeline_mode=pl.Buffered(1))
    return pl.pallas_call(
        functools.partial(_peer_expert_kernel, tb=tb, n_chunks=n_chunks),
        grid=(rows.grid, n_chunks),
        in_specs=[tok1(D_MODEL), tok(PEER_PICKS), tok(PEER_PICKS),
                  pl.BlockSpec((None, ce, D_MODEL), lambda i, c: (layer, c, 0)),
                  pl.BlockSpec((None, ce, D_MODEL), lambda i, c: (layer, c, 0)),
                  tok1(D_MODEL), gate_spec],
        out_specs=tok1(D_MODEL),
        out_shape=jax.ShapeDtypeStruct((n, D_MODEL), F32),
        scratch_shapes=[pltpu.VMEM((tb * GATE_TILE_PITCH, LANES), jnp.uint32),
                        pltpu.VMEM((tb, ce), BF16),
                        pltpu.VMEM((tb, D_MODEL), F32)],
        compiler_params=_cparams(("arbitrary", "arbitrary")),
        name="peer_experts",
    )(h, experts, gates, u, v, x, mod)


def _final_norm_kernel(x_ref, g_ref, o_ref):
    o_ref[...] = _rms(x_ref[...], g_ref[...])


def _final_norm(rows, x, g):
    return pl.pallas_call(
        _final_norm_kernel,
        grid=(rows.grid,),
        in_specs=[rows.rows(D_MODEL), _Rows.full((1, D_MODEL))],
        out_specs=rows.rows(D_MODEL),
        out_shape=jax.ShapeDtypeStruct((rows.n, D_MODEL), F32),
        compiler_params=_cparams(("arbitrary",)),
        name="final_norm",
    )(x, g)


def _pair_swap(w, half):
    return jnp.concatenate([-w[..., half:2 * half], w[..., :half]], axis=-1)


def _rope_cos_sin(pos, rot_dim):
    half = rot_dim // 2
    inv_freq = ROPE_THETA ** (-jnp.arange(half, dtype=F32) * 2.0 / rot_dim)
    ang = pos.astype(F32)[:, None] * inv_freq[None, :]
    return jnp.cos(ang), jnp.sin(ang)


def _mla_tables(pos):
    cos, sin = _rope_cos_sin(pos, MLA_ROPE)
    n = pos.shape[0]
    ones = jnp.ones((n, MLA_NOPE), F32)
    zeros_n = jnp.zeros((n, MLA_NOPE), F32)
    pad = jnp.zeros((n, MLA_HEAD_PAD - MLA_NOPE - MLA_ROPE), F32)
    return {
        "c32": jnp.concatenate([cos, cos], axis=1),
        "s32": jnp.concatenate([sin, sin], axis=1),
        "c128": jnp.concatenate([ones, cos, cos, pad], axis=1),
        "s128": jnp.concatenate([zeros_n, sin, sin, pad], axis=1),
    }


def _swa_tables(pos):
    cos, sin = _rope_cos_sin(pos, SWA_ROT)
    n = pos.shape[0]
    rest = SWA_HEAD_DIM - SWA_ROT
    c = jnp.concatenate([cos, cos, jnp.ones((n, rest), F32)], axis=1)
    s = jnp.concatenate([sin, sin, jnp.zeros((n, rest), F32)], axis=1)
    reps = LANES // SWA_HEAD_DIM
    return {"cos": jnp.tile(c, (1, reps)), "sin": jnp.tile(s, (1, reps))}


def _mla_weights(w_in, q_norm, kv_norm, w_uq, w_uk, w_uv, w_o):
    d = w_in.shape[0]
    half = MLA_ROPE // 2
    kpe_cols = w_in[:, MLA_Q_LORA + MLA_KV_LORA:]
    w_in_ext = jnp.zeros((d, MLA_IN_EXT), F32)
    w_in_ext = w_in_ext.at[:, :MLA_KPE_COL + MLA_ROPE].set(w_in)
    w_in_ext = w_in_ext.at[:, MLA_KPE_SWAP_COL:MLA_KPE_SWAP_COL + MLA_ROPE].set(_pair_swap(kpe_cols, half))
    uq = w_uq.reshape(MLA_Q_LORA, MLA_HEADS, MLA_NOPE + MLA_ROPE)
    uq_pad = jnp.zeros((MLA_Q_LORA, MLA_HEADS, MLA_HEAD_PAD), F32).at[:, :, :MLA_NOPE + MLA_ROPE].set(uq)
    uq_swap = jnp.zeros((MLA_Q_LORA, MLA_HEADS, MLA_HEAD_PAD), F32)
    uq_swap = uq_swap.at[:, :, MLA_NOPE:MLA_NOPE + MLA_ROPE].set(_pair_swap(uq[:, :, MLA_NOPE:], half))
    uk_pad = jnp.zeros((MLA_KV_LORA, MLA_HEADS, MLA_HEAD_PAD), F32).at[:, :, :MLA_NOPE].set(w_uk)
    e_pe = jnp.zeros((MLA_ROPE, MLA_HEADS, MLA_HEAD_PAD), F32)
    e_pe = e_pe.at[jnp.arange(MLA_ROPE), :, MLA_NOPE + jnp.arange(MLA_ROPE)].set(1.0)
    ukt = jnp.zeros((MLA_HEADS, MLA_HEAD_PAD, MLA_KV_LORA), F32)
    ukt = ukt.at[:, :MLA_NOPE, :].set(jnp.transpose(w_uk, (1, 2, 0)))
    hp = MLA_HEADS * MLA_HEAD_PAD
    return {
        "w_in": w_in_ext.astype(BF16),
        "q_norm": q_norm.reshape(1, -1), "kv_norm": kv_norm.reshape(1, -1),
        "w_uq": uq_pad.reshape(MLA_Q_LORA, hp).astype(BF16),
        "w_uq_swap": uq_swap.reshape(MLA_Q_LORA, hp).astype(BF16),
        "w_uk": uk_pad.reshape(MLA_KV_LORA, hp).astype(BF16),
        "e_pe": e_pe.reshape(MLA_ROPE, hp).astype(BF16),
        "w_uvt": w_uv.reshape(MLA_KV_LORA, MLA_HEADS * MLA_V).T.astype(BF16),
        "w_ukt": ukt.astype(BF16),
        "w_uv": w_uv.reshape(MLA_KV_LORA, MLA_HEADS * MLA_V).astype(BF16),
        "w_o": w_o.astype(BF16),
    }


def _swa_weights(w_qkv, b_qkv, w_o):
    half = SWA_ROT // 2
    n_rot_heads = SWA_HEADS + SWA_KV_HEADS

    def swap_cols(w):
        lead = w.shape[:-1]
        wh = w[..., :SWA_NQ + SWA_NK].reshape(lead + (n_rot_heads, SWA_HEAD_DIM))
        sw = jnp.concatenate([_pair_swap(wh[..., :SWA_ROT], half),
                              jnp.zeros(lead + (n_rot_heads, SWA_HEAD_DIM - SWA_ROT), F32)], axis=-1)
        return sw.reshape(lead + (SWA_NQ + SWA_NK,))

    w_ext = jnp.concatenate([w_qkv, swap_cols(w_qkv)], axis=-1)
    b_ext = jnp.concatenate([b_qkv, swap_cols(b_qkv)], axis=-1)
    return {"w_qkv": w_ext.astype(BF16), "b_qkv": b_ext.reshape(1, -1), "w_o": w_o.astype(BF16)}


def kernel(x_prompt, x_sample, cache_mla_ckv, cache_mla_kpe, cache_swa_k, cache_swa_v, page_table,
           c_prompt, c_sample, ln1_g, ln2_g, w_mod, b_mod,
           mla_w_in, mla_q_norm, mla_kv_norm, mla_w_uq, mla_w_uk, mla_w_uv, mla_w_o,
           swa_w_qkv, swa_b_qkv, swa_sinks, swa_w_o,
           peer_w_q, peer_sub_keys, peer_u, peer_v, final_g):
    batch, seq, d = x_prompt.shape
    db, dec_seq, _ = x_sample.shape
    depth = w_mod.shape[0]
    past_len = page_table.shape[1] * PAGE_SIZE
    n_p, n_s = batch * seq, db * dec_seq

    rows_p = _Rows(n_p, min(ROW_BLOCK, seq), seq_len=seq)
    rows_s = _Rows(n_s, n_s)
    route_p = _Rows(n_p, ROUTE_BLOCK, seq_len=seq)
    route_s = _Rows(n_s, min(ROUTE_BLOCK, n_s))
    expert_p = _Rows(n_p, EXPERT_ROW_BLOCK, seq_len=seq)
    expert_s = _Rows(n_s, min(EXPERT_ROW_BLOCK, n_s))

    pos_p = jnp.arange(seq)
    pos_s = jnp.tile(past_len + jnp.arange(dec_seq), db)
    mla_tab_p, mla_tab_s = _mla_tables(pos_p), _mla_tables(pos_s)
    swa_tab_p, swa_tab_s = _swa_tables(pos_p), _swa_tables(pos_s)

    m_all = _modulation_all(jnp.concatenate([c_prompt, c_sample], axis=0), w_mod, b_mod)
    u_b = peer_u.astype(BF16)
    v_b = peer_v.astype(BF16)

    x_p = x_prompt.reshape(n_p, d)
    x_s = x_sample.reshape(n_s, d)
    ckv_p, kpe_p, ckv_s, kpe_s = [], [], [], []
    swk_p, swv_p, swk_s, swv_s = [], [], [], []
    n_buf = cache_swa_k.shape[2]

    for i in range(depth):
        j = i // 2
        mod_p = m_all[i, :batch].reshape(batch, 1, 6 * d)
        mod_s = jnp.repeat(m_all[i, batch:], dec_seq, axis=0)
        g1 = ln1_g[i].reshape(1, d)
        g2 = ln2_g[i].reshape(1, d)
        if i % 2 == 0:
            w = _mla_weights(mla_w_in[j], mla_q_norm[j], mla_kv_norm[j], mla_w_uq[j], mla_w_uk[j],
                             mla_w_uv[j], mla_w_o[j])
            ckv, kpe, q, k, vt = _mla_project(rows_p, x_p, g1, mod_p, mla_tab_p, w, min(MLA_ATTN_BLOCK, seq))
            o = _mla_attention(q, k, vt, batch, seq)
            x_p = _out_proj(rows_p, o, w["w_o"], x_p, mod_p)
            ckv_p.append(ckv.reshape(batch, seq, MLA_KV_LORA))
            kpe_p.append(kpe.reshape(batch, seq, MLA_ROPE))

            ckv, kpe, q, _, _ = _mla_project(rows_s, x_s, g1, mod_s, mla_tab_s, w, rows_s.tb)
            qlat = _mla_absorb_queries(q, w["w_ukt"]).reshape(db, dec_seq * MLA_HEADS, MLA_KV_LORA)
            qpe = q.reshape(n_s, MLA_HEADS, MLA_HEAD_PAD)[:, :, MLA_NOPE:MLA_NOPE + MLA_ROPE]
            qpe = qpe.reshape(db, dec_seq * MLA_HEADS, MLA_ROPE)
            ckv3 = ckv.reshape(db, dec_seq, MLA_KV_LORA)
            kpe3 = kpe.reshape(db, dec_seq, MLA_ROPE)
            olat = _mla_paged_attention(qlat, qpe, ckv3, kpe3, cache_mla_ckv, cache_mla_kpe, page_table, j)
            olat = olat.reshape(n_s, MLA_HEADS * MLA_KV_LORA)
            x_s = _mla_sample_out(rows_s, olat, w["w_uv"], w["w_o"], x_s, mod_s)
            ckv_s.append(ckv3)
            kpe_s.append(kpe3)
        else:
            w = _swa_weights(swa_w_qkv[j], swa_b_qkv[j], swa_w_o[j])
            q, kb, vb, k, v = _swa_project(rows_p, x_p, g1, mod_p, swa_tab_p, w)
            o = _swa_prompt_attention(q, kb, vb, swa_sinks[j], batch, seq)
            x_p = _out_proj(rows_p, o, w["w_o"], x_p, mod_p)
            nb = min(WINDOW, seq)
            swk_p.append(k.reshape(batch, seq, SWA_KV_HEADS, SWA_HEAD_DIM)[:, seq - nb:])
            swv_p.append(v.reshape(batch, seq, SWA_KV_HEADS, SWA_HEAD_DIM)[:, seq - nb:])

            q, kb, vb, k, v = _swa_project(rows_s, x_s, g1, mod_s, swa_tab_s, w)
            buf_k = cache_swa_k[j].reshape(db, n_buf, SWA_NK)
            buf_v = cache_swa_v[j].reshape(db, n_buf, SWA_NK)
            o = _swa_sample_attention(q, k, v, buf_k, buf_v, swa_sinks[j], db, dec_seq)
            x_s = _out_proj(rows_s, o, w["w_o"], x_s, mod_s)
            k_all = jnp.concatenate([buf_k, k.reshape(db, dec_seq, SWA_NK)], axis=1)[:, -n_buf:]
            v_all = jnp.concatenate([buf_v, v.reshape(db, dec_seq, SWA_NK)], axis=1)[:, -n_buf:]
            swk_s.append(k_all.reshape(db, n_buf, SWA_KV_HEADS, SWA_HEAD_DIM))
            swv_s.append(v_all.reshape(db, n_buf, SWA_KV_HEADS, SWA_HEAD_DIM))

        pw = {"w_q": peer_w_q[i].astype(BF16),
              "sub_keys": peer_sub_keys[i].reshape(2 * PEER_HEADS, PEER_N_KEYS, PEER_HALF).astype(BF16)}
        h, experts, gates = _peer_route(route_p, x_p, g2, mod_p, pw)
        x_p = _peer_experts(expert_p, h, experts, gates, u_b, v_b, i, x_p, mod_p)
        h, experts, gates = _peer_route(route_s, x_s, g2, mod_s, pw)
        x_s = _peer_experts(expert_s, h, experts, gates, u_b, v_b, i, x_s, mod_s)

    fg = final_g.reshape(1, d)
    y_p = _final_norm(rows_p, x_p, fg).reshape(batch, seq, d)
    y_s = _final_norm(rows_s, x_s, fg).reshape(db, dec_seq, d)
    return (y_p, y_s,
            jnp.stack(ckv_p), jnp.stack(kpe_p), jnp.stack(ckv_s), jnp.stack(kpe_s),
            jnp.stack(swk_p), jnp.stack(swv_p), jnp.stack(swk_s), jnp.stack(swv_s))
```

```python
import functools

import numpy as np
import jax
import jax.numpy as jnp
from jax import lax
from jax.experimental import pallas as pl
from jax.experimental.pallas import tpu as pltpu

F32 = jnp.float32
BF16 = jnp.bfloat16
I32 = jnp.int32

D_MODEL = 1024
PAGE_SIZE = 128
ROPE_THETA = 500000.0
NORM_EPS = 1e-6
NEG_INF = -1e30

MLA_HEADS = 16
MLA_Q_LORA = 384
MLA_KV_LORA = 256
MLA_NOPE = 64
MLA_ROPE = 32
MLA_V = 64
MLA_SCALE = (MLA_NOPE + MLA_ROPE) ** -0.5

SWA_HEADS = 16
SWA_KV_HEADS = 4
SWA_GROUP = SWA_HEADS // SWA_KV_HEADS
SWA_HEAD_DIM = D_MODEL // SWA_HEADS
SWA_ROT = SWA_HEAD_DIM // 4
SWA_SCALE = SWA_HEAD_DIM ** -0.5
WINDOW = 128

PEER_HEADS = 8
PEER_N_KEYS = 128
PEER_TOPK = 16
PEER_HALF = 128
PEER_PICKS = PEER_HEADS * PEER_TOPK

LANES = 128
SUBLANES = 8
BF16_ROWS = 16
VMEM_LIMIT_BYTES = 56 * 1024 * 1024

ROW_BLOCK = 512
EXPERT_ROW_BLOCK = 512
TOKEN_UNROLL = 64
ROUTE_BLOCK = 256
ROUTE_UNROLL = 4
MLA_ATTN_BLOCK = 1024
SWA_Q_BLOCK = 256
PAGES_PER_STEP = 32
PAGE_GROUP = 32
EXPERT_CHUNK_A = 16
GATE_TILE_PITCH = 72
MOD_COL_BLOCK = 1536


def _cparams(sem):
    return pltpu.CompilerParams(dimension_semantics=sem, vmem_limit_bytes=VMEM_LIMIT_BYTES)


def _dot(a, b):
    return jnp.dot(a, b, preferred_element_type=F32)


def _dot_nt(a, b):
    return lax.dot_general(a, b, (((1,), (1,)), ((), ())), preferred_element_type=F32)


def _rms(x, g):
    return x * lax.rsqrt(jnp.mean(x * x, axis=-1, keepdims=True) + NORM_EPS) * g


def _gelu(x):
    return 0.5 * x * (1.0 + lax.erf(x * np.float32(2.0 ** -0.5)))


def _modulate(x, g, shift, scale):
    return _rms(x, g) * (1.0 + scale) + shift


class _Rows:
    def __init__(self, n, tb, seq_len=None):
        assert n % tb == 0
        self.n, self.tb, self.grid = n, tb, n // tb
        self.bps = None
        if seq_len is not None:
            assert seq_len % tb == 0
            self.bps = seq_len // tb

    def rows(self, width):
        return pl.BlockSpec((self.tb, width), lambda i: (i, 0))

    def mod(self, k):
        if self.bps is None:
            return pl.BlockSpec((self.tb, D_MODEL), lambda i: (i, k))
        bps = self.bps
        return pl.BlockSpec((None, 1, D_MODEL), lambda i: (i // bps, 0, k))

    def pos(self, width):
        if self.bps is None:
            return pl.BlockSpec((self.tb, width), lambda i: (i, 0))
        bps = self.bps
        return pl.BlockSpec((self.tb, width), lambda i: (i % bps, 0))

    @staticmethod
    def full(shape):
        nd = len(shape)
        return pl.BlockSpec(shape, lambda i: (0,) * nd)


def _mod_kernel(c_ref, w_ref, b_ref, o_ref):
    c = c_ref[...]
    a = (c * jax.nn.sigmoid(c)).astype(BF16)
    o_ref[...] = _dot(a, w_ref[...].astype(BF16)) + b_ref[...]


def _modulation_all(c_all, w_mod, b_mod):
    depth, d, n6 = w_mod.shape
    nc = c_all.shape[0]
    nb = n6 // MOD_COL_BLOCK
    return pl.pallas_call(
        _mod_kernel,
        grid=(depth, nb),
        in_specs=[
            pl.BlockSpec((nc, d), lambda l, j: (0, 0)),
            pl.BlockSpec((None, d, MOD_COL_BLOCK), lambda l, j: (l, 0, j)),
            pl.BlockSpec((None, 1, MOD_COL_BLOCK), lambda l, j: (l, 0, j)),
        ],
        out_specs=pl.BlockSpec((None, nc, MOD_COL_BLOCK), lambda l, j: (l, 0, j)),
        out_shape=jax.ShapeDtypeStruct((depth, nc, n6), F32),
        compiler_params=_cparams(("arbitrary", "arbitrary")),
        name="adaln_modulation",
    )(c_all, w_mod, b_mod.reshape(depth, 1, n6))


MLA_IN_EXT = 896
MLA_KPE_COL = 640
MLA_KPE_SWAP_COL = 768
MLA_HEAD_PAD = LANES


def _mla_proj_kernel(x_ref, g_ref, sh_ref, sc_ref, c32_ref, s32_ref, c128_ref, s128_ref,
                     win_ref, qn_ref, kvn_ref, wuq_ref, wuqs_ref, wuk_ref, epe_ref, wuvt_ref,
                     ckv_ref, kpe_ref, q_ref, k_ref, vt_ref):
    h = _modulate(x_ref[...], g_ref[...], sh_ref[...], sc_ref[...]).astype(BF16)
    z = _dot(h, win_ref[...])
    cq = _rms(z[:, :MLA_Q_LORA], qn_ref[...]).astype(BF16)
    ckv = _rms(z[:, MLA_Q_LORA:MLA_Q_LORA + MLA_KV_LORA], kvn_ref[...])
    kpe = (z[:, MLA_KPE_COL:MLA_KPE_COL + MLA_ROPE] * c32_ref[...]
           + z[:, MLA_KPE_SWAP_COL:MLA_KPE_SWAP_COL + MLA_ROPE] * s32_ref[...])
    ckv_ref[...] = ckv
    kpe_ref[...] = kpe
    ckv_b = ckv.astype(BF16)
    q = _dot(cq, wuq_ref[...])
    qs = _dot(cq, wuqs_ref[...])
    cos = c128_ref[...]
    sin = s128_ref[...]
    for hd in range(MLA_HEADS):
        sl = slice(hd * MLA_HEAD_PAD, (hd + 1) * MLA_HEAD_PAD)
        q_ref[:, sl] = (q[:, sl] * cos + qs[:, sl] * sin).astype(BF16)
    k_ref[...] = (_dot(ckv_b, wuk_ref[...]) + _dot(kpe.astype(BF16), epe_ref[...])).astype(BF16)
    vt = _dot_nt(wuvt_ref[...], ckv_b).astype(BF16)
    if len(vt_ref.shape) == 2:
        vt_ref[...] = vt
    else:
        chunks, _, width = vt_ref.shape
        for c in range(chunks):
            vt_ref[c] = vt[:, c * width:(c + 1) * width]


def _mla_project(rows, x, ln_g, mod, tabs, w, chunk):
    n = rows.n
    hp = MLA_HEADS * MLA_HEAD_PAD
    full = _Rows.full
    nv = MLA_HEADS * MLA_V
    if rows.tb >= chunk:
        vt_spec = pl.BlockSpec((rows.tb // chunk, nv, chunk), lambda i: (i, 0, 0))
    else:
        per_chunk = chunk // rows.tb
        vt_spec = pl.BlockSpec((None, nv, rows.tb), lambda i: (i // per_chunk, 0, i % per_chunk))
    return pl.pallas_call(
        _mla_proj_kernel,
        grid=(rows.grid,),
        in_specs=[
            rows.rows(D_MODEL), full((1, D_MODEL)), rows.mod(0), rows.mod(1),
            rows.pos(MLA_ROPE), rows.pos(MLA_ROPE), rows.pos(MLA_HEAD_PAD), rows.pos(MLA_HEAD_PAD),
            full((D_MODEL, MLA_IN_EXT)), full((1, MLA_Q_LORA)), full((1, MLA_KV_LORA)),
            full((MLA_Q_LORA, hp)), full((MLA_Q_LORA, hp)), full((MLA_KV_LORA, hp)),
            full((MLA_ROPE, hp)), full((nv, MLA_KV_LORA)),
        ],
        out_specs=[rows.rows(MLA_KV_LORA), rows.rows(MLA_ROPE), rows.rows(hp), rows.rows(hp),
                   vt_spec],
        out_shape=[
            jax.ShapeDtypeStruct((n, MLA_KV_LORA), F32),
            jax.ShapeDtypeStruct((n, MLA_ROPE), F32),
            jax.ShapeDtypeStruct((n, hp), BF16),
            jax.ShapeDtypeStruct((n, hp), BF16),
            jax.ShapeDtypeStruct((n // chunk, nv, chunk), BF16),
        ],
        compiler_params=_cparams(("arbitrary",)),
        name="mla_project",
    )(x, ln_g, mod, mod, tabs["c32"], tabs["s32"], tabs["c128"], tabs["s128"],
      w["w_in"], w["q_norm"], w["kv_norm"], w["w_uq"], w["w_uq_swap"], w["w_uk"], w["e_pe"], w["w_uvt"])


def _mla_attn_kernel(q_ref, k_ref, vt_ref, o_ref, *, tq, tk):
    assert tq == tk
    qi = pl.program_id(2)
    causal = lax.broadcasted_iota(I32, (tk, tq), 0) <= lax.broadcasted_iota(I32, (tk, tq), 1)
    v_row = lax.broadcasted_iota(I32, (LANES, tk), 0)
    qs = [q_ref[:, hh * LANES:(hh + 1) * LANES] for hh in range(2)]
    v_keep = [(v_row // MLA_V) == hh for hh in range(2)]

    def step(j, carry, masked):
        start = pl.multiple_of(j * tk, tk)
        vt = vt_ref[j]
        new = []
        for hh in range(2):
            m, l, acc = carry[hh]
            kb = k_ref[pl.ds(start, tk), hh * LANES:(hh + 1) * LANES]
            s = _dot_nt(kb, qs[hh]) * MLA_SCALE
            if masked:
                s = jnp.where(causal, s, NEG_INF)
            m_new = jnp.maximum(m, jnp.max(s, axis=0, keepdims=True))
            alpha = jnp.exp(m - m_new)
            p = jnp.exp(s - m_new)
            l = alpha * l + jnp.sum(p, axis=0, keepdims=True)
            acc = alpha * acc + _dot(jnp.where(v_keep[hh], vt, jnp.zeros_like(vt)), p.astype(BF16))
            new.append((m_new, l, acc))
        return tuple(new)

    head0 = (jnp.full((1, tq), NEG_INF, F32), jnp.zeros((1, tq), F32), jnp.zeros((LANES, tq), F32))
    carry = lax.fori_loop(0, qi, lambda j, c: step(j, c, False), (head0, head0))
    (_, l0, acc0), (_, l1, acc1) = step(qi, carry, True)
    o_ref[...] = (acc0 / l0 + acc1 / l1).T.astype(BF16)


def _mla_attention(q, k, vt, batch, seq):
    tq = tk = vt.shape[2]
    nq = seq // tq
    q3 = q.reshape(batch, seq, -1)
    k3 = k.reshape(batch, seq, -1)
    out = pl.pallas_call(
        functools.partial(_mla_attn_kernel, tq=tq, tk=tk),
        grid=(batch, MLA_HEADS // 2, nq),
        in_specs=[
            pl.BlockSpec((None, tq, 2 * LANES), lambda b, hp, i: (b, i, hp)),
            pl.BlockSpec((None, seq, 2 * LANES), lambda b, hp, i: (b, 0, hp)),
            pl.BlockSpec((seq // tk, LANES, tk), lambda b, hp, i: (b, hp, 0)),
        ],
        out_specs=pl.BlockSpec((None, tq, LANES), lambda b, hp, i: (b, i, hp)),
        out_shape=jax.ShapeDtypeStruct((batch, seq, MLA_HEADS * MLA_V), BF16),
        compiler_params=_cparams(("arbitrary", "arbitrary", "arbitrary")),
        name="mla_prompt_attention",
    )(q3, k3, vt)
    return out.reshape(batch * seq, MLA_HEADS * MLA_V)


def _qlat_kernel(q_ref, wk_ref, o_ref):
    for hd in range(MLA_HEADS):
        qh = q_ref[:, hd * MLA_HEAD_PAD:(hd + 1) * MLA_HEAD_PAD]
        o_ref[:, hd * MLA_KV_LORA:(hd + 1) * MLA_KV_LORA] = _dot(qh, wk_ref[hd]).astype(BF16)


def _mla_absorb_queries(q, w_ukt):
    n = q.shape[0]
    return pl.pallas_call(
        _qlat_kernel,
        grid=(1,),
        in_specs=[_Rows.full(q.shape), _Rows.full(w_ukt.shape)],
        out_specs=_Rows.full((n, MLA_HEADS * MLA_KV_LORA)),
        out_shape=jax.ShapeDtypeStruct((n, MLA_HEADS * MLA_KV_LORA), BF16),
        compiler_params=_cparams(("arbitrary",)),
        name="mla_absorb_queries",
    )(q, w_ukt)


def _paged_attn_kernel(pt_ref, qlat_ref, qpe_ref, nckv_ref, nkpe_ref, *rest, n_steps, dec_seq):
    pg = PAGES_PER_STEP
    ckv_refs = rest[:pg]
    kpe_refs = rest[pg:2 * pg]
    o_ref = rest[2 * pg]
    m_sc, l_sc, acc_sc, kc_sc, kp_sc = rest[2 * pg + 1:]
    g = pl.program_id(1)
    rows = qlat_ref.shape[0]

    @pl.when(g == 0)
    def _():
        m_sc[...] = jnp.full(m_sc.shape, NEG_INF, F32)
        l_sc[...] = jnp.zeros(l_sc.shape, F32)
        acc_sc[...] = jnp.zeros(acc_sc.shape, F32)

    qlat = qlat_ref[...]
    qpe = qpe_ref[...]

    def column(row_vec):
        return jnp.broadcast_to(row_vec, (LANES, rows)).T[:, :1]

    def update(s_t, vals):
        m = m_sc[...]
        m_new = jnp.maximum(m, jnp.max(s_t, axis=0, keepdims=True))
        alpha = jnp.exp(m - m_new)
        p_t = jnp.exp(s_t - m_new)
        l_sc[...] = alpha * l_sc[...] + jnp.sum(p_t, axis=0, keepdims=True)
        acc_sc[...] = column(alpha) * acc_sc[...] + _dot(p_t.T.astype(BF16), vals)
        m_sc[...] = m_new

    def scores_t(kc, kp):
        return (_dot_nt(kc, qlat) + _dot_nt(kp, qpe)) * MLA_SCALE

    for grp in range(pg // PAGE_GROUP):
        for i in range(PAGE_GROUP):
            sl = slice(i * PAGE_SIZE, (i + 1) * PAGE_SIZE)
            kc_sc[sl, :] = ckv_refs[grp * PAGE_GROUP + i][...].astype(BF16)
            kp_sc[sl, :] = kpe_refs[grp * PAGE_GROUP + i][...].T.astype(BF16)
        kc = kc_sc[...]
        update(scores_t(kc, kp_sc[...]), kc)

    @pl.when(g == n_steps - 1)
    def _():
        pad = PAGE_SIZE - dec_seq
        ck = jnp.concatenate([nckv_ref[...], jnp.zeros((pad, MLA_KV_LORA), F32)], axis=0).astype(BF16)
        kp = jnp.concatenate([nkpe_ref[...], jnp.zeros((pad, MLA_ROPE), F32)], axis=0).astype(BF16)
        s_t = scores_t(ck, kp)
        k_t = lax.broadcasted_iota(I32, s_t.shape, 0)
        q_t = lax.broadcasted_iota(I32, s_t.shape, 1) // MLA_HEADS
        update(jnp.where(k_t <= q_t, s_t, NEG_INF), ck)
        o_ref[...] = acc_sc[...] / column(l_sc[...])


def _mla_paged_attention(qlat, qpe, new_ckv, new_kpe, cache_ckv, cache_kpe, page_table, layer):
    db, rows, _ = qlat.shape
    dec_seq = new_ckv.shape[1]
    n_pages = page_table.shape[1]
    pg = PAGES_PER_STEP
    assert n_pages % pg == 0
    n_steps = n_pages // pg

    def page_spec(shape, i):
        return pl.BlockSpec((None, None) + shape, lambda b, g, pt: (layer, pt[b, g * pg + i], 0, 0))

    cache_kpe_t = jnp.swapaxes(cache_kpe, 2, 3)
    in_specs = [
        pl.BlockSpec((None, rows, MLA_KV_LORA), lambda b, g, pt: (b, 0, 0)),
        pl.BlockSpec((None, rows, MLA_ROPE), lambda b, g, pt: (b, 0, 0)),
        pl.BlockSpec((None, dec_seq, MLA_KV_LORA), lambda b, g, pt: (b, 0, 0)),
        pl.BlockSpec((None, dec_seq, MLA_ROPE), lambda b, g, pt: (b, 0, 0)),
    ]
    in_specs += [page_spec((PAGE_SIZE, MLA_KV_LORA), i) for i in range(pg)]
    in_specs += [page_spec((MLA_ROPE, PAGE_SIZE), i) for i in range(pg)]
    grid_spec = pltpu.PrefetchScalarGridSpec(
        num_scalar_prefetch=1,
        grid=(db, n_steps),
        in_specs=in_specs,
        out_specs=pl.BlockSpec((None, rows, MLA_KV_LORA), lambda b, g, pt: (b, 0, 0)),
        scratch_shapes=[pltpu.VMEM((1, rows), F32), pltpu.VMEM((1, rows), F32),
                        pltpu.VMEM((rows, MLA_KV_LORA), F32),
                        pltpu.VMEM((PAGE_GROUP * PAGE_SIZE, MLA_KV_LORA), BF16),
                        pltpu.VMEM((PAGE_GROUP * PAGE_SIZE, MLA_ROPE), BF16)],
    )
    return pl.pallas_call(
        functools.partial(_paged_attn_kernel, n_steps=n_steps, dec_seq=dec_seq),
        grid_spec=grid_spec,
        out_shape=jax.ShapeDtypeStruct((db, rows, MLA_KV_LORA), F32),
        compiler_params=_cparams(("arbitrary", "arbitrary")),
        name="mla_paged_attention",
    )(page_table, qlat, qpe, new_ckv, new_kpe, *([cache_ckv] * pg), *([cache_kpe_t] * pg))


def _sample_out_kernel(olat_ref, wuv_ref, wo_ref, x_ref, gate_ref, o_ref):
    col_head = lax.broadcasted_iota(I32, (1, MLA_HEADS * MLA_V), 1) // MLA_V
    wuv = wuv_ref[...]
    o = jnp.zeros((olat_ref.shape[0], MLA_HEADS * MLA_V), F32)
    for hd in range(MLA_HEADS):
        lat = olat_ref[:, hd * MLA_KV_LORA:(hd + 1) * MLA_KV_LORA].astype(BF16)
        o = jnp.where(col_head == hd, _dot(lat, wuv), o)
    o_ref[...] = x_ref[...] + gate_ref[...] * _dot(o.astype(BF16), wo_ref[...])


def _mla_sample_out(rows, olat, w_uv, w_o, x, mod):
    return pl.pallas_call(
        _sample_out_kernel,
        grid=(rows.grid,),
        in_specs=[rows.rows(olat.shape[1]), _Rows.full(w_uv.shape), _Rows.full(w_o.shape),
                  rows.rows(D_MODEL), rows.mod(2)],
        out_specs=rows.rows(D_MODEL),
        out_shape=jax.ShapeDtypeStruct((rows.n, D_MODEL), F32),
        compiler_params=_cparams(("arbitrary",)),
        name="mla_sample_out",
    )(olat, w_uv, w_o, x, mod)


def _out_proj_kernel(o_ref, wo_ref, x_ref, gate_ref, y_ref):
    y_ref[...] = x_ref[...] + gate_ref[...] * _dot(o_ref[...], wo_ref[...])


def _out_proj(rows, o, w_o, x, mod):
    return pl.pallas_call(
        _out_proj_kernel,
        grid=(rows.grid,),
        in_specs=[rows.rows(o.shape[1]), _Rows.full(w_o.shape), rows.rows(D_MODEL), rows.mod(2)],
        out_specs=rows.rows(D_MODEL),
        out_shape=jax.ShapeDtypeStruct((rows.n, D_MODEL), F32),
        compiler_params=_cparams(("arbitrary",)),
        name="attn_out_proj",
    )(o, w_o, x, mod)


SWA_NQ = SWA_HEADS * SWA_HEAD_DIM
SWA_NK = SWA_KV_HEADS * SWA_HEAD_DIM
SWA_QKV = SWA_NQ + 2 * SWA_NK
SWA_EXT = SWA_QKV + SWA_NQ + SWA_NK


def _swa_proj_kernel(x_ref, g_ref, sh_ref, sc_ref, cos_ref, sin_ref, w_ref, b_ref,
                     q_ref, kb_ref, vb_ref, k_ref, v_ref):
    h = _modulate(x_ref[...], g_ref[...], sh_ref[...], sc_ref[...]).astype(BF16)
    z = _dot(h, w_ref[...]) + b_ref[...]
    cos = cos_ref[...]
    sin = sin_ref[...]
    for t in range((SWA_NQ + SWA_NK) // LANES):
        sl = slice(t * LANES, (t + 1) * LANES)
        sw = slice(SWA_QKV + t * LANES, SWA_QKV + (t + 1) * LANES)
        r = z[:, sl] * cos + z[:, sw] * sin
        if t < SWA_NQ // LANES:
            q_ref[:, sl] = (r * SWA_SCALE).astype(BF16)
        else:
            ks = slice(t * LANES - SWA_NQ, (t + 1) * LANES - SWA_NQ)
            k_ref[:, ks] = r
            kb_ref[:, ks] = r.astype(BF16)
    v = z[:, SWA_NQ + SWA_NK:SWA_QKV]
    v_ref[...] = v
    vb_ref[...] = v.astype(BF16)


def _swa_project(rows, x, ln_g, mod, tabs, w):
    n = rows.n
    full = _Rows.full
    return pl.pallas_call(
        _swa_proj_kernel,
        grid=(rows.grid,),
        in_specs=[rows.rows(D_MODEL), full((1, D_MODEL)), rows.mod(0), rows.mod(1),
                  rows.pos(LANES), rows.pos(LANES), full((D_MODEL, SWA_EXT)), full((1, SWA_EXT))],
        out_specs=[rows.rows(SWA_NQ), rows.rows(SWA_NK), rows.rows(SWA_NK), rows.rows(SWA_NK),
                   rows.rows(SWA_NK)],
        out_shape=[
            jax.ShapeDtypeStruct((n, SWA_NQ), BF16),
            jax.ShapeDtypeStruct((n, SWA_NK), BF16),
            jax.ShapeDtypeStruct((n, SWA_NK), BF16),
            jax.ShapeDtypeStruct((n, SWA_NK), F32),
            jax.ShapeDtypeStruct((n, SWA_NK), F32),
        ],
        compiler_params=_cparams(("arbitrary",)),
        name="swa_project",
    )(x, ln_g, mod, mod, tabs["cos"], tabs["sin"], w["w_qkv"], w["b_qkv"])


def _swa_core(q_all, k_all, v_all, mask, sinks_ref, o_ref):
    lane = lax.broadcasted_iota(I32, (1, LANES), 1)
    lo = lane < SWA_HEAD_DIM
    zero = jnp.zeros((), BF16)
    for kh in range(SWA_KV_HEADS):
        tile = kh // 2
        k_t = k_all[:, tile * LANES:(tile + 1) * LANES]
        v_t = v_all[:, tile * LANES:(tile + 1) * LANES]
        k_r = pltpu.roll(k_t, SWA_HEAD_DIM, 1)
        v_r = pltpu.roll(v_t, SWA_HEAD_DIM, 1)
        for pair in range(SWA_GROUP // 2):
            q_t = q_all[:, (kh * 2 + pair) * LANES:(kh * 2 + pair + 1) * LANES]
            out = None
            for half in range(2):
                keep = lo if half == 0 else jnp.logical_not(lo)
                kx = k_t if (kh % 2) == half else k_r
                vx = v_t if (kh % 2) == half else v_r
                s = _dot_nt(jnp.where(keep, q_t, zero), kx)
                s = jnp.where(mask, s, NEG_INF)
                sink = sinks_ref[kh * SWA_GROUP + 2 * pair + half]
                m = jnp.maximum(jnp.max(s, axis=-1, keepdims=True), sink)
                p = jnp.exp(s - m)
                l = jnp.sum(p, axis=-1, keepdims=True) + jnp.exp(sink - m)
                o = _dot((p / l).astype(BF16), jnp.where(keep, vx, zero))
                out = o if out is None else out + o
            o_ref[:, (kh * 2 + pair) * LANES:(kh * 2 + pair + 1) * LANES] = out.astype(o_ref.dtype)


def _swa_prompt_kernel(sinks_ref, q_ref, kc_ref, kp_ref, vc_ref, vp_ref, o_ref, *, tq):
    i = pl.program_id(1)
    k_all = jnp.concatenate([kp_ref[...], kc_ref[...]], axis=0)
    v_all = jnp.concatenate([vp_ref[...], vc_ref[...]], axis=0)
    tk = tq + WINDOW
    q_pos = i * tq + lax.broadcasted_iota(I32, (tq, tk), 0)
    k_pos = i * tq - WINDOW + lax.broadcasted_iota(I32, (tq, tk), 1)
    mask = (k_pos >= 0) & (k_pos <= q_pos) & (q_pos - k_pos < WINDOW)
    _swa_core(q_ref[...], k_all, v_all, mask, sinks_ref, o_ref)


def _swa_prompt_attention(q, kb, vb, sinks, batch, seq):
    tq = SWA_Q_BLOCK
    r = tq // WINDOW
    q3 = q.reshape(batch, seq, SWA_NQ)
    k3 = kb.reshape(batch, seq, SWA_NK)
    v3 = vb.reshape(batch, seq, SWA_NK)
    cur = pl.BlockSpec((None, tq, SWA_NK), lambda b, i: (b, i, 0))
    prev = pl.BlockSpec((None, WINDOW, SWA_NK), lambda b, i: (b, jnp.maximum(i * r - 1, 0), 0))
    out = pl.pallas_call(
        functools.partial(_swa_prompt_kernel, tq=tq),
        grid=(batch, seq // tq),
        in_specs=[pl.BlockSpec(memory_space=pltpu.SMEM),
                  pl.BlockSpec((None, tq, SWA_NQ), lambda b, i: (b, i, 0)), cur, prev, cur, prev],
        out_specs=pl.BlockSpec((None, tq, SWA_NQ), lambda b, i: (b, i, 0)),
        out_shape=jax.ShapeDtypeStruct((batch, seq, SWA_NQ), BF16),
        compiler_params=_cparams(("arbitrary", "arbitrary")),
        name="swa_prompt_attention",
    )(sinks, q3, k3, k3, v3, v3)
    return out.reshape(batch * seq, SWA_NQ)


def _swa_sample_kernel(sinks_ref, q_ref, kbuf_ref, vbuf_ref, kn_ref, vn_ref, o_ref, o_sc, *, n_buf, dec_seq):
    pad = jnp.zeros((BF16_ROWS - dec_seq, SWA_NK), F32)
    k_all = jnp.concatenate([kbuf_ref[...], kn_ref[...], pad], axis=0).astype(BF16)
    v_all = jnp.concatenate([vbuf_ref[...], vn_ref[...], pad], axis=0).astype(BF16)
    tk = n_buf + BF16_ROWS
    t = lax.broadcasted_iota(I32, (BF16_ROWS, tk), 0)
    c = lax.broadcasted_iota(I32, (BF16_ROWS, tk), 1)
    in_buf = c < n_buf
    mask = ((in_buf & ((n_buf + t - c) < WINDOW))
            | (jnp.logical_not(in_buf) & ((c - n_buf) <= jnp.minimum(t, dec_seq - 1))))
    q = jnp.concatenate([q_ref[...].astype(F32), jnp.zeros((BF16_ROWS - dec_seq, SWA_NQ), F32)], axis=0)
    _swa_core(q.astype(BF16), k_all, v_all, mask, sinks_ref, o_sc)
    o_ref[...] = o_sc[:dec_seq, :].astype(o_ref.dtype)


def _swa_sample_attention(q, kn, vn, buf_k, buf_v, sinks, db, dec_seq):
    n_buf = buf_k.shape[1]
    spec = lambda rws, w: pl.BlockSpec((None, rws, w), lambda b: (b, 0, 0))
    out = pl.pallas_call(
        functools.partial(_swa_sample_kernel, n_buf=n_buf, dec_seq=dec_seq),
        grid=(db,),
        in_specs=[pl.BlockSpec(memory_space=pltpu.SMEM), spec(dec_seq, SWA_NQ), spec(n_buf, SWA_NK),
                  spec(n_buf, SWA_NK), spec(dec_seq, SWA_NK), spec(dec_seq, SWA_NK)],
        out_specs=spec(dec_seq, SWA_NQ),
        out_shape=jax.ShapeDtypeStruct((db, dec_seq, SWA_NQ), BF16),
        scratch_shapes=[pltpu.VMEM((BF16_ROWS, SWA_NQ), F32)],
        compiler_params=_cparams(("arbitrary",)),
        name="swa_sample_attention",
    )(sinks, q.reshape(db, dec_seq, SWA_NQ), buf_k, buf_v,
      kn.reshape(db, dec_seq, SWA_NK), vn.reshape(db, dec_seq, SWA_NK))
    return out.reshape(db * dec_seq, SWA_NQ)


def _peer_candidate_tables(width_tokens):
    k = PEER_TOPK
    flat, valid = [], []
    for i, width in _CANDIDATE_RUNS:
        if i is None:
            flat += [k * k] * width
            valid += [False] * width
        elif i >= 8:
            flat += [ii * k for ii in range(8, 16)]
            valid += [True] * 8
        else:
            assert width == k // (i + 1)
            flat += [i * k + j for j in range(width)]
            valid += [True] * width
    assert sum(valid) == 50 and len(flat) % SUBLANES == 0
    flat = np.broadcast_to(np.asarray(flat, np.float32)[:, None], (len(flat), width_tokens))
    pen = np.where(np.asarray(valid), 0.0, -np.inf).astype(np.float32)
    return np.ascontiguousarray(flat), np.ascontiguousarray(np.broadcast_to(pen[:, None], flat.shape))


_CANDIDATE_RUNS = ((0, 16), (1, 8), (2, 5), (4, 3), (3, 4), (5, 2), (6, 2), (8, 8), (7, 2), (None, 6))


def _extract_top(s, row_id, k, payload=None):
    big = jnp.float32(2 ** 30)
    vals, ids = [], []
    for _ in range(k):
        m = jnp.max(s, axis=0, keepdims=True)
        sel = jnp.min(jnp.where(s == m, row_id, big), axis=0, keepdims=True)
        hit = row_id == sel
        if payload is not None:
            ids.append(jnp.max(jnp.where(hit, payload, -1.0), axis=0, keepdims=True))
        else:
            ids.append(sel)
        vals.append(m)
        s = jnp.where(hit, -jnp.inf, s)
    return jnp.concatenate(vals, axis=0), jnp.concatenate(ids, axis=0)


def _route_queries(x_ref, g_ref, sh_ref, sc_ref, wq_ref, qs_sc):
    h = _modulate(x_ref[...], g_ref[...], sh_ref[...], sc_ref[...]).astype(BF16)
    q = _dot(h, wq_ref[...])
    for grp in range(2 * PEER_HEADS):
        qs_sc[grp] = q[:, grp * PEER_HALF:(grp + 1) * PEER_HALF].astype(BF16)
    return h


def _route_first_level(grp, sk_ref, qs_sc, sv_sc, si_sc, tb):
    key_id = lax.broadcasted_iota(I32, (PEER_N_KEYS, tb), 0).astype(F32)
    s = _dot_nt(sk_ref[grp], qs_sc[grp])
    vals, ids = _extract_top(s, key_id, PEER_TOPK)
    sv_sc[grp] = vals
    si_sc[grp] = ids


def _route_second_level(sv_sc, si_sc, flat_ref, pen_ref, pe_sc, pg_sc):
    k = PEER_TOPK

    def head(hd, carry):
        sa, sb = sv_sc[2 * hd], sv_sc[2 * hd + 1]
        ia, ib = si_sc[2 * hd], si_sc[2 * hd + 1]
        cs, ce = [], []
        for i, width in _CANDIDATE_RUNS:
            if i is None:
                cs.append(jnp.zeros((width,) + sa.shape[1:], F32))
                ce.append(jnp.zeros((width,) + sa.shape[1:], F32))
            elif i >= 8:
                cs.append(sa[8:] + sb[:1])
                ce.append(ia[8:] * PEER_N_KEYS + ib[:1])
            else:
                cs.append(sa[i:i + 1] + sb[:width])
                ce.append(ia[i:i + 1] * PEER_N_KEYS + ib[:width])
        cand = jnp.concatenate(cs, axis=0) + pen_ref[...]
        cand_e = jnp.concatenate(ce, axis=0)
        best, experts = _extract_top(cand, flat_ref[...], k, payload=cand_e)
        ex = jnp.exp(best - best[:1])
        gates = ex / jnp.sum(ex, axis=0, keepdims=True)
        pick = pl.ds(pl.multiple_of(hd * k, k), k)
        pe_sc[pick, :] = experts
        pg_sc[pick, :] = gates
        return carry

    lax.fori_loop(0, PEER_HEADS, head, 0, unroll=ROUTE_UNROLL)


def _peer_route_kernel(x_ref, g_ref, sh_ref, sc_ref, wq_ref, sk_ref, flat_ref, pen_ref,
                       h_ref, e_ref, gate_ref, qs_sc, sv_sc, si_sc, pe_sc, pg_sc, *, tb):
    h = _route_queries(x_ref, g_ref, sh_ref, sc_ref, wq_ref, qs_sc)
    h_ref[...] = h

    def first_level(grp, carry):
        _route_first_level(grp, sk_ref, qs_sc, sv_sc, si_sc, tb)
        return carry

    lax.fori_loop(0, 2 * PEER_HEADS, first_level, 0, unroll=ROUTE_UNROLL)
    _route_second_level(sv_sc, si_sc, flat_ref, pen_ref, pe_sc, pg_sc)
    e_ref[...] = pe_sc[...].T.astype(I32)
    gate_ref[...] = pg_sc[...].T


def _peer_route(rows, x, ln_g, mod, w):
    n, tb = rows.n, rows.tb
    flat, pen = _peer_candidate_tables(tb)
    full = _Rows.full
    n_groups = 2 * PEER_HEADS
    return pl.pallas_call(
        functools.partial(_peer_route_kernel, tb=tb),
        grid=(rows.grid,),
        in_specs=[rows.rows(D_MODEL), full((1, D_MODEL)), rows.mod(3), rows.mod(4),
                  full(w["w_q"].shape), full(w["sub_keys"].shape), full(flat.shape), full(pen.shape)],
        out_specs=[rows.rows(D_MODEL), rows.rows(PEER_PICKS), rows.rows(PEER_PICKS)],
        out_shape=[jax.ShapeDtypeStruct((n, D_MODEL), BF16),
                   jax.ShapeDtypeStruct((n, PEER_PICKS), I32),
                   jax.ShapeDtypeStruct((n, PEER_PICKS), F32)],
        scratch_shapes=[pltpu.VMEM((n_groups, tb, PEER_HALF), BF16),
                        pltpu.VMEM((n_groups, PEER_TOPK, tb), F32),
                        pltpu.VMEM((n_groups, PEER_TOPK, tb), F32),
                        pltpu.VMEM((PEER_PICKS, tb), F32),
                        pltpu.VMEM((PEER_PICKS, tb), F32)],
        compiler_params=_cparams(("arbitrary",)),
        name="peer_route",
    )(x, ln_g, mod, mod, w["w_q"], w["sub_keys"], jnp.asarray(flat), jnp.asarray(pen))


def _build_gate_tiles(e_ref, gate_ref, w_sc, tb):
    pitch = GATE_TILE_PITCH
    row = lax.broadcasted_iota(I32, (PEER_N_KEYS, PEER_PICKS), 0)
    half_keys = PEER_N_KEYS // 2
    key_a = jnp.where(row < half_keys, 2 * row, 2 * (row - half_keys) + 1)

    zero_tile = jnp.zeros((PEER_N_KEYS, PEER_PICKS), BF16)

    def operands(t):
        e = e_ref[pl.ds(t, 1), :]
        g = gate_ref[pl.ds(t, 1), :].astype(BF16).astype(F32)
        one_a = jnp.where(key_a == (e >> 7), 1.0, 0.0).astype(BF16)
        g_b = jnp.where(row == (e & (PEER_N_KEYS - 1)), g, 0.0).astype(BF16)
        return one_a, g_b

    def token_pair(i, carry):
        t = 2 * i
        a0, b0 = operands(t)
        a1, b1 = operands(t + 1)
        lhs = jnp.concatenate([a0, a1], axis=1)
        rhs = jnp.concatenate([jnp.concatenate([b0, zero_tile], axis=1),
                               jnp.concatenate([zero_tile, b1], axis=1)], axis=0)
        w = _dot_nt(lhs, rhs)
        even = lax.bitcast_convert_type(w[:half_keys].astype(BF16).astype(F32), jnp.uint32)
        odd = lax.bitcast_convert_type(w[half_keys:].astype(BF16).astype(F32), jnp.uint32)
        word = odd | (even >> 16)
        for j in range(2):
            start = pl.multiple_of((t + j) * pitch, SUBLANES)
            w_sc[pl.ds(start, half_keys), :] = word[:, j * PEER_N_KEYS:(j + 1) * PEER_N_KEYS]
        return carry

    lax.fori_loop(0, tb // 2, token_pair, 0, unroll=TOKEN_UNROLL // 2)


def _expert_chunk(h, u_ref, v_ref, w_sc, p_sc, acc_sc, c, tb):
    for pair in range(EXPERT_CHUNK_A // 2):
        z = _dot_nt(h, u_ref[pair * 2 * PEER_N_KEYS:(pair + 1) * 2 * PEER_N_KEYS, :])
        word = w_sc[pl.ds(c * (EXPERT_CHUNK_A // 2) + pair, tb, stride=GATE_TILE_PITCH), :]
        w_pair = (lax.bitcast_convert_type(word << 16, F32),
                  lax.bitcast_convert_type(word & jnp.uint32(0xFFFF0000), F32))
        for half in range(2):
            al = 2 * pair + half
            sl = slice(al * PEER_N_KEYS, (al + 1) * PEER_N_KEYS)
            p_sc[:, sl] = (w_pair[half] * _gelu(z[:, half * PEER_N_KEYS:(half + 1) * PEER_N_KEYS])).astype(BF16)
    acc_sc[...] += _dot(p_sc[...], v_ref[...])


def _peer_expert_kernel(h_ref, e_ref, gate_ref, u_ref, v_ref, x_ref, g2_ref, o_ref,
                        w_sc, p_sc, acc_sc, *, tb, n_chunks):
    c = pl.program_id(1)

    @pl.when(c == 0)
    def _():
        acc_sc[...] = jnp.zeros(acc_sc.shape, F32)
        _build_gate_tiles(e_ref, gate_ref, w_sc, tb)

    _expert_chunk(h_ref[...], u_ref, v_ref, w_sc, p_sc, acc_sc, c, tb)

    @pl.when(c == n_chunks - 1)
    def _():
        o_ref[...] = x_ref[...] + g2_ref[...] * acc_sc[...]


def _peer_experts(rows, h, experts, gates, u, v, layer, x, mod):
    n, tb = rows.n, rows.tb
    ce = EXPERT_CHUNK_A * PEER_N_KEYS
    n_chunks = u.shape[1] // ce
    bps = rows.bps
    if bps is None:
        gate_spec = pl.BlockSpec((tb, D_MODEL), lambda i, c: (i, 5))
    else:
        gate_spec = pl.BlockSpec((None, 1, D_MODEL), lambda i, c: (i // bps, 0, 5))
    tok = lambda w: pl.BlockSpec((tb, w), lambda i, c: (i, 0))
    return pl.pallas_call(
        functools.partial(_peer_expert_kernel, tb=tb, n_chunks=n_chunks),
        grid=(rows.grid, n_chunks),
        in_specs=[tok(D_MODEL), tok(PEER_PICKS), tok(PEER_PICKS),
                  pl.BlockSpec((None, ce, D_MODEL), lambda i, c: (layer, c, 0)),
                  pl.BlockSpec((None, ce, D_MODEL), lambda i, c: (layer, c, 0)),
                  tok(D_MODEL), gate_spec],
        out_specs=tok(D_MODEL),
        out_shape=jax.ShapeDtypeStruct((n, D_MODEL), F32),
        scratch_shapes=[pltpu.VMEM((tb * GATE_TILE_PITCH, LANES), jnp.uint32),
                        pltpu.VMEM((tb, ce), BF16),
                        pltpu.VMEM((tb, D_MODEL), F32)],
        compiler_params=_cparams(("arbitrary", "arbitrary")),
        name="peer_experts",
    )(h, experts, gates, u, v, x, mod)


def _final_norm_kernel(x_ref, g_ref, o_ref):
    o_ref[...] = _rms(x_ref[...], g_ref[...])


def _final_norm(rows, x, g):
    return pl.pallas_call(
        _final_norm_kernel,
        grid=(rows.grid,),
        in_specs=[rows.rows(D_MODEL), _Rows.full((1, D_MODEL))],
        out_specs=rows.rows(D_MODEL),
        out_shape=jax.ShapeDtypeStruct((rows.n, D_MODEL), F32),
        compiler_params=_cparams(("arbitrary",)),
        name="final_norm",
    )(x, g)


def _pair_swap(w, half):
    return jnp.concatenate([-w[..., half:2 * half], w[..., :half]], axis=-1)


def _rope_cos_sin(pos, rot_dim):
    half = rot_dim // 2
    inv_freq = ROPE_THETA ** (-jnp.arange(half, dtype=F32) * 2.0 / rot_dim)
    ang = pos.astype(F32)[:, None] * inv_freq[None, :]
    return jnp.cos(ang), jnp.sin(ang)


def _mla_tables(pos):
    cos, sin = _rope_cos_sin(pos, MLA_ROPE)
    n = pos.shape[0]
    ones = jnp.ones((n, MLA_NOPE), F32)
    zeros_n = jnp.zeros((n, MLA_NOPE), F32)
    pad = jnp.zeros((n, MLA_HEAD_PAD - MLA_NOPE - MLA_ROPE), F32)
    return {
        "c32": jnp.concatenate([cos, cos], axis=1),
        "s32": jnp.concatenate([sin, sin], axis=1),
        "c128": jnp.concatenate([ones, cos, cos, pad], axis=1),
        "s128": jnp.concatenate([zeros_n, sin, sin, pad], axis=1),
    }


def _swa_tables(pos):
    cos, sin = _rope_cos_sin(pos, SWA_ROT)
    n = pos.shape[0]
    rest = SWA_HEAD_DIM - SWA_ROT
    c = jnp.concatenate([cos, cos, jnp.ones((n, rest), F32)], axis=1)
    s = jnp.concatenate([sin, sin, jnp.zeros((n, rest), F32)], axis=1)
    reps = LANES // SWA_HEAD_DIM
    return {"cos": jnp.tile(c, (1, reps)), "sin": jnp.tile(s, (1, reps))}


def _mla_weights(w_in, q_norm, kv_norm, w_uq, w_uk, w_uv, w_o):
    d = w_in.shape[0]
    half = MLA_ROPE // 2
    kpe_cols = w_in[:, MLA_Q_LORA + MLA_KV_LORA:]
    w_in_ext = jnp.zeros((d, MLA_IN_EXT), F32)
    w_in_ext = w_in_ext.at[:, :MLA_KPE_COL + MLA_ROPE].set(w_in)
    w_in_ext = w_in_ext.at[:, MLA_KPE_SWAP_COL:MLA_KPE_SWAP_COL + MLA_ROPE].set(_pair_swap(kpe_cols, half))
    uq = w_uq.reshape(MLA_Q_LORA, MLA_HEADS, MLA_NOPE + MLA_ROPE)
    uq_pad = jnp.zeros((MLA_Q_LORA, MLA_HEADS, MLA_HEAD_PAD), F32).at[:, :, :MLA_NOPE + MLA_ROPE].set(uq)
    uq_swap = jnp.zeros((MLA_Q_LORA, MLA_HEADS, MLA_HEAD_PAD), F32)
    uq_swap = uq_swap.at[:, :, MLA_NOPE:MLA_NOPE + MLA_ROPE].set(_pair_swap(uq[:, :, MLA_NOPE:], half))
    uk_pad = jnp.zeros((MLA_KV_LORA, MLA_HEADS, MLA_HEAD_PAD), F32).at[:, :, :MLA_NOPE].set(w_uk)
    e_pe = jnp.zeros((MLA_ROPE, MLA_HEADS, MLA_HEAD_PAD), F32)
    e_pe = e_pe.at[jnp.arange(MLA_ROPE), :, MLA_NOPE + jnp.arange(MLA_ROPE)].set(1.0)
    ukt = jnp.zeros((MLA_HEADS, MLA_HEAD_PAD, MLA_KV_LORA), F32)
    ukt = ukt.at[:, :MLA_NOPE, :].set(jnp.transpose(w_uk, (1, 2, 0)))
    hp = MLA_HEADS * MLA_HEAD_PAD
    return {
        "w_in": w_in_ext.astype(BF16),
        "q_norm": q_norm.reshape(1, -1), "kv_norm": kv_norm.reshape(1, -1),
        "w_uq": uq_pad.reshape(MLA_Q_LORA, hp).astype(BF16),
        "w_uq_swap": uq_swap.reshape(MLA_Q_LORA, hp).astype(BF16),
        "w_uk": uk_pad.reshape(MLA_KV_LORA, hp).astype(BF16),
        "e_pe": e_pe.reshape(MLA_ROPE, hp).astype(BF16),
        "w_uvt": w_uv.reshape(MLA_KV_LORA, MLA_HEADS * MLA_V).T.astype(BF16),
        "w_ukt": ukt.astype(BF16),
        "w_uv": w_uv.reshape(MLA_KV_LORA, MLA_HEADS * MLA_V).astype(BF16),
        "w_o": w_o.astype(BF16),
    }


def _swa_weights(w_qkv, b_qkv, w_o):
    half = SWA_ROT // 2
    n_rot_heads = SWA_HEADS + SWA_KV_HEADS

    def swap_cols(w):
        lead = w.shape[:-1]
        wh = w[..., :SWA_NQ + SWA_NK].reshape(lead + (n_rot_heads, SWA_HEAD_DIM))
        sw = jnp.concatenate([_pair_swap(wh[..., :SWA_ROT], half),
                              jnp.zeros(lead + (n_rot_heads, SWA_HEAD_DIM - SWA_ROT), F32)], axis=-1)
        return sw.reshape(lead + (SWA_NQ + SWA_NK,))

    w_ext = jnp.concatenate([w_qkv, swap_cols(w_qkv)], axis=-1)
    b_ext = jnp.concatenate([b_qkv, swap_cols(b_qkv)], axis=-1)
    return {"w_qkv": w_ext.astype(BF16), "b_qkv": b_ext.reshape(1, -1), "w_o": w_o.astype(BF16)}


def kernel(x_prompt, x_sample, cache_mla_ckv, cache_mla_kpe, cache_swa_k, cache_swa_v, page_table,
           c_prompt, c_sample, ln1_g, ln2_g, w_mod, b_mod,
           mla_w_in, mla_q_norm, mla_kv_norm, mla_w_uq, mla_w_uk, mla_w_uv, mla_w_o,
           swa_w_qkv, swa_b_qkv, swa_sinks, swa_w_o,
           peer_w_q, peer_sub_keys, peer_u, peer_v, final_g):
    batch, seq, d = x_prompt.shape
    db, dec_seq, _ = x_sample.shape
    depth = w_mod.shape[0]
    past_len = page_table.shape[1] * PAGE_SIZE
    n_p, n_s = batch * seq, db * dec_seq

    rows_p = _Rows(n_p, min(ROW_BLOCK, seq), seq_len=seq)
    rows_s = _Rows(n_s, n_s)
    route_p = _Rows(n_p, ROUTE_BLOCK, seq_len=seq)
    route_s = _Rows(n_s, min(ROUTE_BLOCK, n_s))
    expert_p = _Rows(n_p, EXPERT_ROW_BLOCK, seq_len=seq)
    expert_s = _Rows(n_s, min(EXPERT_ROW_BLOCK, n_s))

    pos_p = jnp.arange(seq)
    pos_s = jnp.tile(past_len + jnp.arange(dec_seq), db)
    mla_tab_p, mla_tab_s = _mla_tables(pos_p), _mla_tables(pos_s)
    swa_tab_p, swa_tab_s = _swa_tables(pos_p), _swa_tables(pos_s)

    m_all = _modulation_all(jnp.concatenate([c_prompt, c_sample], axis=0), w_mod, b_mod)
    u_b = peer_u.astype(BF16)
    v_b = peer_v.astype(BF16)

    x_p = x_prompt.reshape(n_p, d)
    x_s = x_sample.reshape(n_s, d)
    ckv_p, kpe_p, ckv_s, kpe_s = [], [], [], []
    swk_p, swv_p, swk_s, swv_s = [], [], [], []
    n_buf = cache_swa_k.shape[2]

    for i in range(depth):
        j = i // 2
        mod_p = m_all[i, :batch].reshape(batch, 1, 6 * d)
        mod_s = jnp.repeat(m_all[i, batch:], dec_seq, axis=0)
        g1 = ln1_g[i].reshape(1, d)
        g2 = ln2_g[i].reshape(1, d)
        if i % 2 == 0:
            w = _mla_weights(mla_w_in[j], mla_q_norm[j], mla_kv_norm[j], mla_w_uq[j], mla_w_uk[j],
                             mla_w_uv[j], mla_w_o[j])
            ckv, kpe, q, k, vt = _mla_project(rows_p, x_p, g1, mod_p, mla_tab_p, w, min(MLA_ATTN_BLOCK, seq))
            o = _mla_attention(q, k, vt, batch, seq)
            x_p = _out_proj(rows_p, o, w["w_o"], x_p, mod_p)
            ckv_p.append(ckv.reshape(batch, seq, MLA_KV_LORA))
            kpe_p.append(kpe.reshape(batch, seq, MLA_ROPE))

            ckv, kpe, q, _, _ = _mla_project(rows_s, x_s, g1, mod_s, mla_tab_s, w, rows_s.tb)
            qlat = _mla_absorb_queries(q, w["w_ukt"]).reshape(db, dec_seq * MLA_HEADS, MLA_KV_LORA)
            qpe = q.reshape(n_s, MLA_HEADS, MLA_HEAD_PAD)[:, :, MLA_NOPE:MLA_NOPE + MLA_ROPE]
            qpe = qpe.reshape(db, dec_seq * MLA_HEADS, MLA_ROPE)
            ckv3 = ckv.reshape(db, dec_seq, MLA_KV_LORA)
            kpe3 = kpe.reshape(db, dec_seq, MLA_ROPE)
            olat = _mla_paged_attention(qlat, qpe, ckv3, kpe3, cache_mla_ckv, cache_mla_kpe, page_table, j)
            olat = olat.reshape(n_s, MLA_HEADS * MLA_KV_LORA)
            x_s = _mla_sample_out(rows_s, olat, w["w_uv"], w["w_o"], x_s, mod_s)
            ckv_s.append(ckv3)
            kpe_s.append(kpe3)
        else:
            w = _swa_weights(swa_w_qkv[j], swa_b_qkv[j], swa_w_o[j])
            q, kb, vb, k, v = _swa_project(rows_p, x_p, g1, mod_p, swa_tab_p, w)
            o = _swa_prompt_attention(q, kb, vb, swa_sinks[j], batch, seq)
            x_p = _out_proj(rows_p, o, w["w_o"], x_p, mod_p)
            nb = min(WINDOW, seq)
            swk_p.append(k.reshape(batch, seq, SWA_KV_HEADS, SWA_HEAD_DIM)[:, seq - nb:])
            swv_p.append(v.reshape(batch, seq, SWA_KV_HEADS, SWA_HEAD_DIM)[:, seq - nb:])

            q, kb, vb, k, v = _swa_project(rows_s, x_s, g1, mod_s, swa_tab_s, w)
            buf_k = cache_swa_k[j].reshape(db, n_buf, SWA_NK)
            buf_v = cache_swa_v[j].reshape(db, n_buf, SWA_NK)
            o = _swa_sample_attention(q, k, v, buf_k, buf_v, swa_sinks[j], db, dec_seq)
            x_s = _out_proj(rows_s, o, w["w_o"], x_s, mod_s)
            k_all = jnp.concatenate([buf_k, k.reshape(db, dec_seq, SWA_NK)], axis=1)[:, -n_buf:]
            v_all = jnp.concatenate([buf_v, v.reshape(db, dec_seq, SWA_NK)], axis=1)[:, -n_buf:]
            swk_s.append(k_all.reshape(db, n_buf, SWA_KV_HEADS, SWA_HEAD_DIM))
            swv_s.append(v_all.reshape(db, n_buf, SWA_KV_HEADS, SWA_HEAD_DIM))

        pw = {"w_q": peer_w_q[i].astype(BF16),
              "sub_keys": peer_sub_keys[i].reshape(2 * PEER_HEADS, PEER_N_KEYS, PEER_HALF).astype(BF16)}
        h, experts, gates = _peer_route(route_p, x_p, g2, mod_p, pw)
        x_p = _peer_experts(expert_p, h, experts, gates, u_b, v_b, i, x_p, mod_p)
        h, experts, gates = _peer_route(route_s, x_s, g2, mod_s, pw)
        x_s = _peer_experts(expert_s, h, experts, gates, u_b, v_b, i, x_s, mod_s)

    fg = final_g.reshape(1, d)
    y_p = _final_norm(rows_p, x_p, fg).reshape(batch, seq, d)
    y_s = _final_norm(rows_s, x_s, fg).reshape(db, dec_seq, d)
    return (y_p, y_s,
            jnp.stack(ckv_p), jnp.stack(kpe_p), jnp.stack(ckv_s), jnp.stack(kpe_s),
            jnp.stack(swk_p), jnp.stack(swv_p), jnp.stack(swk_s), jnp.stack(swv_s))
```

```python
import functools

import numpy as np
import jax
import jax.numpy as jnp
from jax import lax
from jax.experimental import pallas as pl
from jax.experimental.pallas import tpu as pltpu

F32 = jnp.float32
BF16 = jnp.bfloat16
I32 = jnp.int32

D_MODEL = 1024
PAGE_SIZE = 128
ROPE_THETA = 500000.0
NORM_EPS = 1e-6
NEG_INF = -1e30

MLA_HEADS = 16
MLA_Q_LORA = 384
MLA_KV_LORA = 256
MLA_NOPE = 64
MLA_ROPE = 32
MLA_V = 64
MLA_SCALE = (MLA_NOPE + MLA_ROPE) ** -0.5

SWA_HEADS = 16
SWA_KV_HEADS = 4
SWA_GROUP = SWA_HEADS // SWA_KV_HEADS
SWA_HEAD_DIM = D_MODEL // SWA_HEADS
SWA_ROT = SWA_HEAD_DIM // 4
SWA_SCALE = SWA_HEAD_DIM ** -0.5
WINDOW = 128

PEER_HEADS = 8
PEER_N_KEYS = 128
PEER_TOPK = 16
PEER_HALF = 128
PEER_PICKS = PEER_HEADS * PEER_TOPK

LANES = 128
SUBLANES = 8
BF16_ROWS = 16
VMEM_LIMIT_BYTES = 56 * 1024 * 1024

ROW_BLOCK = 512
EXPERT_ROW_BLOCK = 512
TOKEN_UNROLL = 64
ROUTE_BLOCK = 256
ROUTE_UNROLL = 4
MLA_ATTN_BLOCK = 1024
SWA_Q_BLOCK = 256
PAGES_PER_STEP = 32
PAGE_GROUP = 32
EXPERT_CHUNK_A = 16
GATE_TILE_PITCH = 72
MOD_COL_BLOCK = 1536


def _cparams(sem):
    return pltpu.CompilerParams(dimension_semantics=sem, vmem_limit_bytes=VMEM_LIMIT_BYTES)


def _dot(a, b):
    return jnp.dot(a, b, preferred_element_type=F32)


def _dot_nt(a, b):
    return lax.dot_general(a, b, (((1,), (1,)), ((), ())), preferred_element_type=F32)


def _rms(x, g):
    return x * lax.rsqrt(jnp.mean(x * x, axis=-1, keepdims=True) + NORM_EPS) * g


def _gelu(x):
    return 0.5 * x * (1.0 + lax.erf(x * np.float32(2.0 ** -0.5)))


def _modulate(x, g, shift, scale):
    return _rms(x, g) * (1.0 + scale) + shift


class _Rows:
    def __init__(self, n, tb, seq_len=None):
        assert n % tb == 0
        self.n, self.tb, self.grid = n, tb, n // tb
        self.bps = None
        if seq_len is not None:
            assert seq_len % tb == 0
            self.bps = seq_len // tb

    def rows(self, width):
        return pl.BlockSpec((self.tb, width), lambda i: (i, 0))

    def mod(self, k):
        if self.bps is None:
            return pl.BlockSpec((self.tb, D_MODEL), lambda i: (i, k))
        bps = self.bps
        return pl.BlockSpec((None, 1, D_MODEL), lambda i: (i // bps, 0, k))

    def pos(self, width):
        if self.bps is None:
            return pl.BlockSpec((self.tb, width), lambda i: (i, 0))
        bps = self.bps
        return pl.BlockSpec((self.tb, width), lambda i: (i % bps, 0))

    @staticmethod
    def full(shape):
        nd = len(shape)
        return pl.BlockSpec(shape, lambda i: (0,) * nd)


def _mod_kernel(c_ref, w_ref, b_ref, o_ref):
    c = c_ref[...]
    a = (c * jax.nn.sigmoid(c)).astype(BF16)
    o_ref[...] = _dot(a, w_ref[...].astype(BF16)) + b_ref[...]


def _modulation_all(c_all, w_mod, b_mod):
    depth, d, n6 = w_mod.shape
    nc = c_all.shape[0]
    nb = n6 // MOD_COL_BLOCK
    return pl.pallas_call(
        _mod_kernel,
        grid=(depth, nb),
        in_specs=[
            pl.BlockSpec((nc, d), lambda l, j: (0, 0)),
            pl.BlockSpec((None, d, MOD_COL_BLOCK), lambda l, j: (l, 0, j)),
            pl.BlockSpec((None, 1, MOD_COL_BLOCK), lambda l, j: (l, 0, j)),
        ],
        out_specs=pl.BlockSpec((None, nc, MOD_COL_BLOCK), lambda l, j: (l, 0, j)),
        out_shape=jax.ShapeDtypeStruct((depth, nc, n6), F32),
        compiler_params=_cparams(("arbitrary", "arbitrary")),
        name="adaln_modulation",
    )(c_all, w_mod, b_mod.reshape(depth, 1, n6))


MLA_IN_EXT = 896
MLA_KPE_COL = 640
MLA_KPE_SWAP_COL = 768
MLA_HEAD_PAD = LANES


def _mla_proj_kernel(x_ref, g_ref, sh_ref, sc_ref, c32_ref, s32_ref, c128_ref, s128_ref,
                     win_ref, qn_ref, kvn_ref, wuq_ref, wuqs_ref, wuk_ref, epe_ref, wuvt_ref,
                     ckv_ref, kpe_ref, q_ref, k_ref, vt_ref):
    h = _modulate(x_ref[...], g_ref[...], sh_ref[...], sc_ref[...]).astype(BF16)
    z = _dot(h, win_ref[...])
    cq = _rms(z[:, :MLA_Q_LORA], qn_ref[...]).astype(BF16)
    ckv = _rms(z[:, MLA_Q_LORA:MLA_Q_LORA + MLA_KV_LORA], kvn_ref[...])
    kpe = (z[:, MLA_KPE_COL:MLA_KPE_COL + MLA_ROPE] * c32_ref[...]
           + z[:, MLA_KPE_SWAP_COL:MLA_KPE_SWAP_COL + MLA_ROPE] * s32_ref[...])
    ckv_ref[...] = ckv
    kpe_ref[...] = kpe
    ckv_b = ckv.astype(BF16)
    q = _dot(cq, wuq_ref[...])
    qs = _dot(cq, wuqs_ref[...])
    cos = c128_ref[...]
    sin = s128_ref[...]
    for hd in range(MLA_HEADS):
        sl = slice(hd * MLA_HEAD_PAD, (hd + 1) * MLA_HEAD_PAD)
        q_ref[:, sl] = (q[:, sl] * cos + qs[:, sl] * sin).astype(BF16)
    k_ref[...] = (_dot(ckv_b, wuk_ref[...]) + _dot(kpe.astype(BF16), epe_ref[...])).astype(BF16)
    vt = _dot_nt(wuvt_ref[...], ckv_b).astype(BF16)
    if len(vt_ref.shape) == 2:
        vt_ref[...] = vt
    else:
        chunks, _, width = vt_ref.shape
        for c in range(chunks):
            vt_ref[c] = vt[:, c * width:(c + 1) * width]


def _mla_project(rows, x, ln_g, mod, tabs, w, chunk):
    n = rows.n
    hp = MLA_HEADS * MLA_HEAD_PAD
    full = _Rows.full
    nv = MLA_HEADS * MLA_V
    if rows.tb >= chunk:
        vt_spec = pl.BlockSpec((rows.tb // chunk, nv, chunk), lambda i: (i, 0, 0))
    else:
        per_chunk = chunk // rows.tb
        vt_spec = pl.BlockSpec((None, nv, rows.tb), lambda i: (i // per_chunk, 0, i % per_chunk))
    return pl.pallas_call(
        _mla_proj_kernel,
        grid=(rows.grid,),
        in_specs=[
            rows.rows(D_MODEL), full((1, D_MODEL)), rows.mod(0), rows.mod(1),
            rows.pos(MLA_ROPE), rows.pos(MLA_ROPE), rows.pos(MLA_HEAD_PAD), rows.pos(MLA_HEAD_PAD),
            full((D_MODEL, MLA_IN_EXT)), full((1, MLA_Q_LORA)), full((1, MLA_KV_LORA)),
            full((MLA_Q_LORA, hp)), full((MLA_Q_LORA, hp)), full((MLA_KV_LORA, hp)),
            full((MLA_ROPE, hp)), full((nv, MLA_KV_LORA)),
        ],
        out_specs=[rows.rows(MLA_KV_LORA), rows.rows(MLA_ROPE), rows.rows(hp), rows.rows(hp),
                   vt_spec],
        out_shape=[
            jax.ShapeDtypeStruct((n, MLA_KV_LORA), F32),
            jax.ShapeDtypeStruct((n, MLA_ROPE), F32),
            jax.ShapeDtypeStruct((n, hp), BF16),
            jax.ShapeDtypeStruct((n, hp), BF16),
            jax.ShapeDtypeStruct((n // chunk, nv, chunk), BF16),
        ],
        compiler_params=_cparams(("arbitrary",)),
        name="mla_project",
    )(x, ln_g, mod, mod, tabs["c32"], tabs["s32"], tabs["c128"], tabs["s128"],
      w["w_in"], w["q_norm"], w["kv_norm"], w["w_uq"], w["w_uq_swap"], w["w_uk"], w["e_pe"], w["w_uvt"])


def _mla_attn_kernel(q_ref, k_ref, vt_ref, o_ref, *, tq, tk):
    assert tq == tk
    qi = pl.program_id(2)
    causal = lax.broadcasted_iota(I32, (tk, tq), 0) <= lax.broadcasted_iota(I32, (tk, tq), 1)
    v_row = lax.broadcasted_iota(I32, (LANES, tk), 0)
    qs = [q_ref[:, hh * LANES:(hh + 1) * LANES] for hh in range(2)]
    v_keep = [(v_row // MLA_V) == hh for hh in range(2)]

    def step(j, carry, masked):
        start = pl.multiple_of(j * tk, tk)
        vt = vt_ref[j]
        new = []
        for hh in range(2):
            m, l, acc = carry[hh]
            kb = k_ref[pl.ds(start, tk), hh * LANES:(hh + 1) * LANES]
            s = _dot_nt(kb, qs[hh]) * MLA_SCALE
            if masked:
                s = jnp.where(causal, s, NEG_INF)
            m_new = jnp.maximum(m, jnp.max(s, axis=0, keepdims=True))
            alpha = jnp.exp(m - m_new)
            p = jnp.exp(s - m_new)
            l = alpha * l + jnp.sum(p, axis=0, keepdims=True)
            acc = alpha * acc + _dot(jnp.where(v_keep[hh], vt, jnp.zeros_like(vt)), p.astype(BF16))
            new.append((m_new, l, acc))
        return tuple(new)

    head0 = (jnp.full((1, tq), NEG_INF, F32), jnp.zeros((1, tq), F32), jnp.zeros((LANES, tq), F32))
    carry = lax.fori_loop(0, qi, lambda j, c: step(j, c, False), (head0, head0))
    (_, l0, acc0), (_, l1, acc1) = step(qi, carry, True)
    o_ref[...] = (acc0 / l0 + acc1 / l1).T.astype(BF16)


def _mla_attention(q, k, vt, batch, seq):
    tq = tk = vt.shape[2]
    nq = seq // tq
    q3 = q.reshape(batch, seq, -1)
    k3 = k.reshape(batch, seq, -1)
    out = pl.pallas_call(
        functools.partial(_mla_attn_kernel, tq=tq, tk=tk),
        grid=(batch, MLA_HEADS // 2, nq),
        in_specs=[
            pl.BlockSpec((None, tq, 2 * LANES), lambda b, hp, i: (b, i, hp)),
            pl.BlockSpec((None, seq, 2 * LANES), lambda b, hp, i: (b, 0, hp)),
            pl.BlockSpec((seq // tk, LANES, tk), lambda b, hp, i: (b, hp, 0)),
        ],
        out_specs=pl.BlockSpec((None, tq, LANES), lambda b, hp, i: (b, i, hp)),
        out_shape=jax.ShapeDtypeStruct((batch, seq, MLA_HEADS * MLA_V), BF16),
        compiler_params=_cparams(("arbitrary", "arbitrary", "arbitrary")),
        name="mla_prompt_attention",
    )(q3, k3, vt)
    return out.reshape(batch * seq, MLA_HEADS * MLA_V)


def _qlat_kernel(q_ref, wk_ref, o_ref):
    for hd in range(MLA_HEADS):
        qh = q_ref[:, hd * MLA_HEAD_PAD:(hd + 1) * MLA_HEAD_PAD]
        o_ref[:, hd * MLA_KV_LORA:(hd + 1) * MLA_KV_LORA] = _dot(qh, wk_ref[hd]).astype(BF16)


def _mla_absorb_queries(q, w_ukt):
    n = q.shape[0]
    return pl.pallas_call(
        _qlat_kernel,
        grid=(1,),
        in_specs=[_Rows.full(q.shape), _Rows.full(w_ukt.shape)],
        out_specs=_Rows.full((n, MLA_HEADS * MLA_KV_LORA)),
        out_shape=jax.ShapeDtypeStruct((n, MLA_HEADS * MLA_KV_LORA), BF16),
        compiler_params=_cparams(("arbitrary",)),
        name="mla_absorb_queries",
    )(q, w_ukt)


def _paged_attn_kernel(pt_ref, qlat_ref, qpe_ref, nckv_ref, nkpe_ref, *rest, n_steps, dec_seq):
    pg = PAGES_PER_STEP
    ckv_refs = rest[:pg]
    kpe_refs = rest[pg:2 * pg]
    o_ref = rest[2 * pg]
    m_sc, l_sc, acc_sc, kc_sc, kp_sc = rest[2 * pg + 1:]
    g = pl.program_id(1)
    rows = qlat_ref.shape[0]

    @pl.when(g == 0)
    def _():
        m_sc[...] = jnp.full(m_sc.shape, NEG_INF, F32)
        l_sc[...] = jnp.zeros(l_sc.shape, F32)
        acc_sc[...] = jnp.zeros(acc_sc.shape, F32)

    qlat = qlat_ref[...]
    qpe = qpe_ref[...]

    def column(row_vec):
        return jnp.broadcast_to(row_vec, (LANES, rows)).T[:, :1]

    def update(s_t, vals):
        m = m_sc[...]
        m_new = jnp.maximum(m, jnp.max(s_t, axis=0, keepdims=True))
        alpha = jnp.exp(m - m_new)
        p_t = jnp.exp(s_t - m_new)
        l_sc[...] = alpha * l_sc[...] + jnp.sum(p_t, axis=0, keepdims=True)
        acc_sc[...] = column(alpha) * acc_sc[...] + _dot(p_t.T.astype(BF16), vals)
        m_sc[...] = m_new

    def scores_t(kc, kp):
        return (_dot_nt(kc, qlat) + _dot_nt(kp, qpe)) * MLA_SCALE

    for grp in range(pg // PAGE_GROUP):
        for i in range(PAGE_GROUP):
            sl = slice(i * PAGE_SIZE, (i + 1) * PAGE_SIZE)
            kc_sc[sl, :] = ckv_refs[grp * PAGE_GROUP + i][...].astype(BF16)
            kp_sc[sl, :] = kpe_refs[grp * PAGE_GROUP + i][...].T.astype(BF16)
        kc = kc_sc[...]
        update(scores_t(kc, kp_sc[...]), kc)

    @pl.when(g == n_steps - 1)
    def _():
        pad = PAGE_SIZE - dec_seq
        ck = jnp.concatenate([nckv_ref[...], jnp.zeros((pad, MLA_KV_LORA), F32)], axis=0).astype(BF16)
        kp = jnp.concatenate([nkpe_ref[...], jnp.zeros((pad, MLA_ROPE), F32)], axis=0).astype(BF16)
        s_t = scores_t(ck, kp)
        k_t = lax.broadcasted_iota(I32, s_t.shape, 0)
        q_t = lax.broadcasted_iota(I32, s_t.shape, 1) // MLA_HEADS
        update(jnp.where(k_t <= q_t, s_t, NEG_INF), ck)
        o_ref[...] = acc_sc[...] / column(l_sc[...])


def _mla_paged_attention(qlat, qpe, new_ckv, new_kpe, cache_ckv, cache_kpe, page_table, layer):
    db, rows, _ = qlat.shape
    dec_seq = new_ckv.shape[1]
    n_pages = page_table.shape[1]
    pg = PAGES_PER_STEP
    assert n_pages % pg == 0
    n_steps = n_pages // pg

    def page_spec(shape, i):
        return pl.BlockSpec((None, None) + shape, lambda b, g, pt: (layer, pt[b, g * pg + i], 0, 0))

    cache_kpe_t = jnp.swapaxes(cache_kpe, 2, 3)
    in_specs = [
        pl.BlockSpec((None, rows, MLA_KV_LORA), lambda b, g, pt: (b, 0, 0)),
        pl.BlockSpec((None, rows, MLA_ROPE), lambda b, g, pt: (b, 0, 0)),
        pl.BlockSpec((None, dec_seq, MLA_KV_LORA), lambda b, g, pt: (b, 0, 0)),
        pl.BlockSpec((None, dec_seq, MLA_ROPE), lambda b, g, pt: (b, 0, 0)),
    ]
    in_specs += [page_spec((PAGE_SIZE, MLA_KV_LORA), i) for i in range(pg)]
    in_specs += [page_spec((MLA_ROPE, PAGE_SIZE), i) for i in range(pg)]
    grid_spec = pltpu.PrefetchScalarGridSpec(
        num_scalar_prefetch=1,
        grid=(db, n_steps),
        in_specs=in_specs,
        out_specs=pl.BlockSpec((None, rows, MLA_KV_LORA), lambda b, g, pt: (b, 0, 0)),
        scratch_shapes=[pltpu.VMEM((1, rows), F32), pltpu.VMEM((1, rows), F32),
                        pltpu.VMEM((rows, MLA_KV_LORA), F32),
                        pltpu.VMEM((PAGE_GROUP * PAGE_SIZE, MLA_KV_LORA), BF16),
                        pltpu.VMEM((PAGE_GROUP * PAGE_SIZE, MLA_ROPE), BF16)],
    )
    return pl.pallas_call(
        functools.partial(_paged_attn_kernel, n_steps=n_steps, dec_seq=dec_seq),
        grid_spec=grid_spec,
        out_shape=jax.ShapeDtypeStruct((db, rows, MLA_KV_LORA), F32),
        compiler_params=_cparams(("arbitrary", "arbitrary")),
        name="mla_paged_attention",
    )(page_table, qlat, qpe, new_ckv, new_kpe, *([cache_ckv] * pg), *([cache_kpe_t] * pg))


def _sample_out_kernel(olat_ref, wuv_ref, wo_ref, x_ref, gate_ref, o_ref):
    col_head = lax.broadcasted_iota(I32, (1, MLA_HEADS * MLA_V), 1) // MLA_V
    wuv = wuv_ref[...]
    o = jnp.zeros((olat_ref.shape[0], MLA_HEADS * MLA_V), F32)
    for hd in range(MLA_HEADS):
        lat = olat_ref[:, hd * MLA_KV_LORA:(hd + 1) * MLA_KV_LORA].astype(BF16)
        o = jnp.where(col_head == hd, _dot(lat, wuv), o)
    o_ref[...] = x_ref[...] + gate_ref[...] * _dot(o.astype(BF16), wo_ref[...])


def _mla_sample_out(rows, olat, w_uv, w_o, x, mod):
    return pl.pallas_call(
        _sample_out_kernel,
        grid=(rows.grid,),
        in_specs=[rows.rows(olat.shape[1]), _Rows.full(w_uv.shape), _Rows.full(w_o.shape),
                  rows.rows(D_MODEL), rows.mod(2)],
        out_specs=rows.rows(D_MODEL),
        out_shape=jax.ShapeDtypeStruct((rows.n, D_MODEL), F32),
        compiler_params=_cparams(("arbitrary",)),
        name="mla_sample_out",
    )(olat, w_uv, w_o, x, mod)


def _out_proj_kernel(o_ref, wo_ref, x_ref, gate_ref, y_ref):
    y_ref[...] = x_ref[...] + gate_ref[...] * _dot(o_ref[...], wo_ref[...])


def _out_proj(rows, o, w_o, x, mod):
    return pl.pallas_call(
        _out_proj_kernel,
        grid=(rows.grid,),
        in_specs=[rows.rows(o.shape[1]), _Rows.full(w_o.shape), rows.rows(D_MODEL), rows.mod(2)],
        out_specs=rows.rows(D_MODEL),
        out_shape=jax.ShapeDtypeStruct((rows.n, D_MODEL), F32),
        compiler_params=_cparams(("arbitrary",)),
        name="attn_out_proj",
    )(o, w_o, x, mod)


SWA_NQ = SWA_HEADS * SWA_HEAD_DIM
SWA_NK = SWA_KV_HEADS * SWA_HEAD_DIM
SWA_QKV = SWA_NQ + 2 * SWA_NK
SWA_EXT = SWA_QKV + SWA_NQ + SWA_NK


def _swa_proj_kernel(x_ref, g_ref, sh_ref, sc_ref, cos_ref, sin_ref, w_ref, b_ref,
                     q_ref, kb_ref, vb_ref, k_ref, v_ref):
    h = _modulate(x_ref[...], g_ref[...], sh_ref[...], sc_ref[...]).astype(BF16)
    z = _dot(h, w_ref[...]) + b_ref[...]
    cos = cos_ref[...]
    sin = sin_ref[...]
    for t in range((SWA_NQ + SWA_NK) // LANES):
        sl = slice(t * LANES, (t + 1) * LANES)
        sw = slice(SWA_QKV + t * LANES, SWA_QKV + (t + 1) * LANES)
        r = z[:, sl] * cos + z[:, sw] * sin
        if t < SWA_NQ // LANES:
            q_ref[:, sl] = (r * SWA_SCALE).astype(BF16)
        else:
            ks = slice(t * LANES - SWA_NQ, (t + 1) * LANES - SWA_NQ)
            k_ref[:, ks] = r
            kb_ref[:, ks] = r.astype(BF16)
    v = z[:, SWA_NQ + SWA_NK:SWA_QKV]
    v_ref[...] = v
    vb_ref[...] = v.astype(BF16)


def _swa_project(rows, x, ln_g, mod, tabs, w):
    n = rows.n
    full = _Rows.full
    return pl.pallas_call(
        _swa_proj_kernel,
        grid=(rows.grid,),
        in_specs=[rows.rows(D_MODEL), full((1, D_MODEL)), rows.mod(0), rows.mod(1),
                  rows.pos(LANES), rows.pos(LANES), full((D_MODEL, SWA_EXT)), full((1, SWA_EXT))],
        out_specs=[rows.rows(SWA_NQ), rows.rows(SWA_NK), rows.rows(SWA_NK), rows.rows(SWA_NK),
                   rows.rows(SWA_NK)],
        out_shape=[
            jax.ShapeDtypeStruct((n, SWA_NQ), BF16),
            jax.ShapeDtypeStruct((n, SWA_NK), BF16),
            jax.ShapeDtypeStruct((n, SWA_NK), BF16),
            jax.ShapeDtypeStruct((n, SWA_NK), F32),
            jax.ShapeDtypeStruct((n, SWA_NK), F32),
        ],
        compiler_params=_cparams(("arbitrary",)),
        name="swa_project",
    )(x, ln_g, mod, mod, tabs["cos"], tabs["sin"], w["w_qkv"], w["b_qkv"])


def _swa_core(q_all, k_all, v_all, mask, sinks_ref, o_ref):
    lane = lax.broadcasted_iota(I32, (1, LANES), 1)
    lo = lane < SWA_HEAD_DIM
    zero = jnp.zeros((), BF16)
    for kh in range(SWA_KV_HEADS):
        tile = kh // 2
        k_t = k_all[:, tile * LANES:(tile + 1) * LANES]
        v_t = v_all[:, tile * LANES:(tile + 1) * LANES]
        k_r = pltpu.roll(k_t, SWA_HEAD_DIM, 1)
        v_r = pltpu.roll(v_t, SWA_HEAD_DIM, 1)
        for pair in range(SWA_GROUP // 2):
            q_t = q_all[:, (kh * 2 + pair) * LANES:(kh * 2 + pair + 1) * LANES]
            out = None
            for half in range(2):
                keep = lo if half == 0 else jnp.logical_not(lo)
                kx = k_t if (kh % 2) == half else k_r
                vx = v_t if (kh % 2) == half else v_r
                s = _dot_nt(jnp.where(keep, q_t, zero), kx)
                s = jnp.where(mask, s, NEG_INF)
                sink = sinks_ref[kh * SWA_GROUP + 2 * pair + half]
                m = jnp.maximum(jnp.max(s, axis=-1, keepdims=True), sink)
                p = jnp.exp(s - m)
                l = jnp.sum(p, axis=-1, keepdims=True) + jnp.exp(sink - m)
                o = _dot((p / l).astype(BF16), jnp.where(keep, vx, zero))
                out = o if out is None else out + o
            o_ref[:, (kh * 2 + pair) * LANES:(kh * 2 + pair + 1) * LANES] = out.astype(o_ref.dtype)


def _swa_prompt_kernel(sinks_ref, q_ref, kc_ref, kp_ref, vc_ref, vp_ref, o_ref, *, tq):
    i = pl.program_id(1)
    k_all = jnp.concatenate([kp_ref[...], kc_ref[...]], axis=0)
    v_all = jnp.concatenate([vp_ref[...], vc_ref[...]], axis=0)
    tk = tq + WINDOW
    q_pos = i * tq + lax.broadcasted_iota(I32, (tq, tk), 0)
    k_pos = i * tq - WINDOW + lax.broadcasted_iota(I32, (tq, tk), 1)
    mask = (k_pos >= 0) & (k_pos <= q_pos) & (q_pos - k_pos < WINDOW)
    _swa_core(q_ref[...], k_all, v_all, mask, sinks_ref, o_ref)


def _swa_prompt_attention(q, kb, vb, sinks, batch, seq):
    tq = SWA_Q_BLOCK
    r = tq // WINDOW
    q3 = q.reshape(batch, seq, SWA_NQ)
    k3 = kb.reshape(batch, seq, SWA_NK)
    v3 = vb.reshape(batch, seq, SWA_NK)
    cur = pl.BlockSpec((None, tq, SWA_NK), lambda b, i: (b, i, 0))
    prev = pl.BlockSpec((None, WINDOW, SWA_NK), lambda b, i: (b, jnp.maximum(i * r - 1, 0), 0))
    out = pl.pallas_call(
        functools.partial(_swa_prompt_kernel, tq=tq),
        grid=(batch, seq // tq),
        in_specs=[pl.BlockSpec(memory_space=pltpu.SMEM),
                  pl.BlockSpec((None, tq, SWA_NQ), lambda b, i: (b, i, 0)), cur, prev, cur, prev],
        out_specs=pl.BlockSpec((None, tq, SWA_NQ), lambda b, i: (b, i, 0)),
        out_shape=jax.ShapeDtypeStruct((batch, seq, SWA_NQ), BF16),
        compiler_params=_cparams(("arbitrary", "arbitrary")),
        name="swa_prompt_attention",
    )(sinks, q3, k3, k3, v3, v3)
    return out.reshape(batch * seq, SWA_NQ)


def _swa_sample_kernel(sinks_ref, q_ref, kbuf_ref, vbuf_ref, kn_ref, vn_ref, o_ref, o_sc, *, n_buf, dec_seq):
    pad = jnp.zeros((BF16_ROWS - dec_seq, SWA_NK), F32)
    k_all = jnp.concatenate([kbuf_ref[...], kn_ref[...], pad], axis=0).astype(BF16)
    v_all = jnp.concatenate([vbuf_ref[...], vn_ref[...], pad], axis=0).astype(BF16)
    tk = n_buf + BF16_ROWS
    t = lax.broadcasted_iota(I32, (BF16_ROWS, tk), 0)
    c = lax.broadcasted_iota(I32, (BF16_ROWS, tk), 1)
    in_buf = c < n_buf
    mask = ((in_buf & ((n_buf + t - c) < WINDOW))
            | (jnp.logical_not(in_buf) & ((c - n_buf) <= jnp.minimum(t, dec_seq - 1))))
    q = jnp.concatenate([q_ref[...].astype(F32), jnp.zeros((BF16_ROWS - dec_seq, SWA_NQ), F32)], axis=0)
    _swa_core(q.astype(BF16), k_all, v_all, mask, sinks_ref, o_sc)
    o_ref[...] = o_sc[:dec_seq, :].astype(o_ref.dtype)


def _swa_sample_attention(q, kn, vn, buf_k, buf_v, sinks, db, dec_seq):
    n_buf = buf_k.shape[1]
    spec = lambda rws, w: pl.BlockSpec((None, rws, w), lambda b: (b, 0, 0))
    out = pl.pallas_call(
        functools.partial(_swa_sample_kernel, n_buf=n_buf, dec_seq=dec_seq),
        grid=(db,),
        in_specs=[pl.BlockSpec(memory_space=pltpu.SMEM), spec(dec_seq, SWA_NQ), spec(n_buf, SWA_NK),
                  spec(n_buf, SWA_NK), spec(dec_seq, SWA_NK), spec(dec_seq, SWA_NK)],
        out_specs=spec(dec_seq, SWA_NQ),
        out_shape=jax.ShapeDtypeStruct((db, dec_seq, SWA_NQ), BF16),
        scratch_shapes=[pltpu.VMEM((BF16_ROWS, SWA_NQ), F32)],
        compiler_params=_cparams(("arbitrary",)),
        name="swa_sample_attention",
    )(sinks, q.reshape(db, dec_seq, SWA_NQ), buf_k, buf_v,
      kn.reshape(db, dec_seq, SWA_NK), vn.reshape(db, dec_seq, SWA_NK))
    return out.reshape(db * dec_seq, SWA_NQ)


def _peer_candidate_tables(width_tokens):
    k = PEER_TOPK
    flat, valid = [], []
    for i, width in _CANDIDATE_RUNS:
        if i is None:
            flat += [k * k] * width
            valid += [False] * width
        elif i >= 8:
            flat += [ii * k for ii in range(8, 16)]
            valid += [True] * 8
        else:
            assert width == k // (i + 1)
            flat += [i * k + j for j in range(width)]
            valid += [True] * width
    assert sum(valid) == 50 and len(flat) % SUBLANES == 0
    flat = np.broadcast_to(np.asarray(flat, np.float32)[:, None], (len(flat), width_tokens))
    pen = np.where(np.asarray(valid), 0.0, -np.inf).astype(np.float32)
    return np.ascontiguousarray(flat), np.ascontiguousarray(np.broadcast_to(pen[:, None], flat.shape))


_CANDIDATE_RUNS = ((0, 16), (1, 8), (2, 5), (4, 3), (3, 4), (5, 2), (6, 2), (8, 8), (7, 2), (None, 6))


def _extract_top(s, row_id, k, payload=None):
    big = jnp.float32(2 ** 30)
    vals, ids = [], []
    for _ in range(k):
        m = jnp.max(s, axis=0, keepdims=True)
        sel = jnp.min(jnp.where(s == m, row_id, big), axis=0, keepdims=True)
        hit = row_id == sel
        if payload is not None:
            ids.append(jnp.max(jnp.where(hit, payload, -1.0), axis=0, keepdims=True))
        else:
            ids.append(sel)
        vals.append(m)
        s = jnp.where(hit, -jnp.inf, s)
    return jnp.concatenate(vals, axis=0), jnp.concatenate(ids, axis=0)


def _route_queries(x_ref, g_ref, sh_ref, sc_ref, wq_ref, qs_sc):
    h = _modulate(x_ref[...], g_ref[...], sh_ref[...], sc_ref[...]).astype(BF16)
    q = _dot(h, wq_ref[...])
    for grp in range(2 * PEER_HEADS):
        qs_sc[grp] = q[:, grp * PEER_HALF:(grp + 1) * PEER_HALF].astype(BF16)
    return h


def _sorting_network(n):
    pairs = []
    p = 1
    while p < n:
        k = p
        while k >= 1:
            for j in range(k % p, n - k, 2 * k):
                for i in range(min(k, n - j - k)):
                    if (i + j) // (2 * p) == (i + j + k) // (2 * p):
                        pairs.append((i + j, i + j + k))
            k //= 2
        p *= 2
    return pairs


def _compare_exchange(vals, ids, a, b):
    swap = vals[b] > vals[a]
    hi, lo = jnp.maximum(vals[a], vals[b]), jnp.minimum(vals[a], vals[b])
    ids[a], ids[b] = jnp.where(swap, ids[b], ids[a]), jnp.where(swap, ids[a], ids[b])
    vals[a], vals[b] = hi, lo


def _top16_by_sorting(s, tb):
    k = PEER_TOPK
    n_tiles = PEER_N_KEYS // SUBLANES
    assert n_tiles == k
    sub = lax.broadcasted_iota(I32, (SUBLANES, tb), 0).astype(F32)
    vals = [s[SUBLANES * v:SUBLANES * (v + 1)] for v in range(n_tiles)]
    ids = [sub + float(SUBLANES * v) for v in range(n_tiles)]
    for a, b in _sorting_network(n_tiles):
        _compare_exchange(vals, ids, a, b)
    shift = SUBLANES // 2
    while shift >= 1:
        other_v = [pltpu.roll(x, shift, 0) for x in vals]
        other_i = [pltpu.roll(x, shift, 0) for x in ids]
        for d in range(k):
            take = other_v[k - 1 - d] > vals[d]
            ids[d] = jnp.where(take, other_i[k - 1 - d], ids[d])
            vals[d] = jnp.maximum(vals[d], other_v[k - 1 - d])
        dist = k // 2
        while dist >= 1:
            for d in range(k):
                if d & dist == 0:
                    _compare_exchange(vals, ids, d, d + dist)
            dist //= 2
        shift //= 2
    top_v = jnp.concatenate([x[:1] for x in vals], axis=0)
    top_i = jnp.concatenate([x[:1] for x in ids], axis=0)
    return top_v, top_i


def _route_first_level(groups, sk_ref, qs_sc, sv_sc, si_sc, tb):
    k = PEER_TOPK
    tied = None
    for grp in groups:
        s = _dot_nt(sk_ref[grp], qs_sc[grp])
        vals, ids = _top16_by_sorting(s, tb)
        sv_sc[grp] = vals
        si_sc[grp] = ids
        n_ge = jnp.sum(jnp.where(s >= vals[k - 1:k], 1.0, 0.0), axis=0, keepdims=True)
        t = jnp.logical_or(jnp.any(n_ge != float(k)), jnp.any(vals[:k - 1] == vals[1:]))
        tied = t if tied is None else jnp.logical_or(tied, t)

    @pl.when(tied)
    def _():
        key_id = lax.broadcasted_iota(I32, (PEER_N_KEYS, tb), 0).astype(F32)
        for grp in groups:
            exact_v, exact_i = _extract_top(_dot_nt(sk_ref[grp], qs_sc[grp]), key_id, k)
            sv_sc[grp] = exact_v
            si_sc[grp] = exact_i


def _route_second_level(sv_sc, si_sc, flat_ref, pen_ref, pe_sc, pg_sc):
    k = PEER_TOPK

    def head(hd, carry):
        sa, sb = sv_sc[2 * hd], sv_sc[2 * hd + 1]
        ia, ib = si_sc[2 * hd], si_sc[2 * hd + 1]
        cs, ce = [], []
        for i, width in _CANDIDATE_RUNS:
            if i is None:
                cs.append(jnp.zeros((width,) + sa.shape[1:], F32))
                ce.append(jnp.zeros((width,) + sa.shape[1:], F32))
            elif i >= 8:
                cs.append(sa[8:] + sb[:1])
                ce.append(ia[8:] * PEER_N_KEYS + ib[:1])
            else:
                cs.append(sa[i:i + 1] + sb[:width])
                ce.append(ia[i:i + 1] * PEER_N_KEYS + ib[:width])
        cand = jnp.concatenate(cs, axis=0) + pen_ref[...]
        cand_e = jnp.concatenate(ce, axis=0)
        best, experts = _extract_top(cand, flat_ref[...], k, payload=cand_e)
        ex = jnp.exp(best - best[:1])
        gates = ex / jnp.sum(ex, axis=0, keepdims=True)
        pick = pl.ds(pl.multiple_of(hd * k, k), k)
        pe_sc[pick, :] = experts
        pg_sc[pick, :] = gates
        return carry

    lax.fori_loop(0, PEER_HEADS, head, 0, unroll=ROUTE_UNROLL)


def _peer_route_kernel(x_ref, g_ref, sh_ref, sc_ref, wq_ref, sk_ref, flat_ref, pen_ref,
                       h_ref, e_ref, gate_ref, qs_sc, sv_sc, si_sc, pe_sc, pg_sc, *, tb):
    h = _route_queries(x_ref, g_ref, sh_ref, sc_ref, wq_ref, qs_sc)
    h_ref[...] = h

    def first_level(trip, carry):
        groups = [trip * ROUTE_UNROLL + u for u in range(ROUTE_UNROLL)]
        _route_first_level(groups, sk_ref, qs_sc, sv_sc, si_sc, tb)
        return carry

    lax.fori_loop(0, 2 * PEER_HEADS // ROUTE_UNROLL, first_level, 0)
    _route_second_level(sv_sc, si_sc, flat_ref, pen_ref, pe_sc, pg_sc)
    e_ref[...] = pe_sc[...].T.astype(I32)
    gate_ref[...] = pg_sc[...].T


def _peer_route(rows, x, ln_g, mod, w):
    n, tb = rows.n, rows.tb
    flat, pen = _peer_candidate_tables(tb)
    full = _Rows.full
    n_groups = 2 * PEER_HEADS
    return pl.pallas_call(
        functools.partial(_peer_route_kernel, tb=tb),
        grid=(rows.grid,),
        in_specs=[rows.rows(D_MODEL), full((1, D_MODEL)), rows.mod(3), rows.mod(4),
                  full(w["w_q"].shape), full(w["sub_keys"].shape), full(flat.shape), full(pen.shape)],
        out_specs=[rows.rows(D_MODEL), rows.rows(PEER_PICKS), rows.rows(PEER_PICKS)],
        out_shape=[jax.ShapeDtypeStruct((n, D_MODEL), BF16),
                   jax.ShapeDtypeStruct((n, PEER_PICKS), I32),
                   jax.ShapeDtypeStruct((n, PEER_PICKS), F32)],
        scratch_shapes=[pltpu.VMEM((n_groups, tb, PEER_HALF), BF16),
                        pltpu.VMEM((n_groups, PEER_TOPK, tb), F32),
                        pltpu.VMEM((n_groups, PEER_TOPK, tb), F32),
                        pltpu.VMEM((PEER_PICKS, tb), F32),
                        pltpu.VMEM((PEER_PICKS, tb), F32)],
        compiler_params=_cparams(("arbitrary",)),
        name="peer_route",
    )(x, ln_g, mod, mod, w["w_q"], w["sub_keys"], jnp.asarray(flat), jnp.asarray(pen))


def _build_gate_tiles(e_ref, gate_ref, w_sc, tb):
    pitch = GATE_TILE_PITCH
    row = lax.broadcasted_iota(I32, (PEER_N_KEYS, PEER_PICKS), 0)
    half_keys = PEER_N_KEYS // 2
    key_a = jnp.where(row < half_keys, 2 * row, 2 * (row - half_keys) + 1)

    zero_tile = jnp.zeros((PEER_N_KEYS, PEER_PICKS), BF16)

    def operands(t):
        e = e_ref[pl.ds(t, 1), :]
        g = gate_ref[pl.ds(t, 1), :].astype(BF16).astype(F32)
        one_a = jnp.where(key_a == (e >> 7), 1.0, 0.0).astype(BF16)
        g_b = jnp.where(row == (e & (PEER_N_KEYS - 1)), g, 0.0).astype(BF16)
        return one_a, g_b

    def token_pair(i, carry):
        t = 2 * i
        a0, b0 = operands(t)
        a1, b1 = operands(t + 1)
        lhs = jnp.concatenate([a0, a1], axis=1)
        rhs = jnp.concatenate([jnp.concatenate([b0, zero_tile], axis=1),
                               jnp.concatenate([zero_tile, b1], axis=1)], axis=0)
        w = _dot_nt(lhs, rhs)
        even = lax.bitcast_convert_type(w[:half_keys].astype(BF16).astype(F32), jnp.uint32)
        odd = lax.bitcast_convert_type(w[half_keys:].astype(BF16).astype(F32), jnp.uint32)
        word = odd | (even >> 16)
        for j in range(2):
            start = pl.multiple_of((t + j) * pitch, SUBLANES)
            w_sc[pl.ds(start, half_keys), :] = word[:, j * PEER_N_KEYS:(j + 1) * PEER_N_KEYS]
        return carry

    lax.fori_loop(0, tb // 2, token_pair, 0, unroll=TOKEN_UNROLL // 2)


def _expert_chunk(h, u_ref, v_ref, w_sc, p_sc, acc_sc, c, tb):
    for pair in range(EXPERT_CHUNK_A // 2):
        z = _dot_nt(h, u_ref[pair * 2 * PEER_N_KEYS:(pair + 1) * 2 * PEER_N_KEYS, :])
        word = w_sc[pl.ds(c * (EXPERT_CHUNK_A // 2) + pair, tb, stride=GATE_TILE_PITCH), :]
        w_pair = (lax.bitcast_convert_type(word << 16, F32),
                  lax.bitcast_convert_type(word & jnp.uint32(0xFFFF0000), F32))
        for half in range(2):
            al = 2 * pair + half
            sl = slice(al * PEER_N_KEYS, (al + 1) * PEER_N_KEYS)
            p_sc[:, sl] = (w_pair[half] * _gelu(z[:, half * PEER_N_KEYS:(half + 1) * PEER_N_KEYS])).astype(BF16)
    acc_sc[...] += _dot(p_sc[...], v_ref[...])


def _peer_expert_kernel(h_ref, e_ref, gate_ref, u_ref, v_ref, x_ref, g2_ref, o_ref,
                        w_sc, p_sc, acc_sc, *, tb, n_chunks):
    c = pl.program_id(1)

    @pl.when(c == 0)
    def _():
        acc_sc[...] = jnp.zeros(acc_sc.shape, F32)
        _build_gate_tiles(e_ref, gate_ref, w_sc, tb)

    _expert_chunk(h_ref[...], u_ref, v_ref, w_sc, p_sc, acc_sc, c, tb)

    @pl.when(c == n_chunks - 1)
    def _():
        o_ref[...] = x_ref[...] + g2_ref[...] * acc_sc[...]


def _peer_experts(rows, h, experts, gates, u, v, layer, x, mod):
    n, tb = rows.n, rows.tb
    ce = EXPERT_CHUNK_A * PEER_N_KEYS
    n_chunks = v.shape[1] // ce
    bps = rows.bps
    if bps is None:
        gate_spec = pl.BlockSpec((tb, D_MODEL), lambda i, c: (i, 5))
    else:
        gate_spec = pl.BlockSpec((None, 1, D_MODEL), lambda i, c: (i // bps, 0, 5))
    tok = lambda w: pl.BlockSpec((tb, w), lambda i, c: (i, 0))
    return pl.pallas_call(
        functools.partial(_peer_expert_kernel, tb=tb, n_chunks=n_chunks),
        grid=(rows.grid, n_chunks),
        in_specs=[tok(D_MODEL), tok(PEER_PICKS), tok(PEER_PICKS),
                  pl.BlockSpec((None, ce, D_MODEL), lambda i, c: (layer, c, 0)),
                  pl.BlockSpec((None, ce, D_MODEL), lambda i, c: (layer, c, 0)),
                  tok(D_MODEL), gate_spec],
        out_specs=tok(D_MODEL),
        out_shape=jax.ShapeDtypeStruct((n, D_MODEL), F32),
        scratch_shapes=[pltpu.VMEM((tb * GATE_TILE_PITCH, LANES), jnp.uint32),
                        pltpu.VMEM((tb, ce), BF16),
                        pltpu.VMEM((tb, D_MODEL), F32)],
        compiler_params=_cparams(("arbitrary", "arbitrary")),
        name="peer_experts",
    )(h, experts, gates, u, v, x, mod)


def _final_norm_kernel(x_ref, g_ref, o_ref):
    o_ref[...] = _rms(x_ref[...], g_ref[...])


def _final_norm(rows, x, g):
    return pl.pallas_call(
        _final_norm_kernel,
        grid=(rows.grid,),
        in_specs=[rows.rows(D_MODEL), _Rows.full((1, D_MODEL))],
        out_specs=rows.rows(D_MODEL),
        out_shape=jax.ShapeDtypeStruct((rows.n, D_MODEL), F32),
        compiler_params=_cparams(("arbitrary",)),
        name="final_norm",
    )(x, g)


def _pair_swap(w, half):
    return jnp.concatenate([-w[..., half:2 * half], w[..., :half]], axis=-1)


def _rope_cos_sin(pos, rot_dim):
    half = rot_dim // 2
    inv_freq = ROPE_THETA ** (-jnp.arange(half, dtype=F32) * 2.0 / rot_dim)
    ang = pos.astype(F32)[:, None] * inv_freq[None, :]
    return jnp.cos(ang), jnp.sin(ang)


def _mla_tables(pos):
    cos, sin = _rope_cos_sin(pos, MLA_ROPE)
    n = pos.shape[0]
    ones = jnp.ones((n, MLA_NOPE), F32)
    zeros_n = jnp.zeros((n, MLA_NOPE), F32)
    pad = jnp.zeros((n, MLA_HEAD_PAD - MLA_NOPE - MLA_ROPE), F32)
    return {
        "c32": jnp.concatenate([cos, cos], axis=1),
        "s32": jnp.concatenate([sin, sin], axis=1),
        "c128": jnp.concatenate([ones, cos, cos, pad], axis=1),
        "s128": jnp.concatenate([zeros_n, sin, sin, pad], axis=1),
    }


def _swa_tables(pos):
    cos, sin = _rope_cos_sin(pos, SWA_ROT)
    n = pos.shape[0]
    rest = SWA_HEAD_DIM - SWA_ROT
    c = jnp.concatenate([cos, cos, jnp.ones((n, rest), F32)], axis=1)
    s = jnp.concatenate([sin, sin, jnp.zeros((n, rest), F32)], axis=1)
    reps = LANES // SWA_HEAD_DIM
    return {"cos": jnp.tile(c, (1, reps)), "sin": jnp.tile(s, (1, reps))}


def _mla_weights(w_in, q_norm, kv_norm, w_uq, w_uk, w_uv, w_o):
    d = w_in.shape[0]
    half = MLA_ROPE // 2
    kpe_cols = w_in[:, MLA_Q_LORA + MLA_KV_LORA:]
    w_in_ext = jnp.zeros((d, MLA_IN_EXT), F32)
    w_in_ext = w_in_ext.at[:, :MLA_KPE_COL + MLA_ROPE].set(w_in)
    w_in_ext = w_in_ext.at[:, MLA_KPE_SWAP_COL:MLA_KPE_SWAP_COL + MLA_ROPE].set(_pair_swap(kpe_cols, half))
    uq = w_uq.reshape(MLA_Q_LORA, MLA_HEADS, MLA_NOPE + MLA_ROPE)
    uq_pad = jnp.zeros((MLA_Q_LORA, MLA_HEADS, MLA_HEAD_PAD), F32).at[:, :, :MLA_NOPE + MLA_ROPE].set(uq)
    uq_swap = jnp.zeros((MLA_Q_LORA, MLA_HEADS, MLA_HEAD_PAD), F32)
    uq_swap = uq_swap.at[:, :, MLA_NOPE:MLA_NOPE + MLA_ROPE].set(_pair_swap(uq[:, :, MLA_NOPE:], half))
    uk_pad = jnp.zeros((MLA_KV_LORA, MLA_HEADS, MLA_HEAD_PAD), F32).at[:, :, :MLA_NOPE].set(w_uk)
    e_pe = jnp.zeros((MLA_ROPE, MLA_HEADS, MLA_HEAD_PAD), F32)
    e_pe = e_pe.at[jnp.arange(MLA_ROPE), :, MLA_NOPE + jnp.arange(MLA_ROPE)].set(1.0)
    ukt = jnp.zeros((MLA_HEADS, MLA_HEAD_PAD, MLA_KV_LORA), F32)
    ukt = ukt.at[:, :MLA_NOPE, :].set(jnp.transpose(w_uk, (1, 2, 0)))
    hp = MLA_HEADS * MLA_HEAD_PAD
    return {
        "w_in": w_in_ext.astype(BF16),
        "q_norm": q_norm.reshape(1, -1), "kv_norm": kv_norm.reshape(1, -1),
        "w_uq": uq_pad.reshape(MLA_Q_LORA, hp).astype(BF16),
        "w_uq_swap": uq_swap.reshape(MLA_Q_LORA, hp).astype(BF16),
        "w_uk": uk_pad.reshape(MLA_KV_LORA, hp).astype(BF16),
        "e_pe": e_pe.reshape(MLA_ROPE, hp).astype(BF16),
        "w_uvt": w_uv.reshape(MLA_KV_LORA, MLA_HEADS * MLA_V).T.astype(BF16),
        "w_ukt": ukt.astype(BF16),
        "w_uv": w_uv.reshape(MLA_KV_LORA, MLA_HEADS * MLA_V).astype(BF16),
        "w_o": w_o.astype(BF16),
    }


def _swa_weights(w_qkv, b_qkv, w_o):
    half = SWA_ROT // 2
    n_rot_heads = SWA_HEADS + SWA_KV_HEADS

    def swap_cols(w):
        lead = w.shape[:-1]
        wh = w[..., :SWA_NQ + SWA_NK].reshape(lead + (n_rot_heads, SWA_HEAD_DIM))
        sw = jnp.concatenate([_pair_swap(wh[..., :SWA_ROT], half),
                              jnp.zeros(lead + (n_rot_heads, SWA_HEAD_DIM - SWA_ROT), F32)], axis=-1)
        return sw.reshape(lead + (SWA_NQ + SWA_NK,))

    w_ext = jnp.concatenate([w_qkv, swap_cols(w_qkv)], axis=-1)
    b_ext = jnp.concatenate([b_qkv, swap_cols(b_qkv)], axis=-1)
    return {"w_qkv": w_ext.astype(BF16), "b_qkv": b_ext.reshape(1, -1), "w_o": w_o.astype(BF16)}


def kernel(x_prompt, x_sample, cache_mla_ckv, cache_mla_kpe, cache_swa_k, cache_swa_v, page_table,
           c_prompt, c_sample, ln1_g, ln2_g, w_mod, b_mod,
           mla_w_in, mla_q_norm, mla_kv_norm, mla_w_uq, mla_w_uk, mla_w_uv, mla_w_o,
           swa_w_qkv, swa_b_qkv, swa_sinks, swa_w_o,
           peer_w_q, peer_sub_keys, peer_u, peer_v, final_g):
    batch, seq, d = x_prompt.shape
    db, dec_seq, _ = x_sample.shape
    depth = w_mod.shape[0]
    past_len = page_table.shape[1] * PAGE_SIZE
    n_p, n_s = batch * seq, db * dec_seq

    rows_p = _Rows(n_p, min(ROW_BLOCK, seq), seq_len=seq)
    rows_s = _Rows(n_s, n_s)
    route_p = _Rows(n_p, ROUTE_BLOCK, seq_len=seq)
    route_s = _Rows(n_s, min(ROUTE_BLOCK, n_s))
    expert_p = _Rows(n_p, EXPERT_ROW_BLOCK, seq_len=seq)
    expert_s = _Rows(n_s, min(EXPERT_ROW_BLOCK, n_s))

    pos_p = jnp.arange(seq)
    pos_s = jnp.tile(past_len + jnp.arange(dec_seq), db)
    mla_tab_p, mla_tab_s = _mla_tables(pos_p), _mla_tables(pos_s)
    swa_tab_p, swa_tab_s = _swa_tables(pos_p), _swa_tables(pos_s)

    m_all = _modulation_all(jnp.concatenate([c_prompt, c_sample], axis=0), w_mod, b_mod)
    u_b = peer_u.astype(BF16)
    v_b = peer_v.astype(BF16)

    x_p = x_prompt.reshape(n_p, d)
    x_s = x_sample.reshape(n_s, d)
    ckv_p, kpe_p, ckv_s, kpe_s = [], [], [], []
    swk_p, swv_p, swk_s, swv_s = [], [], [], []
    n_buf = cache_swa_k.shape[2]

    for i in range(depth):
        j = i // 2
        mod_p = m_all[i, :batch].reshape(batch, 1, 6 * d)
        mod_s = jnp.repeat(m_all[i, batch:], dec_seq, axis=0)
        g1 = ln1_g[i].reshape(1, d)
        g2 = ln2_g[i].reshape(1, d)
        if i % 2 == 0:
            w = _mla_weights(mla_w_in[j], mla_q_norm[j], mla_kv_norm[j], mla_w_uq[j], mla_w_uk[j],
                             mla_w_uv[j], mla_w_o[j])
            ckv, kpe, q, k, vt = _mla_project(rows_p, x_p, g1, mod_p, mla_tab_p, w, min(MLA_ATTN_BLOCK, seq))
            o = _mla_attention(q, k, vt, batch, seq)
            x_p = _out_proj(rows_p, o, w["w_o"], x_p, mod_p)
            ckv_p.append(ckv.reshape(batch, seq, MLA_KV_LORA))
            kpe_p.append(kpe.reshape(batch, seq, MLA_ROPE))

            ckv, kpe, q, _, _ = _mla_project(rows_s, x_s, g1, mod_s, mla_tab_s, w, rows_s.tb)
            qlat = _mla_absorb_queries(q, w["w_ukt"]).reshape(db, dec_seq * MLA_HEADS, MLA_KV_LORA)
            qpe = q.reshape(n_s, MLA_HEADS, MLA_HEAD_PAD)[:, :, MLA_NOPE:MLA_NOPE + MLA_ROPE]
            qpe = qpe.reshape(db, dec_seq * MLA_HEADS, MLA_ROPE)
            ckv3 = ckv.reshape(db, dec_seq, MLA_KV_LORA)
            kpe3 = kpe.reshape(db, dec_seq, MLA_ROPE)
            olat = _mla_paged_attention(qlat, qpe, ckv3, kpe3, cache_mla_ckv, cache_mla_kpe, page_table, j)
            olat = olat.reshape(n_s, MLA_HEADS * MLA_KV_LORA)
            x_s = _mla_sample_out(rows_s, olat, w["w_uv"], w["w_o"], x_s, mod_s)
            ckv_s.append(ckv3)
            kpe_s.append(kpe3)
        else:
            w = _swa_weights(swa_w_qkv[j], swa_b_qkv[j], swa_w_o[j])
            q, kb, vb, k, v = _swa_project(rows_p, x_p, g1, mod_p, swa_tab_p, w)
            o = _swa_prompt_attention(q, kb, vb, swa_sinks[j], batch, seq)
            x_p = _out_proj(rows_p, o, w["w_o"], x_p, mod_p)
            nb = min(WINDOW, seq)
            swk_p.append(k.reshape(batch, seq, SWA_KV_HEADS, SWA_HEAD_DIM)[:, seq - nb:])
            swv_p.append(v.reshape(batch, seq, SWA_KV_HEADS, SWA_HEAD_DIM)[:, seq - nb:])

            q, kb, vb, k, v = _swa_project(rows_s, x_s, g1, mod_s, swa_tab_s, w)
            buf_k = cache_swa_k[j].reshape(db, n_buf, SWA_NK)
            buf_v = cache_swa_v[j].reshape(db, n_buf, SWA_NK)
            o = _swa_sample_attention(q, k, v, buf_k, buf_v, swa_sinks[j], db, dec_seq)
            x_s = _out_proj(rows_s, o, w["w_o"], x_s, mod_s)
            k_all = jnp.concatenate([buf_k, k.reshape(db, dec_seq, SWA_NK)], axis=1)[:, -n_buf:]
            v_all = jnp.concatenate([buf_v, v.reshape(db, dec_seq, SWA_NK)], axis=1)[:, -n_buf:]
            swk_s.append(k_all.reshape(db, n_buf, SWA_KV_HEADS, SWA_HEAD_DIM))
            swv_s.append(v_all.reshape(db, n_buf, SWA_KV_HEADS, SWA_HEAD_DIM))

        pw = {"w_q": peer_w_q[i].astype(BF16),
              "sub_keys": peer_sub_keys[i].reshape(2 * PEER_HEADS, PEER_N_KEYS, PEER_HALF).astype(BF16)}
        h, experts, gates = _peer_route(route_p, x_p, g2, mod_p, pw)
        x_p = _peer_experts(expert_p, h, experts, gates, u_b, v_b, i, x_p, mod_p)
        h, experts, gates = _peer_route(route_s, x_s, g2, mod_s, pw)
        x_s = _peer_experts(expert_s, h, experts, gates, u_b, v_b, i, x_s, mod_s)

    fg = final_g.reshape(1, d)
    y_p = _final_norm(rows_p, x_p, fg).reshape(batch, seq, d)
    y_s = _final_norm(rows_s, x_s, fg).reshape(db, dec_seq, d)
    return (y_p, y_s,
            jnp.stack(ckv_p), jnp.stack(kpe_p), jnp.stack(ckv_s), jnp.stack(kpe_s),
            jnp.stack(swk_p), jnp.stack(swv_p), jnp.stack(swk_s), jnp.stack(swv_s))
```

```python
import functools

import numpy as np
import jax
import jax.numpy as jnp
from jax import lax
from jax.experimental import pallas as pl
from jax.experimental.pallas import tpu as pltpu

F32 = jnp.float32
BF16 = jnp.bfloat16
I32 = jnp.int32

D_MODEL = 1024
PAGE_SIZE = 128
ROPE_THETA = 500000.0
NORM_EPS = 1e-6
NEG_INF = -1e30

MLA_HEADS = 16
MLA_Q_LORA = 384
MLA_KV_LORA = 256
MLA_NOPE = 64
MLA_ROPE = 32
MLA_V = 64
MLA_SCALE = (MLA_NOPE + MLA_ROPE) ** -0.5

SWA_HEADS = 16
SWA_KV_HEADS = 4
SWA_GROUP = SWA_HEADS // SWA_KV_HEADS
SWA_HEAD_DIM = D_MODEL // SWA_HEADS
SWA_ROT = SWA_HEAD_DIM // 4
SWA_SCALE = SWA_HEAD_DIM ** -0.5
WINDOW = 128

PEER_HEADS = 8
PEER_N_KEYS = 128
PEER_TOPK = 16
PEER_HALF = 128
PEER_PICKS = PEER_HEADS * PEER_TOPK

LANES = 128
SUBLANES = 8
BF16_ROWS = 16
VMEM_LIMIT_BYTES = 56 * 1024 * 1024

ROW_BLOCK = 512
EXPERT_ROW_BLOCK = 512
TOKEN_UNROLL = 64
ROUTE_BLOCK = 256
ROUTE_UNROLL = 4
MLA_ATTN_BLOCK = 1024
SWA_SAMPLE_SEQS = 4
SWA_Q_BLOCK = 256
PAGES_PER_STEP = 32
PAGE_GROUP = 32
EXPERT_CHUNK_A = 16
GATE_TILE_PITCH = 72
MOD_COL_BLOCK = 1536


def _cparams(sem):
    return pltpu.CompilerParams(dimension_semantics=sem, vmem_limit_bytes=VMEM_LIMIT_BYTES)


def _dot(a, b):
    return jnp.dot(a, b, preferred_element_type=F32)


def _dot_nt(a, b):
    return lax.dot_general(a, b, (((1,), (1,)), ((), ())), preferred_element_type=F32)


def _rms(x, g):
    return x * lax.rsqrt(jnp.mean(x * x, axis=-1, keepdims=True) + NORM_EPS) * g


def _gelu(x):
    return 0.5 * x * (1.0 + lax.erf(x * np.float32(2.0 ** -0.5)))


def _modulate(x, g, shift, scale):
    return _rms(x, g) * (1.0 + scale) + shift


class _Rows:
    def __init__(self, n, tb, seq_len=None):
        assert n % tb == 0
        self.n, self.tb, self.grid = n, tb, n // tb
        self.bps = None
        if seq_len is not None:
            assert seq_len % tb == 0
            self.bps = seq_len // tb

    def rows(self, width):
        return pl.BlockSpec((self.tb, width), lambda i: (i, 0))

    def mod(self, k):
        if self.bps is None:
            return pl.BlockSpec((self.tb, D_MODEL), lambda i: (i, k))
        bps = self.bps
        return pl.BlockSpec((None, 1, D_MODEL), lambda i: (i // bps, 0, k))

    def pos(self, width):
        if self.bps is None:
            return pl.BlockSpec((self.tb, width), lambda i: (i, 0))
        bps = self.bps
        return pl.BlockSpec((self.tb, width), lambda i: (i % bps, 0))

    @staticmethod
    def full(shape):
        nd = len(shape)
        return pl.BlockSpec(shape, lambda i: (0,) * nd)


def _mod_kernel(c_ref, w_ref, b_ref, o_ref):
    c = c_ref[...]
    a = (c * jax.nn.sigmoid(c)).astype(BF16)
    o_ref[...] = _dot(a, w_ref[...].astype(BF16)) + b_ref[...]


def _modulation_all(c_all, w_mod, b_mod):
    depth, d, n6 = w_mod.shape
    nc = c_all.shape[0]
    nb = n6 // MOD_COL_BLOCK
    return pl.pallas_call(
        _mod_kernel,
        grid=(depth, nb),
        in_specs=[
            pl.BlockSpec((nc, d), lambda l, j: (0, 0)),
            pl.BlockSpec((None, d, MOD_COL_BLOCK), lambda l, j: (l, 0, j)),
            pl.BlockSpec((None, 1, MOD_COL_BLOCK), lambda l, j: (l, 0, j)),
        ],
        out_specs=pl.BlockSpec((None, nc, MOD_COL_BLOCK), lambda l, j: (l, 0, j)),
        out_shape=jax.ShapeDtypeStruct((depth, nc, n6), F32),
        compiler_params=_cparams(("arbitrary", "arbitrary")),
        name="adaln_modulation",
    )(c_all, w_mod, b_mod.reshape(depth, 1, n6))


MLA_IN_EXT = 896
MLA_KPE_COL = 640
MLA_KPE_SWAP_COL = 768
MLA_HEAD_PAD = LANES


def _mla_proj_kernel(x_ref, g_ref, sh_ref, sc_ref, c32_ref, s32_ref, c128_ref, s128_ref,
                     win_ref, qn_ref, kvn_ref, wuq_ref, wuqs_ref, wuk_ref, epe_ref, wuvt_ref,
                     ckv_ref, kpe_ref, q_ref, k_ref, vt_ref):
    h = _modulate(x_ref[...], g_ref[...], sh_ref[...], sc_ref[...]).astype(BF16)
    z = _dot(h, win_ref[...])
    cq = _rms(z[:, :MLA_Q_LORA], qn_ref[...]).astype(BF16)
    ckv = _rms(z[:, MLA_Q_LORA:MLA_Q_LORA + MLA_KV_LORA], kvn_ref[...])
    kpe = (z[:, MLA_KPE_COL:MLA_KPE_COL + MLA_ROPE] * c32_ref[...]
           + z[:, MLA_KPE_SWAP_COL:MLA_KPE_SWAP_COL + MLA_ROPE] * s32_ref[...])
    ckv_ref[...] = ckv
    kpe_ref[...] = kpe
    ckv_b = ckv.astype(BF16)
    q = _dot(cq, wuq_ref[...])
    qs = _dot(cq, wuqs_ref[...])
    cos = c128_ref[...]
    sin = s128_ref[...]
    for hd in range(MLA_HEADS):
        sl = slice(hd * MLA_HEAD_PAD, (hd + 1) * MLA_HEAD_PAD)
        q_ref[:, sl] = (q[:, sl] * cos + qs[:, sl] * sin).astype(BF16)
    k_ref[...] = (_dot(ckv_b, wuk_ref[...]) + _dot(kpe.astype(BF16), epe_ref[...])).astype(BF16)
    vt = _dot_nt(wuvt_ref[...], ckv_b).astype(BF16)
    if len(vt_ref.shape) == 2:
        vt_ref[...] = vt
    else:
        chunks, _, width = vt_ref.shape
        for c in range(chunks):
            vt_ref[c] = vt[:, c * width:(c + 1) * width]


def _mla_project(rows, x, ln_g, mod, tabs, w, chunk):
    n = rows.n
    hp = MLA_HEADS * MLA_HEAD_PAD
    full = _Rows.full
    nv = MLA_HEADS * MLA_V
    if rows.tb >= chunk:
        vt_spec = pl.BlockSpec((rows.tb // chunk, nv, chunk), lambda i: (i, 0, 0))
    else:
        per_chunk = chunk // rows.tb
        vt_spec = pl.BlockSpec((None, nv, rows.tb), lambda i: (i // per_chunk, 0, i % per_chunk))
    return pl.pallas_call(
        _mla_proj_kernel,
        grid=(rows.grid,),
        in_specs=[
            rows.rows(D_MODEL), full((1, D_MODEL)), rows.mod(0), rows.mod(1),
            rows.pos(MLA_ROPE), rows.pos(MLA_ROPE), rows.pos(MLA_HEAD_PAD), rows.pos(MLA_HEAD_PAD),
            full((D_MODEL, MLA_IN_EXT)), full((1, MLA_Q_LORA)), full((1, MLA_KV_LORA)),
            full((MLA_Q_LORA, hp)), full((MLA_Q_LORA, hp)), full((MLA_KV_LORA, hp)),
            full((MLA_ROPE, hp)), full((nv, MLA_KV_LORA)),
        ],
        out_specs=[rows.rows(MLA_KV_LORA), rows.rows(MLA_ROPE), rows.rows(hp), rows.rows(hp),
                   vt_spec],
        out_shape=[
            jax.ShapeDtypeStruct((n, MLA_KV_LORA), F32),
            jax.ShapeDtypeStruct((n, MLA_ROPE), F32),
            jax.ShapeDtypeStruct((n, hp), BF16),
            jax.ShapeDtypeStruct((n, hp), BF16),
            jax.ShapeDtypeStruct((n // chunk, nv, chunk), BF16),
        ],
        compiler_params=_cparams(("arbitrary",)),
        name="mla_project",
    )(x, ln_g, mod, mod, tabs["c32"], tabs["s32"], tabs["c128"], tabs["s128"],
      w["w_in"], w["q_norm"], w["kv_norm"], w["w_uq"], w["w_uq_swap"], w["w_uk"], w["e_pe"], w["w_uvt"])


def _mla_attn_kernel(q_ref, k_ref, vt_ref, o_ref, *, tq, tk):
    assert tq == tk
    qi = pl.program_id(2)
    causal = lax.broadcasted_iota(I32, (tk, tq), 0) <= lax.broadcasted_iota(I32, (tk, tq), 1)
    v_row = lax.broadcasted_iota(I32, (LANES, tk), 0)
    qs = [q_ref[:, hh * LANES:(hh + 1) * LANES] for hh in range(2)]
    v_keep = [(v_row // MLA_V) == hh for hh in range(2)]

    def step(j, carry, masked):
        start = pl.multiple_of(j * tk, tk)
        vt = vt_ref[j]
        new = []
        for hh in range(2):
            m, l, acc = carry[hh]
            kb = k_ref[pl.ds(start, tk), hh * LANES:(hh + 1) * LANES]
            s = _dot_nt(kb, qs[hh]) * MLA_SCALE
            if masked:
                s = jnp.where(causal, s, NEG_INF)
            m_new = jnp.maximum(m, jnp.max(s, axis=0, keepdims=True))
            alpha = jnp.exp(m - m_new)
            p = jnp.exp(s - m_new)
            l = alpha * l + jnp.sum(p, axis=0, keepdims=True)
            acc = alpha * acc + _dot(jnp.where(v_keep[hh], vt, jnp.zeros_like(vt)), p.astype(BF16))
            new.append((m_new, l, acc))
        return tuple(new)

    head0 = (jnp.full((1, tq), NEG_INF, F32), jnp.zeros((1, tq), F32), jnp.zeros((LANES, tq), F32))
    carry = lax.fori_loop(0, qi, lambda j, c: step(j, c, False), (head0, head0))
    (_, l0, acc0), (_, l1, acc1) = step(qi, carry, True)
    o_ref[...] = (acc0 / l0 + acc1 / l1).T.astype(BF16)


def _mla_attention(q, k, vt, batch, seq):
    tq = tk = vt.shape[2]
    nq = seq // tq
    q3 = q.reshape(batch, seq, -1)
    k3 = k.reshape(batch, seq, -1)
    out = pl.pallas_call(
        functools.partial(_mla_attn_kernel, tq=tq, tk=tk),
        grid=(batch, MLA_HEADS // 2, nq),
        in_specs=[
            pl.BlockSpec((None, tq, 2 * LANES), lambda b, hp, i: (b, i, hp)),
            pl.BlockSpec((None, seq, 2 * LANES), lambda b, hp, i: (b, 0, hp)),
            pl.BlockSpec((seq // tk, LANES, tk), lambda b, hp, i: (b, hp, 0)),
        ],
        out_specs=pl.BlockSpec((None, tq, LANES), lambda b, hp, i: (b, i, hp)),
        out_shape=jax.ShapeDtypeStruct((batch, seq, MLA_HEADS * MLA_V), BF16),
        compiler_params=_cparams(("arbitrary", "arbitrary", "arbitrary")),
        name="mla_prompt_attention",
    )(q3, k3, vt)
    return out.reshape(batch * seq, MLA_HEADS * MLA_V)


def _qlat_kernel(q_ref, wk_ref, o_ref):
    for hd in range(MLA_HEADS):
        qh = q_ref[:, hd * MLA_HEAD_PAD:(hd + 1) * MLA_HEAD_PAD]
        o_ref[:, hd * MLA_KV_LORA:(hd + 1) * MLA_KV_LORA] = _dot(qh, wk_ref[hd]).astype(BF16)


def _mla_absorb_queries(q, w_ukt):
    n = q.shape[0]
    return pl.pallas_call(
        _qlat_kernel,
        grid=(1,),
        in_specs=[_Rows.full(q.shape), _Rows.full(w_ukt.shape)],
        out_specs=_Rows.full((n, MLA_HEADS * MLA_KV_LORA)),
        out_shape=jax.ShapeDtypeStruct((n, MLA_HEADS * MLA_KV_LORA), BF16),
        compiler_params=_cparams(("arbitrary",)),
        name="mla_absorb_queries",
    )(q, w_ukt)


def _paged_attn_kernel(pt_ref, qlat_ref, qpe_ref, nckv_ref, nkpe_ref, *rest, n_steps, dec_seq):
    pg = PAGES_PER_STEP
    ckv_refs = rest[:pg]
    kpe_refs = rest[pg:2 * pg]
    o_ref = rest[2 * pg]
    m_sc, l_sc, acc_sc, kc_sc, kp_sc = rest[2 * pg + 1:]
    g = pl.program_id(1)
    rows = qlat_ref.shape[0]

    @pl.when(g == 0)
    def _():
        m_sc[...] = jnp.full(m_sc.shape, NEG_INF, F32)
        l_sc[...] = jnp.zeros(l_sc.shape, F32)
        acc_sc[...] = jnp.zeros(acc_sc.shape, F32)

    qlat = qlat_ref[...]
    qpe = qpe_ref[...]

    def column(row_vec):
        return jnp.broadcast_to(row_vec, (LANES, rows)).T[:, :1]

    def update(s_t, vals):
        m = m_sc[...]
        m_new = jnp.maximum(m, jnp.max(s_t, axis=0, keepdims=True))
        alpha = jnp.exp(m - m_new)
        p_t = jnp.exp(s_t - m_new)
        l_sc[...] = alpha * l_sc[...] + jnp.sum(p_t, axis=0, keepdims=True)
        acc_sc[...] = column(alpha) * acc_sc[...] + _dot(p_t.T.astype(BF16), vals)
        m_sc[...] = m_new

    def scores_t(kc, kp):
        return (_dot_nt(kc, qlat) + _dot_nt(kp, qpe)) * MLA_SCALE

    for grp in range(pg // PAGE_GROUP):
        for i in range(PAGE_GROUP):
            sl = slice(i * PAGE_SIZE, (i + 1) * PAGE_SIZE)
            kc_sc[sl, :] = ckv_refs[grp * PAGE_GROUP + i][...].astype(BF16)
            kp_sc[sl, :] = kpe_refs[grp * PAGE_GROUP + i][...].T.astype(BF16)
        kc = kc_sc[...]
        update(scores_t(kc, kp_sc[...]), kc)

    @pl.when(g == n_steps - 1)
    def _():
        pad = PAGE_SIZE - dec_seq
        ck = jnp.concatenate([nckv_ref[...], jnp.zeros((pad, MLA_KV_LORA), F32)], axis=0).astype(BF16)
        kp = jnp.concatenate([nkpe_ref[...], jnp.zeros((pad, MLA_ROPE), F32)], axis=0).astype(BF16)
        s_t = scores_t(ck, kp)
        k_t = lax.broadcasted_iota(I32, s_t.shape, 0)
        q_t = lax.broadcasted_iota(I32, s_t.shape, 1) // MLA_HEADS
        update(jnp.where(k_t <= q_t, s_t, NEG_INF), ck)
        o_ref[...] = acc_sc[...] / column(l_sc[...])


def _mla_paged_attention(qlat, qpe, new_ckv, new_kpe, cache_ckv, cache_kpe, page_table, layer):
    db, rows, _ = qlat.shape
    dec_seq = new_ckv.shape[1]
    n_pages = page_table.shape[1]
    pg = PAGES_PER_STEP
    assert n_pages % pg == 0
    n_steps = n_pages // pg

    def page_spec(shape, i):
        return pl.BlockSpec((None, None) + shape, lambda b, g, pt: (layer, pt[b, g * pg + i], 0, 0))

    cache_kpe_t = jnp.swapaxes(cache_kpe, 2, 3)
    in_specs = [
        pl.BlockSpec((None, rows, MLA_KV_LORA), lambda b, g, pt: (b, 0, 0)),
        pl.BlockSpec((None, rows, MLA_ROPE), lambda b, g, pt: (b, 0, 0)),
        pl.BlockSpec((None, dec_seq, MLA_KV_LORA), lambda b, g, pt: (b, 0, 0)),
        pl.BlockSpec((None, dec_seq, MLA_ROPE), lambda b, g, pt: (b, 0, 0)),
    ]
    in_specs += [page_spec((PAGE_SIZE, MLA_KV_LORA), i) for i in range(pg)]
    in_specs += [page_spec((MLA_ROPE, PAGE_SIZE), i) for i in range(pg)]
    grid_spec = pltpu.PrefetchScalarGridSpec(
        num_scalar_prefetch=1,
        grid=(db, n_steps),
        in_specs=in_specs,
        out_specs=pl.BlockSpec((None, rows, MLA_KV_LORA), lambda b, g, pt: (b, 0, 0)),
        scratch_shapes=[pltpu.VMEM((1, rows), F32), pltpu.VMEM((1, rows), F32),
                        pltpu.VMEM((rows, MLA_KV_LORA), F32),
                        pltpu.VMEM((PAGE_GROUP * PAGE_SIZE, MLA_KV_LORA), BF16),
                        pltpu.VMEM((PAGE_GROUP * PAGE_SIZE, MLA_ROPE), BF16)],
    )
    return pl.pallas_call(
        functools.partial(_paged_attn_kernel, n_steps=n_steps, dec_seq=dec_seq),
        grid_spec=grid_spec,
        out_shape=jax.ShapeDtypeStruct((db, rows, MLA_KV_LORA), F32),
        compiler_params=_cparams(("arbitrary", "arbitrary")),
        name="mla_paged_attention",
    )(page_table, qlat, qpe, new_ckv, new_kpe, *([cache_ckv] * pg), *([cache_kpe_t] * pg))


def _sample_out_kernel(olat_ref, wuv_ref, wo_ref, x_ref, gate_ref, o_ref):
    col_head = lax.broadcasted_iota(I32, (1, MLA_HEADS * MLA_V), 1) // MLA_V
    wuv = wuv_ref[...]
    o = jnp.zeros((olat_ref.shape[0], MLA_HEADS * MLA_V), F32)
    for hd in range(MLA_HEADS):
        lat = olat_ref[:, hd * MLA_KV_LORA:(hd + 1) * MLA_KV_LORA].astype(BF16)
        o = jnp.where(col_head == hd, _dot(lat, wuv), o)
    o_ref[...] = x_ref[...] + gate_ref[...] * _dot(o.astype(BF16), wo_ref[...])


def _mla_sample_out(rows, olat, w_uv, w_o, x, mod):
    return pl.pallas_call(
        _sample_out_kernel,
        grid=(rows.grid,),
        in_specs=[rows.rows(olat.shape[1]), _Rows.full(w_uv.shape), _Rows.full(w_o.shape),
                  rows.rows(D_MODEL), rows.mod(2)],
        out_specs=rows.rows(D_MODEL),
        out_shape=jax.ShapeDtypeStruct((rows.n, D_MODEL), F32),
        compiler_params=_cparams(("arbitrary",)),
        name="mla_sample_out",
    )(olat, w_uv, w_o, x, mod)


def _out_proj_kernel(o_ref, wo_ref, x_ref, gate_ref, y_ref):
    y_ref[...] = x_ref[...] + gate_ref[...] * _dot(o_ref[...], wo_ref[...])


def _out_proj(rows, o, w_o, x, mod):
    return pl.pallas_call(
        _out_proj_kernel,
        grid=(rows.grid,),
        in_specs=[rows.rows(o.shape[1]), _Rows.full(w_o.shape), rows.rows(D_MODEL), rows.mod(2)],
        out_specs=rows.rows(D_MODEL),
        out_shape=jax.ShapeDtypeStruct((rows.n, D_MODEL), F32),
        compiler_params=_cparams(("arbitrary",)),
        name="attn_out_proj",
    )(o, w_o, x, mod)


SWA_NQ = SWA_HEADS * SWA_HEAD_DIM
SWA_NK = SWA_KV_HEADS * SWA_HEAD_DIM
SWA_QKV = SWA_NQ + 2 * SWA_NK
SWA_EXT = SWA_QKV + SWA_NQ + SWA_NK


def _swa_proj_kernel(x_ref, g_ref, sh_ref, sc_ref, cos_ref, sin_ref, w_ref, b_ref,
                     q_ref, kb_ref, vb_ref, k_ref, v_ref):
    h = _modulate(x_ref[...], g_ref[...], sh_ref[...], sc_ref[...]).astype(BF16)
    z = _dot(h, w_ref[...]) + b_ref[...]
    cos = cos_ref[...]
    sin = sin_ref[...]
    for t in range((SWA_NQ + SWA_NK) // LANES):
        sl = slice(t * LANES, (t + 1) * LANES)
        sw = slice(SWA_QKV + t * LANES, SWA_QKV + (t + 1) * LANES)
        r = z[:, sl] * cos + z[:, sw] * sin
        if t < SWA_NQ // LANES:
            q_ref[:, sl] = (r * SWA_SCALE).astype(BF16)
        else:
            ks = slice(t * LANES - SWA_NQ, (t + 1) * LANES - SWA_NQ)
            k_ref[:, ks] = r
            kb_ref[:, ks] = r.astype(BF16)
    v = z[:, SWA_NQ + SWA_NK:SWA_QKV]
    v_ref[...] = v
    vb_ref[...] = v.astype(BF16)


def _swa_project(rows, x, ln_g, mod, tabs, w):
    n = rows.n
    full = _Rows.full
    return pl.pallas_call(
        _swa_proj_kernel,
        grid=(rows.grid,),
        in_specs=[rows.rows(D_MODEL), full((1, D_MODEL)), rows.mod(0), rows.mod(1),
                  rows.pos(LANES), rows.pos(LANES), full((D_MODEL, SWA_EXT)), full((1, SWA_EXT))],
        out_specs=[rows.rows(SWA_NQ), rows.rows(SWA_NK), rows.rows(SWA_NK), rows.rows(SWA_NK),
                   rows.rows(SWA_NK)],
        out_shape=[
            jax.ShapeDtypeStruct((n, SWA_NQ), BF16),
            jax.ShapeDtypeStruct((n, SWA_NK), BF16),
            jax.ShapeDtypeStruct((n, SWA_NK), BF16),
            jax.ShapeDtypeStruct((n, SWA_NK), F32),
            jax.ShapeDtypeStruct((n, SWA_NK), F32),
        ],
        compiler_params=_cparams(("arbitrary",)),
        name="swa_project",
    )(x, ln_g, mod, mod, tabs["cos"], tabs["sin"], w["w_qkv"], w["b_qkv"])


def _swa_core(q_all, k_all, v_all, mask, sinks_ref, o_ref):
    lane = lax.broadcasted_iota(I32, (1, LANES), 1)
    lo = lane < SWA_HEAD_DIM
    zero = jnp.zeros((), BF16)
    for kh in range(SWA_KV_HEADS):
        tile = kh // 2
        k_t = k_all[:, tile * LANES:(tile + 1) * LANES]
        v_t = v_all[:, tile * LANES:(tile + 1) * LANES]
        k_r = pltpu.roll(k_t, SWA_HEAD_DIM, 1)
        v_r = pltpu.roll(v_t, SWA_HEAD_DIM, 1)
        for pair in range(SWA_GROUP // 2):
            q_t = q_all[:, (kh * 2 + pair) * LANES:(kh * 2 + pair + 1) * LANES]
            out = None
            for half in range(2):
                keep = lo if half == 0 else jnp.logical_not(lo)
                kx = k_t if (kh % 2) == half else k_r
                vx = v_t if (kh % 2) == half else v_r
                s = _dot_nt(jnp.where(keep, q_t, zero), kx)
                s = jnp.where(mask, s, NEG_INF)
                sink = sinks_ref[kh * SWA_GROUP + 2 * pair + half]
                m = jnp.maximum(jnp.max(s, axis=-1, keepdims=True), sink)
                p = jnp.exp(s - m)
                l = jnp.sum(p, axis=-1, keepdims=True) + jnp.exp(sink - m)
                o = _dot((p / l).astype(BF16), jnp.where(keep, vx, zero))
                out = o if out is None else out + o
            o_ref[:, (kh * 2 + pair) * LANES:(kh * 2 + pair + 1) * LANES] = out.astype(o_ref.dtype)


def _swa_prompt_kernel(sinks_ref, q_ref, kc_ref, kp_ref, vc_ref, vp_ref, o_ref, *, tq):
    i = pl.program_id(1)
    k_all = jnp.concatenate([kp_ref[...], kc_ref[...]], axis=0)
    v_all = jnp.concatenate([vp_ref[...], vc_ref[...]], axis=0)
    tk = tq + WINDOW
    q_pos = i * tq + lax.broadcasted_iota(I32, (tq, tk), 0)
    k_pos = i * tq - WINDOW + lax.broadcasted_iota(I32, (tq, tk), 1)
    mask = (k_pos >= 0) & (k_pos <= q_pos) & (q_pos - k_pos < WINDOW)
    _swa_core(q_ref[...], k_all, v_all, mask, sinks_ref, o_ref)


def _swa_prompt_attention(q, kb, vb, sinks, batch, seq):
    tq = SWA_Q_BLOCK
    r = tq // WINDOW
    q3 = q.reshape(batch, seq, SWA_NQ)
    k3 = kb.reshape(batch, seq, SWA_NK)
    v3 = vb.reshape(batch, seq, SWA_NK)
    cur = pl.BlockSpec((None, tq, SWA_NK), lambda b, i: (b, i, 0))
    prev = pl.BlockSpec((None, WINDOW, SWA_NK), lambda b, i: (b, jnp.maximum(i * r - 1, 0), 0))
    out = pl.pallas_call(
        functools.partial(_swa_prompt_kernel, tq=tq),
        grid=(batch, seq // tq),
        in_specs=[pl.BlockSpec(memory_space=pltpu.SMEM),
                  pl.BlockSpec((None, tq, SWA_NQ), lambda b, i: (b, i, 0)), cur, prev, cur, prev],
        out_specs=pl.BlockSpec((None, tq, SWA_NQ), lambda b, i: (b, i, 0)),
        out_shape=jax.ShapeDtypeStruct((batch, seq, SWA_NQ), BF16),
        compiler_params=_cparams(("arbitrary", "arbitrary")),
        name="swa_prompt_attention",
    )(sinks, q3, k3, k3, v3, v3)
    return out.reshape(batch * seq, SWA_NQ)


def _swa_sample_kernel(sinks_ref, q_ref, kbuf_ref, vbuf_ref, kn_ref, vn_ref, o_ref, o_sc, *, n_buf, dec_seq):
    tk = n_buf + BF16_ROWS
    t = lax.broadcasted_iota(I32, (BF16_ROWS, tk), 0)
    c = lax.broadcasted_iota(I32, (BF16_ROWS, tk), 1)
    in_buf = c < n_buf
    mask = ((in_buf & ((n_buf + t - c) < WINDOW))
            | (jnp.logical_not(in_buf) & ((c - n_buf) <= jnp.minimum(t, dec_seq - 1))))
    pad = jnp.zeros((BF16_ROWS - dec_seq, SWA_NK), F32)
    for j in range(q_ref.shape[0]):
        k_all = jnp.concatenate([kbuf_ref[j], kn_ref[j], pad], axis=0).astype(BF16)
        v_all = jnp.concatenate([vbuf_ref[j], vn_ref[j], pad], axis=0).astype(BF16)
        q = jnp.concatenate([q_ref[j].astype(F32), jnp.zeros((BF16_ROWS - dec_seq, SWA_NQ), F32)], axis=0)
        _swa_core(q.astype(BF16), k_all, v_all, mask, sinks_ref, o_sc.at[j])
        o_ref[j] = o_sc[j, :dec_seq, :].astype(o_ref.dtype)


def _swa_sample_attention(q, kn, vn, buf_k, buf_v, sinks, db, dec_seq):
    n_buf = buf_k.shape[1]
    per_step = SWA_SAMPLE_SEQS if db % SWA_SAMPLE_SEQS == 0 else 1
    spec = lambda rws, w: pl.BlockSpec((per_step, rws, w), lambda b: (b, 0, 0))
    out = pl.pallas_call(
        functools.partial(_swa_sample_kernel, n_buf=n_buf, dec_seq=dec_seq),
        grid=(db // per_step,),
        in_specs=[pl.BlockSpec(memory_space=pltpu.SMEM), spec(dec_seq, SWA_NQ), spec(n_buf, SWA_NK),
                  spec(n_buf, SWA_NK), spec(dec_seq, SWA_NK), spec(dec_seq, SWA_NK)],
        out_specs=spec(dec_seq, SWA_NQ),
        out_shape=jax.ShapeDtypeStruct((db, dec_seq, SWA_NQ), BF16),
        scratch_shapes=[pltpu.VMEM((per_step, BF16_ROWS, SWA_NQ), F32)],
        compiler_params=_cparams(("arbitrary",)),
        name="swa_sample_attention",
    )(sinks, q.reshape(db, dec_seq, SWA_NQ), buf_k, buf_v,
      kn.reshape(db, dec_seq, SWA_NK), vn.reshape(db, dec_seq, SWA_NK))
    return out.reshape(db * dec_seq, SWA_NQ)


def _peer_candidate_tables(width_tokens):
    k = PEER_TOPK
    flat, valid = [], []
    for i, width in _CANDIDATE_RUNS:
        if i is None:
            flat += [k * k] * width
            valid += [False] * width
        elif i >= 8:
            flat += [ii * k for ii in range(8, 16)]
            valid += [True] * 8
        else:
            assert width == k // (i + 1)
            flat += [i * k + j for j in range(width)]
            valid += [True] * width
    assert sum(valid) == 50 and len(flat) % SUBLANES == 0
    flat = np.broadcast_to(np.asarray(flat, np.float32)[:, None], (len(flat), width_tokens))
    pen = np.where(np.asarray(valid), 0.0, -np.inf).astype(np.float32)
    return np.ascontiguousarray(flat), np.ascontiguousarray(np.broadcast_to(pen[:, None], flat.shape))


_CANDIDATE_RUNS = ((0, 16), (1, 8), (2, 5), (4, 3), (3, 4), (5, 2), (6, 2), (8, 8), (7, 2), (None, 6))


def _extract_top(s, row_id, k, payload=None):
    big = jnp.float32(2 ** 30)
    vals, ids = [], []
    for _ in range(k):
        m = jnp.max(s, axis=0, keepdims=True)
        sel = jnp.min(jnp.where(s == m, row_id, big), axis=0, keepdims=True)
        hit = row_id == sel
        if payload is not None:
            ids.append(jnp.max(jnp.where(hit, payload, -1.0), axis=0, keepdims=True))
        else:
            ids.append(sel)
        vals.append(m)
        s = jnp.where(hit, -jnp.inf, s)
    return jnp.concatenate(vals, axis=0), jnp.concatenate(ids, axis=0)


def _extract_top_distinct(s, k, payload):
    vals, picked = [], []
    for _ in range(k):
        m = jnp.max(s, axis=0, keepdims=True)
        hit = s == m
        picked.append(jnp.max(jnp.where(hit, payload, -1.0), axis=0, keepdims=True))
        vals.append(m)
        s = jnp.where(hit, -jnp.inf, s)
    return jnp.concatenate(vals, axis=0), jnp.concatenate(picked, axis=0)


def _route_queries(x_ref, g_ref, sh_ref, sc_ref, wq_ref, qs_sc):
    h = _modulate(x_ref[...], g_ref[...], sh_ref[...], sc_ref[...]).astype(BF16)
    q = _dot(h, wq_ref[...])
    for grp in range(2 * PEER_HEADS):
        qs_sc[grp] = q[:, grp * PEER_HALF:(grp + 1) * PEER_HALF].astype(BF16)
    return h


def _sorting_network(n):
    pairs = []
    p = 1
    while p < n:
        k = p
        while k >= 1:
            for j in range(k % p, n - k, 2 * k):
                for i in range(min(k, n - j - k)):
                    if (i + j) // (2 * p) == (i + j + k) // (2 * p):
                        pairs.append((i + j, i + j + k))
            k //= 2
        p *= 2
    return pairs


def _compare_exchange(vals, ids, a, b):
    swap = vals[b] > vals[a]
    hi, lo = jnp.maximum(vals[a], vals[b]), jnp.minimum(vals[a], vals[b])
    ids[a], ids[b] = jnp.where(swap, ids[b], ids[a]), jnp.where(swap, ids[a], ids[b])
    vals[a], vals[b] = hi, lo


def _top16_by_sorting(s, tb):
    k = PEER_TOPK
    n_tiles = PEER_N_KEYS // SUBLANES
    assert n_tiles == k
    sub = lax.broadcasted_iota(I32, (SUBLANES, tb), 0).astype(F32)
    vals = [s[SUBLANES * v:SUBLANES * (v + 1)] for v in range(n_tiles)]
    ids = [sub + float(SUBLANES * v) for v in range(n_tiles)]
    for a, b in _sorting_network(n_tiles):
        _compare_exchange(vals, ids, a, b)
    shift = SUBLANES // 2
    while shift >= 1:
        other_v = [pltpu.roll(x, shift, 0) for x in vals]
        other_i = [pltpu.roll(x, shift, 0) for x in ids]
        for d in range(k):
            take = other_v[k - 1 - d] > vals[d]
            ids[d] = jnp.where(take, other_i[k - 1 - d], ids[d])
            vals[d] = jnp.maximum(vals[d], other_v[k - 1 - d])
        dist = k // 2
        while dist >= 1:
            for d in range(k):
                if d & dist == 0:
                    _compare_exchange(vals, ids, d, d + dist)
            dist //= 2
        shift //= 2
    top_v = jnp.concatenate([x[:1] for x in vals], axis=0)
    top_i = jnp.concatenate([x[:1] for x in ids], axis=0)
    return top_v, top_i


def _route_first_level(groups, sk_ref, qs_sc, sv_sc, si_sc, tb):
    k = PEER_TOPK
    tied = None
    for grp in groups:
        s = _dot_nt(sk_ref[grp], qs_sc[grp])
        vals, ids = _top16_by_sorting(s, tb)
        sv_sc[grp] = vals
        si_sc[grp] = ids
        n_ge = jnp.sum(jnp.where(s >= vals[k - 1:k], 1.0, 0.0), axis=0, keepdims=True)
        t = jnp.logical_or(jnp.any(n_ge != float(k)), jnp.any(vals[:k - 1] == vals[1:]))
        tied = t if tied is None else jnp.logical_or(tied, t)

    @pl.when(tied)
    def _():
        key_id = lax.broadcasted_iota(I32, (PEER_N_KEYS, tb), 0).astype(F32)
        for grp in groups:
            exact_v, exact_i = _extract_top(_dot_nt(sk_ref[grp], qs_sc[grp]), key_id, k)
            sv_sc[grp] = exact_v
            si_sc[grp] = exact_i


def _route_second_level(sv_sc, si_sc, flat_ref, pen_ref, pe_sc, pg_sc):
    k = PEER_TOPK

    def candidates(hd):
        sa, sb = sv_sc[2 * hd], sv_sc[2 * hd + 1]
        ia, ib = si_sc[2 * hd], si_sc[2 * hd + 1]
        cs, ce = [], []
        for i, width in _CANDIDATE_RUNS:
            if i is None:
                cs.append(jnp.zeros((width,) + sa.shape[1:], F32))
                ce.append(jnp.zeros((width,) + sa.shape[1:], F32))
            elif i >= 8:
                cs.append(sa[8:] + sb[:1])
                ce.append(ia[8:] * PEER_N_KEYS + ib[:1])
            else:
                cs.append(sa[i:i + 1] + sb[:width])
                ce.append(ia[i:i + 1] * PEER_N_KEYS + ib[:width])
        return jnp.concatenate(cs, axis=0) + pen_ref[...], jnp.concatenate(ce, axis=0)

    def store(hd, best, experts):
        ex = jnp.exp(best - best[:1])
        pick = pl.ds(pl.multiple_of(hd * k, k), k)
        pe_sc[pick, :] = experts
        pg_sc[pick, :] = ex / jnp.sum(ex, axis=0, keepdims=True)

    def trip(t, carry):
        heads = [t * ROUTE_UNROLL + u for u in range(ROUTE_UNROLL)]
        tied = None
        for hd in heads:
            cand, cand_e = candidates(hd)
            best, experts = _extract_top_distinct(cand, k, cand_e)
            store(hd, best, experts)
            n_ge = jnp.sum(jnp.where(cand >= best[k - 1:k], 1.0, 0.0), axis=0, keepdims=True)
            tt = jnp.logical_or(jnp.any(n_ge != float(k)), jnp.any(best[:k - 1] == best[1:]))
            tied = tt if tied is None else jnp.logical_or(tied, tt)

        @pl.when(tied)
        def _():
            for hd in heads:
                cand, cand_e = candidates(hd)
                store(hd, *_extract_top(cand, flat_ref[...], k, payload=cand_e))

        return carry

    lax.fori_loop(0, PEER_HEADS // ROUTE_UNROLL, trip, 0)


def _peer_route_kernel(x_ref, g_ref, sh_ref, sc_ref, wq_ref, sk_ref, flat_ref, pen_ref,
                       h_ref, e_ref, gate_ref, qs_sc, sv_sc, si_sc, pe_sc, pg_sc, *, tb):
    h = _route_queries(x_ref, g_ref, sh_ref, sc_ref, wq_ref, qs_sc)
    h_ref[...] = h

    def first_level(trip, carry):
        groups = [trip * ROUTE_UNROLL + u for u in range(ROUTE_UNROLL)]
        _route_first_level(groups, sk_ref, qs_sc, sv_sc, si_sc, tb)
        return carry

    lax.fori_loop(0, 2 * PEER_HEADS // ROUTE_UNROLL, first_level, 0)
    _route_second_level(sv_sc, si_sc, flat_ref, pen_ref, pe_sc, pg_sc)
    e_ref[...] = pe_sc[...].T.astype(I32)
    gate_ref[...] = pg_sc[...].T


def _peer_route(rows, x, ln_g, mod, w):
    n, tb = rows.n, rows.tb
    flat, pen = _peer_candidate_tables(tb)
    full = _Rows.full
    n_groups = 2 * PEER_HEADS
    return pl.pallas_call(
        functools.partial(_peer_route_kernel, tb=tb),
        grid=(rows.grid,),
        in_specs=[rows.rows(D_MODEL), full((1, D_MODEL)), rows.mod(3), rows.mod(4),
                  full(w["w_q"].shape), full(w["sub_keys"].shape), full(flat.shape), full(pen.shape)],
        out_specs=[rows.rows(D_MODEL), rows.rows(PEER_PICKS), rows.rows(PEER_PICKS)],
        out_shape=[jax.ShapeDtypeStruct((n, D_MODEL), BF16),
                   jax.ShapeDtypeStruct((n, PEER_PICKS), I32),
                   jax.ShapeDtypeStruct((n, PEER_PICKS), F32)],
        scratch_shapes=[pltpu.VMEM((n_groups, tb, PEER_HALF), BF16),
                        pltpu.VMEM((n_groups, PEER_TOPK, tb), F32),
                        pltpu.VMEM((n_groups, PEER_TOPK, tb), F32),
                        pltpu.VMEM((PEER_PICKS, tb), F32),
                        pltpu.VMEM((PEER_PICKS, tb), F32)],
        compiler_params=_cparams(("arbitrary",)),
        name="peer_route",
    )(x, ln_g, mod, mod, w["w_q"], w["sub_keys"], jnp.asarray(flat), jnp.asarray(pen))


def _build_gate_tiles(e_ref, gate_ref, w_sc, tb):
    pitch = GATE_TILE_PITCH
    row = lax.broadcasted_iota(I32, (PEER_N_KEYS, PEER_PICKS), 0)
    half_keys = PEER_N_KEYS // 2
    key_a = jnp.where(row < half_keys, 2 * row, 2 * (row - half_keys) + 1)

    zero_tile = jnp.zeros((PEER_N_KEYS, PEER_PICKS), BF16)

    def operands(t):
        e = e_ref[pl.ds(t, 1), :]
        g = gate_ref[pl.ds(t, 1), :].astype(BF16).astype(F32)
        one_a = jnp.where(key_a == (e >> 7), 1.0, 0.0).astype(BF16)
        g_b = jnp.where(row == (e & (PEER_N_KEYS - 1)), g, 0.0).astype(BF16)
        return one_a, g_b

    def token_pair(i, carry):
        t = 2 * i
        a0, b0 = operands(t)
        a1, b1 = operands(t + 1)
        lhs = jnp.concatenate([a0, a1], axis=1)
        rhs = jnp.concatenate([jnp.concatenate([b0, zero_tile], axis=1),
                               jnp.concatenate([zero_tile, b1], axis=1)], axis=0)
        w = _dot_nt(lhs, rhs)
        even = lax.bitcast_convert_type(w[:half_keys].astype(BF16).astype(F32), jnp.uint32)
        odd = lax.bitcast_convert_type(w[half_keys:].astype(BF16).astype(F32), jnp.uint32)
        word = odd | (even >> 16)
        for j in range(2):
            start = pl.multiple_of((t + j) * pitch, SUBLANES)
            w_sc[pl.ds(start, half_keys), :] = word[:, j * PEER_N_KEYS:(j + 1) * PEER_N_KEYS]
        return carry

    lax.fori_loop(0, tb // 2, token_pair, 0, unroll=TOKEN_UNROLL // 2)


def _expert_chunk(h, u_ref, v_ref, w_sc, p_sc, acc_sc, c, tb):
    for pair in range(EXPERT_CHUNK_A // 2):
        z = _dot_nt(h, u_ref[pair * 2 * PEER_N_KEYS:(pair + 1) * 2 * PEER_N_KEYS, :])
        word = w_sc[pl.ds(c * (EXPERT_CHUNK_A // 2) + pair, tb, stride=GATE_TILE_PITCH), :]
        w_pair = (lax.bitcast_convert_type(word << 16, F32),
                  lax.bitcast_convert_type(word & jnp.uint32(0xFFFF0000), F32))
        for half in range(2):
            al = 2 * pair + half
            sl = slice(al * PEER_N_KEYS, (al + 1) * PEER_N_KEYS)
            p_sc[:, sl] = (w_pair[half] * _gelu(z[:, half * PEER_N_KEYS:(half + 1) * PEER_N_KEYS])).astype(BF16)
    acc_sc[...] += _dot(p_sc[...], v_ref[...])


def _peer_expert_kernel(h_ref, e_ref, gate_ref, u_ref, v_ref, x_ref, g2_ref, o_ref,
                        w_sc, p_sc, acc_sc, *, tb, n_chunks):
    c = pl.program_id(1)

    @pl.when(c == 0)
    def _():
        acc_sc[...] = jnp.zeros(acc_sc.shape, F32)
        _build_gate_tiles(e_ref, gate_ref, w_sc, tb)

    _expert_chunk(h_ref[...], u_ref, v_ref, w_sc, p_sc, acc_sc, c, tb)

    @pl.when(c == n_chunks - 1)
    def _():
        o_ref[...] = x_ref[...] + g2_ref[...] * acc_sc[...]


def _peer_experts(rows, h, experts, gates, u, v, layer, x, mod):
    n, tb = rows.n, rows.tb
    ce = EXPERT_CHUNK_A * PEER_N_KEYS
    n_chunks = v.shape[1] // ce
    bps = rows.bps
    if bps is None:
        gate_spec = pl.BlockSpec((tb, D_MODEL), lambda i, c: (i, 5))
    else:
        gate_spec = pl.BlockSpec((None, 1, D_MODEL), lambda i, c: (i // bps, 0, 5))
    tok = lambda w: pl.BlockSpec((tb, w), lambda i, c: (i, 0))
    return pl.pallas_call(
        functools.partial(_peer_expert_kernel, tb=tb, n_chunks=n_chunks),
        grid=(rows.grid, n_chunks),
        in_specs=[tok(D_MODEL), tok(PEER_PICKS), tok(PEER_PICKS),
                  pl.BlockSpec((None, ce, D_MODEL), lambda i, c: (layer, c, 0)),
                  pl.BlockSpec((None, ce, D_MODEL), lambda i, c: (layer, c, 0)),
                  tok(D_MODEL), gate_spec],
        out_specs=tok(D_MODEL),
        out_shape=jax.ShapeDtypeStruct((n, D_MODEL), F32),
        scratch_shapes=[pltpu.VMEM((tb * GATE_TILE_PITCH, LANES), jnp.uint32),
                        pltpu.VMEM((tb, ce), BF16),
                        pltpu.VMEM((tb, D_MODEL), F32)],
        compiler_params=_cparams(("arbitrary", "arbitrary")),
        name="peer_experts",
    )(h, experts, gates, u, v, x, mod)


def _final_norm_kernel(x_ref, g_ref, o_ref):
    o_ref[...] = _rms(x_ref[...], g_ref[...])


def _final_norm(rows, x, g):
    return pl.pallas_call(
        _final_norm_kernel,
        grid=(rows.grid,),
        in_specs=[rows.rows(D_MODEL), _Rows.full((1, D_MODEL))],
        out_specs=rows.rows(D_MODEL),
        out_shape=jax.ShapeDtypeStruct((rows.n, D_MODEL), F32),
        compiler_params=_cparams(("arbitrary",)),
        name="final_norm",
    )(x, g)


def _pair_swap(w, half):
    return jnp.concatenate([-w[..., half:2 * half], w[..., :half]], axis=-1)


def _rope_cos_sin(pos, rot_dim):
    half = rot_dim // 2
    inv_freq = ROPE_THETA ** (-jnp.arange(half, dtype=F32) * 2.0 / rot_dim)
    ang = pos.astype(F32)[:, None] * inv_freq[None, :]
    return jnp.cos(ang), jnp.sin(ang)


def _mla_tables(pos):
    cos, sin = _rope_cos_sin(pos, MLA_ROPE)
    n = pos.shape[0]
    ones = jnp.ones((n, MLA_NOPE), F32)
    zeros_n = jnp.zeros((n, MLA_NOPE), F32)
    pad = jnp.zeros((n, MLA_HEAD_PAD - MLA_NOPE - MLA_ROPE), F32)
    return {
        "c32": jnp.concatenate([cos, cos], axis=1),
        "s32": jnp.concatenate([sin, sin], axis=1),
        "c128": jnp.concatenate([ones, cos, cos, pad], axis=1),
        "s128": jnp.concatenate([zeros_n, sin, sin, pad], axis=1),
    }


def _swa_tables(pos):
    cos, sin = _rope_cos_sin(pos, SWA_ROT)
    n = pos.shape[0]
    rest = SWA_HEAD_DIM - SWA_ROT
    c = jnp.concatenate([cos, cos, jnp.ones((n, rest), F32)], axis=1)
    s = jnp.concatenate([sin, sin, jnp.zeros((n, rest), F32)], axis=1)
    reps = LANES // SWA_HEAD_DIM
    return {"cos": jnp.tile(c, (1, reps)), "sin": jnp.tile(s, (1, reps))}


def _mla_weights(w_in, q_norm, kv_norm, w_uq, w_uk, w_uv, w_o):
    d = w_in.shape[0]
    half = MLA_ROPE // 2
    kpe_cols = w_in[:, MLA_Q_LORA + MLA_KV_LORA:]
    w_in_ext = jnp.zeros((d, MLA_IN_EXT), F32)
    w_in_ext = w_in_ext.at[:, :MLA_KPE_COL + MLA_ROPE].set(w_in)
    w_in_ext = w_in_ext.at[:, MLA_KPE_SWAP_COL:MLA_KPE_SWAP_COL + MLA_ROPE].set(_pair_swap(kpe_cols, half))
    uq = w_uq.reshape(MLA_Q_LORA, MLA_HEADS, MLA_NOPE + MLA_ROPE)
    uq_pad = jnp.zeros((MLA_Q_LORA, MLA_HEADS, MLA_HEAD_PAD), F32).at[:, :, :MLA_NOPE + MLA_ROPE].set(uq)
    uq_swap = jnp.zeros((MLA_Q_LORA, MLA_HEADS, MLA_HEAD_PAD), F32)
    uq_swap = uq_swap.at[:, :, MLA_NOPE:MLA_NOPE + MLA_ROPE].set(_pair_swap(uq[:, :, MLA_NOPE:], half))
    uk_pad = jnp.zeros((MLA_KV_LORA, MLA_HEADS, MLA_HEAD_PAD), F32).at[:, :, :MLA_NOPE].set(w_uk)
    e_pe = jnp.zeros((MLA_ROPE, MLA_HEADS, MLA_HEAD_PAD), F32)
    e_pe = e_pe.at[jnp.arange(MLA_ROPE), :, MLA_NOPE + jnp.arange(MLA_ROPE)].set(1.0)
    ukt = jnp.zeros((MLA_HEADS, MLA_HEAD_PAD, MLA_KV_LORA), F32)
    ukt = ukt.at[:, :MLA_NOPE, :].set(jnp.transpose(w_uk, (1, 2, 0)))
    hp = MLA_HEADS * MLA_HEAD_PAD
    return {
        "w_in": w_in_ext.astype(BF16),
        "q_norm": q_norm.reshape(1, -1), "kv_norm": kv_norm.reshape(1, -1),
        "w_uq": uq_pad.reshape(MLA_Q_LORA, hp).astype(BF16),
        "w_uq_swap": uq_swap.reshape(MLA_Q_LORA, hp).astype(BF16),
        "w_uk": uk_pad.reshape(MLA_KV_LORA, hp).astype(BF16),
        "e_pe": e_pe.reshape(MLA_ROPE, hp).astype(BF16),
        "w_uvt": w_uv.reshape(MLA_KV_LORA, MLA_HEADS * MLA_V).T.astype(BF16),
        "w_ukt": ukt.astype(BF16),
        "w_uv": w_uv.reshape(MLA_KV_LORA, MLA_HEADS * MLA_V).astype(BF16),
        "w_o": w_o.astype(BF16),
    }


def _swa_weights(w_qkv, b_qkv, w_o):
    half = SWA_ROT // 2
    n_rot_heads = SWA_HEADS + SWA_KV_HEADS

    def swap_cols(w):
        lead = w.shape[:-1]
        wh = w[..., :SWA_NQ + SWA_NK].reshape(lead + (n_rot_heads, SWA_HEAD_DIM))
        sw = jnp.concatenate([_pair_swap(wh[..., :SWA_ROT], half),
                              jnp.zeros(lead + (n_rot_heads, SWA_HEAD_DIM - SWA_ROT), F32)], axis=-1)
        return sw.reshape(lead + (SWA_NQ + SWA_NK,))

    w_ext = jnp.concatenate([w_qkv, swap_cols(w_qkv)], axis=-1)
    b_ext = jnp.concatenate([b_qkv, swap_cols(b_qkv)], axis=-1)
    return {"w_qkv": w_ext.astype(BF16), "b_qkv": b_ext.reshape(1, -1), "w_o": w_o.astype(BF16)}


def kernel(x_prompt, x_sample, cache_mla_ckv, cache_mla_kpe, cache_swa_k, cache_swa_v, page_table,
           c_prompt, c_sample, ln1_g, ln2_g, w_mod, b_mod,
           mla_w_in, mla_q_norm, mla_kv_norm, mla_w_uq, mla_w_uk, mla_w_uv, mla_w_o,
           swa_w_qkv, swa_b_qkv, swa_sinks, swa_w_o,
           peer_w_q, peer_sub_keys, peer_u, peer_v, final_g):
    batch, seq, d = x_prompt.shape
    db, dec_seq, _ = x_sample.shape
    depth = w_mod.shape[0]
    past_len = page_table.shape[1] * PAGE_SIZE
    n_p, n_s = batch * seq, db * dec_seq

    rows_p = _Rows(n_p, min(ROW_BLOCK, seq), seq_len=seq)
    rows_s = _Rows(n_s, n_s)
    route_p = _Rows(n_p, ROUTE_BLOCK, seq_len=seq)
    route_s = _Rows(n_s, min(ROUTE_BLOCK, n_s))
    expert_p = _Rows(n_p, EXPERT_ROW_BLOCK, seq_len=seq)
    expert_s = _Rows(n_s, min(EXPERT_ROW_BLOCK, n_s))

    pos_p = jnp.arange(seq)
    pos_s = jnp.tile(past_len + jnp.arange(dec_seq), db)
    mla_tab_p, mla_tab_s = _mla_tables(pos_p), _mla_tables(pos_s)
    swa_tab_p, swa_tab_s = _swa_tables(pos_p), _swa_tables(pos_s)

    m_all = _modulation_all(jnp.concatenate([c_prompt, c_sample], axis=0), w_mod, b_mod)
    u_b = peer_u.astype(BF16)
    v_b = peer_v.astype(BF16)

    x_p = x_prompt.reshape(n_p, d)
    x_s = x_sample.reshape(n_s, d)
    ckv_p, kpe_p, ckv_s, kpe_s = [], [], [], []
    swk_p, swv_p, swk_s, swv_s = [], [], [], []
    n_buf = cache_swa_k.shape[2]

    for i in range(depth):
        j = i // 2
        mod_p = m_all[i, :batch].reshape(batch, 1, 6 * d)
        mod_s = jnp.repeat(m_all[i, batch:], dec_seq, axis=0)
        g1 = ln1_g[i].reshape(1, d)
        g2 = ln2_g[i].reshape(1, d)
        if i % 2 == 0:
            w = _mla_weights(mla_w_in[j], mla_q_norm[j], mla_kv_norm[j], mla_w_uq[j], mla_w_uk[j],
                             mla_w_uv[j], mla_w_o[j])
            ckv, kpe, q, k, vt = _mla_project(rows_p, x_p, g1, mod_p, mla_tab_p, w, min(MLA_ATTN_BLOCK, seq))
            o = _mla_attention(q, k, vt, batch, seq)
            x_p = _out_proj(rows_p, o, w["w_o"], x_p, mod_p)
            ckv_p.append(ckv.reshape(batch, seq, MLA_KV_LORA))
            kpe_p.append(kpe.reshape(batch, seq, MLA_ROPE))

            ckv, kpe, q, _, _ = _mla_project(rows_s, x_s, g1, mod_s, mla_tab_s, w, rows_s.tb)
            qlat = _mla_absorb_queries(q, w["w_ukt"]).reshape(db, dec_seq * MLA_HEADS, MLA_KV_LORA)
            qpe = q.reshape(n_s, MLA_HEADS, MLA_HEAD_PAD)[:, :, MLA_NOPE:MLA_NOPE + MLA_ROPE]
            qpe = qpe.reshape(db, dec_seq * MLA_HEADS, MLA_ROPE)
            ckv3 = ckv.reshape(db, dec_seq, MLA_KV_LORA)
            kpe3 = kpe.reshape(db, dec_seq, MLA_ROPE)
            olat = _mla_paged_attention(qlat, qpe, ckv3, kpe3, cache_mla_ckv, cache_mla_kpe, page_table, j)
            olat = olat.reshape(n_s, MLA_HEADS * MLA_KV_LORA)
            x_s = _mla_sample_out(rows_s, olat, w["w_uv"], w["w_o"], x_s, mod_s)
            ckv_s.append(ckv3)
            kpe_s.append(kpe3)
        else:
            w = _swa_weights(swa_w_qkv[j], swa_b_qkv[j], swa_w_o[j])
            q, kb, vb, k, v = _swa_project(rows_p, x_p, g1, mod_p, swa_tab_p, w)
            o = _swa_prompt_attention(q, kb, vb, swa_sinks[j], batch, seq)
            x_p = _out_proj(rows_p, o, w["w_o"], x_p, mod_p)
            nb = min(WINDOW, seq)
            swk_p.append(k.reshape(batch, seq, SWA_KV_HEADS, SWA_HEAD_DIM)[:, seq - nb:])
            swv_p.append(v.reshape(batch, seq, SWA_KV_HEADS, SWA_HEAD_DIM)[:, seq - nb:])

            q, kb, vb, k, v = _swa_project(rows_s, x_s, g1, mod_s, swa_tab_s, w)
            buf_k = cache_swa_k[j].reshape(db, n_buf, SWA_NK)
            buf_v = cache_swa_v[j].reshape(db, n_buf, SWA_NK)
            o = _swa_sample_attention(q, k, v, buf_k, buf_v, swa_sinks[j], db, dec_seq)
            x_s = _out_proj(rows_s, o, w["w_o"], x_s, mod_s)
            k_all = jnp.concatenate([buf_k, k.reshape(db, dec_seq, SWA_NK)], axis=1)[:, -n_buf:]
            v_all = jnp.concatenate([buf_v, v.reshape(db, dec_seq, SWA_NK)], axis=1)[:, -n_buf:]
            swk_s.append(k_all.reshape(db, n_buf, SWA_KV_HEADS, SWA_HEAD_DIM))
            swv_s.append(v_all.reshape(db, n_buf, SWA_KV_HEADS, SWA_HEAD_DIM))

        pw = {"w_q": peer_w_q[i].astype(BF16),
              "sub_keys": peer_sub_keys[i].reshape(2 * PEER_HEADS, PEER_N_KEYS, PEER_HALF).astype(BF16)}
        h, experts, gates = _peer_route(route_p, x_p, g2, mod_p, pw)
        x_p = _peer_experts(expert_p, h, experts, gates, u_b, v_b, i, x_p, mod_p)
        h, experts, gates = _peer_route(route_s, x_s, g2, mod_s, pw)
        x_s = _peer_experts(expert_s, h, experts, gates, u_b, v_b, i, x_s, mod_s)

    fg = final_g.reshape(1, d)
    y_p = _final_norm(rows_p, x_p, fg).reshape(batch, seq, d)
    y_s = _final_norm(rows_s, x_s, fg).reshape(db, dec_seq, d)
    return (y_p, y_s,
            jnp.stack(ckv_p), jnp.stack(kpe_p), jnp.stack(ckv_s), jnp.stack(kpe_s),
            jnp.stack(swk_p), jnp.stack(swv_p), jnp.stack(swk_s), jnp.stack(swv_s))
```

```python
import functools

import numpy as np
import jax
import jax.numpy as jnp
from jax import lax
from jax.experimental import pallas as pl
from jax.experimental.pallas import tpu as pltpu

F32 = jnp.float32
BF16 = jnp.bfloat16
I32 = jnp.int32

D_MODEL = 1024
PAGE_SIZE = 128
ROPE_THETA = 500000.0
NORM_EPS = 1e-6
NEG_INF = -1e30

MLA_HEADS = 16
MLA_Q_LORA = 384
MLA_KV_LORA = 256
MLA_NOPE = 64
MLA_ROPE = 32
MLA_V = 64
MLA_SCALE = (MLA_NOPE + MLA_ROPE) ** -0.5

SWA_HEADS = 16
SWA_KV_HEADS = 4
SWA_GROUP = SWA_HEADS // SWA_KV_HEADS
SWA_HEAD_DIM = D_MODEL // SWA_HEADS
SWA_ROT = SWA_HEAD_DIM // 4
SWA_SCALE = SWA_HEAD_DIM ** -0.5
WINDOW = 128

PEER_HEADS = 8
PEER_N_KEYS = 128
PEER_TOPK = 16
PEER_HALF = 128
PEER_PICKS = PEER_HEADS * PEER_TOPK

LANES = 128
SUBLANES = 8
BF16_ROWS = 16
VMEM_LIMIT_BYTES = 56 * 1024 * 1024

ROW_BLOCK = 512
EXPERT_ROW_BLOCK = 512
TOKEN_UNROLL = 128
ROUTE_BLOCK = 512
ROUTE_UNROLL = 4
MLA_ATTN_BLOCK = 1024
SWA_SAMPLE_SEQS = 4
SWA_Q_BLOCK = 256
PAGES_PER_STEP = 32
PAGE_GROUP = 32
EXPERT_CHUNK_A = 16
GATE_TILE_PITCH = 72
MOD_COL_BLOCK = 1536


def _cparams(sem):
    return pltpu.CompilerParams(dimension_semantics=sem, vmem_limit_bytes=VMEM_LIMIT_BYTES)


def _dot(a, b):
    return jnp.dot(a, b, preferred_element_type=F32)


def _dot_nt(a, b):
    return lax.dot_general(a, b, (((1,), (1,)), ((), ())), preferred_element_type=F32)


def _rms(x, g):
    return x * lax.rsqrt(jnp.mean(x * x, axis=-1, keepdims=True) + NORM_EPS) * g


def _gelu(x):
    return 0.5 * x * (1.0 + lax.erf(x * np.float32(2.0 ** -0.5)))


def _modulate(x, g, shift, scale):
    return _rms(x, g) * (1.0 + scale) + shift


class _Rows:
    def __init__(self, n, tb, seq_len=None):
        assert n % tb == 0
        self.n, self.tb, self.grid = n, tb, n // tb
        self.bps = None
        if seq_len is not None:
            assert seq_len % tb == 0
            self.bps = seq_len // tb

    def rows(self, width):
        return pl.BlockSpec((self.tb, width), lambda i: (i, 0))

    def mod(self, k):
        if self.bps is None:
            return pl.BlockSpec((self.tb, D_MODEL), lambda i: (i, k))
        bps = self.bps
        return pl.BlockSpec((None, 1, D_MODEL), lambda i: (i // bps, 0, k))

    def pos(self, width):
        if self.bps is None:
            return pl.BlockSpec((self.tb, width), lambda i: (i, 0))
        bps = self.bps
        return pl.BlockSpec((self.tb, width), lambda i: (i % bps, 0))

    @staticmethod
    def full(shape):
        nd = len(shape)
        return pl.BlockSpec(shape, lambda i: (0,) * nd)


def _mod_kernel(c_ref, w_ref, b_ref, o_ref):
    c = c_ref[...]
    a = (c * jax.nn.sigmoid(c)).astype(BF16)
    o_ref[...] = _dot(a, w_ref[...].astype(BF16)) + b_ref[...]


def _modulation_all(c_all, w_mod, b_mod):
    depth, d, n6 = w_mod.shape
    nc = c_all.shape[0]
    nb = n6 // MOD_COL_BLOCK
    return pl.pallas_call(
        _mod_kernel,
        grid=(depth, nb),
        in_specs=[
            pl.BlockSpec((nc, d), lambda l, j: (0, 0)),
            pl.BlockSpec((None, d, MOD_COL_BLOCK), lambda l, j: (l, 0, j)),
            pl.BlockSpec((None, 1, MOD_COL_BLOCK), lambda l, j: (l, 0, j)),
        ],
        out_specs=pl.BlockSpec((None, nc, MOD_COL_BLOCK), lambda l, j: (l, 0, j)),
        out_shape=jax.ShapeDtypeStruct((depth, nc, n6), F32),
        compiler_params=_cparams(("arbitrary", "arbitrary")),
        name="adaln_modulation",
    )(c_all, w_mod, b_mod.reshape(depth, 1, n6))


MLA_IN_EXT = 896
MLA_KPE_COL = 640
MLA_KPE_SWAP_COL = 768
MLA_HEAD_PAD = LANES


def _mla_proj_kernel(x_ref, g_ref, sh_ref, sc_ref, c32_ref, s32_ref, c128_ref, s128_ref,
                     win_ref, qn_ref, kvn_ref, wuq_ref, wuqs_ref, wuk_ref, epe_ref, wuvt_ref,
                     ckv_ref, kpe_ref, q_ref, k_ref, vt_ref):
    h = _modulate(x_ref[...], g_ref[...], sh_ref[...], sc_ref[...]).astype(BF16)
    z = _dot(h, win_ref[...])
    cq = _rms(z[:, :MLA_Q_LORA], qn_ref[...]).astype(BF16)
    ckv = _rms(z[:, MLA_Q_LORA:MLA_Q_LORA + MLA_KV_LORA], kvn_ref[...])
    kpe = (z[:, MLA_KPE_COL:MLA_KPE_COL + MLA_ROPE] * c32_ref[...]
           + z[:, MLA_KPE_SWAP_COL:MLA_KPE_SWAP_COL + MLA_ROPE] * s32_ref[...])
    ckv_ref[...] = ckv
    kpe_ref[...] = kpe
    ckv_b = ckv.astype(BF16)
    q = _dot(cq, wuq_ref[...])
    qs = _dot(cq, wuqs_ref[...])
    cos = c128_ref[...]
    sin = s128_ref[...]
    for hd in range(MLA_HEADS):
        sl = slice(hd * MLA_HEAD_PAD, (hd + 1) * MLA_HEAD_PAD)
        q_ref[:, sl] = (q[:, sl] * cos + qs[:, sl] * sin).astype(BF16)
    k_ref[...] = (_dot(ckv_b, wuk_ref[...]) + _dot(kpe.astype(BF16), epe_ref[...])).astype(BF16)
    vt = _dot_nt(wuvt_ref[...], ckv_b).astype(BF16)
    if len(vt_ref.shape) == 2:
        vt_ref[...] = vt
    else:
        chunks, _, width = vt_ref.shape
        for c in range(chunks):
            vt_ref[c] = vt[:, c * width:(c + 1) * width]


def _mla_project(rows, x, ln_g, mod, tabs, w, chunk):
    n = rows.n
    hp = MLA_HEADS * MLA_HEAD_PAD
    full = _Rows.full
    nv = MLA_HEADS * MLA_V
    if rows.tb >= chunk:
        vt_spec = pl.BlockSpec((rows.tb // chunk, nv, chunk), lambda i: (i, 0, 0))
    else:
        per_chunk = chunk // rows.tb
        vt_spec = pl.BlockSpec((None, nv, rows.tb), lambda i: (i // per_chunk, 0, i % per_chunk))
    return pl.pallas_call(
        _mla_proj_kernel,
        grid=(rows.grid,),
        in_specs=[
            rows.rows(D_MODEL), full((1, D_MODEL)), rows.mod(0), rows.mod(1),
            rows.pos(MLA_ROPE), rows.pos(MLA_ROPE), rows.pos(MLA_HEAD_PAD), rows.pos(MLA_HEAD_PAD),
            full((D_MODEL, MLA_IN_EXT)), full((1, MLA_Q_LORA)), full((1, MLA_KV_LORA)),
            full((MLA_Q_LORA, hp)), full((MLA_Q_LORA, hp)), full((MLA_KV_LORA, hp)),
            full((MLA_ROPE, hp)), full((nv, MLA_KV_LORA)),
        ],
        out_specs=[rows.rows(MLA_KV_LORA), rows.rows(MLA_ROPE), rows.rows(hp), rows.rows(hp),
                   vt_spec],
        out_shape=[
            jax.ShapeDtypeStruct((n, MLA_KV_LORA), F32),
            jax.ShapeDtypeStruct((n, MLA_ROPE), F32),
            jax.ShapeDtypeStruct((n, hp), BF16),
            jax.ShapeDtypeStruct((n, hp), BF16),
            jax.ShapeDtypeStruct((n // chunk, nv, chunk), BF16),
        ],
        compiler_params=_cparams(("arbitrary",)),
        name="mla_project",
    )(x, ln_g, mod, mod, tabs["c32"], tabs["s32"], tabs["c128"], tabs["s128"],
      w["w_in"], w["q_norm"], w["kv_norm"], w["w_uq"], w["w_uq_swap"], w["w_uk"], w["e_pe"], w["w_uvt"])


def _mla_attn_kernel(q_ref, k_ref, vt_ref, o_ref, *, tq, tk):
    assert tq == tk
    qi = pl.program_id(2)
    causal = lax.broadcasted_iota(I32, (tk, tq), 0) <= lax.broadcasted_iota(I32, (tk, tq), 1)
    v_row = lax.broadcasted_iota(I32, (LANES, tk), 0)
    qs = [q_ref[:, hh * LANES:(hh + 1) * LANES] for hh in range(2)]
    v_keep = [(v_row // MLA_V) == hh for hh in range(2)]

    def step(j, carry, masked):
        start = pl.multiple_of(j * tk, tk)
        vt = vt_ref[j]
        new = []
        for hh in range(2):
            m, l, acc = carry[hh]
            kb = k_ref[pl.ds(start, tk), hh * LANES:(hh + 1) * LANES]
            s = _dot_nt(kb, qs[hh]) * MLA_SCALE
            if masked:
                s = jnp.where(causal, s, NEG_INF)
            m_new = jnp.maximum(m, jnp.max(s, axis=0, keepdims=True))
            alpha = jnp.exp(m - m_new)
            p = jnp.exp(s - m_new)
            l = alpha * l + jnp.sum(p, axis=0, keepdims=True)
            acc = alpha * acc + _dot(jnp.where(v_keep[hh], vt, jnp.zeros_like(vt)), p.astype(BF16))
            new.append((m_new, l, acc))
        return tuple(new)

    head0 = (jnp.full((1, tq), NEG_INF, F32), jnp.zeros((1, tq), F32), jnp.zeros((LANES, tq), F32))
    carry = lax.fori_loop(0, qi, lambda j, c: step(j, c, False), (head0, head0))
    (_, l0, acc0), (_, l1, acc1) = step(qi, carry, True)
    o_ref[...] = (acc0 / l0 + acc1 / l1).T.astype(BF16)


def _mla_attention(q, k, vt, batch, seq):
    tq = tk = vt.shape[2]
    nq = seq // tq
    q3 = q.reshape(batch, seq, -1)
    k3 = k.reshape(batch, seq, -1)
    out = pl.pallas_call(
        functools.partial(_mla_attn_kernel, tq=tq, tk=tk),
        grid=(batch, MLA_HEADS // 2, nq),
        in_specs=[
            pl.BlockSpec((None, tq, 2 * LANES), lambda b, hp, i: (b, i, hp)),
            pl.BlockSpec((None, seq, 2 * LANES), lambda b, hp, i: (b, 0, hp)),
            pl.BlockSpec((seq // tk, LANES, tk), lambda b, hp, i: (b, hp, 0)),
        ],
        out_specs=pl.BlockSpec((None, tq, LANES), lambda b, hp, i: (b, i, hp)),
        out_shape=jax.ShapeDtypeStruct((batch, seq, MLA_HEADS * MLA_V), BF16),
        compiler_params=_cparams(("arbitrary", "arbitrary", "arbitrary")),
        name="mla_prompt_attention",
    )(q3, k3, vt)
    return out.reshape(batch * seq, MLA_HEADS * MLA_V)


def _qlat_kernel(q_ref, wk_ref, o_ref):
    for hd in range(MLA_HEADS):
        qh = q_ref[:, hd * MLA_HEAD_PAD:(hd + 1) * MLA_HEAD_PAD]
        o_ref[:, hd * MLA_KV_LORA:(hd + 1) * MLA_KV_LORA] = _dot(qh, wk_ref[hd]).astype(BF16)


def _mla_absorb_queries(q, w_ukt):
    n = q.shape[0]
    return pl.pallas_call(
        _qlat_kernel,
        grid=(1,),
        in_specs=[_Rows.full(q.shape), _Rows.full(w_ukt.shape)],
        out_specs=_Rows.full((n, MLA_HEADS * MLA_KV_LORA)),
        out_shape=jax.ShapeDtypeStruct((n, MLA_HEADS * MLA_KV_LORA), BF16),
        compiler_params=_cparams(("arbitrary",)),
        name="mla_absorb_queries",
    )(q, w_ukt)


def _paged_attn_kernel(pt_ref, qlat_ref, qpe_ref, nckv_ref, nkpe_ref, *rest, n_steps, dec_seq):
    pg = PAGES_PER_STEP
    ckv_refs = rest[:pg]
    kpe_refs = rest[pg:2 * pg]
    o_ref = rest[2 * pg]
    m_sc, l_sc, acc_sc, kc_sc, kp_sc = rest[2 * pg + 1:]
    g = pl.program_id(1)
    rows = qlat_ref.shape[0]

    @pl.when(g == 0)
    def _():
        m_sc[...] = jnp.full(m_sc.shape, NEG_INF, F32)
        l_sc[...] = jnp.zeros(l_sc.shape, F32)
        acc_sc[...] = jnp.zeros(acc_sc.shape, F32)

    qlat = qlat_ref[...]
    qpe = qpe_ref[...]

    def column(row_vec):
        return jnp.broadcast_to(row_vec, (LANES, rows)).T[:, :1]

    def update(s_t, vals):
        m = m_sc[...]
        m_new = jnp.maximum(m, jnp.max(s_t, axis=0, keepdims=True))
        alpha = jnp.exp(m - m_new)
        p_t = jnp.exp(s_t - m_new)
        l_sc[...] = alpha * l_sc[...] + jnp.sum(p_t, axis=0, keepdims=True)
        acc_sc[...] = column(alpha) * acc_sc[...] + _dot(p_t.T.astype(BF16), vals)
        m_sc[...] = m_new

    def scores_t(kc, kp):
        return (_dot_nt(kc, qlat) + _dot_nt(kp, qpe)) * MLA_SCALE

    for grp in range(pg // PAGE_GROUP):
        for i in range(PAGE_GROUP):
            sl = slice(i * PAGE_SIZE, (i + 1) * PAGE_SIZE)
            kc_sc[sl, :] = ckv_refs[grp * PAGE_GROUP + i][...].astype(BF16)
            kp_sc[sl, :] = kpe_refs[grp * PAGE_GROUP + i][...].T.astype(BF16)
        kc = kc_sc[...]
        update(scores_t(kc, kp_sc[...]), kc)

    @pl.when(g == n_steps - 1)
    def _():
        pad = PAGE_SIZE - dec_seq
        ck = jnp.concatenate([nckv_ref[...], jnp.zeros((pad, MLA_KV_LORA), F32)], axis=0).astype(BF16)
        kp = jnp.concatenate([nkpe_ref[...], jnp.zeros((pad, MLA_ROPE), F32)], axis=0).astype(BF16)
        s_t = scores_t(ck, kp)
        k_t = lax.broadcasted_iota(I32, s_t.shape, 0)
        q_t = lax.broadcasted_iota(I32, s_t.shape, 1) // MLA_HEADS
        update(jnp.where(k_t <= q_t, s_t, NEG_INF), ck)
        o_ref[...] = acc_sc[...] / column(l_sc[...])


def _mla_paged_attention(qlat, qpe, new_ckv, new_kpe, cache_ckv, cache_kpe, page_table, layer):
    db, rows, _ = qlat.shape
    dec_seq = new_ckv.shape[1]
    n_pages = page_table.shape[1]
    pg = PAGES_PER_STEP
    assert n_pages % pg == 0
    n_steps = n_pages // pg

    def page_spec(shape, i):
        return pl.BlockSpec((None, None) + shape, lambda b, g, pt: (layer, pt[b, g * pg + i], 0, 0))

    cache_kpe_t = jnp.swapaxes(cache_kpe, 2, 3)
    in_specs = [
        pl.BlockSpec((None, rows, MLA_KV_LORA), lambda b, g, pt: (b, 0, 0)),
        pl.BlockSpec((None, rows, MLA_ROPE), lambda b, g, pt: (b, 0, 0)),
        pl.BlockSpec((None, dec_seq, MLA_KV_LORA), lambda b, g, pt: (b, 0, 0)),
        pl.BlockSpec((None, dec_seq, MLA_ROPE), lambda b, g, pt: (b, 0, 0)),
    ]
    in_specs += [page_spec((PAGE_SIZE, MLA_KV_LORA), i) for i in range(pg)]
    in_specs += [page_spec((MLA_ROPE, PAGE_SIZE), i) for i in range(pg)]
    grid_spec = pltpu.PrefetchScalarGridSpec(
        num_scalar_prefetch=1,
        grid=(db, n_steps),
        in_specs=in_specs,
        out_specs=pl.BlockSpec((None, rows, MLA_KV_LORA), lambda b, g, pt: (b, 0, 0)),
        scratch_shapes=[pltpu.VMEM((1, rows), F32), pltpu.VMEM((1, rows), F32),
                        pltpu.VMEM((rows, MLA_KV_LORA), F32),
                        pltpu.VMEM((PAGE_GROUP * PAGE_SIZE, MLA_KV_LORA), BF16),
                        pltpu.VMEM((PAGE_GROUP * PAGE_SIZE, MLA_ROPE), BF16)],
    )
    return pl.pallas_call(
        functools.partial(_paged_attn_kernel, n_steps=n_steps, dec_seq=dec_seq),
        grid_spec=grid_spec,
        out_shape=jax.ShapeDtypeStruct((db, rows, MLA_KV_LORA), F32),
        compiler_params=_cparams(("arbitrary", "arbitrary")),
        name="mla_paged_attention",
    )(page_table, qlat, qpe, new_ckv, new_kpe, *([cache_ckv] * pg), *([cache_kpe_t] * pg))


def _sample_out_kernel(olat_ref, wuv_ref, wo_ref, x_ref, gate_ref, o_ref):
    col_head = lax.broadcasted_iota(I32, (1, MLA_HEADS * MLA_V), 1) // MLA_V
    wuv = wuv_ref[...]
    o = jnp.zeros((olat_ref.shape[0], MLA_HEADS * MLA_V), F32)
    for hd in range(MLA_HEADS):
        lat = olat_ref[:, hd * MLA_KV_LORA:(hd + 1) * MLA_KV_LORA].astype(BF16)
        o = jnp.where(col_head == hd, _dot(lat, wuv), o)
    o_ref[...] = x_ref[...] + gate_ref[...] * _dot(o.astype(BF16), wo_ref[...])


def _mla_sample_out(rows, olat, w_uv, w_o, x, mod):
    return pl.pallas_call(
        _sample_out_kernel,
        grid=(rows.grid,),
        in_specs=[rows.rows(olat.shape[1]), _Rows.full(w_uv.shape), _Rows.full(w_o.shape),
                  rows.rows(D_MODEL), rows.mod(2)],
        out_specs=rows.rows(D_MODEL),
        out_shape=jax.ShapeDtypeStruct((rows.n, D_MODEL), F32),
        compiler_params=_cparams(("arbitrary",)),
        name="mla_sample_out",
    )(olat, w_uv, w_o, x, mod)


def _out_proj_kernel(o_ref, wo_ref, x_ref, gate_ref, y_ref):
    y_ref[...] = x_ref[...] + gate_ref[...] * _dot(o_ref[...], wo_ref[...])


def _out_proj(rows, o, w_o, x, mod):
    return pl.pallas_call(
        _out_proj_kernel,
        grid=(rows.grid,),
        in_specs=[rows.rows(o.shape[1]), _Rows.full(w_o.shape), rows.rows(D_MODEL), rows.mod(2)],
        out_specs=rows.rows(D_MODEL),
        out_shape=jax.ShapeDtypeStruct((rows.n, D_MODEL), F32),
        compiler_params=_cparams(("arbitrary",)),
        name="attn_out_proj",
    )(o, w_o, x, mod)


SWA_NQ = SWA_HEADS * SWA_HEAD_DIM
SWA_NK = SWA_KV_HEADS * SWA_HEAD_DIM
SWA_QKV = SWA_NQ + 2 * SWA_NK
SWA_EXT = SWA_QKV + SWA_NQ + SWA_NK


def _swa_proj_kernel(x_ref, g_ref, sh_ref, sc_ref, cos_ref, sin_ref, w_ref, b_ref,
                     q_ref, kb_ref, vb_ref, k_ref, v_ref):
    h = _modulate(x_ref[...], g_ref[...], sh_ref[...], sc_ref[...]).astype(BF16)
    z = _dot(h, w_ref[...]) + b_ref[...]
    cos = cos_ref[...]
    sin = sin_ref[...]
    for t in range((SWA_NQ + SWA_NK) // LANES):
        sl = slice(t * LANES, (t + 1) * LANES)
        sw = slice(SWA_QKV + t * LANES, SWA_QKV + (t + 1) * LANES)
        r = z[:, sl] * cos + z[:, sw] * sin
        if t < SWA_NQ // LANES:
            q_ref[:, sl] = (r * SWA_SCALE).astype(BF16)
        else:
            ks = slice(t * LANES - SWA_NQ, (t + 1) * LANES - SWA_NQ)
            k_ref[:, ks] = r
            kb_ref[:, ks] = r.astype(BF16)
    v = z[:, SWA_NQ + SWA_NK:SWA_QKV]
    v_ref[...] = v
    vb_ref[...] = v.astype(BF16)


def _swa_project(rows, x, ln_g, mod, tabs, w):
    n = rows.n
    full = _Rows.full
    return pl.pallas_call(
        _swa_proj_kernel,
        grid=(rows.grid,),
        in_specs=[rows.rows(D_MODEL), full((1, D_MODEL)), rows.mod(0), rows.mod(1),
                  rows.pos(LANES), rows.pos(LANES), full((D_MODEL, SWA_EXT)), full((1, SWA_EXT))],
        out_specs=[rows.rows(SWA_NQ), rows.rows(SWA_NK), rows.rows(SWA_NK), rows.rows(SWA_NK),
                   rows.rows(SWA_NK)],
        out_shape=[
            jax.ShapeDtypeStruct((n, SWA_NQ), BF16),
            jax.ShapeDtypeStruct((n, SWA_NK), BF16),
            jax.ShapeDtypeStruct((n, SWA_NK), BF16),
            jax.ShapeDtypeStruct((n, SWA_NK), F32),
            jax.ShapeDtypeStruct((n, SWA_NK), F32),
        ],
        compiler_params=_cparams(("arbitrary",)),
        name="swa_project",
    )(x, ln_g, mod, mod, tabs["cos"], tabs["sin"], w["w_qkv"], w["b_qkv"])


def _swa_core(q_all, k_all, v_all, mask, sinks_ref, o_ref):
    lane = lax.broadcasted_iota(I32, (1, LANES), 1)
    lo = lane < SWA_HEAD_DIM
    zero = jnp.zeros((), BF16)
    for kh in range(SWA_KV_HEADS):
        tile = kh // 2
        k_t = k_all[:, tile * LANES:(tile + 1) * LANES]
        v_t = v_all[:, tile * LANES:(tile + 1) * LANES]
        k_r = pltpu.roll(k_t, SWA_HEAD_DIM, 1)
        v_r = pltpu.roll(v_t, SWA_HEAD_DIM, 1)
        for pair in range(SWA_GROUP // 2):
            q_t = q_all[:, (kh * 2 + pair) * LANES:(kh * 2 + pair + 1) * LANES]
            out = None
            for half in range(2):
                keep = lo if half == 0 else jnp.logical_not(lo)
                kx = k_t if (kh % 2) == half else k_r
                vx = v_t if (kh % 2) == half else v_r
                s = _dot_nt(jnp.where(keep, q_t, zero), kx)
                s = jnp.where(mask, s, NEG_INF)
                sink = sinks_ref[kh * SWA_GROUP + 2 * pair + half]
                m = jnp.maximum(jnp.max(s, axis=-1, keepdims=True), sink)
                p = jnp.exp(s - m)
                l = jnp.sum(p, axis=-1, keepdims=True) + jnp.exp(sink - m)
                o = _dot((p / l).astype(BF16), jnp.where(keep, vx, zero))
                out = o if out is None else out + o
            o_ref[:, (kh * 2 + pair) * LANES:(kh * 2 + pair + 1) * LANES] = out.astype(o_ref.dtype)


def _swa_prompt_kernel(sinks_ref, q_ref, kc_ref, kp_ref, vc_ref, vp_ref, o_ref, *, tq):
    i = pl.program_id(1)
    k_all = jnp.concatenate([kp_ref[...], kc_ref[...]], axis=0)
    v_all = jnp.concatenate([vp_ref[...], vc_ref[...]], axis=0)
    tk = tq + WINDOW
    q_pos = i * tq + lax.broadcasted_iota(I32, (tq, tk), 0)
    k_pos = i * tq - WINDOW + lax.broadcasted_iota(I32, (tq, tk), 1)
    mask = (k_pos >= 0) & (k_pos <= q_pos) & (q_pos - k_pos < WINDOW)
    _swa_core(q_ref[...], k_all, v_all, mask, sinks_ref, o_ref)


def _swa_prompt_attention(q, kb, vb, sinks, batch, seq):
    tq = SWA_Q_BLOCK
    r = tq // WINDOW
    q3 = q.reshape(batch, seq, SWA_NQ)
    k3 = kb.reshape(batch, seq, SWA_NK)
    v3 = vb.reshape(batch, seq, SWA_NK)
    cur = pl.BlockSpec((None, tq, SWA_NK), lambda b, i: (b, i, 0))
    prev = pl.BlockSpec((None, WINDOW, SWA_NK), lambda b, i: (b, jnp.maximum(i * r - 1, 0), 0))
    out = pl.pallas_call(
        functools.partial(_swa_prompt_kernel, tq=tq),
        grid=(batch, seq // tq),
        in_specs=[pl.BlockSpec(memory_space=pltpu.SMEM),
                  pl.BlockSpec((None, tq, SWA_NQ), lambda b, i: (b, i, 0)), cur, prev, cur, prev],
        out_specs=pl.BlockSpec((None, tq, SWA_NQ), lambda b, i: (b, i, 0)),
        out_shape=jax.ShapeDtypeStruct((batch, seq, SWA_NQ), BF16),
        compiler_params=_cparams(("arbitrary", "arbitrary")),
        name="swa_prompt_attention",
    )(sinks, q3, k3, k3, v3, v3)
    return out.reshape(batch * seq, SWA_NQ)


def _swa_sample_kernel(sinks_ref, q_ref, kbuf_ref, vbuf_ref, kn_ref, vn_ref, o_ref, o_sc, *, n_buf, dec_seq):
    tk = n_buf + BF16_ROWS
    t = lax.broadcasted_iota(I32, (BF16_ROWS, tk), 0)
    c = lax.broadcasted_iota(I32, (BF16_ROWS, tk), 1)
    in_buf = c < n_buf
    mask = ((in_buf & ((n_buf + t - c) < WINDOW))
            | (jnp.logical_not(in_buf) & ((c - n_buf) <= jnp.minimum(t, dec_seq - 1))))
    pad = jnp.zeros((BF16_ROWS - dec_seq, SWA_NK), F32)
    for j in range(q_ref.shape[0]):
        k_all = jnp.concatenate([kbuf_ref[j], kn_ref[j], pad], axis=0).astype(BF16)
        v_all = jnp.concatenate([vbuf_ref[j], vn_ref[j], pad], axis=0).astype(BF16)
        q = jnp.concatenate([q_ref[j].astype(F32), jnp.zeros((BF16_ROWS - dec_seq, SWA_NQ), F32)], axis=0)
        _swa_core(q.astype(BF16), k_all, v_all, mask, sinks_ref, o_sc.at[j])
        o_ref[j] = o_sc[j, :dec_seq, :].astype(o_ref.dtype)


def _swa_sample_attention(q, kn, vn, buf_k, buf_v, sinks, db, dec_seq):
    n_buf = buf_k.shape[1]
    per_step = SWA_SAMPLE_SEQS if db % SWA_SAMPLE_SEQS == 0 else 1
    spec = lambda rws, w: pl.BlockSpec((per_step, rws, w), lambda b: (b, 0, 0))
    out = pl.pallas_call(
        functools.partial(_swa_sample_kernel, n_buf=n_buf, dec_seq=dec_seq),
        grid=(db // per_step,),
        in_specs=[pl.BlockSpec(memory_space=pltpu.SMEM), spec(dec_seq, SWA_NQ), spec(n_buf, SWA_NK),
                  spec(n_buf, SWA_NK), spec(dec_seq, SWA_NK), spec(dec_seq, SWA_NK)],
        out_specs=spec(dec_seq, SWA_NQ),
        out_shape=jax.ShapeDtypeStruct((db, dec_seq, SWA_NQ), BF16),
        scratch_shapes=[pltpu.VMEM((per_step, BF16_ROWS, SWA_NQ), F32)],
        compiler_params=_cparams(("arbitrary",)),
        name="swa_sample_attention",
    )(sinks, q.reshape(db, dec_seq, SWA_NQ), buf_k, buf_v,
      kn.reshape(db, dec_seq, SWA_NK), vn.reshape(db, dec_seq, SWA_NK))
    return out.reshape(db * dec_seq, SWA_NQ)


def _peer_candidate_tables(width_tokens):
    k = PEER_TOPK
    flat, valid = [], []
    for i, width in _CANDIDATE_RUNS:
        if i is None:
            flat += [k * k] * width
            valid += [False] * width
        elif i >= 8:
            flat += [ii * k for ii in range(8, 16)]
            valid += [True] * 8
        else:
            assert width == k // (i + 1)
            flat += [i * k + j for j in range(width)]
            valid += [True] * width
    assert sum(valid) == 50 and len(flat) % SUBLANES == 0
    flat = np.broadcast_to(np.asarray(flat, np.float32)[:, None], (len(flat), width_tokens))
    pen = np.where(np.asarray(valid), 0.0, -np.inf).astype(np.float32)
    return np.ascontiguousarray(flat), np.ascontiguousarray(np.broadcast_to(pen[:, None], flat.shape))


_CANDIDATE_RUNS = ((0, 16), (1, 8), (2, 5), (4, 3), (3, 4), (5, 2), (6, 2), (8, 8), (7, 2), (None, 6))


def _extract_top(s, row_id, k, payload=None):
    big = jnp.float32(2 ** 30)
    vals, ids = [], []
    for _ in range(k):
        m = jnp.max(s, axis=0, keepdims=True)
        sel = jnp.min(jnp.where(s == m, row_id, big), axis=0, keepdims=True)
        hit = row_id == sel
        if payload is not None:
            ids.append(jnp.max(jnp.where(hit, payload, -1.0), axis=0, keepdims=True))
        else:
            ids.append(sel)
        vals.append(m)
        s = jnp.where(hit, -jnp.inf, s)
    return jnp.concatenate(vals, axis=0), jnp.concatenate(ids, axis=0)


def _extract_top_distinct(s, k, payload):
    vals, picked = [], []
    for _ in range(k):
        m = jnp.max(s, axis=0, keepdims=True)
        hit = s == m
        picked.append(jnp.max(jnp.where(hit, payload, -1.0), axis=0, keepdims=True))
        vals.append(m)
        s = jnp.where(hit, -jnp.inf, s)
    return jnp.concatenate(vals, axis=0), jnp.concatenate(picked, axis=0)


def _route_queries(x_ref, g_ref, sh_ref, sc_ref, wq_ref, qs_sc):
    h = _modulate(x_ref[...], g_ref[...], sh_ref[...], sc_ref[...]).astype(BF16)
    q = _dot(h, wq_ref[...])
    for grp in range(2 * PEER_HEADS):
        qs_sc[grp] = q[:, grp * PEER_HALF:(grp + 1) * PEER_HALF].astype(BF16)
    return h


def _sorting_network(n):
    pairs = []
    p = 1
    while p < n:
        k = p
        while k >= 1:
            for j in range(k % p, n - k, 2 * k):
                for i in range(min(k, n - j - k)):
                    if (i + j) // (2 * p) == (i + j + k) // (2 * p):
                        pairs.append((i + j, i + j + k))
            k //= 2
        p *= 2
    return pairs


def _compare_exchange(vals, ids, a, b):
    swap = vals[b] > vals[a]
    hi, lo = jnp.maximum(vals[a], vals[b]), jnp.minimum(vals[a], vals[b])
    ids[a], ids[b] = jnp.where(swap, ids[b], ids[a]), jnp.where(swap, ids[a], ids[b])
    vals[a], vals[b] = hi, lo


def _top16_by_sorting(s, tb):
    k = PEER_TOPK
    n_tiles = PEER_N_KEYS // SUBLANES
    assert n_tiles == k
    sub = lax.broadcasted_iota(I32, (SUBLANES, tb), 0).astype(F32)
    vals = [s[SUBLANES * v:SUBLANES * (v + 1)] for v in range(n_tiles)]
    ids = [sub + float(SUBLANES * v) for v in range(n_tiles)]
    for a, b in _sorting_network(n_tiles):
        _compare_exchange(vals, ids, a, b)
    shift = SUBLANES // 2
    while shift >= 1:
        other_v = [pltpu.roll(x, shift, 0) for x in vals]
        other_i = [pltpu.roll(x, shift, 0) for x in ids]
        for d in range(k):
            take = other_v[k - 1 - d] > vals[d]
            ids[d] = jnp.where(take, other_i[k - 1 - d], ids[d])
            vals[d] = jnp.maximum(vals[d], other_v[k - 1 - d])
        dist = k // 2
        while dist >= 1:
            for d in range(k):
                if d & dist == 0:
                    _compare_exchange(vals, ids, d, d + dist)
            dist //= 2
        shift //= 2
    top_v = jnp.concatenate([x[:1] for x in vals], axis=0)
    top_i = jnp.concatenate([x[:1] for x in ids], axis=0)
    return top_v, top_i


def _route_first_level(groups, sk_ref, qs_sc, sv_sc, si_sc, tb):
    k = PEER_TOPK
    tied = None
    for grp in groups:
        s = _dot_nt(sk_ref[grp], qs_sc[grp])
        vals, ids = _top16_by_sorting(s, tb)
        sv_sc[grp] = vals
        si_sc[grp] = ids
        n_ge = jnp.sum(jnp.where(s >= vals[k - 1:k], 1.0, 0.0), axis=0, keepdims=True)
        t = jnp.logical_or(jnp.any(n_ge != float(k)), jnp.any(vals[:k - 1] == vals[1:]))
        tied = t if tied is None else jnp.logical_or(tied, t)

    @pl.when(tied)
    def _():
        key_id = lax.broadcasted_iota(I32, (PEER_N_KEYS, tb), 0).astype(F32)
        for grp in groups:
            exact_v, exact_i = _extract_top(_dot_nt(sk_ref[grp], qs_sc[grp]), key_id, k)
            sv_sc[grp] = exact_v
            si_sc[grp] = exact_i


def _route_second_level(sv_sc, si_sc, flat_ref, pen_ref, pe_sc, pg_sc):
    k = PEER_TOPK

    def candidates(hd):
        sa, sb = sv_sc[2 * hd], sv_sc[2 * hd + 1]
        ia, ib = si_sc[2 * hd], si_sc[2 * hd + 1]
        cs, ce = [], []
        for i, width in _CANDIDATE_RUNS:
            if i is None:
                cs.append(jnp.zeros((width,) + sa.shape[1:], F32))
                ce.append(jnp.zeros((width,) + sa.shape[1:], F32))
            elif i >= 8:
                cs.append(sa[8:] + sb[:1])
                ce.append(ia[8:] * PEER_N_KEYS + ib[:1])
            else:
                cs.append(sa[i:i + 1] + sb[:width])
                ce.append(ia[i:i + 1] * PEER_N_KEYS + ib[:width])
        return jnp.concatenate(cs, axis=0) + pen_ref[...], jnp.concatenate(ce, axis=0)

    def store(hd, best, experts):
        ex = jnp.exp(best - best[:1])
        pick = pl.ds(pl.multiple_of(hd * k, k), k)
        pe_sc[pick, :] = experts
        pg_sc[pick, :] = ex / jnp.sum(ex, axis=0, keepdims=True)

    def trip(t, carry):
        heads = [t * ROUTE_UNROLL + u for u in range(ROUTE_UNROLL)]
        tied = None
        for hd in heads:
            cand, cand_e = candidates(hd)
            best, experts = _extract_top_distinct(cand, k, cand_e)
            store(hd, best, experts)
            n_ge = jnp.sum(jnp.where(cand >= best[k - 1:k], 1.0, 0.0), axis=0, keepdims=True)
            tt = jnp.logical_or(jnp.any(n_ge != float(k)), jnp.any(best[:k - 1] == best[1:]))
            tied = tt if tied is None else jnp.logical_or(tied, tt)

        @pl.when(tied)
        def _():
            for hd in heads:
                cand, cand_e = candidates(hd)
                store(hd, *_extract_top(cand, flat_ref[...], k, payload=cand_e))

        return carry

    lax.fori_loop(0, PEER_HEADS // ROUTE_UNROLL, trip, 0)


def _peer_route_kernel(x_ref, g_ref, sh_ref, sc_ref, wq_ref, sk_ref, flat_ref, pen_ref,
                       h_ref, e_ref, gate_ref, qs_sc, sv_sc, si_sc, pe_sc, pg_sc, *, tb):
    h = _route_queries(x_ref, g_ref, sh_ref, sc_ref, wq_ref, qs_sc)
    h_ref[...] = h

    def first_level(trip, carry):
        groups = [trip * ROUTE_UNROLL + u for u in range(ROUTE_UNROLL)]
        _route_first_level(groups, sk_ref, qs_sc, sv_sc, si_sc, tb)
        return carry

    lax.fori_loop(0, 2 * PEER_HEADS // ROUTE_UNROLL, first_level, 0)
    _route_second_level(sv_sc, si_sc, flat_ref, pen_ref, pe_sc, pg_sc)
    e_ref[...] = pe_sc[...].T.astype(I32)
    gate_ref[...] = pg_sc[...].T


def _peer_route(rows, x, ln_g, mod, w):
    n, tb = rows.n, rows.tb
    flat, pen = _peer_candidate_tables(tb)
    full = _Rows.full
    n_groups = 2 * PEER_HEADS
    return pl.pallas_call(
        functools.partial(_peer_route_kernel, tb=tb),
        grid=(rows.grid,),
        in_specs=[rows.rows(D_MODEL), full((1, D_MODEL)), rows.mod(3), rows.mod(4),
                  full(w["w_q"].shape), full(w["sub_keys"].shape), full(flat.shape), full(pen.shape)],
        out_specs=[rows.rows(D_MODEL), rows.rows(PEER_PICKS), rows.rows(PEER_PICKS)],
        out_shape=[jax.ShapeDtypeStruct((n, D_MODEL), BF16),
                   jax.ShapeDtypeStruct((n, PEER_PICKS), I32),
                   jax.ShapeDtypeStruct((n, PEER_PICKS), F32)],
        scratch_shapes=[pltpu.VMEM((n_groups, tb, PEER_HALF), BF16),
                        pltpu.VMEM((n_groups, PEER_TOPK, tb), F32),
                        pltpu.VMEM((n_groups, PEER_TOPK, tb), F32),
                        pltpu.VMEM((PEER_PICKS, tb), F32),
                        pltpu.VMEM((PEER_PICKS, tb), F32)],
        compiler_params=_cparams(("arbitrary",)),
        name="peer_route",
    )(x, ln_g, mod, mod, w["w_q"], w["sub_keys"], jnp.asarray(flat), jnp.asarray(pen))


def _build_gate_tiles(e_ref, gate_ref, w_sc, tb):
    pitch = GATE_TILE_PITCH
    row = lax.broadcasted_iota(I32, (PEER_N_KEYS, PEER_PICKS), 0)
    half_keys = PEER_N_KEYS // 2
    key_a = jnp.where(row < half_keys, 2 * row, 2 * (row - half_keys) + 1)

    zero_tile = jnp.zeros((PEER_N_KEYS, PEER_PICKS), BF16)

    def operands(t):
        e = e_ref[pl.ds(t, 1), :]
        g = gate_ref[pl.ds(t, 1), :].astype(BF16).astype(F32)
        one_a = jnp.where(key_a == (e >> 7), 1.0, 0.0).astype(BF16)
        g_b = jnp.where(row == (e & (PEER_N_KEYS - 1)), g, 0.0).astype(BF16)
        return one_a, g_b

    def token_pair(i, carry):
        t = 2 * i
        a0, b0 = operands(t)
        a1, b1 = operands(t + 1)
        lhs = jnp.concatenate([a0, a1], axis=1)
        rhs = jnp.concatenate([jnp.concatenate([b0, zero_tile], axis=1),
                               jnp.concatenate([zero_tile, b1], axis=1)], axis=0)
        w = _dot_nt(lhs, rhs)
        even = lax.bitcast_convert_type(w[:half_keys].astype(BF16).astype(F32), jnp.uint32)
        odd = lax.bitcast_convert_type(w[half_keys:].astype(BF16).astype(F32), jnp.uint32)
        word = odd | (even >> 16)
        for j in range(2):
            start = pl.multiple_of((t + j) * pitch, SUBLANES)
            w_sc[pl.ds(start, half_keys), :] = word[:, j * PEER_N_KEYS:(j + 1) * PEER_N_KEYS]
        return carry

    lax.fori_loop(0, tb // 2, token_pair, 0, unroll=TOKEN_UNROLL // 2)


def _expert_chunk(h, u_ref, v_ref, w_sc, p_sc, acc_sc, c, tb):
    for pair in range(EXPERT_CHUNK_A // 2):
        z = _dot_nt(h, u_ref[pair * 2 * PEER_N_KEYS:(pair + 1) * 2 * PEER_N_KEYS, :])
        word = w_sc[pl.ds(c * (EXPERT_CHUNK_A // 2) + pair, tb, stride=GATE_TILE_PITCH), :]
        w_pair = (lax.bitcast_convert_type(word << 16, F32),
                  lax.bitcast_convert_type(word & jnp.uint32(0xFFFF0000), F32))
        for half in range(2):
            al = 2 * pair + half
            sl = slice(al * PEER_N_KEYS, (al + 1) * PEER_N_KEYS)
            p_sc[:, sl] = (w_pair[half] * _gelu(z[:, half * PEER_N_KEYS:(half + 1) * PEER_N_KEYS])).astype(BF16)
    acc_sc[...] += _dot(p_sc[...], v_ref[...])


def _peer_expert_kernel(h_ref, e_ref, gate_ref, u_ref, v_ref, x_ref, g2_ref, o_ref,
                        w_sc, p_sc, acc_sc, *, tb, n_chunks):
    c = pl.program_id(1)

    @pl.when(c == 0)
    def _():
        acc_sc[...] = jnp.zeros(acc_sc.shape, F32)
        _build_gate_tiles(e_ref, gate_ref, w_sc, tb)

    _expert_chunk(h_ref[...], u_ref, v_ref, w_sc, p_sc, acc_sc, c, tb)

    @pl.when(c == n_chunks - 1)
    def _():
        o_ref[...] = x_ref[...] + g2_ref[...] * acc_sc[...]


def _peer_experts(rows, h, experts, gates, u, v, layer, x, mod):
    n, tb = rows.n, rows.tb
    ce = EXPERT_CHUNK_A * PEER_N_KEYS
    n_chunks = v.shape[1] // ce
    bps = rows.bps
    if bps is None:
        gate_spec = pl.BlockSpec((tb, D_MODEL), lambda i, c: (i, 5))
    else:
        gate_spec = pl.BlockSpec((None, 1, D_MODEL), lambda i, c: (i // bps, 0, 5))
    tok = lambda w: pl.BlockSpec((tb, w), lambda i, c: (i, 0))
    return pl.pallas_call(
        functools.partial(_peer_expert_kernel, tb=tb, n_chunks=n_chunks),
        grid=(rows.grid, n_chunks),
        in_specs=[tok(D_MODEL), tok(PEER_PICKS), tok(PEER_PICKS),
                  pl.BlockSpec((None, ce, D_MODEL), lambda i, c: (layer, c, 0)),
                  pl.BlockSpec((None, ce, D_MODEL), lambda i, c: (layer, c, 0)),
                  tok(D_MODEL), gate_spec],
        out_specs=tok(D_MODEL),
        out_shape=jax.ShapeDtypeStruct((n, D_MODEL), F32),
        scratch_shapes=[pltpu.VMEM((tb * GATE_TILE_PITCH, LANES), jnp.uint32),
                        pltpu.VMEM((tb, ce), BF16),
                        pltpu.VMEM((tb, D_MODEL), F32)],
        compiler_params=_cparams(("arbitrary", "arbitrary")),
        name="peer_experts",
    )(h, experts, gates, u, v, x, mod)


def _final_norm_kernel(x_ref, g_ref, o_ref):
    o_ref[...] = _rms(x_ref[...], g_ref[...])


def _final_norm(rows, x, g):
    return pl.pallas_call(
        _final_norm_kernel,
        grid=(rows.grid,),
        in_specs=[rows.rows(D_MODEL), _Rows.full((1, D_MODEL))],
        out_specs=rows.rows(D_MODEL),
        out_shape=jax.ShapeDtypeStruct((rows.n, D_MODEL), F32),
        compiler_params=_cparams(("arbitrary",)),
        name="final_norm",
    )(x, g)


def _pair_swap(w, half):
    return jnp.concatenate([-w[..., half:2 * half], w[..., :half]], axis=-1)


def _rope_cos_sin(pos, rot_dim):
    half = rot_dim // 2
    inv_freq = ROPE_THETA ** (-jnp.arange(half, dtype=F32) * 2.0 / rot_dim)
    ang = pos.astype(F32)[:, None] * inv_freq[None, :]
    return jnp.cos(ang), jnp.sin(ang)


def _mla_tables(pos):
    cos, sin = _rope_cos_sin(pos, MLA_ROPE)
    n = pos.shape[0]
    ones = jnp.ones((n, MLA_NOPE), F32)
    zeros_n = jnp.zeros((n, MLA_NOPE), F32)
    pad = jnp.zeros((n, MLA_HEAD_PAD - MLA_NOPE - MLA_ROPE), F32)
    return {
        "c32": jnp.concatenate([cos, cos], axis=1),
        "s32": jnp.concatenate([sin, sin], axis=1),
        "c128": jnp.concatenate([ones, cos, cos, pad], axis=1),
        "s128": jnp.concatenate([zeros_n, sin, sin, pad], axis=1),
    }


def _swa_tables(pos):
    cos, sin = _rope_cos_sin(pos, SWA_ROT)
    n = pos.shape[0]
    rest = SWA_HEAD_DIM - SWA_ROT
    c = jnp.concatenate([cos, cos, jnp.ones((n, rest), F32)], axis=1)
    s = jnp.concatenate([sin, sin, jnp.zeros((n, rest), F32)], axis=1)
    reps = LANES // SWA_HEAD_DIM
    return {"cos": jnp.tile(c, (1, reps)), "sin": jnp.tile(s, (1, reps))}


def _mla_weights(w_in, q_norm, kv_norm, w_uq, w_uk, w_uv, w_o):
    d = w_in.shape[0]
    half = MLA_ROPE // 2
    kpe_cols = w_in[:, MLA_Q_LORA + MLA_KV_LORA:]
    w_in_ext = jnp.zeros((d, MLA_IN_EXT), F32)
    w_in_ext = w_in_ext.at[:, :MLA_KPE_COL + MLA_ROPE].set(w_in)
    w_in_ext = w_in_ext.at[:, MLA_KPE_SWAP_COL:MLA_KPE_SWAP_COL + MLA_ROPE].set(_pair_swap(kpe_cols, half))
    uq = w_uq.reshape(MLA_Q_LORA, MLA_HEADS, MLA_NOPE + MLA_ROPE)
    uq_pad = jnp.zeros((MLA_Q_LORA, MLA_HEADS, MLA_HEAD_PAD), F32).at[:, :, :MLA_NOPE + MLA_ROPE].set(uq)
    uq_swap = jnp.zeros((MLA_Q_LORA, MLA_HEADS, MLA_HEAD_PAD), F32)
    uq_swap = uq_swap.at[:, :, MLA_NOPE:MLA_NOPE + MLA_ROPE].set(_pair_swap(uq[:, :, MLA_NOPE:], half))
    uk_pad = jnp.zeros((MLA_KV_LORA, MLA_HEADS, MLA_HEAD_PAD), F32).at[:, :, :MLA_NOPE].set(w_uk)
    e_pe = jnp.zeros((MLA_ROPE, MLA_HEADS, MLA_HEAD_PAD), F32)
    e_pe = e_pe.at[jnp.arange(MLA_ROPE), :, MLA_NOPE + jnp.arange(MLA_ROPE)].set(1.0)
    ukt = jnp.zeros((MLA_HEADS, MLA_HEAD_PAD, MLA_KV_LORA), F32)
    ukt = ukt.at[:, :MLA_NOPE, :].set(jnp.transpose(w_uk, (1, 2, 0)))
    hp = MLA_HEADS * MLA_HEAD_PAD
    return {
        "w_in": w_in_ext.astype(BF16),
        "q_norm": q_norm.reshape(1, -1), "kv_norm": kv_norm.reshape(1, -1),
        "w_uq": uq_pad.reshape(MLA_Q_LORA, hp).astype(BF16),
        "w_uq_swap": uq_swap.reshape(MLA_Q_LORA, hp).astype(BF16),
        "w_uk": uk_pad.reshape(MLA_KV_LORA, hp).astype(BF16),
        "e_pe": e_pe.reshape(MLA_ROPE, hp).astype(BF16),
        "w_uvt": w_uv.reshape(MLA_KV_LORA, MLA_HEADS * MLA_V).T.astype(BF16),
        "w_ukt": ukt.astype(BF16),
        "w_uv": w_uv.reshape(MLA_KV_LORA, MLA_HEADS * MLA_V).astype(BF16),
        "w_o": w_o.astype(BF16),
    }


def _swa_weights(w_qkv, b_qkv, w_o):
    half = SWA_ROT // 2
    n_rot_heads = SWA_HEADS + SWA_KV_HEADS

    def swap_cols(w):
        lead = w.shape[:-1]
        wh = w[..., :SWA_NQ + SWA_NK].reshape(lead + (n_rot_heads, SWA_HEAD_DIM))
        sw = jnp.concatenate([_pair_swap(wh[..., :SWA_ROT], half),
                              jnp.zeros(lead + (n_rot_heads, SWA_HEAD_DIM - SWA_ROT), F32)], axis=-1)
        return sw.reshape(lead + (SWA_NQ + SWA_NK,))

    w_ext = jnp.concatenate([w_qkv, swap_cols(w_qkv)], axis=-1)
    b_ext = jnp.concatenate([b_qkv, swap_cols(b_qkv)], axis=-1)
    return {"w_qkv": w_ext.astype(BF16), "b_qkv": b_ext.reshape(1, -1), "w_o": w_o.astype(BF16)}


def kernel(x_prompt, x_sample, cache_mla_ckv, cache_mla_kpe, cache_swa_k, cache_swa_v, page_table,
           c_prompt, c_sample, ln1_g, ln2_g, w_mod, b_mod,
           mla_w_in, mla_q_norm, mla_kv_norm, mla_w_uq, mla_w_uk, mla_w_uv, mla_w_o,
           swa_w_qkv, swa_b_qkv, swa_sinks, swa_w_o,
           peer_w_q, peer_sub_keys, peer_u, peer_v, final_g):
    batch, seq, d = x_prompt.shape
    db, dec_seq, _ = x_sample.shape
    depth = w_mod.shape[0]
    past_len = page_table.shape[1] * PAGE_SIZE
    n_p, n_s = batch * seq, db * dec_seq

    rows_p = _Rows(n_p, min(ROW_BLOCK, seq), seq_len=seq)
    rows_s = _Rows(n_s, n_s)
    route_p = _Rows(n_p, ROUTE_BLOCK, seq_len=seq)
    route_s = _Rows(n_s, min(ROUTE_BLOCK, n_s))
    expert_p = _Rows(n_p, EXPERT_ROW_BLOCK, seq_len=seq)
    expert_s = _Rows(n_s, min(EXPERT_ROW_BLOCK, n_s))

    pos_p = jnp.arange(seq)
    pos_s = jnp.tile(past_len + jnp.arange(dec_seq), db)
    mla_tab_p, mla_tab_s = _mla_tables(pos_p), _mla_tables(pos_s)
    swa_tab_p, swa_tab_s = _swa_tables(pos_p), _swa_tables(pos_s)

    m_all = _modulation_all(jnp.concatenate([c_prompt, c_sample], axis=0), w_mod, b_mod)
    u_b = peer_u.astype(BF16)
    v_b = peer_v.astype(BF16)

    x_p = x_prompt.reshape(n_p, d)
    x_s = x_sample.reshape(n_s, d)
    ckv_p, kpe_p, ckv_s, kpe_s = [], [], [], []
    swk_p, swv_p, swk_s, swv_s = [], [], [], []
    n_buf = cache_swa_k.shape[2]

    for i in range(depth):
        j = i // 2
        mod_p = m_all[i, :batch].reshape(batch, 1, 6 * d)
        mod_s = jnp.repeat(m_all[i, batch:], dec_seq, axis=0)
        g1 = ln1_g[i].reshape(1, d)
        g2 = ln2_g[i].reshape(1, d)
        if i % 2 == 0:
            w = _mla_weights(mla_w_in[j], mla_q_norm[j], mla_kv_norm[j], mla_w_uq[j], mla_w_uk[j],
                             mla_w_uv[j], mla_w_o[j])
            ckv, kpe, q, k, vt = _mla_project(rows_p, x_p, g1, mod_p, mla_tab_p, w, min(MLA_ATTN_BLOCK, seq))
            o = _mla_attention(q, k, vt, batch, seq)
            x_p = _out_proj(rows_p, o, w["w_o"], x_p, mod_p)
            ckv_p.append(ckv.reshape(batch, seq, MLA_KV_LORA))
            kpe_p.append(kpe.reshape(batch, seq, MLA_ROPE))

            ckv, kpe, q, _, _ = _mla_project(rows_s, x_s, g1, mod_s, mla_tab_s, w, rows_s.tb)
            qlat = _mla_absorb_queries(q, w["w_ukt"]).reshape(db, dec_seq * MLA_HEADS, MLA_KV_LORA)
            qpe = q.reshape(n_s, MLA_HEADS, MLA_HEAD_PAD)[:, :, MLA_NOPE:MLA_NOPE + MLA_ROPE]
            qpe = qpe.reshape(db, dec_seq * MLA_HEADS, MLA_ROPE)
            ckv3 = ckv.reshape(db, dec_seq, MLA_KV_LORA)
            kpe3 = kpe.reshape(db, dec_seq, MLA_ROPE)
            olat = _mla_paged_attention(qlat, qpe, ckv3, kpe3, cache_mla_ckv, cache_mla_kpe, page_table, j)
            olat = olat.reshape(n_s, MLA_HEADS * MLA_KV_LORA)
            x_s = _mla_sample_out(rows_s, olat, w["w_uv"], w["w_o"], x_s, mod_s)
            ckv_s.append(ckv3)
            kpe_s.append(kpe3)
        else:
            w = _swa_weights(swa_w_qkv[j], swa_b_qkv[j], swa_w_o[j])
            q, kb, vb, k, v = _swa_project(rows_p, x_p, g1, mod_p, swa_tab_p, w)
            o = _swa_prompt_attention(q, kb, vb, swa_sinks[j], batch, seq)
            x_p = _out_proj(rows_p, o, w["w_o"], x_p, mod_p)
            nb = min(WINDOW, seq)
            swk_p.append(k.reshape(batch, seq, SWA_KV_HEADS, SWA_HEAD_DIM)[:, seq - nb:])
            swv_p.append(v.reshape(batch, seq, SWA_KV_HEADS, SWA_HEAD_DIM)[:, seq - nb:])

            q, kb, vb, k, v = _swa_project(rows_s, x_s, g1, mod_s, swa_tab_s, w)
            buf_k = cache_swa_k[j].reshape(db, n_buf, SWA_NK)
            buf_v = cache_swa_v[j].reshape(db, n_buf, SWA_NK)
            o = _swa_sample_attention(q, k, v, buf_k, buf_v, swa_sinks[j], db, dec_seq)
            x_s = _out_proj(rows_s, o, w["w_o"], x_s, mod_s)
            k_all = jnp.concatenate([buf_k, k.reshape(db, dec_seq, SWA_NK)], axis=1)[:, -n_buf:]
            v_all = jnp.concatenate([buf_v, v.reshape(db, dec_seq, SWA_NK)], axis=1)[:, -n_buf:]
            swk_s.append(k_all.reshape(db, n_buf, SWA_KV_HEADS, SWA_HEAD_DIM))
            swv_s.append(v_all.reshape(db, n_buf, SWA_KV_HEADS, SWA_HEAD_DIM))

        pw = {"w_q": peer_w_q[i].astype(BF16),
              "sub_keys": peer_sub_keys[i].reshape(2 * PEER_HEADS, PEER_N_KEYS, PEER_HALF).astype(BF16)}
        h, experts, gates = _peer_route(route_p, x_p, g2, mod_p, pw)
        x_p = _peer_experts(expert_p, h, experts, gates, u_b, v_b, i, x_p, mod_p)
        h, experts, gates = _peer_route(route_s, x_s, g2, mod_s, pw)
        x_s = _peer_experts(expert_s, h, experts, gates, u_b, v_b, i, x_s, mod_s)

    fg = final_g.reshape(1, d)
    y_p = _final_norm(rows_p, x_p, fg).reshape(batch, seq, d)
    y_s = _final_norm(rows_s, x_s, fg).reshape(db, dec_seq, d)
    return (y_p, y_s,
            jnp.stack(ckv_p), jnp.stack(kpe_p), jnp.stack(ckv_s), jnp.stack(kpe_s),
            jnp.stack(swk_p), jnp.stack(swv_p), jnp.stack(swk_s), jnp.stack(swv_s))
```

```python
import functools

import numpy as np
import jax
import jax.numpy as jnp
from jax import lax
from jax.experimental import pallas as pl
from jax.experimental.pallas import tpu as pltpu

F32 = jnp.float32
BF16 = jnp.bfloat16
I32 = jnp.int32

D_MODEL = 1024
PAGE_SIZE = 128
ROPE_THETA = 500000.0
NORM_EPS = 1e-6
NEG_INF = -1e30

MLA_HEADS = 16
MLA_Q_LORA = 384
MLA_KV_LORA = 256
MLA_NOPE = 64
MLA_ROPE = 32
MLA_V = 64
MLA_SCALE = (MLA_NOPE + MLA_ROPE) ** -0.5

SWA_HEADS = 16
SWA_KV_HEADS = 4
SWA_GROUP = SWA_HEADS // SWA_KV_HEADS
SWA_HEAD_DIM = D_MODEL // SWA_HEADS
SWA_ROT = SWA_HEAD_DIM // 4
SWA_SCALE = SWA_HEAD_DIM ** -0.5
WINDOW = 128

PEER_HEADS = 8
PEER_N_KEYS = 128
PEER_TOPK = 16
PEER_HALF = 128
PEER_PICKS = PEER_HEADS * PEER_TOPK

LANES = 128
SUBLANES = 8
BF16_ROWS = 16
VMEM_LIMIT_BYTES = 56 * 1024 * 1024

ROW_BLOCK = 512
EXPERT_ROW_BLOCK = 512
TOKEN_UNROLL = 128
ROUTE_BLOCK = 512
ROUTE_UNROLL = 4
MLA_ATTN_BLOCK = 1024
SWA_SAMPLE_SEQS = 4
SWA_Q_BLOCK = 256
PAGES_PER_STEP = 64
PAGE_GROUP = 64
EXPERT_CHUNK_A = 16
GATE_TILE_PITCH = 72
MOD_COL_BLOCK = 1536


def _cparams(sem):
    return pltpu.CompilerParams(dimension_semantics=sem, vmem_limit_bytes=VMEM_LIMIT_BYTES)


def _dot(a, b):
    return jnp.dot(a, b, preferred_element_type=F32)


def _dot_nt(a, b):
    return lax.dot_general(a, b, (((1,), (1,)), ((), ())), preferred_element_type=F32)


def _rms(x, g):
    return x * lax.rsqrt(jnp.mean(x * x, axis=-1, keepdims=True) + NORM_EPS) * g


def _gelu(x):
    return 0.5 * x * (1.0 + lax.erf(x * np.float32(2.0 ** -0.5)))


def _modulate(x, g, shift, scale):
    return _rms(x, g) * (1.0 + scale) + shift


class _Rows:
    def __init__(self, n, tb, seq_len=None):
        assert n % tb == 0
        self.n, self.tb, self.grid = n, tb, n // tb
        self.bps = None
        if seq_len is not None:
            assert seq_len % tb == 0
            self.bps = seq_len // tb

    def rows(self, width):
        return pl.BlockSpec((self.tb, width), lambda i: (i, 0))

    def mod(self, k):
        if self.bps is None:
            return pl.BlockSpec((self.tb, D_MODEL), lambda i: (i, k))
        bps = self.bps
        return pl.BlockSpec((None, 1, D_MODEL), lambda i: (i // bps, 0, k))

    def pos(self, width):
        if self.bps is None:
            return pl.BlockSpec((self.tb, width), lambda i: (i, 0))
        bps = self.bps
        return pl.BlockSpec((self.tb, width), lambda i: (i % bps, 0))

    @staticmethod
    def full(shape):
        nd = len(shape)
        return pl.BlockSpec(shape, lambda i: (0,) * nd)


def _mod_kernel(c_ref, w_ref, b_ref, o_ref):
    c = c_ref[...]
    a = (c * jax.nn.sigmoid(c)).astype(BF16)
    o_ref[...] = _dot(a, w_ref[...].astype(BF16)) + b_ref[...]


def _modulation_all(c_all, w_mod, b_mod):
    depth, d, n6 = w_mod.shape
    nc = c_all.shape[0]
    nb = n6 // MOD_COL_BLOCK
    return pl.pallas_call(
        _mod_kernel,
        grid=(depth, nb),
        in_specs=[
            pl.BlockSpec((nc, d), lambda l, j: (0, 0)),
            pl.BlockSpec((None, d, MOD_COL_BLOCK), lambda l, j: (l, 0, j)),
            pl.BlockSpec((None, 1, MOD_COL_BLOCK), lambda l, j: (l, 0, j)),
        ],
        out_specs=pl.BlockSpec((None, nc, MOD_COL_BLOCK), lambda l, j: (l, 0, j)),
        out_shape=jax.ShapeDtypeStruct((depth, nc, n6), F32),
        compiler_params=_cparams(("arbitrary", "arbitrary")),
        name="adaln_modulation",
    )(c_all, w_mod, b_mod.reshape(depth, 1, n6))


MLA_IN_EXT = 896
MLA_KPE_COL = 640
MLA_KPE_SWAP_COL = 768
MLA_HEAD_PAD = LANES


def _mla_proj_kernel(x_ref, g_ref, sh_ref, sc_ref, c32_ref, s32_ref, c128_ref, s128_ref,
                     win_ref, qn_ref, kvn_ref, wuq_ref, wuqs_ref, wuk_ref, epe_ref, wuvt_ref,
                     ckv_ref, kpe_ref, q_ref, k_ref, vt_ref):
    h = _modulate(x_ref[...], g_ref[...], sh_ref[...], sc_ref[...]).astype(BF16)
    z = _dot(h, win_ref[...])
    cq = _rms(z[:, :MLA_Q_LORA], qn_ref[...]).astype(BF16)
    ckv = _rms(z[:, MLA_Q_LORA:MLA_Q_LORA + MLA_KV_LORA], kvn_ref[...])
    kpe = (z[:, MLA_KPE_COL:MLA_KPE_COL + MLA_ROPE] * c32_ref[...]
           + z[:, MLA_KPE_SWAP_COL:MLA_KPE_SWAP_COL + MLA_ROPE] * s32_ref[...])
    ckv_ref[...] = ckv
    kpe_ref[...] = kpe
    ckv_b = ckv.astype(BF16)
    q = _dot(cq, wuq_ref[...])
    qs = _dot(cq, wuqs_ref[...])
    cos = c128_ref[...]
    sin = s128_ref[...]
    for hd in range(MLA_HEADS):
        sl = slice(hd * MLA_HEAD_PAD, (hd + 1) * MLA_HEAD_PAD)
        q_ref[:, sl] = (q[:, sl] * cos + qs[:, sl] * sin).astype(BF16)
    k_ref[...] = (_dot(ckv_b, wuk_ref[...]) + _dot(kpe.astype(BF16), epe_ref[...])).astype(BF16)
    vt = _dot_nt(wuvt_ref[...], ckv_b).astype(BF16)
    if len(vt_ref.shape) == 2:
        vt_ref[...] = vt
    else:
        chunks, _, width = vt_ref.shape
        for c in range(chunks):
            vt_ref[c] = vt[:, c * width:(c + 1) * width]


def _mla_project(rows, x, ln_g, mod, tabs, w, chunk):
    n = rows.n
    hp = MLA_HEADS * MLA_HEAD_PAD
    full = _Rows.full
    nv = MLA_HEADS * MLA_V
    if rows.tb >= chunk:
        vt_spec = pl.BlockSpec((rows.tb // chunk, nv, chunk), lambda i: (i, 0, 0))
    else:
        per_chunk = chunk // rows.tb
        vt_spec = pl.BlockSpec((None, nv, rows.tb), lambda i: (i // per_chunk, 0, i % per_chunk))
    return pl.pallas_call(
        _mla_proj_kernel,
        grid=(rows.grid,),
        in_specs=[
            rows.rows(D_MODEL), full((1, D_MODEL)), rows.mod(0), rows.mod(1),
            rows.pos(MLA_ROPE), rows.pos(MLA_ROPE), rows.pos(MLA_HEAD_PAD), rows.pos(MLA_HEAD_PAD),
            full((D_MODEL, MLA_IN_EXT)), full((1, MLA_Q_LORA)), full((1, MLA_KV_LORA)),
            full((MLA_Q_LORA, hp)), full((MLA_Q_LORA, hp)), full((MLA_KV_LORA, hp)),
            full((MLA_ROPE, hp)), full((nv, MLA_KV_LORA)),
        ],
        out_specs=[rows.rows(MLA_KV_LORA), rows.rows(MLA_ROPE), rows.rows(hp), rows.rows(hp),
                   vt_spec],
        out_shape=[
            jax.ShapeDtypeStruct((n, MLA_KV_LORA), F32),
            jax.ShapeDtypeStruct((n, MLA_ROPE), F32),
            jax.ShapeDtypeStruct((n, hp), BF16),
            jax.ShapeDtypeStruct((n, hp), BF16),
            jax.ShapeDtypeStruct((n // chunk, nv, chunk), BF16),
        ],
        compiler_params=_cparams(("arbitrary",)),
        name="mla_project",
    )(x, ln_g, mod, mod, tabs["c32"], tabs["s32"], tabs["c128"], tabs["s128"],
      w["w_in"], w["q_norm"], w["kv_norm"], w["w_uq"], w["w_uq_swap"], w["w_uk"], w["e_pe"], w["w_uvt"])


def _mla_attn_kernel(q_ref, k_ref, vt_ref, o_ref, *, tq, tk):
    assert tq == tk
    qi = pl.program_id(2)
    causal = lax.broadcasted_iota(I32, (tk, tq), 0) <= lax.broadcasted_iota(I32, (tk, tq), 1)
    v_row = lax.broadcasted_iota(I32, (LANES, tk), 0)
    qs = [q_ref[:, hh * LANES:(hh + 1) * LANES] for hh in range(2)]
    v_keep = [(v_row // MLA_V) == hh for hh in range(2)]

    def step(j, carry, masked):
        start = pl.multiple_of(j * tk, tk)
        vt = vt_ref[j]
        new = []
        for hh in range(2):
            m, l, acc = carry[hh]
            kb = k_ref[pl.ds(start, tk), hh * LANES:(hh + 1) * LANES]
            s = _dot_nt(kb, qs[hh]) * MLA_SCALE
            if masked:
                s = jnp.where(causal, s, NEG_INF)
            m_new = jnp.maximum(m, jnp.max(s, axis=0, keepdims=True))
            alpha = jnp.exp(m - m_new)
            p = jnp.exp(s - m_new)
            l = alpha * l + jnp.sum(p, axis=0, keepdims=True)
            acc = alpha * acc + _dot(jnp.where(v_keep[hh], vt, jnp.zeros_like(vt)), p.astype(BF16))
            new.append((m_new, l, acc))
        return tuple(new)

    head0 = (jnp.full((1, tq), NEG_INF, F32), jnp.zeros((1, tq), F32), jnp.zeros((LANES, tq), F32))
    carry = lax.fori_loop(0, qi, lambda j, c: step(j, c, False), (head0, head0))
    (_, l0, acc0), (_, l1, acc1) = step(qi, carry, True)
    o_ref[...] = (acc0 / l0 + acc1 / l1).T.astype(BF16)


def _mla_attention(q, k, vt, batch, seq):
    tq = tk = vt.shape[2]
    nq = seq // tq
    q3 = q.reshape(batch, seq, -1)
    k3 = k.reshape(batch, seq, -1)
    out = pl.pallas_call(
        functools.partial(_mla_attn_kernel, tq=tq, tk=tk),
        grid=(batch, MLA_HEADS // 2, nq),
        in_specs=[
            pl.BlockSpec((None, tq, 2 * LANES), lambda b, hp, i: (b, i, hp)),
            pl.BlockSpec((None, seq, 2 * LANES), lambda b, hp, i: (b, 0, hp)),
            pl.BlockSpec((seq // tk, LANES, tk), lambda b, hp, i: (b, hp, 0)),
        ],
        out_specs=pl.BlockSpec((None, tq, LANES), lambda b, hp, i: (b, i, hp)),
        out_shape=jax.ShapeDtypeStruct((batch, seq, MLA_HEADS * MLA_V), BF16),
        compiler_params=_cparams(("arbitrary", "arbitrary", "arbitrary")),
        name="mla_prompt_attention",
    )(q3, k3, vt)
    return out.reshape(batch * seq, MLA_HEADS * MLA_V)


def _qlat_kernel(q_ref, wk_ref, o_ref):
    for hd in range(MLA_HEADS):
        qh = q_ref[:, hd * MLA_HEAD_PAD:(hd + 1) * MLA_HEAD_PAD]
        o_ref[:, hd * MLA_KV_LORA:(hd + 1) * MLA_KV_LORA] = _dot(qh, wk_ref[hd]).astype(BF16)


def _mla_absorb_queries(q, w_ukt):
    n = q.shape[0]
    return pl.pallas_call(
        _qlat_kernel,
        grid=(1,),
        in_specs=[_Rows.full(q.shape), _Rows.full(w_ukt.shape)],
        out_specs=_Rows.full((n, MLA_HEADS * MLA_KV_LORA)),
        out_shape=jax.ShapeDtypeStruct((n, MLA_HEADS * MLA_KV_LORA), BF16),
        compiler_params=_cparams(("arbitrary",)),
        name="mla_absorb_queries",
    )(q, w_ukt)


def _paged_attn_kernel(pt_ref, qlat_ref, qpe_ref, nckv_ref, nkpe_ref, *rest, n_steps, dec_seq):
    pg = PAGES_PER_STEP
    ckv_refs = rest[:pg]
    kpe_refs = rest[pg:2 * pg]
    o_ref = rest[2 * pg]
    m_sc, l_sc, acc_sc, kc_sc, kp_sc = rest[2 * pg + 1:]
    g = pl.program_id(1)
    rows = qlat_ref.shape[0]

    @pl.when(g == 0)
    def _():
        m_sc[...] = jnp.full(m_sc.shape, NEG_INF, F32)
        l_sc[...] = jnp.zeros(l_sc.shape, F32)
        acc_sc[...] = jnp.zeros(acc_sc.shape, F32)

    qlat = qlat_ref[...]
    qpe = qpe_ref[...]

    def column(row_vec):
        return jnp.broadcast_to(row_vec, (LANES, rows)).T[:, :1]

    def update(s_t, vals):
        m = m_sc[...]
        m_new = jnp.maximum(m, jnp.max(s_t, axis=0, keepdims=True))
        alpha = jnp.exp(m - m_new)
        p_t = jnp.exp(s_t - m_new)
        l_sc[...] = alpha * l_sc[...] + jnp.sum(p_t, axis=0, keepdims=True)
        acc_sc[...] = column(alpha) * acc_sc[...] + _dot(p_t.T.astype(BF16), vals)
        m_sc[...] = m_new

    def scores_t(kc, kp):
        return (_dot_nt(kc, qlat) + _dot_nt(kp, qpe)) * MLA_SCALE

    for grp in range(pg // PAGE_GROUP):
        for i in range(PAGE_GROUP):
            sl = slice(i * PAGE_SIZE, (i + 1) * PAGE_SIZE)
            kc_sc[sl, :] = ckv_refs[grp * PAGE_GROUP + i][...].astype(BF16)
            kp_sc[sl, :] = kpe_refs[grp * PAGE_GROUP + i][...].T.astype(BF16)
        kc = kc_sc[...]
        update(scores_t(kc, kp_sc[...]), kc)

    @pl.when(g == n_steps - 1)
    def _():
        pad = PAGE_SIZE - dec_seq
        ck = jnp.concatenate([nckv_ref[...], jnp.zeros((pad, MLA_KV_LORA), F32)], axis=0).astype(BF16)
        kp = jnp.concatenate([nkpe_ref[...], jnp.zeros((pad, MLA_ROPE), F32)], axis=0).astype(BF16)
        s_t = scores_t(ck, kp)
        k_t = lax.broadcasted_iota(I32, s_t.shape, 0)
        q_t = lax.broadcasted_iota(I32, s_t.shape, 1) // MLA_HEADS
        update(jnp.where(k_t <= q_t, s_t, NEG_INF), ck)
        o_ref[...] = acc_sc[...] / column(l_sc[...])


def _mla_paged_attention(qlat, qpe, new_ckv, new_kpe, cache_ckv, cache_kpe, page_table, layer):
    db, rows, _ = qlat.shape
    dec_seq = new_ckv.shape[1]
    n_pages = page_table.shape[1]
    pg = PAGES_PER_STEP
    assert n_pages % pg == 0
    n_steps = n_pages // pg

    def page_spec(shape, i):
        return pl.BlockSpec((None, None) + shape, lambda b, g, pt: (layer, pt[b, g * pg + i], 0, 0))

    cache_kpe_t = jnp.swapaxes(cache_kpe, 2, 3)
    in_specs = [
        pl.BlockSpec((None, rows, MLA_KV_LORA), lambda b, g, pt: (b, 0, 0)),
        pl.BlockSpec((None, rows, MLA_ROPE), lambda b, g, pt: (b, 0, 0)),
        pl.BlockSpec((None, dec_seq, MLA_KV_LORA), lambda b, g, pt: (b, 0, 0)),
        pl.BlockSpec((None, dec_seq, MLA_ROPE), lambda b, g, pt: (b, 0, 0)),
    ]
    in_specs += [page_spec((PAGE_SIZE, MLA_KV_LORA), i) for i in range(pg)]
    in_specs += [page_spec((MLA_ROPE, PAGE_SIZE), i) for i in range(pg)]
    grid_spec = pltpu.PrefetchScalarGridSpec(
        num_scalar_prefetch=1,
        grid=(db, n_steps),
        in_specs=in_specs,
        out_specs=pl.BlockSpec((None, rows, MLA_KV_LORA), lambda b, g, pt: (b, 0, 0)),
        scratch_shapes=[pltpu.VMEM((1, rows), F32), pltpu.VMEM((1, rows), F32),
                        pltpu.VMEM((rows, MLA_KV_LORA), F32),
                        pltpu.VMEM((PAGE_GROUP * PAGE_SIZE, MLA_KV_LORA), BF16),
                        pltpu.VMEM((PAGE_GROUP * PAGE_SIZE, MLA_ROPE), BF16)],
    )
    return pl.pallas_call(
        functools.partial(_paged_attn_kernel, n_steps=n_steps, dec_seq=dec_seq),
        grid_spec=grid_spec,
        out_shape=jax.ShapeDtypeStruct((db, rows, MLA_KV_LORA), F32),
        compiler_params=_cparams(("arbitrary", "arbitrary")),
        name="mla_paged_attention",
    )(page_table, qlat, qpe, new_ckv, new_kpe, *([cache_ckv] * pg), *([cache_kpe_t] * pg))


def _sample_out_kernel(olat_ref, wuv_ref, wo_ref, x_ref, gate_ref, o_ref):
    col_head = lax.broadcasted_iota(I32, (1, MLA_HEADS * MLA_V), 1) // MLA_V
    wuv = wuv_ref[...]
    o = jnp.zeros((olat_ref.shape[0], MLA_HEADS * MLA_V), F32)
    for hd in range(MLA_HEADS):
        lat = olat_ref[:, hd * MLA_KV_LORA:(hd + 1) * MLA_KV_LORA].astype(BF16)
        o = jnp.where(col_head == hd, _dot(lat, wuv), o)
    o_ref[...] = x_ref[...] + gate_ref[...] * _dot(o.astype(BF16), wo_ref[...])


def _mla_sample_out(rows, olat, w_uv, w_o, x, mod):
    return pl.pallas_call(
        _sample_out_kernel,
        grid=(rows.grid,),
        in_specs=[rows.rows(olat.shape[1]), _Rows.full(w_uv.shape), _Rows.full(w_o.shape),
                  rows.rows(D_MODEL), rows.mod(2)],
        out_specs=rows.rows(D_MODEL),
        out_shape=jax.ShapeDtypeStruct((rows.n, D_MODEL), F32),
        compiler_params=_cparams(("arbitrary",)),
        name="mla_sample_out",
    )(olat, w_uv, w_o, x, mod)


def _out_proj_kernel(o_ref, wo_ref, x_ref, gate_ref, y_ref):
    y_ref[...] = x_ref[...] + gate_ref[...] * _dot(o_ref[...], wo_ref[...])


def _out_proj(rows, o, w_o, x, mod):
    return pl.pallas_call(
        _out_proj_kernel,
        grid=(rows.grid,),
        in_specs=[rows.rows(o.shape[1]), _Rows.full(w_o.shape), rows.rows(D_MODEL), rows.mod(2)],
        out_specs=rows.rows(D_MODEL),
        out_shape=jax.ShapeDtypeStruct((rows.n, D_MODEL), F32),
        compiler_params=_cparams(("arbitrary",)),
        name="attn_out_proj",
    )(o, w_o, x, mod)


SWA_NQ = SWA_HEADS * SWA_HEAD_DIM
SWA_NK = SWA_KV_HEADS * SWA_HEAD_DIM
SWA_QKV = SWA_NQ + 2 * SWA_NK
SWA_EXT = SWA_QKV + SWA_NQ + SWA_NK


def _swa_proj_kernel(x_ref, g_ref, sh_ref, sc_ref, cos_ref, sin_ref, w_ref, b_ref,
                     q_ref, kb_ref, vb_ref, k_ref, v_ref):
    h = _modulate(x_ref[...], g_ref[...], sh_ref[...], sc_ref[...]).astype(BF16)
    z = _dot(h, w_ref[...]) + b_ref[...]
    cos = cos_ref[...]
    sin = sin_ref[...]
    for t in range((SWA_NQ + SWA_NK) // LANES):
        sl = slice(t * LANES, (t + 1) * LANES)
        sw = slice(SWA_QKV + t * LANES, SWA_QKV + (t + 1) * LANES)
        r = z[:, sl] * cos + z[:, sw] * sin
        if t < SWA_NQ // LANES:
            q_ref[:, sl] = (r * SWA_SCALE).astype(BF16)
        else:
            ks = slice(t * LANES - SWA_NQ, (t + 1) * LANES - SWA_NQ)
            k_ref[:, ks] = r
            kb_ref[:, ks] = r.astype(BF16)
    v = z[:, SWA_NQ + SWA_NK:SWA_QKV]
    v_ref[...] = v
    vb_ref[...] = v.astype(BF16)


def _swa_project(rows, x, ln_g, mod, tabs, w):
    n = rows.n
    full = _Rows.full
    return pl.pallas_call(
        _swa_proj_kernel,
        grid=(rows.grid,),
        in_specs=[rows.rows(D_MODEL), full((1, D_MODEL)), rows.mod(0), rows.mod(1),
                  rows.pos(LANES), rows.pos(LANES), full((D_MODEL, SWA_EXT)), full((1, SWA_EXT))],
        out_specs=[rows.rows(SWA_NQ), rows.rows(SWA_NK), rows.rows(SWA_NK), rows.rows(SWA_NK),
                   rows.rows(SWA_NK)],
        out_shape=[
            jax.ShapeDtypeStruct((n, SWA_NQ), BF16),
            jax.ShapeDtypeStruct((n, SWA_NK), BF16),
            jax.ShapeDtypeStruct((n, SWA_NK), BF16),
            jax.ShapeDtypeStruct((n, SWA_NK), F32),
            jax.ShapeDtypeStruct((n, SWA_NK), F32),
        ],
        compiler_params=_cparams(("arbitrary",)),
        name="swa_project",
    )(x, ln_g, mod, mod, tabs["cos"], tabs["sin"], w["w_qkv"], w["b_qkv"])


def _swa_core(q_all, k_all, v_all, mask, sinks_ref, o_ref):
    lane = lax.broadcasted_iota(I32, (1, LANES), 1)
    lo = lane < SWA_HEAD_DIM
    zero = jnp.zeros((), BF16)
    for kh in range(SWA_KV_HEADS):
        tile = kh // 2
        k_t = k_all[:, tile * LANES:(tile + 1) * LANES]
        v_t = v_all[:, tile * LANES:(tile + 1) * LANES]
        k_r = pltpu.roll(k_t, SWA_HEAD_DIM, 1)
        v_r = pltpu.roll(v_t, SWA_HEAD_DIM, 1)
        for pair in range(SWA_GROUP // 2):
            q_t = q_all[:, (kh * 2 + pair) * LANES:(kh * 2 + pair + 1) * LANES]
            out = None
            for half in range(2):
                keep = lo if half == 0 else jnp.logical_not(lo)
                kx = k_t if (kh % 2) == half else k_r
                vx = v_t if (kh % 2) == half else v_r
                s = _dot_nt(jnp.where(keep, q_t, zero), kx)
                s = jnp.where(mask, s, NEG_INF)
                sink = sinks_ref[kh * SWA_GROUP + 2 * pair + half]
                m = jnp.maximum(jnp.max(s, axis=-1, keepdims=True), sink)
                p = jnp.exp(s - m)
                l = jnp.sum(p, axis=-1, keepdims=True) + jnp.exp(sink - m)
                o = _dot((p / l).astype(BF16), jnp.where(keep, vx, zero))
                out = o if out is None else out + o
            o_ref[:, (kh * 2 + pair) * LANES:(kh * 2 + pair + 1) * LANES] = out.astype(o_ref.dtype)


def _swa_prompt_kernel(sinks_ref, q_ref, kc_ref, kp_ref, vc_ref, vp_ref, o_ref, *, tq):
    i = pl.program_id(1)
    k_all = jnp.concatenate([kp_ref[...], kc_ref[...]], axis=0)
    v_all = jnp.concatenate([vp_ref[...], vc_ref[...]], axis=0)
    tk = tq + WINDOW
    q_pos = i * tq + lax.broadcasted_iota(I32, (tq, tk), 0)
    k_pos = i * tq - WINDOW + lax.broadcasted_iota(I32, (tq, tk), 1)
    mask = (k_pos >= 0) & (k_pos <= q_pos) & (q_pos - k_pos < WINDOW)
    _swa_core(q_ref[...], k_all, v_all, mask, sinks_ref, o_ref)


def _swa_prompt_attention(q, kb, vb, sinks, batch, seq):
    tq = SWA_Q_BLOCK
    r = tq // WINDOW
    q3 = q.reshape(batch, seq, SWA_NQ)
    k3 = kb.reshape(batch, seq, SWA_NK)
    v3 = vb.reshape(batch, seq, SWA_NK)
    cur = pl.BlockSpec((None, tq, SWA_NK), lambda b, i: (b, i, 0))
    prev = pl.BlockSpec((None, WINDOW, SWA_NK), lambda b, i: (b, jnp.maximum(i * r - 1, 0), 0))
    out = pl.pallas_call(
        functools.partial(_swa_prompt_kernel, tq=tq),
        grid=(batch, seq // tq),
        in_specs=[pl.BlockSpec(memory_space=pltpu.SMEM),
                  pl.BlockSpec((None, tq, SWA_NQ), lambda b, i: (b, i, 0)), cur, prev, cur, prev],
        out_specs=pl.BlockSpec((None, tq, SWA_NQ), lambda b, i: (b, i, 0)),
        out_shape=jax.ShapeDtypeStruct((batch, seq, SWA_NQ), BF16),
        compiler_params=_cparams(("arbitrary", "arbitrary")),
        name="swa_prompt_attention",
    )(sinks, q3, k3, k3, v3, v3)
    return out.reshape(batch * seq, SWA_NQ)


def _swa_sample_kernel(sinks_ref, q_ref, kbuf_ref, vbuf_ref, kn_ref, vn_ref, o_ref, o_sc, *, n_buf, dec_seq):
    tk = n_buf + BF16_ROWS
    t = lax.broadcasted_iota(I32, (BF16_ROWS, tk), 0)
    c = lax.broadcasted_iota(I32, (BF16_ROWS, tk), 1)
    in_buf = c < n_buf
    mask = ((in_buf & ((n_buf + t - c) < WINDOW))
            | (jnp.logical_not(in_buf) & ((c - n_buf) <= jnp.minimum(t, dec_seq - 1))))
    pad = jnp.zeros((BF16_ROWS - dec_seq, SWA_NK), F32)
    for j in range(q_ref.shape[0]):
        k_all = jnp.concatenate([kbuf_ref[j], kn_ref[j], pad], axis=0).astype(BF16)
        v_all = jnp.concatenate([vbuf_ref[j], vn_ref[j], pad], axis=0).astype(BF16)
        q = jnp.concatenate([q_ref[j].astype(F32), jnp.zeros((BF16_ROWS - dec_seq, SWA_NQ), F32)], axis=0)
        _swa_core(q.astype(BF16), k_all, v_all, mask, sinks_ref, o_sc.at[j])
        o_ref[j] = o_sc[j, :dec_seq, :].astype(o_ref.dtype)


def _swa_sample_attention(q, kn, vn, buf_k, buf_v, sinks, db, dec_seq):
    n_buf = buf_k.shape[1]
    per_step = SWA_SAMPLE_SEQS if db % SWA_SAMPLE_SEQS == 0 else 1
    spec = lambda rws, w: pl.BlockSpec((per_step, rws, w), lambda b: (b, 0, 0))
    out = pl.pallas_call(
        functools.partial(_swa_sample_kernel, n_buf=n_buf, dec_seq=dec_seq),
        grid=(db // per_step,),
        in_specs=[pl.BlockSpec(memory_space=pltpu.SMEM), spec(dec_seq, SWA_NQ), spec(n_buf, SWA_NK),
                  spec(n_buf, SWA_NK), spec(dec_seq, SWA_NK), spec(dec_seq, SWA_NK)],
        out_specs=spec(dec_seq, SWA_NQ),
        out_shape=jax.ShapeDtypeStruct((db, dec_seq, SWA_NQ), BF16),
        scratch_shapes=[pltpu.VMEM((per_step, BF16_ROWS, SWA_NQ), F32)],
        compiler_params=_cparams(("arbitrary",)),
        name="swa_sample_attention",
    )(sinks, q.reshape(db, dec_seq, SWA_NQ), buf_k, buf_v,
      kn.reshape(db, dec_seq, SWA_NK), vn.reshape(db, dec_seq, SWA_NK))
    return out.reshape(db * dec_seq, SWA_NQ)


def _peer_candidate_tables(width_tokens):
    k = PEER_TOPK
    flat, valid = [], []
    for i, width in _CANDIDATE_RUNS:
        if i is None:
            flat += [k * k] * width
            valid += [False] * width
        elif i >= 8:
            flat += [ii * k for ii in range(8, 16)]
            valid += [True] * 8
        else:
            assert width == k // (i + 1)
            flat += [i * k + j for j in range(width)]
            valid += [True] * width
    assert sum(valid) == 50 and len(flat) % SUBLANES == 0
    flat = np.broadcast_to(np.asarray(flat, np.float32)[:, None], (len(flat), width_tokens))
    pen = np.where(np.asarray(valid), 0.0, -np.inf).astype(np.float32)
    return np.ascontiguousarray(flat), np.ascontiguousarray(np.broadcast_to(pen[:, None], flat.shape))


_CANDIDATE_RUNS = ((0, 16), (1, 8), (2, 5), (4, 3), (3, 4), (5, 2), (6, 2), (8, 8), (7, 2), (None, 6))


def _extract_top(s, row_id, k, payload=None):
    big = jnp.float32(2 ** 30)
    vals, ids = [], []
    for _ in range(k):
        m = jnp.max(s, axis=0, keepdims=True)
        sel = jnp.min(jnp.where(s == m, row_id, big), axis=0, keepdims=True)
        hit = row_id == sel
        if payload is not None:
            ids.append(jnp.max(jnp.where(hit, payload, -1.0), axis=0, keepdims=True))
        else:
            ids.append(sel)
        vals.append(m)
        s = jnp.where(hit, -jnp.inf, s)
    return jnp.concatenate(vals, axis=0), jnp.concatenate(ids, axis=0)


def _extract_top_distinct(s, k, payload):
    vals, picked = [], []
    for _ in range(k):
        m = jnp.max(s, axis=0, keepdims=True)
        hit = s == m
        picked.append(jnp.max(jnp.where(hit, payload, -1.0), axis=0, keepdims=True))
        vals.append(m)
        s = jnp.where(hit, -jnp.inf, s)
    return jnp.concatenate(vals, axis=0), jnp.concatenate(picked, axis=0)


def _route_queries(x_ref, g_ref, sh_ref, sc_ref, wq_ref, qs_sc):
    h = _modulate(x_ref[...], g_ref[...], sh_ref[...], sc_ref[...]).astype(BF16)
    q = _dot(h, wq_ref[...])
    for grp in range(2 * PEER_HEADS):
        qs_sc[grp] = q[:, grp * PEER_HALF:(grp + 1) * PEER_HALF].astype(BF16)
    return h


def _sorting_network(n):
    pairs = []
    p = 1
    while p < n:
        k = p
        while k >= 1:
            for j in range(k % p, n - k, 2 * k):
                for i in range(min(k, n - j - k)):
                    if (i + j) // (2 * p) == (i + j + k) // (2 * p):
                        pairs.append((i + j, i + j + k))
            k //= 2
        p *= 2
    return pairs


def _compare_exchange(vals, ids, a, b):
    swap = vals[b] > vals[a]
    hi, lo = jnp.maximum(vals[a], vals[b]), jnp.minimum(vals[a], vals[b])
    ids[a], ids[b] = jnp.where(swap, ids[b], ids[a]), jnp.where(swap, ids[a], ids[b])
    vals[a], vals[b] = hi, lo


def _top16_by_sorting(s, tb):
    k = PEER_TOPK
    n_tiles = PEER_N_KEYS // SUBLANES
    assert n_tiles == k
    sub = lax.broadcasted_iota(I32, (SUBLANES, tb), 0).astype(F32)
    vals = [s[SUBLANES * v:SUBLANES * (v + 1)] for v in range(n_tiles)]
    ids = [sub + float(SUBLANES * v) for v in range(n_tiles)]
    for a, b in _sorting_network(n_tiles):
        _compare_exchange(vals, ids, a, b)
    shift = SUBLANES // 2
    while shift >= 1:
        other_v = [pltpu.roll(x, shift, 0) for x in vals]
        other_i = [pltpu.roll(x, shift, 0) for x in ids]
        for d in range(k):
            take = other_v[k - 1 - d] > vals[d]
            ids[d] = jnp.where(take, other_i[k - 1 - d], ids[d])
            vals[d] = jnp.maximum(vals[d], other_v[k - 1 - d])
        dist = k // 2
        while dist >= 1:
            for d in range(k):
                if d & dist == 0:
                    _compare_exchange(vals, ids, d, d + dist)
            dist //= 2
        shift //= 2
    top_v = jnp.concatenate([x[:1] for x in vals], axis=0)
    top_i = jnp.concatenate([x[:1] for x in ids], axis=0)
    return top_v, top_i


def _route_first_level(groups, sk_ref, qs_sc, sv_sc, si_sc, tb):
    k = PEER_TOPK
    tied = None
    for grp in groups:
        s = _dot_nt(sk_ref[grp], qs_sc[grp])
        vals, ids = _top16_by_sorting(s, tb)
        sv_sc[grp] = vals
        si_sc[grp] = ids
        n_ge = jnp.sum(jnp.where(s >= vals[k - 1:k], 1.0, 0.0), axis=0, keepdims=True)
        t = jnp.logical_or(jnp.any(n_ge != float(k)), jnp.any(vals[:k - 1] == vals[1:]))
        tied = t if tied is None else jnp.logical_or(tied, t)

    @pl.when(tied)
    def _():
        key_id = lax.broadcasted_iota(I32, (PEER_N_KEYS, tb), 0).astype(F32)
        for grp in groups:
            exact_v, exact_i = _extract_top(_dot_nt(sk_ref[grp], qs_sc[grp]), key_id, k)
            sv_sc[grp] = exact_v
            si_sc[grp] = exact_i


def _route_second_level(sv_sc, si_sc, flat_ref, pen_ref, pe_sc, pg_sc):
    k = PEER_TOPK

    def candidates(hd):
        sa, sb = sv_sc[2 * hd], sv_sc[2 * hd + 1]
        ia, ib = si_sc[2 * hd], si_sc[2 * hd + 1]
        cs, ce = [], []
        for i, width in _CANDIDATE_RUNS:
            if i is None:
                cs.append(jnp.zeros((width,) + sa.shape[1:], F32))
                ce.append(jnp.zeros((width,) + sa.shape[1:], F32))
            elif i >= 8:
                cs.append(sa[8:] + sb[:1])
                ce.append(ia[8:] * PEER_N_KEYS + ib[:1])
            else:
                cs.append(sa[i:i + 1] + sb[:width])
                ce.append(ia[i:i + 1] * PEER_N_KEYS + ib[:width])
        return jnp.concatenate(cs, axis=0) + pen_ref[...], jnp.concatenate(ce, axis=0)

    def store(hd, best, experts):
        ex = jnp.exp(best - best[:1])
        pick = pl.ds(pl.multiple_of(hd * k, k), k)
        pe_sc[pick, :] = experts
        pg_sc[pick, :] = ex / jnp.sum(ex, axis=0, keepdims=True)

    def trip(t, carry):
        heads = [t * ROUTE_UNROLL + u for u in range(ROUTE_UNROLL)]
        tied = None
        for hd in heads:
            cand, cand_e = candidates(hd)
            best, experts = _extract_top_distinct(cand, k, cand_e)
            store(hd, best, experts)
            n_ge = jnp.sum(jnp.where(cand >= best[k - 1:k], 1.0, 0.0), axis=0, keepdims=True)
            tt = jnp.logical_or(jnp.any(n_ge != float(k)), jnp.any(best[:k - 1] == best[1:]))
            tied = tt if tied is None else jnp.logical_or(tied, tt)

        @pl.when(tied)
        def _():
            for hd in heads:
                cand, cand_e = candidates(hd)
                store(hd, *_extract_top(cand, flat_ref[...], k, payload=cand_e))

        return carry

    lax.fori_loop(0, PEER_HEADS // ROUTE_UNROLL, trip, 0)


def _peer_route_kernel(x_ref, g_ref, sh_ref, sc_ref, wq_ref, sk_ref, flat_ref, pen_ref,
                       h_ref, e_ref, gate_ref, qs_sc, sv_sc, si_sc, pe_sc, pg_sc, *, tb):
    h = _route_queries(x_ref, g_ref, sh_ref, sc_ref, wq_ref, qs_sc)
    h_ref[...] = h

    def first_level(trip, carry):
        groups = [trip * ROUTE_UNROLL + u for u in range(ROUTE_UNROLL)]
        _route_first_level(groups, sk_ref, qs_sc, sv_sc, si_sc, tb)
        return carry

    lax.fori_loop(0, 2 * PEER_HEADS // ROUTE_UNROLL, first_level, 0)
    _route_second_level(sv_sc, si_sc, flat_ref, pen_ref, pe_sc, pg_sc)
    e_ref[...] = pe_sc[...].T.astype(I32)
    gate_ref[...] = pg_sc[...].T


def _peer_route(rows, x, ln_g, mod, w):
    n, tb = rows.n, rows.tb
    flat, pen = _peer_candidate_tables(tb)
    full = _Rows.full
    n_groups = 2 * PEER_HEADS
    return pl.pallas_call(
        functools.partial(_peer_route_kernel, tb=tb),
        grid=(rows.grid,),
        in_specs=[rows.rows(D_MODEL), full((1, D_MODEL)), rows.mod(3), rows.mod(4),
                  full(w["w_q"].shape), full(w["sub_keys"].shape), full(flat.shape), full(pen.shape)],
        out_specs=[rows.rows(D_MODEL), rows.rows(PEER_PICKS), rows.rows(PEER_PICKS)],
        out_shape=[jax.ShapeDtypeStruct((n, D_MODEL), BF16),
                   jax.ShapeDtypeStruct((n, PEER_PICKS), I32),
                   jax.ShapeDtypeStruct((n, PEER_PICKS), F32)],
        scratch_shapes=[pltpu.VMEM((n_groups, tb, PEER_HALF), BF16),
                        pltpu.VMEM((n_groups, PEER_TOPK, tb), F32),
                        pltpu.VMEM((n_groups, PEER_TOPK, tb), F32),
                        pltpu.VMEM((PEER_PICKS, tb), F32),
                        pltpu.VMEM((PEER_PICKS, tb), F32)],
        compiler_params=_cparams(("arbitrary",)),
        name="peer_route",
    )(x, ln_g, mod, mod, w["w_q"], w["sub_keys"], jnp.asarray(flat), jnp.asarray(pen))


def _build_gate_tiles(e_ref, gate_ref, w_sc, tb):
    pitch = GATE_TILE_PITCH
    row = lax.broadcasted_iota(I32, (PEER_N_KEYS, PEER_PICKS), 0)
    half_keys = PEER_N_KEYS // 2
    key_a = jnp.where(row < half_keys, 2 * row, 2 * (row - half_keys) + 1)

    zero_tile = jnp.zeros((PEER_N_KEYS, PEER_PICKS), BF16)

    def operands(t):
        e = e_ref[pl.ds(t, 1), :]
        g = gate_ref[pl.ds(t, 1), :].astype(BF16).astype(F32)
        one_a = jnp.where(key_a == (e >> 7), 1.0, 0.0).astype(BF16)
        g_b = jnp.where(row == (e & (PEER_N_KEYS - 1)), g, 0.0).astype(BF16)
        return one_a, g_b

    def token_pair(i, carry):
        t = 2 * i
        a0, b0 = operands(t)
        a1, b1 = operands(t + 1)
        lhs = jnp.concatenate([a0, a1], axis=1)
        rhs = jnp.concatenate([jnp.concatenate([b0, zero_tile], axis=1),
                               jnp.concatenate([zero_tile, b1], axis=1)], axis=0)
        w = _dot_nt(lhs, rhs)
        even = lax.bitcast_convert_type(w[:half_keys].astype(BF16).astype(F32), jnp.uint32)
        odd = lax.bitcast_convert_type(w[half_keys:].astype(BF16).astype(F32), jnp.uint32)
        word = odd | (even >> 16)
        for j in range(2):
            start = pl.multiple_of((t + j) * pitch, SUBLANES)
            w_sc[pl.ds(start, half_keys), :] = word[:, j * PEER_N_KEYS:(j + 1) * PEER_N_KEYS]
        return carry

    lax.fori_loop(0, tb // 2, token_pair, 0, unroll=TOKEN_UNROLL // 2)


def _expert_chunk(h, u_ref, v_ref, w_sc, p_sc, acc_sc, c, tb):
    for pair in range(EXPERT_CHUNK_A // 2):
        z = _dot_nt(h, u_ref[pair * 2 * PEER_N_KEYS:(pair + 1) * 2 * PEER_N_KEYS, :])
        word = w_sc[pl.ds(c * (EXPERT_CHUNK_A // 2) + pair, tb, stride=GATE_TILE_PITCH), :]
        w_pair = (lax.bitcast_convert_type(word << 16, F32),
                  lax.bitcast_convert_type(word & jnp.uint32(0xFFFF0000), F32))
        for half in range(2):
            al = 2 * pair + half
            sl = slice(al * PEER_N_KEYS, (al + 1) * PEER_N_KEYS)
            p_sc[:, sl] = (w_pair[half] * _gelu(z[:, half * PEER_N_KEYS:(half + 1) * PEER_N_KEYS])).astype(BF16)
    acc_sc[...] += _dot(p_sc[...], v_ref[...])


def _peer_expert_kernel(h_ref, e_ref, gate_ref, u_ref, v_ref, x_ref, g2_ref, o_ref,
                        w_sc, p_sc, acc_sc, *, tb, n_chunks):
    c = pl.program_id(1)

    @pl.when(c == 0)
    def _():
        acc_sc[...] = jnp.zeros(acc_sc.shape, F32)
        _build_gate_tiles(e_ref, gate_ref, w_sc, tb)

    _expert_chunk(h_ref[...], u_ref, v_ref, w_sc, p_sc, acc_sc, c, tb)

    @pl.when(c == n_chunks - 1)
    def _():
        o_ref[...] = x_ref[...] + g2_ref[...] * acc_sc[...]


def _peer_experts(rows, h, experts, gates, u, v, layer, x, mod):
    n, tb = rows.n, rows.tb
    ce = EXPERT_CHUNK_A * PEER_N_KEYS
    n_chunks = v.shape[1] // ce
    bps = rows.bps
    if bps is None:
        gate_spec = pl.BlockSpec((tb, D_MODEL), lambda i, c: (i, 5))
    else:
        gate_spec = pl.BlockSpec((None, 1, D_MODEL), lambda i, c: (i // bps, 0, 5))
    tok = lambda w: pl.BlockSpec((tb, w), lambda i, c: (i, 0))
    return pl.pallas_call(
        functools.partial(_peer_expert_kernel, tb=tb, n_chunks=n_chunks),
        grid=(rows.grid, n_chunks),
        in_specs=[tok(D_MODEL), tok(PEER_PICKS), tok(PEER_PICKS),
                  pl.BlockSpec((None, ce, D_MODEL), lambda i, c: (layer, c, 0)),
                  pl.BlockSpec((None, ce, D_MODEL), lambda i, c: (layer, c, 0)),
                  tok(D_MODEL), gate_spec],
        out_specs=tok(D_MODEL),
        out_shape=jax.ShapeDtypeStruct((n, D_MODEL), F32),
        scratch_shapes=[pltpu.VMEM((tb * GATE_TILE_PITCH, LANES), jnp.uint32),
                        pltpu.VMEM((tb, ce), BF16),
                        pltpu.VMEM((tb, D_MODEL), F32)],
        compiler_params=_cparams(("arbitrary", "arbitrary")),
        name="peer_experts",
    )(h, experts, gates, u, v, x, mod)


def _final_norm_kernel(x_ref, g_ref, o_ref):
    o_ref[...] = _rms(x_ref[...], g_ref[...])


def _final_norm(rows, x, g):
    return pl.pallas_call(
        _final_norm_kernel,
        grid=(rows.grid,),
        in_specs=[rows.rows(D_MODEL), _Rows.full((1, D_MODEL))],
        out_specs=rows.rows(D_MODEL),
        out_shape=jax.ShapeDtypeStruct((rows.n, D_MODEL), F32),
        compiler_params=_cparams(("arbitrary",)),
        name="final_norm",
    )(x, g)


def _pair_swap(w, half):
    return jnp.concatenate([-w[..., half:2 * half], w[..., :half]], axis=-1)


def _rope_cos_sin(pos, rot_dim):
    half = rot_dim // 2
    inv_freq = ROPE_THETA ** (-jnp.arange(half, dtype=F32) * 2.0 / rot_dim)
    ang = pos.astype(F32)[:, None] * inv_freq[None, :]
    return jnp.cos(ang), jnp.sin(ang)


def _mla_tables(pos):
    cos, sin = _rope_cos_sin(pos, MLA_ROPE)
    n = pos.shape[0]
    ones = jnp.ones((n, MLA_NOPE), F32)
    zeros_n = jnp.zeros((n, MLA_NOPE), F32)
    pad = jnp.zeros((n, MLA_HEAD_PAD - MLA_NOPE - MLA_ROPE), F32)
    return {
        "c32": jnp.concatenate([cos, cos], axis=1),
        "s32": jnp.concatenate([sin, sin], axis=1),
        "c128": jnp.concatenate([ones, cos, cos, pad], axis=1),
        "s128": jnp.concatenate([zeros_n, sin, sin, pad], axis=1),
    }


def _swa_tables(pos):
    cos, sin = _rope_cos_sin(pos, SWA_ROT)
    n = pos.shape[0]
    rest = SWA_HEAD_DIM - SWA_ROT
    c = jnp.concatenate([cos, cos, jnp.ones((n, rest), F32)], axis=1)
    s = jnp.concatenate([sin, sin, jnp.zeros((n, rest), F32)], axis=1)
    reps = LANES // SWA_HEAD_DIM
    return {"cos": jnp.tile(c, (1, reps)), "sin": jnp.tile(s, (1, reps))}


def _mla_weights(w_in, q_norm, kv_norm, w_uq, w_uk, w_uv, w_o):
    d = w_in.shape[0]
    half = MLA_ROPE // 2
    kpe_cols = w_in[:, MLA_Q_LORA + MLA_KV_LORA:]
    w_in_ext = jnp.zeros((d, MLA_IN_EXT), F32)
    w_in_ext = w_in_ext.at[:, :MLA_KPE_COL + MLA_ROPE].set(w_in)
    w_in_ext = w_in_ext.at[:, MLA_KPE_SWAP_COL:MLA_KPE_SWAP_COL + MLA_ROPE].set(_pair_swap(kpe_cols, half))
    uq = w_uq.reshape(MLA_Q_LORA, MLA_HEADS, MLA_NOPE + MLA_ROPE)
    uq_pad = jnp.zeros((MLA_Q_LORA, MLA_HEADS, MLA_HEAD_PAD), F32).at[:, :, :MLA_NOPE + MLA_ROPE].set(uq)
    uq_swap = jnp.zeros((MLA_Q_LORA, MLA_HEADS, MLA_HEAD_PAD), F32)
    uq_swap = uq_swap.at[:, :, MLA_NOPE:MLA_NOPE + MLA_ROPE].set(_pair_swap(uq[:, :, MLA_NOPE:], half))
    uk_pad = jnp.zeros((MLA_KV_LORA, MLA_HEADS, MLA_HEAD_PAD), F32).at[:, :, :MLA_NOPE].set(w_uk)
    e_pe = jnp.zeros((MLA_ROPE, MLA_HEADS, MLA_HEAD_PAD), F32)
    e_pe = e_pe.at[jnp.arange(MLA_ROPE), :, MLA_NOPE + jnp.arange(MLA_ROPE)].set(1.0)
    ukt = jnp.zeros((MLA_HEADS, MLA_HEAD_PAD, MLA_KV_LORA), F32)
    ukt = ukt.at[:, :MLA_NOPE, :].set(jnp.transpose(w_uk, (1, 2, 0)))
    hp = MLA_HEADS * MLA_HEAD_PAD
    return {
        "w_in": w_in_ext.astype(BF16),
        "q_norm": q_norm.reshape(1, -1), "kv_norm": kv_norm.reshape(1, -1),
        "w_uq": uq_pad.reshape(MLA_Q_LORA, hp).astype(BF16),
        "w_uq_swap": uq_swap.reshape(MLA_Q_LORA, hp).astype(BF16),
        "w_uk": uk_pad.reshape(MLA_KV_LORA, hp).astype(BF16),
        "e_pe": e_pe.reshape(MLA_ROPE, hp).astype(BF16),
        "w_uvt": w_uv.reshape(MLA_KV_LORA, MLA_HEADS * MLA_V).T.astype(BF16),
        "w_ukt": ukt.astype(BF16),
        "w_uv": w_uv.reshape(MLA_KV_LORA, MLA_HEADS * MLA_V).astype(BF16),
        "w_o": w_o.astype(BF16),
    }


def _swa_weights(w_qkv, b_qkv, w_o):
    half = SWA_ROT // 2
    n_rot_heads = SWA_HEADS + SWA_KV_HEADS

    def swap_cols(w):
        lead = w.shape[:-1]
        wh = w[..., :SWA_NQ + SWA_NK].reshape(lead + (n_rot_heads, SWA_HEAD_DIM))
        sw = jnp.concatenate([_pair_swap(wh[..., :SWA_ROT], half),
                              jnp.zeros(lead + (n_rot_heads, SWA_HEAD_DIM - SWA_ROT), F32)], axis=-1)
        return sw.reshape(lead + (SWA_NQ + SWA_NK,))

    w_ext = jnp.concatenate([w_qkv, swap_cols(w_qkv)], axis=-1)
    b_ext = jnp.concatenate([b_qkv, swap_cols(b_qkv)], axis=-1)
    return {"w_qkv": w_ext.astype(BF16), "b_qkv": b_ext.reshape(1, -1), "w_o": w_o.astype(BF16)}


def kernel(x_prompt, x_sample, cache_mla_ckv, cache_mla_kpe, cache_swa_k, cache_swa_v, page_table,
           c_prompt, c_sample, ln1_g, ln2_g, w_mod, b_mod,
           mla_w_in, mla_q_norm, mla_kv_norm, mla_w_uq, mla_w_uk, mla_w_uv, mla_w_o,
           swa_w_qkv, swa_b_qkv, swa_sinks, swa_w_o,
           peer_w_q, peer_sub_keys, peer_u, peer_v, final_g):
    batch, seq, d = x_prompt.shape
    db, dec_seq, _ = x_sample.shape
    depth = w_mod.shape[0]
    past_len = page_table.shape[1] * PAGE_SIZE
    n_p, n_s = batch * seq, db * dec_seq

    rows_p = _Rows(n_p, min(ROW_BLOCK, seq), seq_len=seq)
    rows_s = _Rows(n_s, n_s)
    route_p = _Rows(n_p, ROUTE_BLOCK, seq_len=seq)
    route_s = _Rows(n_s, min(ROUTE_BLOCK, n_s))
    expert_p = _Rows(n_p, EXPERT_ROW_BLOCK, seq_len=seq)
    expert_s = _Rows(n_s, min(EXPERT_ROW_BLOCK, n_s))

    pos_p = jnp.arange(seq)
    pos_s = jnp.tile(past_len + jnp.arange(dec_seq), db)
    mla_tab_p, mla_tab_s = _mla_tables(pos_p), _mla_tables(pos_s)
    swa_tab_p, swa_tab_s = _swa_tables(pos_p), _swa_tables(pos_s)

    m_all = _modulation_all(jnp.concatenate([c_prompt, c_sample], axis=0), w_mod, b_mod)
    u_b = peer_u.astype(BF16)
    v_b = peer_v.astype(BF16)

    x_p = x_prompt.reshape(n_p, d)
    x_s = x_sample.reshape(n_s, d)
    ckv_p, kpe_p, ckv_s, kpe_s = [], [], [], []
    swk_p, swv_p, swk_s, swv_s = [], [], [], []
    n_buf = cache_swa_k.shape[2]

    for i in range(depth):
        j = i // 2
        mod_p = m_all[i, :batch].reshape(batch, 1, 6 * d)
        mod_s = jnp.repeat(m_all[i, batch:], dec_seq, axis=0)
        g1 = ln1_g[i].reshape(1, d)
        g2 = ln2_g[i].reshape(1, d)
        if i % 2 == 0:
            w = _mla_weights(mla_w_in[j], mla_q_norm[j], mla_kv_norm[j], mla_w_uq[j], mla_w_uk[j],
                             mla_w_uv[j], mla_w_o[j])
            ckv, kpe, q, k, vt = _mla_project(rows_p, x_p, g1, mod_p, mla_tab_p, w, min(MLA_ATTN_BLOCK, seq))
            o = _mla_attention(q, k, vt, batch, seq)
            x_p = _out_proj(rows_p, o, w["w_o"], x_p, mod_p)
            ckv_p.append(ckv.reshape(batch, seq, MLA_KV_LORA))
            kpe_p.append(kpe.reshape(batch, seq, MLA_ROPE))

            ckv, kpe, q, _, _ = _mla_project(rows_s, x_s, g1, mod_s, mla_tab_s, w, rows_s.tb)
            qlat = _mla_absorb_queries(q, w["w_ukt"]).reshape(db, dec_seq * MLA_HEADS, MLA_KV_LORA)
            qpe = q.reshape(n_s, MLA_HEADS, MLA_HEAD_PAD)[:, :, MLA_NOPE:MLA_NOPE + MLA_ROPE]
            qpe = qpe.reshape(db, dec_seq * MLA_HEADS, MLA_ROPE)
            ckv3 = ckv.reshape(db, dec_seq, MLA_KV_LORA)
            kpe3 = kpe.reshape(db, dec_seq, MLA_ROPE)
            olat = _mla_paged_attention(qlat, qpe, ckv3, kpe3, cache_mla_ckv, cache_mla_kpe, page_table, j)
            olat = olat.reshape(n_s, MLA_HEADS * MLA_KV_LORA)
            x_s = _mla_sample_out(rows_s, olat, w["w_uv"], w["w_o"], x_s, mod_s)
            ckv_s.append(ckv3)
            kpe_s.append(kpe3)
        else:
            w = _swa_weights(swa_w_qkv[j], swa_b_qkv[j], swa_w_o[j])
            q, kb, vb, k, v = _swa_project(rows_p, x_p, g1, mod_p, swa_tab_p, w)
            o = _swa_prompt_attention(q, kb, vb, swa_sinks[j], batch, seq)
            x_p = _out_proj(rows_p, o, w["w_o"], x_p, mod_p)
            nb = min(WINDOW, seq)
            swk_p.append(k.reshape(batch, seq, SWA_KV_HEADS, SWA_HEAD_DIM)[:, seq - nb:])
            swv_p.append(v.reshape(batch, seq, SWA_KV_HEADS, SWA_HEAD_DIM)[:, seq - nb:])

            q, kb, vb, k, v = _swa_project(rows_s, x_s, g1, mod_s, swa_tab_s, w)
            buf_k = cache_swa_k[j].reshape(db, n_buf, SWA_NK)
            buf_v = cache_swa_v[j].reshape(db, n_buf, SWA_NK)
            o = _swa_sample_attention(q, k, v, buf_k, buf_v, swa_sinks[j], db, dec_seq)
            x_s = _out_proj(rows_s, o, w["w_o"], x_s, mod_s)
            k_all = jnp.concatenate([buf_k, k.reshape(db, dec_seq, SWA_NK)], axis=1)[:, -n_buf:]
            v_all = jnp.concatenate([buf_v, v.reshape(db, dec_seq, SWA_NK)], axis=1)[:, -n_buf:]
            swk_s.append(k_all.reshape(db, n_buf, SWA_KV_HEADS, SWA_HEAD_DIM))
            swv_s.append(v_all.reshape(db, n_buf, SWA_KV_HEADS, SWA_HEAD_DIM))

        pw = {"w_q": peer_w_q[i].astype(BF16),
              "sub_keys": peer_sub_keys[i].reshape(2 * PEER_HEADS, PEER_N_KEYS, PEER_HALF).astype(BF16)}
        h, experts, gates = _peer_route(route_p, x_p, g2, mod_p, pw)
        x_p = _peer_experts(expert_p, h, experts, gates, u_b, v_b, i, x_p, mod_p)
        h, experts, gates = _peer_route(route_s, x_s, g2, mod_s, pw)
        x_s = _peer_experts(expert_s, h, experts, gates, u_b, v_b, i, x_s, mod_s)

    fg = final_g.reshape(1, d)
    y_p = _final_norm(rows_p, x_p, fg).reshape(batch, seq, d)
    y_s = _final_norm(rows_s, x_s, fg).reshape(db, dec_seq, d)
    return (y_p, y_s,
            jnp.stack(ckv_p), jnp.stack(kpe_p), jnp.stack(ckv_s), jnp.stack(kpe_s),
            jnp.stack(swk_p), jnp.stack(swv_p), jnp.stack(swk_s), jnp.stack(swv_s))
```
